```python
import jax, jax.numpy as jnp
from jax import lax
import numpy as np

D_MODEL = 2048
BATCH = 8
SEQ = 8192
DEPTH = 1

CHUNK = 64
Q_BLOCK = 128
HEAD_DIM = 128
SB_HEADS = (D_MODEL // 2) // HEAD_DIM
SB_WIDTH = SB_HEADS * HEAD_DIM
HG_EXPAND = 128
HG_HEADS = (D_MODEL // 2) // HG_EXPAND
HG_WIDTH = HG_HEADS * HG_EXPAND
HG_VDIM = HG_WIDTH // HG_HEADS
MIX_WIDTH = SB_WIDTH + HG_WIDTH
IN_COLS = 3 * SB_WIDTH + 4 * HG_WIDTH
D_FF = 4 * D_MODEL
NORM_EPS = 1e-5

kernel_name = "hymba_stickbreak_hgrn2_block"


def _rmsnorm(x, w):
    xf = x.astype(jnp.float32)
    y = xf * lax.rsqrt(jnp.mean(xf * xf, axis=-1, keepdims=True) + NORM_EPS)
    return (y * w.astype(jnp.float32)).astype(x.dtype)


def _heads(t, n_heads):
    b, s, _ = t.shape
    return t.reshape(b, s, n_heads, -1).transpose(0, 2, 1, 3)


def _merge(t):
    b, n, s, d = t.shape
    return t.transpose(0, 2, 1, 3).reshape(b, s, n * d)


def _stick_breaking(q, k, v):
    seq = q.shape[2]
    scale = HEAD_DIM ** -0.5
    outs = []
    for blk in range(seq // Q_BLOCK):
        q0 = blk * Q_BLOCK
        q1 = q0 + Q_BLOCK
        qb = q[:, :, q0:q1].astype(jnp.float32)
        kp = k[:, :, :q1].astype(jnp.float32)
        vp = v[:, :, :q1]
        z = jnp.einsum('bhqd,bhkd->bhqk', qb, kp) * scale
        mask = jnp.arange(q1)[None, :] < jnp.arange(q0, q1)[:, None]
        log_1m = jnp.where(mask, jax.nn.log_sigmoid(-z), 0.0)
        log_after = lax.cumsum(log_1m, axis=3, reverse=True) - log_1m
        w = jnp.where(mask, jnp.exp(jax.nn.log_sigmoid(z) + log_after), 0.0)
        outs.append(jnp.einsum('bhqk,bhkd->bhqd', w.astype(vp.dtype), vp))
    return jnp.concatenate(outs, axis=2)


def _hgrn2(q, k, v, g):
    b_, h_, s_, dk = q.shape
    dv = v.shape[-1]
    n = s_ // CHUNK

    def to_chunks(t):
        return jnp.moveaxis(t.astype(jnp.float32).reshape(b_, h_, n, CHUNK, t.shape[-1]), 2, 0)

    causal = jnp.tril(jnp.ones((CHUNK, CHUNK), dtype=bool))

    def step(state, inp):
        qc, kc, vc, gc = inp
        cum = jnp.cumsum(gc, axis=2)
        diff = jnp.where(causal[:, :, None],
                         cum[:, :, :, None, :] - cum[:, :, None, :, :], -jnp.inf)
        scores = jnp.einsum('bhtk,bhtsk,bhsk->bhts', qc, jnp.exp(diff), kc)
        out = (jnp.einsum('bhts,bhsv->bhtv', scores, vc)
               + jnp.einsum('bhtk,bhkv->bhtv', qc * jnp.exp(cum), state))
        last = cum[:, :, -1:, :]
        new_state = (jnp.exp(last[:, :, 0, :])[..., None] * state
                     + jnp.einsum('bhsk,bhsv->bhkv', kc * jnp.exp(last - cum), vc))
        return new_state, out

    state0 = jnp.zeros((b_, h_, dk, dv), jnp.float32)
    _, outs = lax.scan(step, state0, (to_chunks(q), to_chunks(k), to_chunks(v), to_chunks(g)))
    return jnp.moveaxis(outs, 0, 2).reshape(b_, h_, s_, dv)


def _fwd_setup_inputs(seed: int = 0) -> dict:
    key = jax.random.key(seed)
    ks = jax.random.split(key, 11)
    f32 = jnp.float32
    x = jax.random.normal(ks[0], (BATCH, SEQ, D_MODEL), f32)
    attn_norm_w = 1.0 + 0.01 * jax.random.normal(ks[1], (DEPTH, D_MODEL), f32)
    w_in = jax.random.normal(ks[2], (DEPTH, D_MODEL, IN_COLS), f32) * D_MODEL ** -0.5
    lb_logits = 0.1 * jax.random.normal(ks[3], (DEPTH + 1, HG_WIDTH), f32)
    sb_norm_w = 1.0 + 0.01 * jax.random.normal(ks[4], (DEPTH, HEAD_DIM), f32)
    hg_norm_w = 1.0 + 0.01 * jax.random.normal(ks[5], (DEPTH, HG_VDIM), f32)
    w_out = jax.random.normal(ks[6], (DEPTH, MIX_WIDTH, D_MODEL), f32) * MIX_WIDTH ** -0.5
    mlp_norm_w = 1.0 + 0.01 * jax.random.normal(ks[7], (DEPTH, D_MODEL), f32)
    w_up = jax.random.normal(ks[8], (DEPTH, D_MODEL, D_FF), f32) * D_MODEL ** -0.5
    w_down = jax.random.normal(ks[9], (DEPTH, D_FF, D_MODEL), f32) * D_FF ** -0.5
    final_norm_w = 1.0 + 0.01 * jax.random.normal(ks[10], (D_MODEL,), f32)
    return {"x": x, "attn_norm_w": attn_norm_w, "w_in": w_in, "lb_logits": lb_logits,
            "sb_norm_w": sb_norm_w, "hg_norm_w": hg_norm_w, "w_out": w_out,
            "mlp_norm_w": mlp_norm_w, "w_up": w_up, "w_down": w_down,
            "final_norm_w": final_norm_w}


def _fwd_reference(x, attn_norm_w, w_in, lb_logits, sb_norm_w, hg_norm_w, w_out,
              mlp_norm_w, w_up, w_down, final_norm_w):
    splits = [SB_WIDTH, 2 * SB_WIDTH, 3 * SB_WIDTH, 3 * SB_WIDTH + HG_WIDTH,
              3 * SB_WIDTH + 2 * HG_WIDTH, 3 * SB_WIDTH + 3 * HG_WIDTH]
    lower_bounds = jnp.cumsum(jax.nn.softmax(lb_logits.astype(jnp.float32), axis=0), axis=0)
    h = x
    for layer in range(DEPTH):
        u = _rmsnorm(h, attn_norm_w[layer])
        proj = u @ w_in[layer]
        sb_q, sb_k, sb_v, hg_q, hg_f, hg_i, hg_g = jnp.split(proj, splits, axis=-1)

        o_a = _stick_breaking(_heads(sb_q, SB_HEADS), _heads(sb_k, SB_HEADS), _heads(sb_v, SB_HEADS))
        o_a = _merge(_rmsnorm(o_a, sb_norm_w[layer]))

        lb = lower_bounds[layer]
        f_logit = hg_f.astype(jnp.float32)
        log_f = jnp.log(lb + (1.0 - lb) * jax.nn.sigmoid(f_logit))
        k_in = (1.0 - lb) * jax.nn.sigmoid(-f_logit)
        o_b = _hgrn2(_heads(jax.nn.silu(hg_q), HG_HEADS), _heads(k_in, HG_HEADS),
                     _heads(hg_i, HG_HEADS), _heads(log_f, HG_HEADS)).astype(u.dtype)
        o_b = _merge(_rmsnorm(o_b, hg_norm_w[layer])) * jax.nn.silu(hg_g)

        h = h + jnp.concatenate([o_a, o_b], axis=-1) @ w_out[layer]

        m = _rmsnorm(h, mlp_norm_w[layer])
        h = h + jnp.square(jax.nn.relu(m @ w_up[layer])) @ w_down[layer]
    return _rmsnorm(h, final_norm_w)


import jax as _jax
import jax.numpy as _jnp

TWIN_FORMAT = 'train_step'
FWD_PARAMS = ['x', 'attn_norm_w', 'w_in', 'lb_logits', 'sb_norm_w', 'hg_norm_w', 'w_out', 'mlp_norm_w', 'w_up', 'w_down', 'final_norm_w']
TWIN_WEIGHTS = ['attn_norm_w', 'w_in', 'lb_logits', 'sb_norm_w', 'hg_norm_w', 'w_out', 'mlp_norm_w', 'w_up', 'w_down', 'final_norm_w']
TWIN_DIFF_INPUT = 'x'
TWIN_INPUTS = ['x', 'attn_norm_w', 'w_in', 'lb_logits', 'sb_norm_w', 'hg_norm_w', 'w_out', 'mlp_norm_w', 'w_up', 'w_down', 'final_norm_w', 'loss_target', 'm_attn_norm_w', 'm_w_in', 'm_lb_logits', 'm_sb_norm_w', 'm_hg_norm_w', 'm_w_out', 'm_mlp_norm_w', 'm_w_up', 'm_w_down', 'm_final_norm_w', 'v_attn_norm_w', 'v_w_in', 'v_lb_logits', 'v_sb_norm_w', 'v_hg_norm_w', 'v_w_out', 'v_mlp_norm_w', 'v_w_up', 'v_w_down', 'v_final_norm_w']
TWIN_OUTPUTS = ['loss', 'grad_x', 'grad_attn_norm_w', 'grad_w_in', 'grad_lb_logits', 'grad_sb_norm_w', 'grad_hg_norm_w', 'grad_w_out', 'grad_mlp_norm_w', 'grad_w_up', 'grad_w_down', 'grad_final_norm_w', 'delta_attn_norm_w', 'delta_w_in', 'delta_lb_logits', 'delta_sb_norm_w', 'delta_hg_norm_w', 'delta_w_out', 'delta_mlp_norm_w', 'delta_w_up', 'delta_w_down', 'delta_final_norm_w', 'new_m_attn_norm_w', 'new_m_w_in', 'new_m_lb_logits', 'new_m_sb_norm_w', 'new_m_hg_norm_w', 'new_m_w_out', 'new_m_mlp_norm_w', 'new_m_w_up', 'new_m_w_down', 'new_m_final_norm_w', 'new_v_attn_norm_w', 'new_v_w_in', 'new_v_lb_logits', 'new_v_sb_norm_w', 'new_v_hg_norm_w', 'new_v_w_out', 'new_v_mlp_norm_w', 'new_v_w_up', 'new_v_w_down', 'new_v_final_norm_w']
TWIN_LEAF_KINDS = {'loss': 'loss', 'grad_x': 'grad_x', 'grad_attn_norm_w': 'grad_w', 'grad_w_in': 'grad_w', 'grad_lb_logits': 'grad_w', 'grad_sb_norm_w': 'grad_w', 'grad_hg_norm_w': 'grad_w', 'grad_w_out': 'grad_w', 'grad_mlp_norm_w': 'grad_w', 'grad_w_up': 'grad_w', 'grad_w_down': 'grad_w', 'grad_final_norm_w': 'grad_w', 'delta_attn_norm_w': 'delta_w', 'delta_w_in': 'delta_w', 'delta_lb_logits': 'delta_w', 'delta_sb_norm_w': 'delta_w', 'delta_hg_norm_w': 'delta_w', 'delta_w_out': 'delta_w', 'delta_mlp_norm_w': 'delta_w', 'delta_w_up': 'delta_w', 'delta_w_down': 'delta_w', 'delta_final_norm_w': 'delta_w', 'new_m_attn_norm_w': 'new_m', 'new_m_w_in': 'new_m', 'new_m_lb_logits': 'new_m', 'new_m_sb_norm_w': 'new_m', 'new_m_hg_norm_w': 'new_m', 'new_m_w_out': 'new_m', 'new_m_mlp_norm_w': 'new_m', 'new_m_w_up': 'new_m', 'new_m_w_down': 'new_m', 'new_m_final_norm_w': 'new_m', 'new_v_attn_norm_w': 'new_v', 'new_v_w_in': 'new_v', 'new_v_lb_logits': 'new_v', 'new_v_sb_norm_w': 'new_v', 'new_v_hg_norm_w': 'new_v', 'new_v_w_out': 'new_v', 'new_v_mlp_norm_w': 'new_v', 'new_v_w_up': 'new_v', 'new_v_w_down': 'new_v', 'new_v_final_norm_w': 'new_v'}


def _forward(args):
    return _fwd_reference(*[args[k] for k in FWD_PARAMS])


def _output_shape():
    def fwd():
        inp = _fwd_setup_inputs(0)
        return _fwd_reference(*[inp[k] for k in FWD_PARAMS])
    out = _jax.eval_shape(fwd)
    return out.shape, out.dtype

N_MICROBATCH = 1
ADAM_LR = 0.001
ADAM_B1 = 0.9
ADAM_B2 = 0.999
ADAM_EPS = 1e-08
ADAM_WD = 0.01
ADAM_STEP = 10
PER_EXAMPLE_BATCH_AXIS = {'x': 0, 'loss_target': 0}
SHARED_INPUTS = []
_WEIGHT_DTYPES = {'attn_norm_w': _jnp.float32, 'w_in': _jnp.float32, 'lb_logits': _jnp.float32, 'sb_norm_w': _jnp.float32, 'hg_norm_w': _jnp.float32, 'w_out': _jnp.float32, 'mlp_norm_w': _jnp.float32, 'w_up': _jnp.float32, 'w_down': _jnp.float32, 'final_norm_w': _jnp.float32}
MOMENT_SCALE = {'attn_norm_w': 1.132931e-01, 'w_in': 5.843776e-02, 'lb_logits': 5.962205e-03, 'sb_norm_w': 2.665883e-01, 'hg_norm_w': 1.848333e-01, 'w_out': 8.579591e-02, 'mlp_norm_w': 9.984102e-02, 'w_up': 4.965830e-02, 'w_down': 8.997393e-02, 'final_norm_w': 3.216430e+01}


def _to_microbatches(a, axis):
    t = _jnp.moveaxis(a, axis, 0)
    t = t.reshape((N_MICROBATCH, t.shape[0] // N_MICROBATCH) + t.shape[1:])
    return _jnp.moveaxis(t, 1, axis + 1)


def setup_inputs(seed: int = 0) -> dict:
    inp = _fwd_setup_inputs(seed)
    key = _jax.random.fold_in(_jax.random.key(seed), 7919)
    shape, _ = _output_shape()
    out = dict(inp)
    out["loss_target"] = _jax.random.normal(_jax.random.fold_in(key, 0), shape, _jnp.float32)
    for i, name in enumerate(TWIN_WEIGHTS):
        w = inp[name].astype(_jnp.float32)
        if MOMENT_SCALE is None:
            s = _jnp.sqrt(_jnp.mean(_jnp.square(w)) + 1e-30)
        else:
            s = MOMENT_SCALE[name]
        km, kv = _jax.random.split(_jax.random.fold_in(key, i + 1))
        out[name] = w
        out["m_" + name] = s * _jax.random.normal(km, w.shape, _jnp.float32)
        out["v_" + name] = (s * s) * _jax.random.uniform(kv, w.shape, _jnp.float32, 0.5, 1.5)
    if N_MICROBATCH > 1:
        for name, axis in PER_EXAMPLE_BATCH_AXIS.items():
            out[name] = _to_microbatches(out[name], axis)
    return {'x': out['x'], 'attn_norm_w': out['attn_norm_w'], 'w_in': out['w_in'], 'lb_logits': out['lb_logits'], 'sb_norm_w': out['sb_norm_w'], 'hg_norm_w': out['hg_norm_w'], 'w_out': out['w_out'], 'mlp_norm_w': out['mlp_norm_w'], 'w_up': out['w_up'], 'w_down': out['w_down'], 'final_norm_w': out['final_norm_w'], 'loss_target': out['loss_target'], 'm_attn_norm_w': out['m_attn_norm_w'], 'm_w_in': out['m_w_in'], 'm_lb_logits': out['m_lb_logits'], 'm_sb_norm_w': out['m_sb_norm_w'], 'm_hg_norm_w': out['m_hg_norm_w'], 'm_w_out': out['m_w_out'], 'm_mlp_norm_w': out['m_mlp_norm_w'], 'm_w_up': out['m_w_up'], 'm_w_down': out['m_w_down'], 'm_final_norm_w': out['m_final_norm_w'], 'v_attn_norm_w': out['v_attn_norm_w'], 'v_w_in': out['v_w_in'], 'v_lb_logits': out['v_lb_logits'], 'v_sb_norm_w': out['v_sb_norm_w'], 'v_hg_norm_w': out['v_hg_norm_w'], 'v_w_out': out['v_w_out'], 'v_mlp_norm_w': out['v_mlp_norm_w'], 'v_w_up': out['v_w_up'], 'v_w_down': out['v_w_down'], 'v_final_norm_w': out['v_final_norm_w']}


def _loss(weights, diff, rest, loss_target):
    with _jax.named_scope("forward"):
        args = {**rest, TWIN_DIFF_INPUT: diff, **{k: w.astype(_WEIGHT_DTYPES[k]) for k, w in weights.items()}}
        y = _forward(args)
    with _jax.named_scope("loss_head"):
        err = _jnp.square(y.astype(_jnp.float32) - loss_target)
        return 0.5 * _jnp.sum(_jnp.mean(err, axis=-1)) if err.ndim else 0.5 * err


def _adamw(w, g, m, v):
    m = ADAM_B1 * m + (1.0 - ADAM_B1) * g
    v = ADAM_B2 * v + (1.0 - ADAM_B2) * _jnp.square(g)
    m_hat = m / (1.0 - ADAM_B1 ** ADAM_STEP)
    v_hat = v / (1.0 - ADAM_B2 ** ADAM_STEP)
    delta = -ADAM_LR * (m_hat / (_jnp.sqrt(v_hat) + ADAM_EPS) + ADAM_WD * w)
    return delta, m, v


def reference(x, attn_norm_w, w_in, lb_logits, sb_norm_w, hg_norm_w, w_out, mlp_norm_w, w_up, w_down, final_norm_w, loss_target, m_attn_norm_w, m_w_in, m_lb_logits, m_sb_norm_w, m_hg_norm_w, m_w_out, m_mlp_norm_w, m_w_up, m_w_down, m_final_norm_w, v_attn_norm_w, v_w_in, v_lb_logits, v_sb_norm_w, v_hg_norm_w, v_w_out, v_mlp_norm_w, v_w_up, v_w_down, v_final_norm_w):
    given = dict(x=x, attn_norm_w=attn_norm_w, w_in=w_in, lb_logits=lb_logits, sb_norm_w=sb_norm_w, hg_norm_w=hg_norm_w, w_out=w_out, mlp_norm_w=mlp_norm_w, w_up=w_up, w_down=w_down, final_norm_w=final_norm_w, loss_target=loss_target, m_attn_norm_w=m_attn_norm_w, m_w_in=m_w_in, m_lb_logits=m_lb_logits, m_sb_norm_w=m_sb_norm_w, m_hg_norm_w=m_hg_norm_w, m_w_out=m_w_out, m_mlp_norm_w=m_mlp_norm_w, m_w_up=m_w_up, m_w_down=m_w_down, m_final_norm_w=m_final_norm_w, v_attn_norm_w=v_attn_norm_w, v_w_in=v_w_in, v_lb_logits=v_lb_logits, v_sb_norm_w=v_sb_norm_w, v_hg_norm_w=v_hg_norm_w, v_w_out=v_w_out, v_mlp_norm_w=v_mlp_norm_w, v_w_up=v_w_up, v_w_down=v_w_down, v_final_norm_w=v_final_norm_w)
    weights = {n: given[n] for n in TWIN_WEIGHTS}
    shared = {n: given[n] for n in SHARED_INPUTS}
    per_example = {n: given[n] for n in ['x']}
    grad_fn = _jax.value_and_grad(_loss, argnums=(0, 1))

    def one_microbatch(ex, loss_target):
        ex = dict(ex)
        diff = ex.pop(TWIN_DIFF_INPUT)
        return grad_fn(weights, diff, {**shared, **ex}, loss_target)

    if N_MICROBATCH == 1:
        loss, (grad_w, grad_x) = one_microbatch(per_example, given["loss_target"])
    else:
        def body(carry, xs):
            loss_sum, grad_sum = carry
            l_k, (gw_k, gx_k) = one_microbatch(xs[0], xs[1])
            with _jax.named_scope("update"):
                return (loss_sum + l_k, _jax.tree.map(_jnp.add, grad_sum, gw_k)), gx_k

        init = (_jnp.zeros((), _jnp.float32), _jax.tree.map(_jnp.zeros_like, weights))
        (loss, grad_w), grad_x = _jax.lax.scan(body, init, (per_example, given["loss_target"]))
    with _jax.named_scope("update"):
        delta_w, new_m, new_v = {}, {}, {}
        for n in TWIN_WEIGHTS:
            delta_w[n], new_m[n], new_v[n] = _adamw(weights[n], grad_w[n], given["m_" + n], given["v_" + n])
    return (loss, grad_x, *[grad_w[n] for n in TWIN_WEIGHTS], *[delta_w[n] for n in TWIN_WEIGHTS],
            *[new_m[n] for n in TWIN_WEIGHTS], *[new_v[n] for n in TWIN_WEIGHTS])
```

```python
import functools

import jax
import jax.numpy as jnp
from jax import lax
from jax.experimental import pallas as pl
from jax.experimental.pallas import tpu as pltpu

F32 = jnp.float32
BF16 = jnp.bfloat16
MESH = pl.DeviceIdType.MESH

HEAD = 128
NORM_EPS = 1e-5
N_CHIPS = 4
ATTN_BLOCK = 256
HG_CHUNK = 32
HG_ROWS = 256
VMEM_LIMIT = 56 * 1024 * 1024

ADAM_LR = 0.001
ADAM_B1 = 0.9
ADAM_B2 = 0.999
ADAM_EPS = 1e-08
ADAM_WD = 0.01
ADAM_STEP = 10


def _pick(n, cands):
    for c in cands:
        if n % c == 0:
            return c
    return n


def _params(sem):
    return pltpu.CompilerParams(dimension_semantics=sem, vmem_limit_bytes=VMEM_LIMIT)


def _dot(a, b):
    return jnp.dot(a, b, preferred_element_type=F32)


def _dot_nt(a, b):
    return lax.dot_general(a, b, (((1,), (1,)), ((), ())), preferred_element_type=F32)


def _dot_tn(a, b):
    return lax.dot_general(a, b, (((0,), (0,)), ((), ())), preferred_element_type=F32)


def _hilo(x):
    hi = x.astype(BF16)
    return hi, (x - hi.astype(F32)).astype(BF16)


def _dot_split(tri, x):
    hi, lo = _hilo(x)
    return _dot(tri, hi) + _dot(tri, lo)


def _split_dot(x, tri):
    hi, lo = _hilo(x)
    return _dot(hi, tri) + _dot(lo, tri)


def _dot3(dot, a, b):
    ah, al = _hilo(a)
    bh, bl = _hilo(b)
    return dot(ah, bh) + (dot(ah, bl) + dot(al, bh))


def _sigmoid(x):
    return 1.0 / (1.0 + jnp.exp(-x))


def _mm_body(kind, nk, n_extra, n_out, epi):
    dot = {"nn": _dot, "nt": _dot_nt, "tn": _dot_tn}[kind]

    def finish(acc, extra_refs, out_refs):
        res = epi(acc, *[e[...] for e in extra_refs]) if epi is not None else (acc,)
        for o, r in zip(out_refs, res):
            o[...] = r.astype(o.dtype)

    def body(a_ref, b_ref, *rest):
        extra_refs = rest[:n_extra]
        out_refs = rest[n_extra:n_extra + n_out]
        if nk == 1:
            finish(dot(a_ref[...], b_ref[...]), extra_refs, out_refs)
            return
        acc_ref = rest[n_extra + n_out]
        k = pl.program_id(2)

        @pl.when(k == 0)
        def _():
            acc_ref[...] = jnp.zeros_like(acc_ref)

        acc_ref[...] += dot(a_ref[...], b_ref[...])

        @pl.when(k == nk - 1)
        def _():
            finish(acc_ref[...], extra_refs, out_refs)

    return body


def _mm_nn(a, w, out_dtypes, *, tm, tn, tk, name, epi=None, extras=()):
    m, r = a.shape
    p, _, c = w.shape
    npc = c // tn
    nk = r // tk
    body = _mm_body("nn", nk, len(extras), len(out_dtypes), epi)
    tile = pl.BlockSpec((tm, tn), lambda i, j, k: (i, j))
    return pl.pallas_call(
        body, name=name,
        grid=(m // tm, p * npc, nk),
        in_specs=[pl.BlockSpec((tm, tk), lambda i, j, k: (i, k)),
                  pl.BlockSpec((None, tk, tn), lambda i, j, k: (j // npc, k, j % npc))] + [tile] * len(extras),
        out_specs=[tile] * len(out_dtypes),
        out_shape=[jax.ShapeDtypeStruct((m, p * c), d) for d in out_dtypes],
        scratch_shapes=[pltpu.VMEM((tm, tn), F32)] if nk > 1 else [],
        compiler_params=_params(("parallel", "parallel", "arbitrary")),
    )(a, w, *extras)


def _mm_nt(a, w, out_dtypes, *, tm, tn, tk, name, epi=None, extras=()):
    m, _ = a.shape
    p, r, c = w.shape
    kpc = c // tk
    nk = p * kpc
    body = _mm_body("nt", nk, len(extras), len(out_dtypes), epi)
    tile = pl.BlockSpec((tm, tn), lambda i, j, k: (i, j))
    return pl.pallas_call(
        body, name=name,
        grid=(m // tm, r // tn, nk),
        in_specs=[pl.BlockSpec((tm, tk), lambda i, j, k: (i, k)),
                  pl.BlockSpec((None, tn, tk), lambda i, j, k: (k // kpc, j, k % kpc))] + [tile] * len(extras),
        out_specs=[tile] * len(out_dtypes),
        out_shape=[jax.ShapeDtypeStruct((m, r), d) for d in out_dtypes],
        scratch_shapes=[pltpu.VMEM((tm, tn), F32)] if nk > 1 else [],
        compiler_params=_params(("parallel", "parallel", "arbitrary")),
    )(a, w, *extras)


def _mm_tn(a, g, p, *, tm, tn, tk, name):
    t, r = a.shape
    c = g.shape[1] // p
    npc = c // tn
    nk = t // tk
    body = _mm_body("tn", nk, 0, 1, None)
    return pl.pallas_call(
        body, name=name,
        grid=(r // tm, p * npc, nk),
        in_specs=[pl.BlockSpec((tk, tm), lambda i, j, k: (k, i)),
                  pl.BlockSpec((tk, tn), lambda i, j, k: (k, j))],
        out_specs=[pl.BlockSpec((None, tm, tn), lambda i, j, k: (j // npc, i, j % npc))],
        out_shape=[jax.ShapeDtypeStruct((p, r, c), F32)],
        scratch_shapes=[pltpu.VMEM((tm, tn), F32)] if nk > 1 else [],
        compiler_params=_params(("parallel", "parallel", "arbitrary")),
    )(a, g)[0]


def _rows(fn, row_ins, const_ins, row_outs, acc_outs, *, tm, name):
    specs, arrays = [], []
    t = None
    for item in row_ins:
        if isinstance(item, tuple):
            arr, width, cb = item
            specs.append(pl.BlockSpec((tm, width), functools.partial(lambda i, cb: (i, cb), cb=cb)))
        else:
            arr = item
            specs.append(pl.BlockSpec((tm, arr.shape[1]), lambda i: (i, 0)))
        arrays.append(arr)
        t = arr.shape[0]
    for arr in const_ins:
        specs.append(pl.BlockSpec(arr.shape, lambda i: (0, 0)))
        arrays.append(arr)
    n_in, n_row, n_acc = len(arrays), len(row_outs), len(acc_outs)

    def body(*refs):
        ins = [r[...] for r in refs[:n_in]]
        outs = refs[n_in:]
        row_res, acc_res = fn(ins[:len(row_ins)], ins[len(row_ins):])
        for o, r in zip(outs[:n_row], row_res):
            o[...] = r.astype(o.dtype)
        if n_acc:
            i = pl.program_id(0)

            @pl.when(i == 0)
            def _():
                for o in outs[n_row:]:
                    o[...] = jnp.zeros_like(o)

            for o, r in zip(outs[n_row:], acc_res):
                o[...] += r

    res = pl.pallas_call(
        body, name=name,
        grid=(t // tm,),
        in_specs=specs,
        out_specs=[pl.BlockSpec((tm, c), lambda i: (i, 0)) for c, _ in row_outs]
                  + [pl.BlockSpec((1, c), lambda i: (0, 0)) for c in acc_outs],
        out_shape=[jax.ShapeDtypeStruct((t, c), d) for c, d in row_outs]
                  + [jax.ShapeDtypeStruct((1, c), F32) for c in acc_outs],
        compiler_params=_params(("arbitrary",)),
    )(*arrays)
    return res[:n_row], res[n_row:]


def _rstd(x):
    return lax.rsqrt(jnp.mean(x * x, axis=-1, keepdims=True) + NORM_EPS)


def _rms_bwd(x, w, dy):
    r = _rstd(x)
    n = x * r
    dn = dy * w
    dx = r * (dn - n * jnp.mean(dn * n, axis=-1, keepdims=True))
    return dx, dy * n


def _colsum(x):
    return jnp.sum(x, axis=0, keepdims=True)


def _heads_map(fn, width, *tiles):
    outs = None
    for h in range(width // HEAD):
        res = fn(*[t[:, h * HEAD:(h + 1) * HEAD] for t in tiles])
        if outs is None:
            outs = [[] for _ in res]
        for lst, r in zip(outs, res):
            lst.append(r)
    return [jnp.concatenate(lst, axis=1) for lst in outs]


def _log_one_minus_beta(z):
    return -(jnp.maximum(z, 0.0) + jnp.log(1.0 + jnp.exp(-jnp.abs(z))))


def _attn_fwd(proj, after_tri, norm_w, n_heads, name):
    t = proj.shape[0]
    blk = min(ATTN_BLOCK, t)
    scale = HEAD ** -0.5

    def body(q_ref, k_ref, v_ref, tri_ref, w_ref, o_ref, mix_ref, tot_ref):
        i = pl.program_id(1)
        q = q_ref[...].astype(BF16)
        tri = tri_ref[...]
        rows = lax.broadcasted_iota(jnp.int32, (blk, blk), 0)
        cols = lax.broadcasted_iota(jnp.int32, (blk, blk), 1)

        def block(j, acc_l, acc_o, mask):
            sl = pl.ds(pl.multiple_of(j * blk, blk), blk)
            z = _dot_nt(q, k_ref[sl, :].astype(BF16)) * scale
            lm = _log_one_minus_beta(z)
            lmm = lm if mask is None else jnp.where(mask, lm, 0.0)
            w = jnp.exp(z + lm + acc_l + _split_dot(lmm, tri))
            if mask is not None:
                w = jnp.where(mask, w, 0.0)
            return (acc_l + jnp.sum(lmm, axis=1, keepdims=True),
                    acc_o + _dot(w.astype(BF16), v_ref[sl, :].astype(BF16)))

        acc_l, acc_o = block(i, jnp.zeros((blk, 1), F32), jnp.zeros((blk, HEAD), F32), cols < rows)
        acc_l, acc_o = lax.fori_loop(1, i + 1, lambda jj, c: block(i - jj, c[0], c[1], None), (acc_l, acc_o))
        o_ref[...] = acc_o
        mix_ref[...] = (acc_o * _rstd(acc_o) * w_ref[...]).astype(BF16)
        tot_ref[...] = jnp.broadcast_to(acc_l, (blk, HEAD))

    width = n_heads * HEAD
    qblk = pl.BlockSpec((blk, HEAD), lambda h, i: (i, h))
    return pl.pallas_call(
        body, name=name,
        grid=(n_heads, t // blk),
        in_specs=[qblk,
                  pl.BlockSpec((t, HEAD), lambda h, i: (0, n_heads + h)),
                  pl.BlockSpec((t, HEAD), lambda h, i: (0, 2 * n_heads + h)),
                  pl.BlockSpec((blk, blk), lambda h, i: (0, 0)),
                  pl.BlockSpec((1, HEAD), lambda h, i: (0, 0))],
        out_specs=[qblk, qblk, qblk],
        out_shape=[jax.ShapeDtypeStruct((t, width), F32), jax.ShapeDtypeStruct((t, width), BF16),
                   jax.ShapeDtypeStruct((t, width), F32)],
        compiler_params=_params(("parallel", "arbitrary")),
    )(proj, proj, proj, after_tri, norm_w)


def _attn_bwd(proj, tot, do, upto_tri, before_tri, n_heads, name):
    t = proj.shape[0]
    blk = min(ATTN_BLOCK, t)
    scale = HEAD ** -0.5

    def body(q_ref, k_ref, v_ref, tot_ref, do_ref, upto_ref, before_ref, dq_ref, dk_ref, dv_ref):
        i = pl.program_id(1)

        @pl.when(i == 0)
        def _():
            dk_ref[...] = jnp.zeros_like(dk_ref)
            dv_ref[...] = jnp.zeros_like(dv_ref)

        q = q_ref[...].astype(BF16)
        dob = do_ref[...].astype(BF16)
        total = tot_ref[:, 0:1]
        upto = upto_ref[...]
        before = before_ref[...]
        rows = lax.broadcasted_iota(jnp.int32, (blk, blk), 0)
        cols = lax.broadcasted_iota(jnp.int32, (blk, blk), 1)

        def block(j, seen_l, seen_g, dq, mask):
            sl = pl.ds(pl.multiple_of(j * blk, blk), blk)
            kb = k_ref[sl, :].astype(BF16)
            z = _dot_nt(q, kb) * scale
            lm = _log_one_minus_beta(z)
            lmm = lm if mask is None else jnp.where(mask, lm, 0.0)
            after = (total - seen_l) - _split_dot(lmm, upto)
            w = jnp.exp(z + lm + after)
            if mask is not None:
                w = jnp.where(mask, w, 0.0)
            sig = jnp.exp(z + lm)
            g = w * _dot_nt(dob, v_ref[sl, :].astype(BF16))
            g_before = seen_g + _split_dot(g, before)
            dz = (g * (1.0 - sig) - g_before * sig) * scale
            if mask is not None:
                dz = jnp.where(mask, dz, 0.0)
            dzb = dz.astype(BF16)
            dk_ref[sl, :] += _dot_tn(dzb, q)
            dv_ref[sl, :] += _dot_tn(w.astype(BF16), dob)
            return (seen_l + jnp.sum(lmm, axis=1, keepdims=True), seen_g + jnp.sum(g, axis=1, keepdims=True),
                    dq + _dot(dzb, kb))

        zero = jnp.zeros((blk, 1), F32)
        carry = lax.fori_loop(0, i, lambda j, c: block(j, c[0], c[1], c[2], None),
                              (zero, zero, jnp.zeros((blk, HEAD), F32)))
        dq_ref[...] = block(i, carry[0], carry[1], carry[2], cols < rows)[2]

    width = n_heads * HEAD
    qblk = pl.BlockSpec((blk, HEAD), lambda h, i: (i, h))
    full = pl.BlockSpec((t, HEAD), lambda h, i: (0, h))
    tri = pl.BlockSpec((blk, blk), lambda h, i: (0, 0))
    return pl.pallas_call(
        body, name=name,
        grid=(n_heads, t // blk),
        in_specs=[qblk,
                  pl.BlockSpec((t, HEAD), lambda h, i: (0, n_heads + h)),
                  pl.BlockSpec((t, HEAD), lambda h, i: (0, 2 * n_heads + h)),
                  qblk, qblk, tri, tri],
        out_specs=[qblk, full, full],
        out_shape=[jax.ShapeDtypeStruct((t, width), F32)] * 3,
        compiler_params=_params(("parallel", "arbitrary")),
    )(proj, proj, proj, tot, do, upto_tri, before_tri)


def _lower_bound(logits):
    l0, l1 = logits[0:1, :], logits[1:2, :]
    mx = jnp.maximum(l0, l1)
    e0, e1 = jnp.exp(l0 - mx), jnp.exp(l1 - mx)
    return e0 / (e0 + e1)


def _hg_chunk(qc, kc, gc, tri_lo):
    c = qc.shape[0]
    cum = _dot_split(tri_lo, gc)
    mid = cum[c // 2 - 1:c // 2, :]
    last = cum[c - 1:c, :]
    qt = qc * jnp.exp(cum - mid)
    kt = kc * jnp.exp(mid - cum)
    qe = qc * jnp.exp(cum)
    kd = kc * jnp.exp(last - cum)
    return cum, mid, last, qt, kt, qe, kd


def _hgrn_fwd(proj, lb_logits, norm_w, tri_lo, n_heads, heads_per_step, name):
    t = proj.shape[0]
    bt = min(HG_ROWS, t)
    c = HG_CHUNK
    nc = bt // c
    hw = heads_per_step * HEAD
    width = n_heads * HEAD
    col0 = 3 * width // hw

    def body(hq_ref, hf_ref, hi_ref, hgate_ref, lbl_ref, w_ref, tri_ref, o_ref, mix_ref, st_ref,
             state, q_scr, k_scr, g_scr):
        @pl.when(pl.program_id(1) == 0)
        def _():
            state[...] = jnp.zeros_like(state)

        lb = _lower_bound(lbl_ref[...])
        f = hf_ref[...]
        g_scr[...] = jnp.log(lb + (1.0 - lb) * _sigmoid(f))
        k_scr[...] = (1.0 - lb) * _sigmoid(-f)
        hq = hq_ref[...]
        q_scr[...] = hq * _sigmoid(hq)
        tri = tri_ref[...]
        causal = lax.broadcasted_iota(jnp.int32, (c, c), 1) <= lax.broadcasted_iota(jnp.int32, (c, c), 0)

        def chunk(ci, carry):
            r = pl.ds(pl.multiple_of(ci * c, c), c)
            for h in range(heads_per_step):
                cs = slice(h * HEAD, (h + 1) * HEAD)
                vc = hi_ref[r, cs].astype(BF16)
                _, _, last, qt, kt, qe, kd = _hg_chunk(q_scr[r, cs], k_scr[r, cs], g_scr[r, cs], tri)
                st = state[h]
                st_ref[ci, :, cs] = st
                a = jnp.where(causal, _dot_nt(qt.astype(BF16), kt.astype(BF16)), 0.0)
                o_ref[r, cs] = _dot(a.astype(BF16), vc) + _dot_nt(qe.astype(BF16), st.astype(BF16))
                state[h] = st * jnp.exp(last) + _dot_tn(vc, kd.astype(BF16))
            return carry

        lax.fori_loop(0, nc, chunk, 0)

        def finish(o, gate):
            return ((o * _rstd(o) * w_ref[...]) * (gate * _sigmoid(gate)),)

        mix_ref[...] = _heads_map(finish, hw, o_ref[...], hgate_ref[...])[0].astype(BF16)

    def col(group):
        return pl.BlockSpec((bt, hw), functools.partial(lambda hp, tb, g: (tb, col0 + g * (width // hw) + hp), g=group))

    blk = pl.BlockSpec((bt, hw), lambda hp, tb: (tb, hp))
    return pl.pallas_call(
        body, name=name,
        grid=(n_heads // heads_per_step, t // bt),
        in_specs=[col(0), col(1), col(2), col(3),
                  pl.BlockSpec((2, hw), lambda hp, tb: (0, hp)),
                  pl.BlockSpec((1, HEAD), lambda hp, tb: (0, 0)),
                  pl.BlockSpec((c, c), lambda hp, tb: (0, 0))],
        out_specs=[blk, blk, pl.BlockSpec((nc, HEAD, hw), lambda hp, tb: (tb, 0, hp))],
        out_shape=[jax.ShapeDtypeStruct((t, width), F32), jax.ShapeDtypeStruct((t, width), BF16),
                   jax.ShapeDtypeStruct((t // c, HEAD, width), F32)],
        scratch_shapes=[pltpu.VMEM((heads_per_step, HEAD, HEAD), F32)] + [pltpu.VMEM((bt, hw), F32)] * 3,
        compiler_params=_params(("parallel", "arbitrary")),
    )(proj, proj, proj, proj, lb_logits, norm_w, tri_lo)


def _hgrn_bwd(proj, do, states, lb_logits, tri_lo, tri_up, n_heads, heads_per_step, name):
    t = proj.shape[0]
    bt = min(HG_ROWS, t)
    c = HG_CHUNK
    nc = bt // c
    nb = t // bt
    hw = heads_per_step * HEAD
    width = n_heads * HEAD
    col0 = 3 * width // hw

    def body(hq_ref, hf_ref, hi_ref, do_ref, st_ref, lbl_ref, lo_ref, up_ref, dq_ref, df_ref, di_ref, dlb_ref,
             dstate, q_scr, k_scr, g_scr, dk_scr, dg_scr):
        @pl.when(pl.program_id(1) == 0)
        def _():
            dstate[...] = jnp.zeros_like(dstate)
            dlb_ref[...] = jnp.zeros_like(dlb_ref)

        lb = _lower_bound(lbl_ref[...])
        f = hf_ref[...]
        sg = _sigmoid(f)
        sgn = _sigmoid(-f)
        den = lb + (1.0 - lb) * sg
        g_scr[...] = jnp.log(den)
        k_scr[...] = (1.0 - lb) * sgn
        hq = hq_ref[...]
        sq = _sigmoid(hq)
        q_scr[...] = hq * sq
        tri_lo_v = lo_ref[...]
        tri_up_v = up_ref[...]
        causal = lax.broadcasted_iota(jnp.int32, (c, c), 1) <= lax.broadcasted_iota(jnp.int32, (c, c), 0)
        last_row = lax.broadcasted_iota(jnp.int32, (c, HEAD), 0) == c - 1

        def chunk(cc, carry):
            ci = nc - 1 - cc
            r = pl.ds(pl.multiple_of(ci * c, c), c)
            for h in range(heads_per_step):
                cs = slice(h * HEAD, (h + 1) * HEAD)
                qc, kc = q_scr[r, cs], k_scr[r, cs]
                vc = hi_ref[r, cs]
                doc = do_ref[r, cs]
                cum, mid, last, qt, kt, qe, kd = _hg_chunk(qc, kc, g_scr[r, cs], tri_lo_v)
                st = st_ref[ci, :, cs]
                dst = dstate[h]
                a = jnp.where(causal, _dot3(_dot_nt, qt, kt), 0.0)
                da = jnp.where(causal, _dot3(_dot_nt, doc, vc), 0.0)
                di_ref[r, cs] = _dot3(_dot_tn, a, doc) + _dot3(_dot_nt, kd, dst)
                dq_inter = _dot3(_dot, doc, st) * jnp.exp(cum)
                dk_inter = _dot3(_dot, vc, dst) * jnp.exp(last - cum)
                dq = _dot3(_dot, da, kt) * jnp.exp(cum - mid) + dq_inter
                dk = _dot3(_dot_tn, da, qt) * jnp.exp(mid - cum) + dk_inter
                e_last = jnp.exp(last)
                d_last = _colsum(kc * dk_inter) + e_last * _colsum(dst * st)
                dcum = qc * dq - kc * dk + jnp.where(last_row, d_last, 0.0)
                dq_ref[r, cs] = dq
                dk_scr[r, cs] = dk
                dg_scr[r, cs] = _dot_split(tri_up_v, dcum)
                dstate[h] = dst * e_last + _dot3(_dot_tn, doc, qe)
            return carry

        lax.fori_loop(0, nc, chunk, 0)

        e = (dg_scr[...] / den - dk_scr[...]) * sgn
        df_ref[...] = e * (1.0 - lb) * sg
        dlb_ref[...] += _colsum(e)
        dq_ref[...] = dq_ref[...] * (sq * (1.0 + hq * (1.0 - sq)))

    def col(group):
        return pl.BlockSpec((bt, hw), functools.partial(
            lambda hp, tb, g: (nb - 1 - tb, col0 + g * (width // hw) + hp), g=group))

    blk = pl.BlockSpec((bt, hw), lambda hp, tb: (nb - 1 - tb, hp))
    tri = pl.BlockSpec((c, c), lambda hp, tb: (0, 0))
    return pl.pallas_call(
        body, name=name,
        grid=(n_heads // heads_per_step, nb),
        in_specs=[col(0), col(1), col(2), blk,
                  pl.BlockSpec((nc, HEAD, hw), lambda hp, tb: (nb - 1 - tb, 0, hp)),
                  pl.BlockSpec((2, hw), lambda hp, tb: (0, hp)), tri, tri],
        out_specs=[blk, blk, blk, pl.BlockSpec((1, hw), lambda hp, tb: (0, hp))],
        out_shape=[jax.ShapeDtypeStruct((t, width), F32)] * 3 + [jax.ShapeDtypeStruct((1, width), F32)],
        scratch_shapes=[pltpu.VMEM((heads_per_step, HEAD, HEAD), F32)] + [pltpu.VMEM((bt, hw), F32)] * 5,
        compiler_params=_params(("parallel", "arbitrary")),
    )(proj, proj, proj, do, states, lb_logits, tri_lo, tri_up)


def _place():
    x, y, c = lax.axis_index("x"), lax.axis_index("y"), lax.axis_index("c")
    chips = [(1 - x, y), (x, 1 - y), (1 - x, 1 - y)]
    return x, y, c, chips


ANY = pl.BlockSpec(memory_space=pl.ANY)


def _gather_weights(shards, name):
    n = len(shards)

    def body(*refs):
        ins, outs = refs[:n], refs[n:2 * n]
        send, recv, local = refs[2 * n:]
        x, y, c, chips = _place()
        mine = 2 * x + y

        def half(ref, who, core):
            h = ref.shape[-2] // 2
            return ref.at[who, pl.ds(core * h, h), :]

        def copy(w, k, src, dst, to):
            return pltpu.make_async_remote_copy(src_ref=src, dst_ref=dst, send_sem=send.at[6 * w + k],
                                                recv_sem=recv.at[6 * w + k], device_id=to, device_id_type=MESH)

        own, sent = [], []
        for w in range(n):
            own.append(pltpu.make_async_copy(ins[w], outs[w].at[mine], local.at[w]))
            own[-1].start()
            h = ins[w].shape[0] // 2
            for k, (qx, qy) in enumerate(chips):
                sent.append(copy(w, k, ins[w].at[pl.ds(c * h, h), :], half(outs[w], mine, c), (qx, qy, c)))
                sent[-1].start()
        for w in range(n):
            for k, (qx, qy) in enumerate(chips):
                landed = half(outs[w], 2 * qx + qy, c)
                copy(w, k, landed, landed, (x, y, c)).wait_recv()
                sent.append(copy(w, 3 + k, landed, landed, (x, y, 1 - c)))
                sent[-1].start()
        for w in range(n):
            for k, (qx, qy) in enumerate(chips):
                passed = half(outs[w], 2 * qx + qy, 1 - c)
                copy(w, 3 + k, passed, passed, (x, y, c)).wait_recv()
        for cp in sent:
            cp.wait_send()
        for cp in own:
            cp.wait()

    return pl.pallas_call(
        body, name=name,
        in_specs=[ANY] * n, out_specs=[ANY] * n,
        out_shape=[jax.ShapeDtypeStruct((N_CHIPS,) + s.shape, s.dtype) for s in shards],
        scratch_shapes=[pltpu.SemaphoreType.DMA((6 * n,)), pltpu.SemaphoreType.DMA((6 * n,)),
                        pltpu.SemaphoreType.DMA((n,))],
        compiler_params=pltpu.CompilerParams(has_side_effects=True),
    )(*shards)


def _sibling_swap(grads, name):
    n = len(grads)

    def body(*refs):
        ins, outs = refs[:n], refs[n:2 * n]
        send, recv = refs[2 * n:]
        x, y, c, _ = _place()
        cps = []
        for w in range(n):
            h = ins[w].shape[1] // 2
            cps.append(pltpu.make_async_remote_copy(
                src_ref=ins[w].at[:, pl.ds((1 - c) * h, h), :], dst_ref=outs[w], send_sem=send.at[w],
                recv_sem=recv.at[w], device_id=(x, y, 1 - c), device_id_type=MESH))
            cps[-1].start()
        for cp in cps:
            cp.wait()

    return pl.pallas_call(
        body, name=name, in_specs=[ANY] * n, out_specs=[ANY] * n,
        out_shape=[jax.ShapeDtypeStruct((g.shape[0], g.shape[1] // 2, g.shape[2]), g.dtype) for g in grads],
        scratch_shapes=[pltpu.SemaphoreType.DMA((n,)), pltpu.SemaphoreType.DMA((n,))],
    )(*grads)


def _chip_scatter(sums, name):
    n = len(sums)

    def body(*refs):
        ins, outs = refs[:n], refs[n:2 * n]
        send, recv = refs[2 * n:]
        _, _, c, chips = _place()
        cps = []
        for w in range(n):
            for k, (qx, qy) in enumerate(chips):
                cps.append(pltpu.make_async_remote_copy(
                    src_ref=ins[w].at[2 * qx + qy], dst_ref=outs[w].at[k], send_sem=send.at[3 * w + k],
                    recv_sem=recv.at[3 * w + k], device_id=(qx, qy, c), device_id_type=MESH))
                cps[-1].start()
        for cp in cps:
            cp.wait()

    return pl.pallas_call(
        body, name=name, in_specs=[ANY] * n, out_specs=[ANY] * n,
        out_shape=[jax.ShapeDtypeStruct((3,) + s.shape[1:], s.dtype) for s in sums],
        scratch_shapes=[pltpu.SemaphoreType.DMA((3 * n,)), pltpu.SemaphoreType.DMA((3 * n,))],
    )(*sums)


def _sibling_join(halves, name):
    n = len(halves)

    def body(*refs):
        ins, outs = refs[:n], refs[n:2 * n]
        send, recv, local = refs[2 * n:]
        x, y, c, _ = _place()
        cps, own = [], []
        for w in range(n):
            h = ins[w].shape[0]
            rows = outs[w].at[pl.ds(c * h, h), :]
            own.append(pltpu.make_async_copy(ins[w], rows, local.at[w]))
            own[-1].start()
            cps.append(pltpu.make_async_remote_copy(
                src_ref=ins[w], dst_ref=rows, send_sem=send.at[w], recv_sem=recv.at[w],
                device_id=(x, y, 1 - c), device_id_type=MESH))
            cps[-1].start()
        for w in range(n):
            h = ins[w].shape[0]
            theirs = outs[w].at[pl.ds((1 - c) * h, h), :]
            pltpu.make_async_remote_copy(src_ref=theirs, dst_ref=theirs, send_sem=send.at[w], recv_sem=recv.at[w],
                                         device_id=(x, y, c), device_id_type=MESH).wait_recv()
        for cp in cps:
            cp.wait_send()
        for cp in own:
            cp.wait()

    return pl.pallas_call(
        body, name=name, in_specs=[ANY] * n, out_specs=[ANY] * n,
        out_shape=[jax.ShapeDtypeStruct((2 * s.shape[0], s.shape[1]), s.dtype) for s in halves],
        scratch_shapes=[pltpu.SemaphoreType.DMA((n,)), pltpu.SemaphoreType.DMA((n,)), pltpu.SemaphoreType.DMA((n,))],
    )(*halves)


def _all_sum_small(vec, name):
    n = vec.shape[1]

    def body(v_ref, out_ref, buf, send, recv):
        x, y, c, _ = _place()
        me = 4 * x + 2 * y + c
        buf[me] = v_ref[...]
        peers = []
        for mask in range(1, 8):
            px = 1 - x if mask & 4 else x
            py = 1 - y if mask & 2 else y
            pc = 1 - c if mask & 1 else c
            peers.append((px, py, pc))
        cps = []
        for k, peer in enumerate(peers):
            cps.append(pltpu.make_async_remote_copy(src_ref=buf.at[me], dst_ref=buf.at[me], send_sem=send.at[k],
                                                    recv_sem=recv.at[k], device_id=peer, device_id_type=MESH))
            cps[-1].start()
        for k, (px, py, pc) in enumerate(peers):
            slot = buf.at[4 * px + 2 * py + pc]
            pltpu.make_async_remote_copy(src_ref=slot, dst_ref=slot, send_sem=send.at[k], recv_sem=recv.at[k],
                                         device_id=(x, y, c), device_id_type=MESH).wait_recv()
        for cp in cps:
            cp.wait_send()
        total = buf[0]
        for d in range(1, 8):
            total = total + buf[d]
        out_ref[...] = total

    vm = pl.BlockSpec(memory_space=pltpu.VMEM)
    return pl.pallas_call(
        body, name=name, in_specs=[vm], out_specs=vm,
        out_shape=jax.ShapeDtypeStruct(vec.shape, F32),
        scratch_shapes=[pltpu.VMEM((8, 8, n), F32), pltpu.SemaphoreType.DMA((7,)), pltpu.SemaphoreType.DMA((7,))],
    )(vec)


def _pair_sum(g, buf, ids, name):
    p, r, c = g.shape
    h = r // 2
    tr = _pick(h, (256, 128, 64, 32, 16))
    nh = h // tr

    def body(ids_ref, g_ref, b_ref, sums_ref, own_ref):
        s = g_ref[...] + b_ref[...]
        sums_ref[...] = s.astype(BF16)

        @pl.when(pl.program_id(1) == ids_ref[1])
        def _():
            own_ref[...] = s

    return pl.pallas_call(
        body, name=name,
        grid_spec=pltpu.PrefetchScalarGridSpec(
            num_scalar_prefetch=1, grid=(nh, p),
            in_specs=[pl.BlockSpec((None, tr, c), lambda i, q, ids: (q, ids[0] * nh + i, 0)),
                      pl.BlockSpec((None, tr, c), lambda i, q, ids: (q, i, 0))],
            out_specs=[pl.BlockSpec((None, tr, c), lambda i, q, ids: (q, i, 0)),
                       pl.BlockSpec((tr, c), lambda i, q, ids: (i, 0))]),
        out_shape=[jax.ShapeDtypeStruct((p, h, c), BF16), jax.ShapeDtypeStruct((h, c), F32)],
        compiler_params=_params(("parallel", "arbitrary")),
    )(ids, g, buf)


def _final_sum(own, others, name):
    h, c = own.shape
    tr = _pick(h, (256, 128, 64, 32, 16))

    def body(own_ref, oth_ref, out_ref):
        s = own_ref[...]
        for k in range(3):
            s = s + oth_ref[k].astype(F32)
        out_ref[...] = s

    return pl.pallas_call(
        body, name=name, grid=(h // tr,),
        in_specs=[pl.BlockSpec((tr, c), lambda i: (i, 0)), pl.BlockSpec((3, tr, c), lambda i: (0, i, 0))],
        out_specs=pl.BlockSpec((tr, c), lambda i: (i, 0)),
        out_shape=jax.ShapeDtypeStruct((h, c), F32),
        compiler_params=_params(("parallel",)),
    )(own, others)


def _adamw(w, g, m, v, name):
    r, c = w.shape
    tm = _pick(r, (256, 128, 64, 32, 16, 8)) if r >= 8 else r

    def fn(rows, _):
        w_, g_, m_, v_ = rows
        m2 = ADAM_B1 * m_ + (1.0 - ADAM_B1) * g_
        v2 = ADAM_B2 * v_ + (1.0 - ADAM_B2) * (g_ * g_)
        m_hat = m2 / (1.0 - ADAM_B1 ** ADAM_STEP)
        v_hat = v2 / (1.0 - ADAM_B2 ** ADAM_STEP)
        delta = -ADAM_LR * (m_hat / (jnp.sqrt(v_hat) + ADAM_EPS) + ADAM_WD * w_)
        return [delta, m2, v2], []

    outs, _ = _rows(fn, [w, g, m, v], [], [(c, F32)] * 3, [], tm=tm, name=name)
    return outs


def kernel(x, attn_norm_w, w_in, lb_logits, sb_norm_w, hg_norm_w, w_out, mlp_norm_w, w_up, w_down, final_norm_w, loss_target, m_attn_norm_w, m_w_in, m_lb_logits, m_sb_norm_w, m_hg_norm_w, m_w_out, m_mlp_norm_w, m_w_up, m_w_down, m_final_norm_w, v_attn_norm_w, v_w_in, v_lb_logits, v_sb_norm_w, v_hg_norm_w, v_w_out, v_mlp_norm_w, v_w_up, v_w_down, v_final_norm_w):
    xs, tgt = x[0], loss_target[0]
    t, d = xs.shape
    width = d // 2
    n_heads = width // HEAD
    hps = min(4, n_heads)
    final_w = final_norm_w.reshape(1, d)
    tm_rows = _pick(t, (256, 128))
    tm = _pick(t, (1024, 512, 256))
    blk = min(ATTN_BLOCK, t)
    ones_a = jnp.ones((blk, blk), F32)
    after_tri = jnp.tril(ones_a, -1).astype(BF16)
    upto_tri = jnp.triu(ones_a).astype(BF16)
    before_tri = jnp.triu(ones_a, 1).astype(BF16)
    ones_c = jnp.ones((HG_CHUNK, HG_CHUNK), F32)
    tri_lo, tri_up = jnp.tril(ones_c).astype(BF16), jnp.triu(ones_c).astype(BF16)
    cx, cy, cc = lax.axis_index("x"), lax.axis_index("y"), lax.axis_index("c")
    ids = jnp.stack([cc, 2 * cx + cy]).astype(jnp.int32)

    shards = [w_in[0], w_out[0], w_up[0], w_down[0]]
    cast = []
    for i, s in enumerate(shards):
        (b,), _ = _rows(lambda r, _c: ([r[0]], []), [s], [], [(s.shape[1], BF16)], [],
                        tm=_pick(s.shape[0], (256, 128, 64, 32, 16)), name=f"cast_w{i}")
        cast.append(b)
    g_in, g_out, g_up, g_down = _gather_weights(cast, "gather_weights")
    d_ff = N_CHIPS * w_up.shape[2]
    w_out_all = g_out.reshape(1, d, d)
    w_down_all = g_down.reshape(1, d_ff, d)
    cs_in, cs_up = g_in.shape[2], g_up.shape[2]
    tn_in = _pick(cs_in, (1792, 896, 512, 256, 128))
    tn_up = _pick(cs_up, (1024, 512, 256))
    tn_d = _pick(d, (1024, 512, 256))
    tk_d = _pick(d, (2048, 1024, 512))

    (u,), _ = _rows(lambda r, c_: ([r[0] * _rstd(r[0]) * c_[0]], []), [xs], [attn_norm_w], [(d, BF16)], [],
                    tm=tm_rows, name="norm_in")
    (proj,) = _mm_nn(u, g_in, [F32], tm=_pick(t, (512, 256)), tn=tn_in, tk=tk_d, name="proj_in")
    o_a, mix_a, sb_tot = _attn_fwd(proj, after_tri, sb_norm_w, n_heads, "sb_fwd")
    o_b, mix_b, states = _hgrn_fwd(proj, lb_logits, hg_norm_w, tri_lo, n_heads, hps, "hg_fwd")
    mix = jnp.concatenate([mix_a, mix_b], axis=1)
    (h1,) = _mm_nn(mix, w_out_all, [F32], tm=tm, tn=tn_d, tk=tk_d, name="proj_out",
                   epi=lambda acc, res: (acc + res,), extras=(xs,))
    (mn,), _ = _rows(lambda r, c_: ([r[0] * _rstd(r[0]) * c_[0]], []), [h1], [mlp_norm_w], [(d, BF16)], [],
                     tm=tm_rows, name="norm_mlp")
    up_b, act = _mm_nn(mn, g_up, [BF16, BF16], tm=tm, tn=tn_up, tk=tk_d, name="mlp_up",
                       epi=lambda acc: (acc, jnp.square(jnp.maximum(acc, 0.0))))
    (h2,) = _mm_nn(act, w_down_all, [F32], tm=tm, tn=tn_d, tk=_pick(d_ff, (2048, 1024)), name="mlp_down",
                   epi=lambda acc, res: (acc + res,), extras=(h1,))

    def head(rows, consts):
        hh, tg = rows
        w = consts[0]
        n = hh * _rstd(hh)
        err = n * w - tg
        dhh, dw_rows = _rms_bwd(hh, w, err * (1.0 / d))
        return [dhh, dhh], [_colsum(dw_rows), _colsum(err * err)]

    (dh2, dh2_b), (g_final, loss_cols) = _rows(head, [h2, tgt], [final_w], [(d, F32), (d, BF16)], [d, d],
                                                 tm=tm_rows, name="loss_head")

    (dup,) = _mm_nt(dh2_b, w_down_all, [BF16], tm=tm, tn=_pick(d_ff, (1024, 512)), tk=tk_d, name="mlp_down_dx",
                    epi=lambda acc, upv: (acc * (2.0 * jnp.maximum(upv.astype(F32), 0.0)),), extras=(up_b,))
    gw_down = _mm_tn(act, dh2_b, 1, tm=_pick(d_ff, (1024, 512)), tn=tn_d, tk=_pick(t, (1024, 512, 256)),
                     name="mlp_down_dw")
    (dmn,) = _mm_nt(dup, g_up, [F32], tm=tm, tn=tn_d, tk=_pick(cs_up, (2048, 1024, 512)), name="mlp_up_dx")
    gw_up = _mm_tn(mn, dup, N_CHIPS, tm=tn_d, tn=tn_up, tk=_pick(t, (1024, 512, 256)), name="mlp_up_dw")

    def norm_back(rows, consts):
        xx, dy, skip = rows
        dx, dw_rows = _rms_bwd(xx, consts[0], dy)
        tot = dx + skip
        return [tot, tot], [_colsum(dw_rows)]

    (dh1, dh1_b), (g_mlp_norm,) = _rows(norm_back, [h1, dmn, dh2], [mlp_norm_w], [(d, F32), (d, BF16)], [d],
                                         tm=tm_rows, name="norm_mlp_bwd")

    (dmix,) = _mm_nt(dh1_b, w_out_all, [F32], tm=tm, tn=tn_d, tk=tk_d, name="proj_out_dx")
    gw_out = _mm_tn(mix, dh1_b, 1, tm=tn_d, tn=tn_d, tk=_pick(t, (1024, 512, 256)), name="proj_out_dw")

    def sb_norm_back(rows, consts):
        dx, dw_rows = _heads_map(lambda o, dy: _rms_bwd(o, consts[0], dy), width, *rows)
        dw = sum(_colsum(dw_rows[:, h * HEAD:(h + 1) * HEAD]) for h in range(n_heads))
        return [dx], [dw]

    (do_a,), (g_sb_norm,) = _rows(sb_norm_back, [o_a, (dmix, width, 0)], [sb_norm_w], [(width, F32)], [HEAD],
                                  tm=tm_rows, name="sb_norm_bwd")
    dq_a, dk_a, dv_a = _attn_bwd(proj, sb_tot, do_a, upto_tri, before_tri, n_heads, "sb_bwd")

    def hg_out_back(rows, consts):
        def one(o, gate, dy):
            sg = _sigmoid(gate)
            silu = gate * sg
            n = o * _rstd(o) * consts[0]
            do, dw_rows = _rms_bwd(o, consts[0], dy * silu)
            return do, dy * n * (sg * (1.0 + gate * (1.0 - sg))), dw_rows
        do, dgate, dw_rows = _heads_map(one, width, *rows)
        dw = sum(_colsum(dw_rows[:, h * HEAD:(h + 1) * HEAD]) for h in range(n_heads))
        return [do, dgate], [dw]

    (do_b, dgate), (g_hg_norm,) = _rows(hg_out_back, [o_b, (proj, width, 6), (dmix, width, 1)], [hg_norm_w],
                                         [(width, F32)] * 2, [HEAD], tm=tm_rows, name="hg_out_bwd")
    dhq, dhf, dhi, dlb = _hgrn_bwd(proj, do_b, states, lb_logits, tri_lo, tri_up, n_heads, hps, "hg_bwd")

    (dproj,), _ = _rows(lambda r, _c: ([jnp.concatenate([p.astype(BF16) for p in r], axis=1)], []),
                        [dq_a, dk_a, dv_a, dhq, dhf, dhi, dgate], [], [(7 * width, BF16)], [],
                        tm=tm_rows, name="pack_dproj")
    (du,) = _mm_nt(dproj, g_in, [F32], tm=tm, tn=tn_d, tk=_pick(cs_in, (1792, 896, 512, 256, 128)), name="proj_in_dx")
    gw_in = _mm_tn(u, dproj, N_CHIPS, tm=tn_d, tn=tn_in, tk=_pick(t, (1024, 512, 256)), name="proj_in_dw")
    (dx,), (g_attn_norm,) = _rows(lambda r, c_: (lambda dxx, dwr: ([dxx + r[2]], [_colsum(dwr)]))(
        *_rms_bwd(r[0], c_[0], r[1])), [xs, du, dh1], [attn_norm_w], [(d, F32)], [d], tm=tm_rows, name="norm_in_bwd")

    grads = [gw_in, gw_out.reshape(N_CHIPS, d // N_CHIPS, d), gw_up, gw_down.reshape(N_CHIPS, d_ff // N_CHIPS, d)]
    theirs = _sibling_swap(grads, "grads_to_sibling")
    pair = [_pair_sum(g, b, ids, f"grads_pair_sum{i}") for i, (g, b) in enumerate(zip(grads, theirs))]
    landed = _chip_scatter([p[0] for p in pair], "grads_to_chips")
    halves = [_final_sum(p[1], r, f"grads_final_sum{i}") for i, (p, r) in enumerate(zip(pair, landed))]
    g_w_in, g_w_out, g_w_up, g_w_down = _sibling_join(halves, "grads_join")

    pieces = [g_attn_norm, g_mlp_norm, g_final, g_sb_norm, g_hg_norm, dlb, loss_cols]
    sizes = [p.shape[1] for p in pieces]
    flat = jnp.concatenate(pieces, axis=1)
    n_small = -(-flat.shape[1] // 1024) * 1024
    flat = jnp.pad(flat, ((0, 0), (0, n_small - flat.shape[1]))).reshape(8, n_small // 8)
    flat = _all_sum_small(flat, "small_all_sum").reshape(1, n_small)
    offs = [sum(sizes[:i]) for i in range(len(sizes))]
    g_attn_norm, g_mlp_norm, g_final, g_sb_norm, g_hg_norm, dlb, loss_cols = [
        flat[:, o:o + s] for o, s in zip(offs, sizes)]

    def small_tail(lbl_ref, dlb_ref, loss_ref, glb_ref, out_ref):
        lb = _lower_bound(lbl_ref[...])
        g0 = dlb_ref[...] * lb * (1.0 - lb)
        glb_ref[0:1, :] = g0
        glb_ref[1:2, :] = -g0
        out_ref[...] = jnp.zeros_like(out_ref) + 0.5 * jnp.sum(loss_ref[...]) * (1.0 / d)

    vm = pl.BlockSpec(memory_space=pltpu.VMEM)
    g_lb, loss11 = pl.pallas_call(
        small_tail, name="small_tail", in_specs=[vm, vm, vm], out_specs=[vm, vm],
        out_shape=[jax.ShapeDtypeStruct(lb_logits.shape, F32), jax.ShapeDtypeStruct((1, 128), F32)],
    )(lb_logits, dlb, loss_cols)
    loss = loss11[0, 0]

    names = ["attn_norm_w", "w_in", "lb_logits", "sb_norm_w", "hg_norm_w", "w_out", "mlp_norm_w", "w_up", "w_down",
             "final_norm_w"]
    ws = [attn_norm_w, w_in[0], lb_logits, sb_norm_w, hg_norm_w, w_out[0], mlp_norm_w, w_up[0], w_down[0], final_w]
    gs = [g_attn_norm, g_w_in, g_lb, g_sb_norm, g_hg_norm, g_w_out, g_mlp_norm, g_w_up, g_w_down, g_final]
    ms = [m_attn_norm_w, m_w_in[0], m_lb_logits, m_sb_norm_w, m_hg_norm_w, m_w_out[0], m_mlp_norm_w, m_w_up[0],
          m_w_down[0], m_final_norm_w.reshape(1, d)]
    vs = [v_attn_norm_w, v_w_in[0], v_lb_logits, v_sb_norm_w, v_hg_norm_w, v_w_out[0], v_mlp_norm_w, v_w_up[0],
          v_w_down[0], v_final_norm_w.reshape(1, d)]
    shapes = [attn_norm_w.shape, w_in.shape, lb_logits.shape, sb_norm_w.shape, hg_norm_w.shape, w_out.shape,
              mlp_norm_w.shape, w_up.shape, w_down.shape, final_norm_w.shape]
    deltas, new_ms, new_vs = [], [], []
    for nm, w_, g_, m_, v_ in zip(names, ws, gs, ms, vs):
        dl, m2, v2 = _adamw(w_, g_, m_, v_, "adamw_" + nm)
        deltas.append(dl)
        new_ms.append(m2)
        new_vs.append(v2)

    def shaped(lst):
        return [a.reshape(s) for a, s in zip(lst, shapes)]

    return (loss, dx[None], *shaped(gs), *shaped(deltas), *shaped(new_ms), *shaped(new_vs))
```

```python
import functools

import jax
import jax.numpy as jnp
from jax import lax
from jax.experimental import pallas as pl
from jax.experimental.pallas import tpu as pltpu

F32 = jnp.float32
BF16 = jnp.bfloat16
MESH = pl.DeviceIdType.MESH

HEAD = 128
NORM_EPS = 1e-5
N_CHIPS = 4
ATTN_BLOCK = 256
ATTN_SUBS = 2
HG_CHUNK = 32
HG_ROWS = 256
VMEM_LIMIT = 56 * 1024 * 1024

ADAM_LR = 0.001
ADAM_B1 = 0.9
ADAM_B2 = 0.999
ADAM_EPS = 1e-08
ADAM_WD = 0.01
ADAM_STEP = 10


def _pick(n, cands):
    for c in cands:
        if n % c == 0:
            return c
    return n


def _params(sem):
    return pltpu.CompilerParams(dimension_semantics=sem, vmem_limit_bytes=VMEM_LIMIT)


def _dot(a, b):
    return jnp.dot(a, b, preferred_element_type=F32)


def _dot_nt(a, b):
    return lax.dot_general(a, b, (((1,), (1,)), ((), ())), preferred_element_type=F32)


def _dot_tn(a, b):
    return lax.dot_general(a, b, (((0,), (0,)), ((), ())), preferred_element_type=F32)


def _hilo(x):
    hi = x.astype(BF16)
    return hi, (x - hi.astype(F32)).astype(BF16)


def _dot_split(tri, x):
    hi, lo = _hilo(x)
    return _dot(tri, hi) + _dot(tri, lo)


def _split_dot(x, tri):
    hi, lo = _hilo(x)
    return _dot(hi, tri) + _dot(lo, tri)


def _dot3(dot, a, b):
    return dot(a[0], b[0]) + (dot(a[0], b[1]) + dot(a[1], b[0]))


def _sigmoid(x):
    return 1.0 / (1.0 + jnp.exp(-x))


def _mm_body(kind, nk, n_extra, n_out, epi):
    dot = {"nn": _dot, "nt": _dot_nt, "tn": _dot_tn}[kind]

    def finish(acc, extra_refs, out_refs):
        res = epi(acc, *[e[...] for e in extra_refs]) if epi is not None else (acc,)
        for o, r in zip(out_refs, res):
            o[...] = r.astype(o.dtype)

    def body(a_ref, b_ref, *rest):
        extra_refs = rest[:n_extra]
        out_refs = rest[n_extra:n_extra + n_out]
        if nk == 1:
            finish(dot(a_ref[...], b_ref[...]), extra_refs, out_refs)
            return
        acc_ref = rest[n_extra + n_out]
        k = pl.program_id(2)

        @pl.when(k == 0)
        def _():
            acc_ref[...] = jnp.zeros_like(acc_ref)

        acc_ref[...] += dot(a_ref[...], b_ref[...])

        @pl.when(k == nk - 1)
        def _():
            finish(acc_ref[...], extra_refs, out_refs)

    return body


def _mm_nn(a, w, out_dtypes, *, tm, tn, tk, name, epi=None, extras=()):
    m, r = a.shape
    p, _, c = w.shape
    npc = c // tn
    nk = r // tk
    body = _mm_body("nn", nk, len(extras), len(out_dtypes), epi)
    tile = pl.BlockSpec((tm, tn), lambda i, j, k: (i, j))
    return pl.pallas_call(
        body, name=name,
        grid=(m // tm, p * npc, nk),
        in_specs=[pl.BlockSpec((tm, tk), lambda i, j, k: (i, k)),
                  pl.BlockSpec((None, tk, tn), lambda i, j, k: (j // npc, k, j % npc))] + [tile] * len(extras),
        out_specs=[tile] * len(out_dtypes),
        out_shape=[jax.ShapeDtypeStruct((m, p * c), d) for d in out_dtypes],
        scratch_shapes=[pltpu.VMEM((tm, tn), F32)] if nk > 1 else [],
        compiler_params=_params(("parallel", "parallel", "arbitrary")),
    )(a, w, *extras)


def _mm_nt(a, w, out_dtypes, *, tm, tn, tk, name, epi=None, extras=()):
    m, _ = a.shape
    p, r, c = w.shape
    kpc = c // tk
    nk = p * kpc
    body = _mm_body("nt", nk, len(extras), len(out_dtypes), epi)
    tile = pl.BlockSpec((tm, tn), lambda i, j, k: (i, j))
    return pl.pallas_call(
        body, name=name,
        grid=(m // tm, r // tn, nk),
        in_specs=[pl.BlockSpec((tm, tk), lambda i, j, k: (i, k)),
                  pl.BlockSpec((None, tn, tk), lambda i, j, k: (k // kpc, j, k % kpc))] + [tile] * len(extras),
        out_specs=[tile] * len(out_dtypes),
        out_shape=[jax.ShapeDtypeStruct((m, r), d) for d in out_dtypes],
        scratch_shapes=[pltpu.VMEM((tm, tn), F32)] if nk > 1 else [],
        compiler_params=_params(("parallel", "parallel", "arbitrary")),
    )(a, w, *extras)


def _mm_tn(a, g, p, *, tm, tn, tk, name):
    t, r = a.shape
    c = g.shape[1] // p
    npc = c // tn
    nk = t // tk
    body = _mm_body("tn", nk, 0, 1, None)
    return pl.pallas_call(
        body, name=name,
        grid=(r // tm, p * npc, nk),
        in_specs=[pl.BlockSpec((tk, tm), lambda i, j, k: (k, i)),
                  pl.BlockSpec((tk, tn), lambda i, j, k: (k, j))],
        out_specs=[pl.BlockSpec((None, tm, tn), lambda i, j, k: (j // npc, i, j % npc))],
        out_shape=[jax.ShapeDtypeStruct((p, r, c), F32)],
        scratch_shapes=[pltpu.VMEM((tm, tn), F32)] if nk > 1 else [],
        compiler_params=_params(("parallel", "parallel", "arbitrary")),
    )(a, g)[0]


def _rows(fn, row_ins, const_ins, row_outs, acc_outs, *, tm, name):
    specs, arrays = [], []
    t = None
    for item in row_ins:
        if isinstance(item, tuple):
            arr, width, cb = item
            specs.append(pl.BlockSpec((tm, width), functools.partial(lambda i, cb: (i, cb), cb=cb)))
        else:
            arr = item
            specs.append(pl.BlockSpec((tm, arr.shape[1]), lambda i: (i, 0)))
        arrays.append(arr)
        t = arr.shape[0]
    for arr in const_ins:
        specs.append(pl.BlockSpec(arr.shape, lambda i: (0, 0)))
        arrays.append(arr)
    n_in, n_row, n_acc = len(arrays), len(row_outs), len(acc_outs)

    def body(*refs):
        ins = [r[...] for r in refs[:n_in]]
        outs = refs[n_in:]
        row_res, acc_res = fn(ins[:len(row_ins)], ins[len(row_ins):])
        for o, r in zip(outs[:n_row], row_res):
            o[...] = r.astype(o.dtype)
        if n_acc:
            i = pl.program_id(0)

            @pl.when(i == 0)
            def _():
                for o in outs[n_row:]:
                    o[...] = jnp.zeros_like(o)

            for o, r in zip(outs[n_row:], acc_res):
                o[...] += r

    res = pl.pallas_call(
        body, name=name,
        grid=(t // tm,),
        in_specs=specs,
        out_specs=[pl.BlockSpec((tm, c), lambda i: (i, 0)) for c, _ in row_outs]
                  + [pl.BlockSpec((1, c), lambda i: (0, 0)) for c in acc_outs],
        out_shape=[jax.ShapeDtypeStruct((t, c), d) for c, d in row_outs]
                  + [jax.ShapeDtypeStruct((1, c), F32) for c in acc_outs],
        compiler_params=_params(("arbitrary",)),
    )(*arrays)
    return res[:n_row], res[n_row:]


def _rstd(x):
    return lax.rsqrt(jnp.mean(x * x, axis=-1, keepdims=True) + NORM_EPS)


def _rms_bwd(x, w, dy):
    r = _rstd(x)
    n = x * r
    dn = dy * w
    dx = r * (dn - n * jnp.mean(dn * n, axis=-1, keepdims=True))
    return dx, dy * n


def _colsum(x):
    return jnp.sum(x, axis=0, keepdims=True)


def _heads_map(fn, width, *tiles):
    outs = None
    for h in range(width // HEAD):
        res = fn(*[t[:, h * HEAD:(h + 1) * HEAD] for t in tiles])
        if outs is None:
            outs = [[] for _ in res]
        for lst, r in zip(outs, res):
            lst.append(r)
    return [jnp.concatenate(lst, axis=1) for lst in outs]


def _log_one_minus_beta(z):
    return -(jnp.maximum(z, 0.0) + jnp.log(1.0 + jnp.exp(-jnp.abs(z))))


def _attn_fwd(proj, after_tri, norm_w, n_heads, name):
    t = proj.shape[0]
    blk = min(ATTN_BLOCK, t)
    ns = min(ATTN_SUBS, t // blk)
    scale = HEAD ** -0.5

    def body(q_ref, k_ref, v_ref, tri_ref, w_ref, o_ref, mix_ref, tot_ref):
        i = pl.program_id(1)
        qs = [q_ref[a * blk:(a + 1) * blk, :].astype(BF16) for a in range(ns)]
        tri = tri_ref[...]
        diag = lax.broadcasted_iota(jnp.int32, (blk, blk), 1) < lax.broadcasted_iota(jnp.int32, (blk, blk), 0)

        def load(j):
            sl = pl.ds(pl.multiple_of(j * blk, blk), blk)
            return k_ref[sl, :].astype(BF16), v_ref[sl, :].astype(BF16)

        def sub(a, kb, vb, acc_l, acc_o, mask):
            z = _dot_nt(qs[a], kb) * scale
            lm = _log_one_minus_beta(z)
            lmm = lm if mask is None else jnp.where(mask, lm, 0.0)
            w = jnp.exp(z + lm + acc_l + _split_dot(lmm, tri))
            if mask is not None:
                w = jnp.where(mask, w, 0.0)
            return acc_l + jnp.sum(lmm, axis=1, keepdims=True), acc_o + _dot(w.astype(BF16), vb)

        carry = [(jnp.zeros((blk, 1), F32), jnp.zeros((blk, HEAD), F32))] * ns
        for jr in reversed(range(ns)):
            kb, vb = load(ns * i + jr)
            carry = [sub(a, kb, vb, *carry[a], None if a > jr else diag) if a >= jr else carry[a] for a in range(ns)]

        def step(jj, c):
            kb, vb = load(ns * i - 1 - jj)
            return tuple(sub(a, kb, vb, *c[a], None) for a in range(ns))

        carry = lax.fori_loop(0, ns * i, step, tuple(carry))
        for a in range(ns):
            acc_l, acc_o = carry[a]
            r = slice(a * blk, (a + 1) * blk)
            o_ref[r, :] = acc_o
            mix_ref[r, :] = (acc_o * _rstd(acc_o) * w_ref[...]).astype(BF16)
            tot_ref[r, :] = jnp.broadcast_to(acc_l, (blk, HEAD))

    width = n_heads * HEAD
    qblk = pl.BlockSpec((ns * blk, HEAD), lambda h, i: (i, h))
    return pl.pallas_call(
        body, name=name,
        grid=(n_heads, t // (ns * blk)),
        in_specs=[qblk,
                  pl.BlockSpec((t, HEAD), lambda h, i: (0, n_heads + h)),
                  pl.BlockSpec((t, HEAD), lambda h, i: (0, 2 * n_heads + h)),
                  pl.BlockSpec((blk, blk), lambda h, i: (0, 0)),
                  pl.BlockSpec((1, HEAD), lambda h, i: (0, 0))],
        out_specs=[qblk, qblk, qblk],
        out_shape=[jax.ShapeDtypeStruct((t, width), F32), jax.ShapeDtypeStruct((t, width), BF16),
                   jax.ShapeDtypeStruct((t, width), F32)],
        compiler_params=_params(("parallel", "arbitrary")),
    )(proj, proj, proj, after_tri, norm_w)


def _attn_bwd(proj, tot, do, upto_tri, before_tri, n_heads, name):
    t = proj.shape[0]
    blk = min(ATTN_BLOCK, t)
    ns = min(ATTN_SUBS, t // blk)
    scale = HEAD ** -0.5

    def body(q_ref, k_ref, v_ref, tot_ref, do_ref, upto_ref, before_ref, dq_ref, dk_ref, dv_ref):
        i = pl.program_id(1)

        @pl.when(i == 0)
        def _():
            dk_ref[...] = jnp.zeros_like(dk_ref)
            dv_ref[...] = jnp.zeros_like(dv_ref)

        qs = [q_ref[a * blk:(a + 1) * blk, :].astype(BF16) for a in range(ns)]
        dos = [do_ref[a * blk:(a + 1) * blk, :].astype(BF16) for a in range(ns)]
        totals = [tot_ref[a * blk:(a + 1) * blk, 0:1] for a in range(ns)]
        upto = upto_ref[...]
        before = before_ref[...]
        diag = lax.broadcasted_iota(jnp.int32, (blk, blk), 1) < lax.broadcasted_iota(jnp.int32, (blk, blk), 0)

        def sub(a, kb, vb, seen_l, seen_g, dq, mask):
            z = _dot_nt(qs[a], kb) * scale
            lm = _log_one_minus_beta(z)
            lmm = lm if mask is None else jnp.where(mask, lm, 0.0)
            after = (totals[a] - seen_l) - _split_dot(lmm, upto)
            w = jnp.exp(z + lm + after)
            if mask is not None:
                w = jnp.where(mask, w, 0.0)
            sig = jnp.exp(z + lm)
            g = w * _dot_nt(dos[a], vb)
            g_before = seen_g + _split_dot(g, before)
            dz = (g * (1.0 - sig) - g_before * sig) * scale
            if mask is not None:
                dz = jnp.where(mask, dz, 0.0)
            dzb = dz.astype(BF16)
            carry = (seen_l + jnp.sum(lmm, axis=1, keepdims=True), seen_g + jnp.sum(g, axis=1, keepdims=True),
                     dq + _dot(dzb, kb))
            return carry, _dot_tn(dzb, qs[a]), _dot_tn(w.astype(BF16), dos[a])

        def key_block(j, carry, masks):
            sl = pl.ds(pl.multiple_of(j * blk, blk), blk)
            kb, vb = k_ref[sl, :].astype(BF16), v_ref[sl, :].astype(BF16)
            res = [sub(a, kb, vb, *carry[a], masks[a]) for a in range(ns) if masks[a] is not False]
            dk_ref[sl, :] += sum(r[1] for r in res)
            dv_ref[sl, :] += sum(r[2] for r in res)
            done = iter(res)
            return tuple(next(done)[0] if masks[a] is not False else carry[a] for a in range(ns))

        zero = jnp.zeros((blk, 1), F32)
        carry = lax.fori_loop(0, ns * i, lambda j, c: key_block(j, c, [None] * ns),
                              ((zero, zero, jnp.zeros((blk, HEAD), F32)),) * ns)
        for jr in range(ns):
            carry = key_block(ns * i + jr, carry, [None if a > jr else diag if a == jr else False for a in range(ns)])
        for a in range(ns):
            dq_ref[a * blk:(a + 1) * blk, :] = carry[a][2]

    width = n_heads * HEAD
    qblk = pl.BlockSpec((ns * blk, HEAD), lambda h, i: (i, h))
    full = pl.BlockSpec((t, HEAD), lambda h, i: (0, h))
    tri = pl.BlockSpec((blk, blk), lambda h, i: (0, 0))
    return pl.pallas_call(
        body, name=name,
        grid=(n_heads, t // (ns * blk)),
        in_specs=[qblk,
                  pl.BlockSpec((t, HEAD), lambda h, i: (0, n_heads + h)),
                  pl.BlockSpec((t, HEAD), lambda h, i: (0, 2 * n_heads + h)),
                  qblk, qblk, tri, tri],
        out_specs=[qblk, full, full],
        out_shape=[jax.ShapeDtypeStruct((t, width), F32)] * 3,
        compiler_params=_params(("parallel", "arbitrary")),
    )(proj, proj, proj, tot, do, upto_tri, before_tri)


def _lower_bound(logits):
    l0, l1 = logits[0:1, :], logits[1:2, :]
    mx = jnp.maximum(l0, l1)
    e0, e1 = jnp.exp(l0 - mx), jnp.exp(l1 - mx)
    return e0 / (e0 + e1)


def _hg_chunk(qc, kc, gc, tri_lo):
    c = qc.shape[0]
    cum = _dot_split(tri_lo, gc)
    mid = cum[c // 2 - 1:c // 2, :]
    last = cum[c - 1:c, :]
    qt = qc * jnp.exp(cum - mid)
    kt = kc * jnp.exp(mid - cum)
    qe = qc * jnp.exp(cum)
    kd = kc * jnp.exp(last - cum)
    return cum, mid, last, qt, kt, qe, kd


def _hgrn_fwd(proj, lb_logits, norm_w, tri_lo, n_heads, heads_per_step, name):
    t = proj.shape[0]
    bt = min(HG_ROWS, t)
    c = HG_CHUNK
    nc = bt // c
    hw = heads_per_step * HEAD
    width = n_heads * HEAD
    col0 = 3 * width // hw

    def body(hq_ref, hf_ref, hi_ref, hgate_ref, lbl_ref, w_ref, tri_ref, o_ref, mix_ref, st_ref,
             state, q_scr, k_scr, g_scr):
        @pl.when(pl.program_id(1) == 0)
        def _():
            state[...] = jnp.zeros_like(state)

        lb = _lower_bound(lbl_ref[...])
        f = hf_ref[...]
        g_scr[...] = jnp.log(lb + (1.0 - lb) * _sigmoid(f))
        k_scr[...] = (1.0 - lb) * _sigmoid(-f)
        hq = hq_ref[...]
        q_scr[...] = hq * _sigmoid(hq)
        tri = tri_ref[...]
        causal = lax.broadcasted_iota(jnp.int32, (c, c), 1) <= lax.broadcasted_iota(jnp.int32, (c, c), 0)

        def chunk(ci, carry):
            r = pl.ds(pl.multiple_of(ci * c, c), c)
            vc = hi_ref[r, :].astype(BF16)
            _, _, last, qt, kt, qe, kd = _hg_chunk(q_scr[r, :], k_scr[r, :], g_scr[r, :], tri)
            qt, kt, qe, kd = qt.astype(BF16), kt.astype(BF16), qe.astype(BF16), kd.astype(BF16)
            e_last = jnp.exp(last)
            old = [state[h] for h in range(heads_per_step)]
            outs, new = [], []
            for h in range(heads_per_step):
                cs = slice(h * HEAD, (h + 1) * HEAD)
                a = jnp.where(causal, _dot_nt(qt[:, cs], kt[:, cs]), 0.0)
                outs.append(_dot(a.astype(BF16), vc[:, cs]) + _dot_nt(qe[:, cs], old[h].astype(BF16)))
                new.append(old[h] * e_last[:, cs] + _dot_tn(vc[:, cs], kd[:, cs]))
            for h in range(heads_per_step):
                st_ref[ci, :, h * HEAD:(h + 1) * HEAD] = old[h]
                state[h] = new[h]
            o_ref[r, :] = jnp.concatenate(outs, axis=1)
            return carry

        lax.fori_loop(0, nc, chunk, 0)

        def finish(o, gate):
            return ((o * _rstd(o) * w_ref[...]) * (gate * _sigmoid(gate)),)

        mix_ref[...] = _heads_map(finish, hw, o_ref[...], hgate_ref[...])[0].astype(BF16)

    def col(group):
        return pl.BlockSpec((bt, hw), functools.partial(lambda hp, tb, g: (tb, col0 + g * (width // hw) + hp), g=group))

    blk = pl.BlockSpec((bt, hw), lambda hp, tb: (tb, hp))
    return pl.pallas_call(
        body, name=name,
        grid=(n_heads // heads_per_step, t // bt),
        in_specs=[col(0), col(1), col(2), col(3),
                  pl.BlockSpec((2, hw), lambda hp, tb: (0, hp)),
                  pl.BlockSpec((1, HEAD), lambda hp, tb: (0, 0)),
                  pl.BlockSpec((c, c), lambda hp, tb: (0, 0))],
        out_specs=[blk, blk, pl.BlockSpec((nc, HEAD, hw), lambda hp, tb: (tb, 0, hp))],
        out_shape=[jax.ShapeDtypeStruct((t, width), F32), jax.ShapeDtypeStruct((t, width), BF16),
                   jax.ShapeDtypeStruct((t // c, HEAD, width), F32)],
        scratch_shapes=[pltpu.VMEM((heads_per_step, HEAD, HEAD), F32)] + [pltpu.VMEM((bt, hw), F32)] * 3,
        compiler_params=_params(("parallel", "arbitrary")),
    )(proj, proj, proj, proj, lb_logits, norm_w, tri_lo)


def _hgrn_bwd(proj, do, states, lb_logits, tri_lo, tri_up, n_heads, heads_per_step, name):
    t = proj.shape[0]
    bt = min(HG_ROWS, t)
    c = HG_CHUNK
    nc = bt // c
    nb = t // bt
    hw = heads_per_step * HEAD
    width = n_heads * HEAD
    col0 = 3 * width // hw

    def body(hq_ref, hf_ref, hi_ref, do_ref, st_ref, lbl_ref, lo_ref, up_ref, dq_ref, df_ref, di_ref, dlb_ref,
             dstate, q_scr, k_scr, g_scr, dk_scr, dg_scr):
        @pl.when(pl.program_id(1) == 0)
        def _():
            dstate[...] = jnp.zeros_like(dstate)
            dlb_ref[...] = jnp.zeros_like(dlb_ref)

        lb = _lower_bound(lbl_ref[...])
        f = hf_ref[...]
        sg = _sigmoid(f)
        sgn = _sigmoid(-f)
        den = lb + (1.0 - lb) * sg
        g_scr[...] = jnp.log(den)
        k_scr[...] = (1.0 - lb) * sgn
        hq = hq_ref[...]
        sq = _sigmoid(hq)
        q_scr[...] = hq * sq
        tri_lo_v = lo_ref[...]
        tri_up_v = up_ref[...]
        causal = lax.broadcasted_iota(jnp.int32, (c, c), 1) <= lax.broadcasted_iota(jnp.int32, (c, c), 0)
        last_row = lax.broadcasted_iota(jnp.int32, (c, hw), 0) == c - 1

        def chunk(cc, carry):
            ci = nc - 1 - cc
            r = pl.ds(pl.multiple_of(ci * c, c), c)
            qc, kc = q_scr[r, :], k_scr[r, :]
            cum, mid, last, qt, kt, qe, kd = _hg_chunk(qc, kc, g_scr[r, :], tri_lo_v)
            qt, kt, qe, kd, doc, vc = [_hilo(v) for v in (qt, kt, qe, kd, do_ref[r, :], hi_ref[r, :])]
            e_last = jnp.exp(last)
            sts = [st_ref[ci, :, h * HEAD:(h + 1) * HEAD] for h in range(heads_per_step)]
            dsts = [dstate[h] for h in range(heads_per_step)]
            di, dq_inter, dk_inter, dq_intra, dk_intra, st_sums, new = [], [], [], [], [], [], []
            for h in range(heads_per_step):
                cs = slice(h * HEAD, (h + 1) * HEAD)

                def head(pair):
                    return pair[0][:, cs], pair[1][:, cs]

                st, dst = _hilo(sts[h]), _hilo(dsts[h])
                a = _hilo(jnp.where(causal, _dot3(_dot_nt, head(qt), head(kt)), 0.0))
                da = _hilo(jnp.where(causal, _dot3(_dot_nt, head(doc), head(vc)), 0.0))
                di.append(_dot3(_dot_tn, a, head(doc)) + _dot3(_dot_nt, head(kd), dst))
                dq_inter.append(_dot3(_dot, head(doc), st))
                dk_inter.append(_dot3(_dot, head(vc), dst))
                dq_intra.append(_dot3(_dot, da, head(kt)))
                dk_intra.append(_dot3(_dot_tn, da, head(qt)))
                st_sums.append(_colsum(dsts[h] * sts[h]))
                new.append(dsts[h] * e_last[:, cs] + _dot3(_dot_tn, head(doc), head(qe)))

            def wide(parts):
                return jnp.concatenate(parts, axis=1)

            dq_inter = wide(dq_inter) * jnp.exp(cum)
            dk_inter = wide(dk_inter) * jnp.exp(last - cum)
            dq = wide(dq_intra) * jnp.exp(cum - mid) + dq_inter
            dk = wide(dk_intra) * jnp.exp(mid - cum) + dk_inter
            d_last = _colsum(kc * dk_inter) + e_last * wide(st_sums)
            dcum = qc * dq - kc * dk + jnp.where(last_row, d_last, 0.0)
            for h in range(heads_per_step):
                dstate[h] = new[h]
            di_ref[r, :] = wide(di)
            dq_ref[r, :] = dq
            dk_scr[r, :] = dk
            dg_scr[r, :] = _dot_split(tri_up_v, dcum)
            return carry

        lax.fori_loop(0, nc, chunk, 0)

        e = (dg_scr[...] / den - dk_scr[...]) * sgn
        df_ref[...] = e * (1.0 - lb) * sg
        dlb_ref[...] += _colsum(e)
        dq_ref[...] = dq_ref[...] * (sq * (1.0 + hq * (1.0 - sq)))

    def col(group):
        return pl.BlockSpec((bt, hw), functools.partial(
            lambda hp, tb, g: (nb - 1 - tb, col0 + g * (width // hw) + hp), g=group))

    blk = pl.BlockSpec((bt, hw), lambda hp, tb: (nb - 1 - tb, hp))
    tri = pl.BlockSpec((c, c), lambda hp, tb: (0, 0))
    return pl.pallas_call(
        body, name=name,
        grid=(n_heads // heads_per_step, nb),
        in_specs=[col(0), col(1), col(2), blk,
                  pl.BlockSpec((nc, HEAD, hw), lambda hp, tb: (nb - 1 - tb, 0, hp)),
                  pl.BlockSpec((2, hw), lambda hp, tb: (0, hp)), tri, tri],
        out_specs=[blk, blk, blk, pl.BlockSpec((1, hw), lambda hp, tb: (0, hp))],
        out_shape=[jax.ShapeDtypeStruct((t, width), F32)] * 3 + [jax.ShapeDtypeStruct((1, width), F32)],
        scratch_shapes=[pltpu.VMEM((heads_per_step, HEAD, HEAD), F32)] + [pltpu.VMEM((bt, hw), F32)] * 5,
        compiler_params=_params(("parallel", "arbitrary")),
    )(proj, proj, proj, do, states, lb_logits, tri_lo, tri_up)


def _place():
    x, y, c = lax.axis_index("x"), lax.axis_index("y"), lax.axis_index("c")
    chips = [(1 - x, y), (x, 1 - y), (1 - x, 1 - y)]
    return x, y, c, chips


ANY = pl.BlockSpec(memory_space=pl.ANY)


def _cast_to_slot(shard, ids, name):
    r, c = shard.shape
    tm = _pick(r, (256, 128, 64, 32, 16))

    def body(ids_ref, s_ref, o_ref):
        o_ref[...] = s_ref[...].astype(BF16)

    return pl.pallas_call(
        body, name=name,
        grid_spec=pltpu.PrefetchScalarGridSpec(
            num_scalar_prefetch=1, grid=(r // tm,),
            in_specs=[pl.BlockSpec((tm, c), lambda i, ids: (i, 0))],
            out_specs=pl.BlockSpec((None, tm, c), lambda i, ids: (ids[1], i, 0))),
        out_shape=jax.ShapeDtypeStruct((N_CHIPS, r, c), BF16),
        compiler_params=_params(("parallel",)),
    )(ids, shard)


def _gather_weights(slots, name):
    n = len(slots)

    def body(*refs):
        bufs = refs[n:2 * n]
        send, recv = refs[2 * n:]
        x, y, c, chips = _place()
        mine = 2 * x + y

        def half(ref, who, core):
            h = ref.shape[-2] // 2
            return ref.at[who, pl.ds(core * h, h), :]

        def copy(w, k, rows, to):
            return pltpu.make_async_remote_copy(src_ref=rows, dst_ref=rows, send_sem=send.at[6 * w + k],
                                                recv_sem=recv.at[6 * w + k], device_id=to, device_id_type=MESH)

        sent = []
        for w in range(n):
            for k, (qx, qy) in enumerate(chips):
                sent.append(copy(w, k, half(bufs[w], mine, c), (qx, qy, c)))
                sent[-1].start()
        for w in range(n):
            for k, (qx, qy) in enumerate(chips):
                landed = half(bufs[w], 2 * qx + qy, c)
                copy(w, k, landed, (x, y, c)).wait_recv()
                sent.append(copy(w, 3 + k, landed, (x, y, 1 - c)))
                sent[-1].start()
        for w in range(n):
            for k, (qx, qy) in enumerate(chips):
                copy(w, 3 + k, half(bufs[w], 2 * qx + qy, 1 - c), (x, y, c)).wait_recv()
        for cp in sent:
            cp.wait_send()

    return pl.pallas_call(
        body, name=name,
        in_specs=[ANY] * n, out_specs=[ANY] * n,
        out_shape=[jax.ShapeDtypeStruct(s.shape, s.dtype) for s in slots],
        input_output_aliases={w: w for w in range(n)},
        scratch_shapes=[pltpu.SemaphoreType.DMA((6 * n,)), pltpu.SemaphoreType.DMA((6 * n,))],
    )(*slots)


def _sibling_swap(grads, name):
    n = len(grads)

    def body(*refs):
        ins, outs = refs[:n], refs[n:2 * n]
        send, recv = refs[2 * n:]
        x, y, c, _ = _place()
        cps = []
        for w in range(n):
            h = ins[w].shape[1] // 2
            cps.append(pltpu.make_async_remote_copy(
                src_ref=ins[w].at[:, pl.ds((1 - c) * h, h), :], dst_ref=outs[w], send_sem=send.at[w],
                recv_sem=recv.at[w], device_id=(x, y, 1 - c), device_id_type=MESH))
            cps[-1].start()
        for cp in cps:
            cp.wait()

    return pl.pallas_call(
        body, name=name, in_specs=[ANY] * n, out_specs=[ANY] * n,
        out_shape=[jax.ShapeDtypeStruct((g.shape[0], g.shape[1] // 2, g.shape[2]), g.dtype) for g in grads],
        scratch_shapes=[pltpu.SemaphoreType.DMA((n,)), pltpu.SemaphoreType.DMA((n,))],
    )(*grads)


def _chip_scatter(sums, name):
    n = len(sums)

    def body(*refs):
        ins, outs = refs[:n], refs[n:2 * n]
        send, recv = refs[2 * n:]
        _, _, c, chips = _place()
        cps = []
        for w in range(n):
            for k, (qx, qy) in enumerate(chips):
                cps.append(pltpu.make_async_remote_copy(
                    src_ref=ins[w].at[2 * qx + qy], dst_ref=outs[w].at[k], send_sem=send.at[3 * w + k],
                    recv_sem=recv.at[3 * w + k], device_id=(qx, qy, c), device_id_type=MESH))
                cps[-1].start()
        for cp in cps:
            cp.wait()

    return pl.pallas_call(
        body, name=name, in_specs=[ANY] * n, out_specs=[ANY] * n,
        out_shape=[jax.ShapeDtypeStruct((3,) + s.shape[1:], s.dtype) for s in sums],
        scratch_shapes=[pltpu.SemaphoreType.DMA((3 * n,)), pltpu.SemaphoreType.DMA((3 * n,))],
    )(*sums)


def _sibling_join(fulls, name):
    n = len(fulls)

    def body(*refs):
        bufs = refs[n:2 * n]
        send, recv = refs[2 * n:]
        x, y, c, _ = _place()
        cps = []
        for w in range(n):
            h = bufs[w].shape[0] // 2
            rows = bufs[w].at[pl.ds(c * h, h), :]
            cps.append(pltpu.make_async_remote_copy(
                src_ref=rows, dst_ref=rows, send_sem=send.at[w], recv_sem=recv.at[w],
                device_id=(x, y, 1 - c), device_id_type=MESH))
            cps[-1].start()
        for w in range(n):
            h = bufs[w].shape[0] // 2
            theirs = bufs[w].at[pl.ds((1 - c) * h, h), :]
            pltpu.make_async_remote_copy(src_ref=theirs, dst_ref=theirs, send_sem=send.at[w], recv_sem=recv.at[w],
                                         device_id=(x, y, c), device_id_type=MESH).wait_recv()
        for cp in cps:
            cp.wait_send()

    return pl.pallas_call(
        body, name=name, in_specs=[ANY] * n, out_specs=[ANY] * n,
        out_shape=[jax.ShapeDtypeStruct(s.shape, s.dtype) for s in fulls],
        input_output_aliases={w: w for w in range(n)},
        scratch_shapes=[pltpu.SemaphoreType.DMA((n,)), pltpu.SemaphoreType.DMA((n,))],
    )(*fulls)


def _all_sum_small(vec, name):
    n = vec.shape[1]

    def body(v_ref, out_ref, buf, send, recv):
        x, y, c, _ = _place()
        me = 4 * x + 2 * y + c
        buf[me] = v_ref[...]
        peers = []
        for mask in range(1, 8):
            px = 1 - x if mask & 4 else x
            py = 1 - y if mask & 2 else y
            pc = 1 - c if mask & 1 else c
            peers.append((px, py, pc))
        cps = []
        for k, peer in enumerate(peers):
            cps.append(pltpu.make_async_remote_copy(src_ref=buf.at[me], dst_ref=buf.at[me], send_sem=send.at[k],
                                                    recv_sem=recv.at[k], device_id=peer, device_id_type=MESH))
            cps[-1].start()
        for k, (px, py, pc) in enumerate(peers):
            slot = buf.at[4 * px + 2 * py + pc]
            pltpu.make_async_remote_copy(src_ref=slot, dst_ref=slot, send_sem=send.at[k], recv_sem=recv.at[k],
                                         device_id=(x, y, c), device_id_type=MESH).wait_recv()
        for cp in cps:
            cp.wait_send()
        total = buf[0]
        for d in range(1, 8):
            total = total + buf[d]
        out_ref[...] = total

    vm = pl.BlockSpec(memory_space=pltpu.VMEM)
    return pl.pallas_call(
        body, name=name, in_specs=[vm], out_specs=vm,
        out_shape=jax.ShapeDtypeStruct(vec.shape, F32),
        scratch_shapes=[pltpu.VMEM((8, 8, n), F32), pltpu.SemaphoreType.DMA((7,)), pltpu.SemaphoreType.DMA((7,))],
    )(vec)


def _pair_sum(g, buf, ids, name):
    p, r, c = g.shape
    h = r // 2
    tr = _pick(h, (256, 128, 64, 32, 16))
    nh = h // tr

    def body(ids_ref, g_ref, b_ref, sums_ref, own_ref):
        s = g_ref[...] + b_ref[...]
        sums_ref[...] = s.astype(BF16)

        @pl.when(pl.program_id(1) == ids_ref[1])
        def _():
            own_ref[...] = s

    return pl.pallas_call(
        body, name=name,
        grid_spec=pltpu.PrefetchScalarGridSpec(
            num_scalar_prefetch=1, grid=(nh, p),
            in_specs=[pl.BlockSpec((None, tr, c), lambda i, q, ids: (q, ids[0] * nh + i, 0)),
                      pl.BlockSpec((None, tr, c), lambda i, q, ids: (q, i, 0))],
            out_specs=[pl.BlockSpec((None, tr, c), lambda i, q, ids: (q, i, 0)),
                       pl.BlockSpec((tr, c), lambda i, q, ids: (i, 0))]),
        out_shape=[jax.ShapeDtypeStruct((p, h, c), BF16), jax.ShapeDtypeStruct((h, c), F32)],
        compiler_params=_params(("parallel", "arbitrary")),
    )(ids, g, buf)


def _final_sum(own, others, ids, name):
    h, c = own.shape
    tr = _pick(h, (256, 128, 64, 32, 16))
    nh = h // tr

    def body(ids_ref, own_ref, oth_ref, out_ref):
        s = own_ref[...]
        for k in range(3):
            s = s + oth_ref[k].astype(F32)
        out_ref[...] = s

    return pl.pallas_call(
        body, name=name,
        grid_spec=pltpu.PrefetchScalarGridSpec(
            num_scalar_prefetch=1, grid=(nh,),
            in_specs=[pl.BlockSpec((tr, c), lambda i, ids: (i, 0)),
                      pl.BlockSpec((3, tr, c), lambda i, ids: (0, i, 0))],
            out_specs=pl.BlockSpec((tr, c), lambda i, ids: (ids[0] * nh + i, 0))),
        out_shape=jax.ShapeDtypeStruct((2 * h, c), F32),
        compiler_params=_params(("parallel",)),
    )(ids, own, others)


def _adamw(w, g, m, v, name):
    r, c = w.shape
    tm = _pick(r, (256, 128, 64, 32, 16, 8)) if r >= 8 else r

    def fn(rows, _):
        w_, g_, m_, v_ = rows
        m2 = ADAM_B1 * m_ + (1.0 - ADAM_B1) * g_
        v2 = ADAM_B2 * v_ + (1.0 - ADAM_B2) * (g_ * g_)
        m_hat = m2 / (1.0 - ADAM_B1 ** ADAM_STEP)
        v_hat = v2 / (1.0 - ADAM_B2 ** ADAM_STEP)
        delta = -ADAM_LR * (m_hat / (jnp.sqrt(v_hat) + ADAM_EPS) + ADAM_WD * w_)
        return [delta, m2, v2], []

    outs, _ = _rows(fn, [w, g, m, v], [], [(c, F32)] * 3, [], tm=tm, name=name)
    return outs


def kernel(x, attn_norm_w, w_in, lb_logits, sb_norm_w, hg_norm_w, w_out, mlp_norm_w, w_up, w_down, final_norm_w, loss_target, m_attn_norm_w, m_w_in, m_lb_logits, m_sb_norm_w, m_hg_norm_w, m_w_out, m_mlp_norm_w, m_w_up, m_w_down, m_final_norm_w, v_attn_norm_w, v_w_in, v_lb_logits, v_sb_norm_w, v_hg_norm_w, v_w_out, v_mlp_norm_w, v_w_up, v_w_down, v_final_norm_w):
    xs, tgt = x[0], loss_target[0]
    t, d = xs.shape
    width = d // 2
    n_heads = width // HEAD
    hps = min(8, n_heads)
    final_w = final_norm_w.reshape(1, d)
    tm_rows = _pick(t, (256, 128))
    tm = _pick(t, (1024, 512, 256))
    blk = min(ATTN_BLOCK, t)
    ones_a = jnp.ones((blk, blk), F32)
    after_tri = jnp.tril(ones_a, -1).astype(BF16)
    upto_tri = jnp.triu(ones_a).astype(BF16)
    before_tri = jnp.triu(ones_a, 1).astype(BF16)
    ones_c = jnp.ones((HG_CHUNK, HG_CHUNK), F32)
    tri_lo, tri_up = jnp.tril(ones_c).astype(BF16), jnp.triu(ones_c).astype(BF16)
    cx, cy, cc = lax.axis_index("x"), lax.axis_index("y"), lax.axis_index("c")
    ids = jnp.stack([cc, 2 * cx + cy]).astype(jnp.int32)

    shards = [w_in[0], w_out[0], w_up[0], w_down[0]]
    cast = [_cast_to_slot(s, ids, f"cast_w{i}") for i, s in enumerate(shards)]
    g_in, g_out, g_up, g_down = _gather_weights(cast, "gather_weights")
    d_ff = N_CHIPS * w_up.shape[2]
    w_out_all = g_out.reshape(1, d, d)
    w_down_all = g_down.reshape(1, d_ff, d)
    cs_in, cs_up = g_in.shape[2], g_up.shape[2]
    tn_in = _pick(cs_in, (1792, 896, 512, 256, 128))
    tn_up = _pick(cs_up, (1024, 512, 256))
    tn_d = _pick(d, (1024, 512, 256))
    tk_d = _pick(d, (2048, 1024, 512))

    (u,), _ = _rows(lambda r, c_: ([r[0] * _rstd(r[0]) * c_[0]], []), [xs], [attn_norm_w], [(d, BF16)], [],
                    tm=tm_rows, name="norm_in")
    (proj,) = _mm_nn(u, g_in, [F32], tm=_pick(t, (512, 256)), tn=tn_in, tk=tk_d, name="proj_in")
    o_a, mix_a, sb_tot = _attn_fwd(proj, after_tri, sb_norm_w, n_heads, "sb_fwd")
    o_b, mix_b, states = _hgrn_fwd(proj, lb_logits, hg_norm_w, tri_lo, n_heads, hps, "hg_fwd")
    mix = jnp.concatenate([mix_a, mix_b], axis=1)
    (h1,) = _mm_nn(mix, w_out_all, [F32], tm=tm, tn=tn_d, tk=tk_d, name="proj_out",
                   epi=lambda acc, res: (acc + res,), extras=(xs,))
    (mn,), _ = _rows(lambda r, c_: ([r[0] * _rstd(r[0]) * c_[0]], []), [h1], [mlp_norm_w], [(d, BF16)], [],
                     tm=tm_rows, name="norm_mlp")
    up_b, act = _mm_nn(mn, g_up, [BF16, BF16], tm=tm, tn=tn_up, tk=tk_d, name="mlp_up",
                       epi=lambda acc: (acc, jnp.square(jnp.maximum(acc, 0.0))))
    (h2,) = _mm_nn(act, w_down_all, [F32], tm=tm, tn=tn_d, tk=_pick(d_ff, (2048, 1024)), name="mlp_down",
                   epi=lambda acc, res: (acc + res,), extras=(h1,))

    def head(rows, consts):
        hh, tg = rows
        w = consts[0]
        n = hh * _rstd(hh)
        err = n * w - tg
        dhh, dw_rows = _rms_bwd(hh, w, err * (1.0 / d))
        return [dhh, dhh], [_colsum(dw_rows), _colsum(err * err)]

    (dh2, dh2_b), (g_final, loss_cols) = _rows(head, [h2, tgt], [final_w], [(d, F32), (d, BF16)], [d, d],
                                                 tm=tm_rows, name="loss_head")

    (dup,) = _mm_nt(dh2_b, w_down_all, [BF16], tm=tm, tn=_pick(d_ff, (1024, 512)), tk=tk_d, name="mlp_down_dx",
                    epi=lambda acc, upv: (acc * (2.0 * jnp.maximum(upv.astype(F32), 0.0)),), extras=(up_b,))
    gw_down = _mm_tn(act, dh2_b, 1, tm=_pick(d_ff, (1024, 512)), tn=tn_d, tk=_pick(t, (1024, 512, 256)),
                     name="mlp_down_dw")
    (dmn,) = _mm_nt(dup, g_up, [F32], tm=tm, tn=tn_d, tk=_pick(cs_up, (2048, 1024, 512)), name="mlp_up_dx")
    gw_up = _mm_tn(mn, dup, N_CHIPS, tm=tn_d, tn=tn_up, tk=_pick(t, (1024, 512, 256)), name="mlp_up_dw")

    def norm_back(rows, consts):
        xx, dy, skip = rows
        dx, dw_rows = _rms_bwd(xx, consts[0], dy)
        tot = dx + skip
        return [tot, tot], [_colsum(dw_rows)]

    (dh1, dh1_b), (g_mlp_norm,) = _rows(norm_back, [h1, dmn, dh2], [mlp_norm_w], [(d, F32), (d, BF16)], [d],
                                         tm=tm_rows, name="norm_mlp_bwd")

    (dmix,) = _mm_nt(dh1_b, w_out_all, [F32], tm=tm, tn=tn_d, tk=tk_d, name="proj_out_dx")
    gw_out = _mm_tn(mix, dh1_b, 1, tm=tn_d, tn=tn_d, tk=_pick(t, (1024, 512, 256)), name="proj_out_dw")

    def sb_norm_back(rows, consts):
        dx, dw_rows = _heads_map(lambda o, dy: _rms_bwd(o, consts[0], dy), width, *rows)
        dw = sum(_colsum(dw_rows[:, h * HEAD:(h + 1) * HEAD]) for h in range(n_heads))
        return [dx], [dw]

    (do_a,), (g_sb_norm,) = _rows(sb_norm_back, [o_a, (dmix, width, 0)], [sb_norm_w], [(width, F32)], [HEAD],
                                  tm=tm_rows, name="sb_norm_bwd")
    dq_a, dk_a, dv_a = _attn_bwd(proj, sb_tot, do_a, upto_tri, before_tri, n_heads, "sb_bwd")

    def hg_out_back(rows, consts):
        def one(o, gate, dy):
            sg = _sigmoid(gate)
            silu = gate * sg
            n = o * _rstd(o) * consts[0]
            do, dw_rows = _rms_bwd(o, consts[0], dy * silu)
            return do, dy * n * (sg * (1.0 + gate * (1.0 - sg))), dw_rows
        do, dgate, dw_rows = _heads_map(one, width, *rows)
        dw = sum(_colsum(dw_rows[:, h * HEAD:(h + 1) * HEAD]) for h in range(n_heads))
        return [do, dgate], [dw]

    (do_b, dgate), (g_hg_norm,) = _rows(hg_out_back, [o_b, (proj, width, 6), (dmix, width, 1)], [hg_norm_w],
                                         [(width, F32)] * 2, [HEAD], tm=tm_rows, name="hg_out_bwd")
    dhq, dhf, dhi, dlb = _hgrn_bwd(proj, do_b, states, lb_logits, tri_lo, tri_up, n_heads, hps, "hg_bwd")

    (dproj,), _ = _rows(lambda r, _c: ([jnp.concatenate([p.astype(BF16) for p in r], axis=1)], []),
                        [dq_a, dk_a, dv_a, dhq, dhf, dhi, dgate], [], [(7 * width, BF16)], [],
                        tm=tm_rows, name="pack_dproj")
    (du,) = _mm_nt(dproj, g_in, [F32], tm=tm, tn=tn_d, tk=_pick(cs_in, (1792, 896, 512, 256, 128)), name="proj_in_dx")
    gw_in = _mm_tn(u, dproj, N_CHIPS, tm=tn_d, tn=tn_in, tk=_pick(t, (1024, 512, 256)), name="proj_in_dw")
    (dx,), (g_attn_norm,) = _rows(lambda r, c_: (lambda dxx, dwr: ([dxx + r[2]], [_colsum(dwr)]))(
        *_rms_bwd(r[0], c_[0], r[1])), [xs, du, dh1], [attn_norm_w], [(d, F32)], [d], tm=tm_rows, name="norm_in_bwd")

    grads = [gw_in, gw_out.reshape(N_CHIPS, d // N_CHIPS, d), gw_up, gw_down.reshape(N_CHIPS, d_ff // N_CHIPS, d)]
    theirs = _sibling_swap(grads, "grads_to_sibling")
    pair = [_pair_sum(g, b, ids, f"grads_pair_sum{i}") for i, (g, b) in enumerate(zip(grads, theirs))]
    landed = _chip_scatter([p[0] for p in pair], "grads_to_chips")
    halves = [_final_sum(p[1], r, ids, f"grads_final_sum{i}") for i, (p, r) in enumerate(zip(pair, landed))]
    g_w_in, g_w_out, g_w_up, g_w_down = _sibling_join(halves, "grads_join")

    pieces = [g_attn_norm, g_mlp_norm, g_final, g_sb_norm, g_hg_norm, dlb, loss_cols]
    sizes = [p.shape[1] for p in pieces]
    flat = jnp.concatenate(pieces, axis=1)
    n_small = -(-flat.shape[1] // 1024) * 1024
    flat = jnp.pad(flat, ((0, 0), (0, n_small - flat.shape[1]))).reshape(8, n_small // 8)
    flat = _all_sum_small(flat, "small_all_sum").reshape(1, n_small)
    offs = [sum(sizes[:i]) for i in range(len(sizes))]
    g_attn_norm, g_mlp_norm, g_final, g_sb_norm, g_hg_norm, dlb, loss_cols = [
        flat[:, o:o + s] for o, s in zip(offs, sizes)]

    def small_tail(lbl_ref, dlb_ref, loss_ref, glb_ref, out_ref):
        lb = _lower_bound(lbl_ref[...])
        g0 = dlb_ref[...] * lb * (1.0 - lb)
        glb_ref[0:1, :] = g0
        glb_ref[1:2, :] = -g0
        out_ref[...] = jnp.zeros_like(out_ref) + 0.5 * jnp.sum(loss_ref[...]) * (1.0 / d)

    vm = pl.BlockSpec(memory_space=pltpu.VMEM)
    g_lb, loss11 = pl.pallas_call(
        small_tail, name="small_tail", in_specs=[vm, vm, vm], out_specs=[vm, vm],
        out_shape=[jax.ShapeDtypeStruct(lb_logits.shape, F32), jax.ShapeDtypeStruct((1, 128), F32)],
    )(lb_logits, dlb, loss_cols)
    loss = loss11[0, 0]

    names = ["attn_norm_w", "w_in", "lb_logits", "sb_norm_w", "hg_norm_w", "w_out", "mlp_norm_w", "w_up", "w_down",
             "final_norm_w"]
    ws = [attn_norm_w, w_in[0], lb_logits, sb_norm_w, hg_norm_w, w_out[0], mlp_norm_w, w_up[0], w_down[0], final_w]
    gs = [g_attn_norm, g_w_in, g_lb, g_sb_norm, g_hg_norm, g_w_out, g_mlp_norm, g_w_up, g_w_down, g_final]
    ms = [m_attn_norm_w, m_w_in[0], m_lb_logits, m_sb_norm_w, m_hg_norm_w, m_w_out[0], m_mlp_norm_w, m_w_up[0],
          m_w_down[0], m_final_norm_w.reshape(1, d)]
    vs = [v_attn_norm_w, v_w_in[0], v_lb_logits, v_sb_norm_w, v_hg_norm_w, v_w_out[0], v_mlp_norm_w, v_w_up[0],
          v_w_down[0], v_final_norm_w.reshape(1, d)]
    shapes = [attn_norm_w.shape, w_in.shape, lb_logits.shape, sb_norm_w.shape, hg_norm_w.shape, w_out.shape,
              mlp_norm_w.shape, w_up.shape, w_down.shape, final_norm_w.shape]
    deltas, new_ms, new_vs = [], [], []
    for nm, w_, g_, m_, v_ in zip(names, ws, gs, ms, vs):
        dl, m2, v2 = _adamw(w_, g_, m_, v_, "adamw_" + nm)
        deltas.append(dl)
        new_ms.append(m2)
        new_vs.append(v2)

    def shaped(lst):
        return [a.reshape(s) for a, s in zip(lst, shapes)]

    return (loss, dx[None], *shaped(gs), *shaped(deltas), *shaped(new_ms), *shaped(new_vs))
```

```python
import functools

import jax
import jax.numpy as jnp
from jax import lax
from jax.experimental import pallas as pl
from jax.experimental.pallas import tpu as pltpu

F32 = jnp.float32
BF16 = jnp.bfloat16
MESH = pl.DeviceIdType.MESH

HEAD = 128
NORM_EPS = 1e-5
N_CHIPS = 4
ATTN_BLOCK = 256
ATTN_ROWS = 1024
HG_CHUNK = 32
HG_ROWS = 256
VMEM_LIMIT = 56 * 1024 * 1024

ADAM_LR = 0.001
ADAM_B1 = 0.9
ADAM_B2 = 0.999
ADAM_EPS = 1e-08
ADAM_WD = 0.01
ADAM_STEP = 10


def _pick(n, cands):
    for c in cands:
        if n % c == 0:
            return c
    return n


def _params(sem):
    return pltpu.CompilerParams(dimension_semantics=sem, vmem_limit_bytes=VMEM_LIMIT)


def _dot(a, b):
    return jnp.dot(a, b, preferred_element_type=F32)


def _dot_nt(a, b):
    return lax.dot_general(a, b, (((1,), (1,)), ((), ())), preferred_element_type=F32)


def _dot_tn(a, b):
    return lax.dot_general(a, b, (((0,), (0,)), ((), ())), preferred_element_type=F32)


def _hilo(x):
    hi = x.astype(BF16)
    return hi, (x - hi.astype(F32)).astype(BF16)


def _dot_split(tri, x):
    hi, lo = _hilo(x)
    return _dot(tri, hi) + _dot(tri, lo)


def _split_dot(x, tri):
    hi, lo = _hilo(x)
    return _dot(hi, tri) + _dot(lo, tri)


def _dot3(dot, a, b):
    return dot(a[0], b[0]) + (dot(a[0], b[1]) + dot(a[1], b[0]))


def _sigmoid(x):
    return 1.0 / (1.0 + jnp.exp(-x))


def _mm_body(kind, nk, n_extra, n_out, epi):
    dot = {"nn": _dot, "nt": _dot_nt, "tn": _dot_tn}[kind]

    def finish(acc, extra_refs, out_refs):
        res = epi(acc, *[e[...] for e in extra_refs]) if epi is not None else (acc,)
        for o, r in zip(out_refs, res):
            o[...] = r.astype(o.dtype)

    def body(a_ref, b_ref, *rest):
        extra_refs = rest[:n_extra]
        out_refs = rest[n_extra:n_extra + n_out]
        if nk == 1:
            finish(dot(a_ref[...], b_ref[...]), extra_refs, out_refs)
            return
        acc_ref = rest[n_extra + n_out]
        k = pl.program_id(2)

        @pl.when(k == 0)
        def _():
            acc_ref[...] = jnp.zeros_like(acc_ref)

        acc_ref[...] += dot(a_ref[...], b_ref[...])

        @pl.when(k == nk - 1)
        def _():
            finish(acc_ref[...], extra_refs, out_refs)

    return body


def _mm_nn(a, w, out_dtypes, *, tm, tn, tk, name, epi=None, extras=()):
    m, r = a.shape
    p, _, c = w.shape
    npc = c // tn
    nk = r // tk
    body = _mm_body("nn", nk, len(extras), len(out_dtypes), epi)
    tile = pl.BlockSpec((tm, tn), lambda i, j, k: (i, j))
    return pl.pallas_call(
        body, name=name,
        grid=(m // tm, p * npc, nk),
        in_specs=[pl.BlockSpec((tm, tk), lambda i, j, k: (i, k)),
                  pl.BlockSpec((None, tk, tn), lambda i, j, k: (j // npc, k, j % npc))] + [tile] * len(extras),
        out_specs=[tile] * len(out_dtypes),
        out_shape=[jax.ShapeDtypeStruct((m, p * c), d) for d in out_dtypes],
        scratch_shapes=[pltpu.VMEM((tm, tn), F32)] if nk > 1 else [],
        compiler_params=_params(("parallel", "parallel", "arbitrary")),
    )(a, w, *extras)


def _mm_nt(a, w, out_dtypes, *, tm, tn, tk, name, epi=None, extras=()):
    m, _ = a.shape
    p, r, c = w.shape
    kpc = c // tk
    nk = p * kpc
    body = _mm_body("nt", nk, len(extras), len(out_dtypes), epi)
    tile = pl.BlockSpec((tm, tn), lambda i, j, k: (i, j))
    return pl.pallas_call(
        body, name=name,
        grid=(m // tm, r // tn, nk),
        in_specs=[pl.BlockSpec((tm, tk), lambda i, j, k: (i, k)),
                  pl.BlockSpec((None, tn, tk), lambda i, j, k: (k // kpc, j, k % kpc))] + [tile] * len(extras),
        out_specs=[tile] * len(out_dtypes),
        out_shape=[jax.ShapeDtypeStruct((m, r), d) for d in out_dtypes],
        scratch_shapes=[pltpu.VMEM((tm, tn), F32)] if nk > 1 else [],
        compiler_params=_params(("parallel", "parallel", "arbitrary")),
    )(a, w, *extras)


def _mm_tn(a, g, p, *, tm, tn, tk, name):
    t, r = a.shape
    c = g.shape[1] // p
    npc = c // tn
    nk = t // tk
    body = _mm_body("tn", nk, 0, 1, None)
    return pl.pallas_call(
        body, name=name,
        grid=(r // tm, p * npc, nk),
        in_specs=[pl.BlockSpec((tk, tm), lambda i, j, k: (k, i)),
                  pl.BlockSpec((tk, tn), lambda i, j, k: (k, j))],
        out_specs=[pl.BlockSpec((None, tm, tn), lambda i, j, k: (j // npc, i, j % npc))],
        out_shape=[jax.ShapeDtypeStruct((p, r, c), F32)],
        scratch_shapes=[pltpu.VMEM((tm, tn), F32)] if nk > 1 else [],
        compiler_params=_params(("parallel", "parallel", "arbitrary")),
    )(a, g)[0]


def _rows(fn, row_ins, const_ins, row_outs, acc_outs, *, tm, name):
    specs, arrays = [], []
    t = None
    for item in row_ins:
        if isinstance(item, tuple):
            arr, width, cb = item
            specs.append(pl.BlockSpec((tm, width), functools.partial(lambda i, cb: (i, cb), cb=cb)))
        else:
            arr = item
            specs.append(pl.BlockSpec((tm, arr.shape[1]), lambda i: (i, 0)))
        arrays.append(arr)
        t = arr.shape[0]
    for arr in const_ins:
        specs.append(pl.BlockSpec(arr.shape, lambda i: (0, 0)))
        arrays.append(arr)
    n_in, n_row, n_acc = len(arrays), len(row_outs), len(acc_outs)

    def body(*refs):
        ins = [r[...] for r in refs[:n_in]]
        outs = refs[n_in:]
        row_res, acc_res = fn(ins[:len(row_ins)], ins[len(row_ins):])
        for o, r in zip(outs[:n_row], row_res):
            o[...] = r.astype(o.dtype)
        if n_acc:
            i = pl.program_id(0)

            @pl.when(i == 0)
            def _():
                for o in outs[n_row:]:
                    o[...] = jnp.zeros_like(o)

            for o, r in zip(outs[n_row:], acc_res):
                o[...] += r

    res = pl.pallas_call(
        body, name=name,
        grid=(t // tm,),
        in_specs=specs,
        out_specs=[pl.BlockSpec((tm, c), lambda i: (i, 0)) for c, _ in row_outs]
                  + [pl.BlockSpec((1, c), lambda i: (0, 0)) for c in acc_outs],
        out_shape=[jax.ShapeDtypeStruct((t, c), d) for c, d in row_outs]
                  + [jax.ShapeDtypeStruct((1, c), F32) for c in acc_outs],
        compiler_params=_params(("arbitrary",)),
    )(*arrays)
    return res[:n_row], res[n_row:]


def _rstd(x):
    return lax.rsqrt(jnp.mean(x * x, axis=-1, keepdims=True) + NORM_EPS)


def _rms_bwd(x, w, dy):
    r = _rstd(x)
    n = x * r
    dn = dy * w
    dx = r * (dn - n * jnp.mean(dn * n, axis=-1, keepdims=True))
    return dx, dy * n


def _colsum(x):
    return jnp.sum(x, axis=0, keepdims=True)


def _heads_map(fn, width, *tiles):
    outs = None
    for h in range(width // HEAD):
        res = fn(*[t[:, h * HEAD:(h + 1) * HEAD] for t in tiles])
        if outs is None:
            outs = [[] for _ in res]
        for lst, r in zip(outs, res):
            lst.append(r)
    return [jnp.concatenate(lst, axis=1) for lst in outs]


def _log_one_minus_beta(z):
    return -(jnp.maximum(z, 0.0) + jnp.log(1.0 + jnp.exp(-jnp.abs(z))))


def _attn_fwd(proj, after_tri, norm_w, n_heads, name):
    t = proj.shape[0]
    blk = min(ATTN_BLOCK, t)
    qb = min(ATTN_ROWS, t)
    ns = qb // blk
    scale = HEAD ** -0.5

    def body(q_ref, k_ref, v_ref, tri_ref, w_ref, o_ref, mix_ref, tot_ref):
        i = pl.program_id(1)
        q = (q_ref[...] * scale).astype(BF16)
        tri = tri_ref[...]

        def part(r0, j, acc_l, acc_o, masked):
            sl = pl.ds(pl.multiple_of(j * blk, blk), blk)
            m = qb - r0
            z = _dot_nt(q[r0:, :], k_ref[sl, :].astype(BF16))
            lm = _log_one_minus_beta(z)
            if masked:
                mask = lax.broadcasted_iota(jnp.int32, (m, blk), 1) < lax.broadcasted_iota(jnp.int32, (m, blk), 0)
                lmm = jnp.where(mask, lm, 0.0)
            else:
                lmm = lm
            w = jnp.exp(z + lm + acc_l[r0:, :] + _split_dot(lmm, tri))
            if masked:
                w = jnp.where(mask, w, 0.0)
            new_l = acc_l[r0:, :] + jnp.sum(lmm, axis=1, keepdims=True)
            new_o = acc_o[r0:, :] + _dot(w.astype(BF16), v_ref[sl, :].astype(BF16))
            if r0:
                new_l = jnp.concatenate([acc_l[:r0, :], new_l], axis=0)
                new_o = jnp.concatenate([acc_o[:r0, :], new_o], axis=0)
            return new_l, new_o

        acc = (jnp.zeros((qb, 1), F32), jnp.zeros((qb, HEAD), F32))
        for jr in reversed(range(ns)):
            acc = part(jr * blk, ns * i + jr, *acc, True)
        acc_l, acc_o = lax.fori_loop(0, ns * i, lambda jj, c: part(0, ns * i - 1 - jj, *c, False), acc)
        o_ref[...] = acc_o
        mix_ref[...] = (acc_o * _rstd(acc_o) * w_ref[...]).astype(BF16)
        tot_ref[...] = jnp.broadcast_to(acc_l, (qb, HEAD))

    width = n_heads * HEAD
    qblk = pl.BlockSpec((qb, HEAD), lambda h, i: (i, h))
    return pl.pallas_call(
        body, name=name,
        grid=(n_heads, t // qb),
        in_specs=[qblk,
                  pl.BlockSpec((t, HEAD), lambda h, i: (0, n_heads + h)),
                  pl.BlockSpec((t, HEAD), lambda h, i: (0, 2 * n_heads + h)),
                  pl.BlockSpec((blk, blk), lambda h, i: (0, 0)),
                  pl.BlockSpec((1, HEAD), lambda h, i: (0, 0))],
        out_specs=[qblk, qblk, qblk],
        out_shape=[jax.ShapeDtypeStruct((t, width), F32), jax.ShapeDtypeStruct((t, width), BF16),
                   jax.ShapeDtypeStruct((t, width), F32)],
        compiler_params=_params(("parallel", "arbitrary")),
    )(proj, proj, proj, after_tri, norm_w)


def _attn_bwd(proj, tot, do, upto_tri, before_tri, n_heads, name):
    t = proj.shape[0]
    blk = min(ATTN_BLOCK, t)
    qb = min(ATTN_ROWS, t)
    ns = qb // blk
    scale = HEAD ** -0.5

    def body(q_ref, k_ref, v_ref, tot_ref, do_ref, upto_ref, before_ref, dq_ref, dk_ref, dv_ref):
        i = pl.program_id(1)

        @pl.when(i == 0)
        def _():
            dk_ref[...] = jnp.zeros_like(dk_ref)
            dv_ref[...] = jnp.zeros_like(dv_ref)

        q = (q_ref[...] * scale).astype(BF16)
        dob = do_ref[...].astype(BF16)
        total = tot_ref[:, 0:1]
        upto = upto_ref[...]
        before = before_ref[...]

        def part(r0, j, seen_l, seen_g, dq, masked):
            sl = pl.ds(pl.multiple_of(j * blk, blk), blk)
            m = qb - r0
            qq, dd = q[r0:, :], dob[r0:, :]
            kb = k_ref[sl, :].astype(BF16)
            z = _dot_nt(qq, kb)
            lm = _log_one_minus_beta(z)
            if masked:
                mask = lax.broadcasted_iota(jnp.int32, (m, blk), 1) < lax.broadcasted_iota(jnp.int32, (m, blk), 0)
                lmm = jnp.where(mask, lm, 0.0)
            else:
                lmm = lm
            after = (total[r0:, :] - seen_l[r0:, :]) - _split_dot(lmm, upto)
            w = jnp.exp(z + lm + after)
            if masked:
                w = jnp.where(mask, w, 0.0)
            sig = jnp.exp(z + lm)
            g = w * _dot_nt(dd, v_ref[sl, :].astype(BF16))
            g_before = seen_g[r0:, :] + _split_dot(g, before)
            dz = g * (1.0 - sig) - g_before * sig
            if masked:
                dz = jnp.where(mask, dz, 0.0)
            dzb = dz.astype(BF16)
            dk_ref[sl, :] += _dot_tn(dzb, qq)
            dv_ref[sl, :] += _dot_tn(w.astype(BF16), dd)
            new = (seen_l[r0:, :] + jnp.sum(lmm, axis=1, keepdims=True),
                   seen_g[r0:, :] + jnp.sum(g, axis=1, keepdims=True), dq[r0:, :] + _dot(dzb, kb))
            if r0:
                new = tuple(jnp.concatenate([old[:r0, :], n], axis=0) for old, n in zip((seen_l, seen_g, dq), new))
            return new

        zero = jnp.zeros((qb, 1), F32)
        carry = lax.fori_loop(0, ns * i, lambda j, c: part(0, j, *c, False), (zero, zero, jnp.zeros((qb, HEAD), F32)))
        for jr in range(ns):
            carry = part(jr * blk, ns * i + jr, *carry, True)
        dq_ref[...] = carry[2] * scale

    width = n_heads * HEAD
    qblk = pl.BlockSpec((qb, HEAD), lambda h, i: (i, h))
    full = pl.BlockSpec((t, HEAD), lambda h, i: (0, h))
    tri = pl.BlockSpec((blk, blk), lambda h, i: (0, 0))
    return pl.pallas_call(
        body, name=name,
        grid=(n_heads, t // qb),
        in_specs=[qblk,
                  pl.BlockSpec((t, HEAD), lambda h, i: (0, n_heads + h)),
                  pl.BlockSpec((t, HEAD), lambda h, i: (0, 2 * n_heads + h)),
                  qblk, qblk, tri, tri],
        out_specs=[qblk, full, full],
        out_shape=[jax.ShapeDtypeStruct((t, width), F32)] * 3,
        compiler_params=_params(("parallel", "arbitrary")),
    )(proj, proj, proj, tot, do, upto_tri, before_tri)


def _lower_bound(logits):
    l0, l1 = logits[0:1, :], logits[1:2, :]
    mx = jnp.maximum(l0, l1)
    e0, e1 = jnp.exp(l0 - mx), jnp.exp(l1 - mx)
    return e0 / (e0 + e1)


def _hg_chunk(qc, kc, gc, tri_lo):
    c = qc.shape[0]
    cum = _dot_split(tri_lo, gc)
    mid = cum[c // 2 - 1:c // 2, :]
    last = cum[c - 1:c, :]
    qt = qc * jnp.exp(cum - mid)
    kt = kc * jnp.exp(mid - cum)
    qe = qc * jnp.exp(cum)
    kd = kc * jnp.exp(last - cum)
    return cum, mid, last, qt, kt, qe, kd


def _hgrn_fwd(proj, lb_logits, norm_w, tri_lo, n_heads, heads_per_step, name):
    t = proj.shape[0]
    bt = min(HG_ROWS, t)
    c = HG_CHUNK
    nc = bt // c
    hw = heads_per_step * HEAD
    width = n_heads * HEAD
    col0 = 3 * width // hw

    def body(hq_ref, hf_ref, hi_ref, hgate_ref, lbl_ref, w_ref, tri_ref, o_ref, mix_ref, st_ref,
             state, q_scr, k_scr, g_scr):
        @pl.when(pl.program_id(1) == 0)
        def _():
            state[...] = jnp.zeros_like(state)

        lb = _lower_bound(lbl_ref[...])
        f = hf_ref[...]
        g_scr[...] = jnp.log(lb + (1.0 - lb) * _sigmoid(f))
        k_scr[...] = (1.0 - lb) * _sigmoid(-f)
        hq = hq_ref[...]
        q_scr[...] = hq * _sigmoid(hq)
        tri = tri_ref[...]
        causal = lax.broadcasted_iota(jnp.int32, (c, c), 1) <= lax.broadcasted_iota(jnp.int32, (c, c), 0)

        def chunk(ci, carry):
            r = pl.ds(pl.multiple_of(ci * c, c), c)
            vc = hi_ref[r, :].astype(BF16)
            _, _, last, qt, kt, qe, kd = _hg_chunk(q_scr[r, :], k_scr[r, :], g_scr[r, :], tri)
            qt, kt, qe, kd = qt.astype(BF16), kt.astype(BF16), qe.astype(BF16), kd.astype(BF16)
            e_last = jnp.exp(last)
            old = [state[h] for h in range(heads_per_step)]
            outs, new = [], []
            for h in range(heads_per_step):
                cs = slice(h * HEAD, (h + 1) * HEAD)
                a = jnp.where(causal, _dot_nt(qt[:, cs], kt[:, cs]), 0.0)
                outs.append(_dot(a.astype(BF16), vc[:, cs]) + _dot_nt(qe[:, cs], old[h].astype(BF16)))
                new.append(old[h] * e_last[:, cs] + _dot_tn(vc[:, cs], kd[:, cs]))
            for h in range(heads_per_step):
                st_ref[ci, :, h * HEAD:(h + 1) * HEAD] = old[h]
                state[h] = new[h]
            o_ref[r, :] = jnp.concatenate(outs, axis=1)
            return carry

        lax.fori_loop(0, nc, chunk, 0)

        def finish(o, gate):
            return ((o * _rstd(o) * w_ref[...]) * (gate * _sigmoid(gate)),)

        mix_ref[...] = _heads_map(finish, hw, o_ref[...], hgate_ref[...])[0].astype(BF16)

    def col(group):
        return pl.BlockSpec((bt, hw), functools.partial(lambda hp, tb, g: (tb, col0 + g * (width // hw) + hp), g=group))

    blk = pl.BlockSpec((bt, hw), lambda hp, tb: (tb, hp))
    return pl.pallas_call(
        body, name=name,
        grid=(n_heads // heads_per_step, t // bt),
        in_specs=[col(0), col(1), col(2), col(3),
                  pl.BlockSpec((2, hw), lambda hp, tb: (0, hp)),
                  pl.BlockSpec((1, HEAD), lambda hp, tb: (0, 0)),
                  pl.BlockSpec((c, c), lambda hp, tb: (0, 0))],
        out_specs=[blk, blk, pl.BlockSpec((nc, HEAD, hw), lambda hp, tb: (tb, 0, hp))],
        out_shape=[jax.ShapeDtypeStruct((t, width), F32), jax.ShapeDtypeStruct((t, width), BF16),
                   jax.ShapeDtypeStruct((t // c, HEAD, width), F32)],
        scratch_shapes=[pltpu.VMEM((heads_per_step, HEAD, HEAD), F32)] + [pltpu.VMEM((bt, hw), F32)] * 3,
        compiler_params=_params(("parallel", "arbitrary")),
    )(proj, proj, proj, proj, lb_logits, norm_w, tri_lo)


def _hgrn_bwd(proj, do, states, lb_logits, tri_lo, tri_up, n_heads, heads_per_step, name):
    t = proj.shape[0]
    bt = min(HG_ROWS, t)
    c = HG_CHUNK
    nc = bt // c
    nb = t // bt
    hw = heads_per_step * HEAD
    width = n_heads * HEAD
    col0 = 3 * width // hw

    def body(hq_ref, hf_ref, hi_ref, do_ref, st_ref, lbl_ref, lo_ref, up_ref, dq_ref, df_ref, di_ref, dlb_ref,
             dstate, q_scr, k_scr, g_scr, dk_scr, dg_scr):
        @pl.when(pl.program_id(1) == 0)
        def _():
            dstate[...] = jnp.zeros_like(dstate)
            dlb_ref[...] = jnp.zeros_like(dlb_ref)

        lb = _lower_bound(lbl_ref[...])
        f = hf_ref[...]
        sg = _sigmoid(f)
        sgn = _sigmoid(-f)
        den = lb + (1.0 - lb) * sg
        g_scr[...] = jnp.log(den)
        k_scr[...] = (1.0 - lb) * sgn
        hq = hq_ref[...]
        sq = _sigmoid(hq)
        q_scr[...] = hq * sq
        tri_lo_v = lo_ref[...]
        tri_up_v = up_ref[...]
        causal = lax.broadcasted_iota(jnp.int32, (c, c), 1) <= lax.broadcasted_iota(jnp.int32, (c, c), 0)
        last_row = lax.broadcasted_iota(jnp.int32, (c, hw), 0) == c - 1

        def chunk(cc, carry):
            ci = nc - 1 - cc
            r = pl.ds(pl.multiple_of(ci * c, c), c)
            qc, kc = q_scr[r, :], k_scr[r, :]
            cum, mid, last, qt, kt, qe, kd = _hg_chunk(qc, kc, g_scr[r, :], tri_lo_v)
            qt, kt, qe, kd, doc, vc = [_hilo(v) for v in (qt, kt, qe, kd, do_ref[r, :], hi_ref[r, :])]
            e_last = jnp.exp(last)
            sts = [st_ref[ci, :, h * HEAD:(h + 1) * HEAD] for h in range(heads_per_step)]
            dsts = [dstate[h] for h in range(heads_per_step)]
            di, dq_inter, dk_inter, dq_intra, dk_intra, st_sums, new = [], [], [], [], [], [], []
            for h in range(heads_per_step):
                cs = slice(h * HEAD, (h + 1) * HEAD)

                def head(pair):
                    return pair[0][:, cs], pair[1][:, cs]

                st, dst = _hilo(sts[h]), _hilo(dsts[h])
                a = _hilo(jnp.where(causal, _dot3(_dot_nt, head(qt), head(kt)), 0.0))
                da = _hilo(jnp.where(causal, _dot3(_dot_nt, head(doc), head(vc)), 0.0))
                di.append(_dot3(_dot_tn, a, head(doc)) + _dot3(_dot_nt, head(kd), dst))
                dq_inter.append(_dot3(_dot, head(doc), st))
                dk_inter.append(_dot3(_dot, head(vc), dst))
                dq_intra.append(_dot3(_dot, da, head(kt)))
                dk_intra.append(_dot3(_dot_tn, da, head(qt)))
                st_sums.append(_colsum(dsts[h] * sts[h]))
                new.append(dsts[h] * e_last[:, cs] + _dot3(_dot_tn, head(doc), head(qe)))

            def wide(parts):
                return jnp.concatenate(parts, axis=1)

            dq_inter = wide(dq_inter) * jnp.exp(cum)
            dk_inter = wide(dk_inter) * jnp.exp(last - cum)
            dq = wide(dq_intra) * jnp.exp(cum - mid) + dq_inter
            dk = wide(dk_intra) * jnp.exp(mid - cum) + dk_inter
            d_last = _colsum(kc * dk_inter) + e_last * wide(st_sums)
            dcum = qc * dq - kc * dk + jnp.where(last_row, d_last, 0.0)
            for h in range(heads_per_step):
                dstate[h] = new[h]
            di_ref[r, :] = wide(di)
            dq_ref[r, :] = dq
            dk_scr[r, :] = dk
            dg_scr[r, :] = _dot_split(tri_up_v, dcum)
            return carry

        lax.fori_loop(0, nc, chunk, 0)

        e = (dg_scr[...] / den - dk_scr[...]) * sgn
        df_ref[...] = e * (1.0 - lb) * sg
        dlb_ref[...] += _colsum(e)
        dq_ref[...] = dq_ref[...] * (sq * (1.0 + hq * (1.0 - sq)))

    def col(group):
        return pl.BlockSpec((bt, hw), functools.partial(
            lambda hp, tb, g: (nb - 1 - tb, col0 + g * (width // hw) + hp), g=group))

    blk = pl.BlockSpec((bt, hw), lambda hp, tb: (nb - 1 - tb, hp))
    tri = pl.BlockSpec((c, c), lambda hp, tb: (0, 0))
    return pl.pallas_call(
        body, name=name,
        grid=(n_heads // heads_per_step, nb),
        in_specs=[col(0), col(1), col(2), blk,
                  pl.BlockSpec((nc, HEAD, hw), lambda hp, tb: (nb - 1 - tb, 0, hp)),
                  pl.BlockSpec((2, hw), lambda hp, tb: (0, hp)), tri, tri],
        out_specs=[blk, blk, blk, pl.BlockSpec((1, hw), lambda hp, tb: (0, hp))],
        out_shape=[jax.ShapeDtypeStruct((t, width), F32)] * 3 + [jax.ShapeDtypeStruct((1, width), F32)],
        scratch_shapes=[pltpu.VMEM((heads_per_step, HEAD, HEAD), F32)] + [pltpu.VMEM((bt, hw), F32)] * 5,
        compiler_params=_params(("parallel", "arbitrary")),
    )(proj, proj, proj, do, states, lb_logits, tri_lo, tri_up)


def _place():
    x, y, c = lax.axis_index("x"), lax.axis_index("y"), lax.axis_index("c")
    chips = [(1 - x, y), (x, 1 - y), (1 - x, 1 - y)]
    return x, y, c, chips


ANY = pl.BlockSpec(memory_space=pl.ANY)


def _cast_to_slot(shard, ids, name):
    r, c = shard.shape
    tm = _pick(r, (256, 128, 64, 32, 16))

    def body(ids_ref, s_ref, o_ref):
        o_ref[...] = s_ref[...].astype(BF16)

    return pl.pallas_call(
        body, name=name,
        grid_spec=pltpu.PrefetchScalarGridSpec(
            num_scalar_prefetch=1, grid=(r // tm,),
            in_specs=[pl.BlockSpec((tm, c), lambda i, ids: (i, 0))],
            out_specs=pl.BlockSpec((None, tm, c), lambda i, ids: (ids[1], i, 0))),
        out_shape=jax.ShapeDtypeStruct((N_CHIPS, r, c), BF16),
        compiler_params=_params(("parallel",)),
    )(ids, shard)


def _gather_weights(slots, name):
    n = len(slots)

    def body(*refs):
        bufs = refs[n:2 * n]
        send, recv = refs[2 * n:]
        x, y, c, chips = _place()
        mine = 2 * x + y

        def half(ref, who, core):
            h = ref.shape[-2] // 2
            return ref.at[who, pl.ds(core * h, h), :]

        def copy(w, k, rows, to):
            return pltpu.make_async_remote_copy(src_ref=rows, dst_ref=rows, send_sem=send.at[6 * w + k],
                                                recv_sem=recv.at[6 * w + k], device_id=to, device_id_type=MESH)

        sent = []
        for w in range(n):
            for k, (qx, qy) in enumerate(chips):
                sent.append(copy(w, k, half(bufs[w], mine, c), (qx, qy, c)))
                sent[-1].start()
        for w in range(n):
            for k, (qx, qy) in enumerate(chips):
                landed = half(bufs[w], 2 * qx + qy, c)
                copy(w, k, landed, (x, y, c)).wait_recv()
                sent.append(copy(w, 3 + k, landed, (x, y, 1 - c)))
                sent[-1].start()
        for w in range(n):
            for k, (qx, qy) in enumerate(chips):
                copy(w, 3 + k, half(bufs[w], 2 * qx + qy, 1 - c), (x, y, c)).wait_recv()
        for cp in sent:
            cp.wait_send()

    return pl.pallas_call(
        body, name=name,
        in_specs=[ANY] * n, out_specs=[ANY] * n,
        out_shape=[jax.ShapeDtypeStruct(s.shape, s.dtype) for s in slots],
        input_output_aliases={w: w for w in range(n)},
        scratch_shapes=[pltpu.SemaphoreType.DMA((6 * n,)), pltpu.SemaphoreType.DMA((6 * n,))],
    )(*slots)


def _sibling_swap(grads, name):
    n = len(grads)

    def body(*refs):
        ins, outs = refs[:n], refs[n:2 * n]
        send, recv = refs[2 * n:]
        x, y, c, _ = _place()
        cps = []
        for w in range(n):
            h = ins[w].shape[1] // 2
            cps.append(pltpu.make_async_remote_copy(
                src_ref=ins[w].at[:, pl.ds((1 - c) * h, h), :], dst_ref=outs[w], send_sem=send.at[w],
                recv_sem=recv.at[w], device_id=(x, y, 1 - c), device_id_type=MESH))
            cps[-1].start()
        for cp in cps:
            cp.wait()

    return pl.pallas_call(
        body, name=name, in_specs=[ANY] * n, out_specs=[ANY] * n,
        out_shape=[jax.ShapeDtypeStruct((g.shape[0], g.shape[1] // 2, g.shape[2]), g.dtype) for g in grads],
        scratch_shapes=[pltpu.SemaphoreType.DMA((n,)), pltpu.SemaphoreType.DMA((n,))],
    )(*grads)


def _chip_scatter(sums, name):
    n = len(sums)

    def body(*refs):
        ins, outs = refs[:n], refs[n:2 * n]
        send, recv = refs[2 * n:]
        _, _, c, chips = _place()
        cps = []
        for w in range(n):
            for k, (qx, qy) in enumerate(chips):
                cps.append(pltpu.make_async_remote_copy(
                    src_ref=ins[w].at[2 * qx + qy], dst_ref=outs[w].at[k], send_sem=send.at[3 * w + k],
                    recv_sem=recv.at[3 * w + k], device_id=(qx, qy, c), device_id_type=MESH))
                cps[-1].start()
        for cp in cps:
            cp.wait()

    return pl.pallas_call(
        body, name=name, in_specs=[ANY] * n, out_specs=[ANY] * n,
        out_shape=[jax.ShapeDtypeStruct((3,) + s.shape[1:], s.dtype) for s in sums],
        scratch_shapes=[pltpu.SemaphoreType.DMA((3 * n,)), pltpu.SemaphoreType.DMA((3 * n,))],
    )(*sums)


def _sibling_join(fulls, name):
    n = len(fulls)

    def body(*refs):
        bufs = refs[n:2 * n]
        send, recv = refs[2 * n:]
        x, y, c, _ = _place()
        cps = []
        for w in range(n):
            h = bufs[w].shape[0] // 2
            rows = bufs[w].at[pl.ds(c * h, h), :]
            cps.append(pltpu.make_async_remote_copy(
                src_ref=rows, dst_ref=rows, send_sem=send.at[w], recv_sem=recv.at[w],
                device_id=(x, y, 1 - c), device_id_type=MESH))
            cps[-1].start()
        for w in range(n):
            h = bufs[w].shape[0] // 2
            theirs = bufs[w].at[pl.ds((1 - c) * h, h), :]
            pltpu.make_async_remote_copy(src_ref=theirs, dst_ref=theirs, send_sem=send.at[w], recv_sem=recv.at[w],
                                         device_id=(x, y, c), device_id_type=MESH).wait_recv()
        for cp in cps:
            cp.wait_send()

    return pl.pallas_call(
        body, name=name, in_specs=[ANY] * n, out_specs=[ANY] * n,
        out_shape=[jax.ShapeDtypeStruct(s.shape, s.dtype) for s in fulls],
        input_output_aliases={w: w for w in range(n)},
        scratch_shapes=[pltpu.SemaphoreType.DMA((n,)), pltpu.SemaphoreType.DMA((n,))],
    )(*fulls)


def _all_sum_small(vec, name):
    n = vec.shape[1]

    def body(v_ref, out_ref, buf, send, recv):
        x, y, c, _ = _place()
        me = 4 * x + 2 * y + c
        buf[me] = v_ref[...]
        peers = []
        for mask in range(1, 8):
            px = 1 - x if mask & 4 else x
            py = 1 - y if mask & 2 else y
            pc = 1 - c if mask & 1 else c
            peers.append((px, py, pc))
        cps = []
        for k, peer in enumerate(peers):
            cps.append(pltpu.make_async_remote_copy(src_ref=buf.at[me], dst_ref=buf.at[me], send_sem=send.at[k],
                                                    recv_sem=recv.at[k], device_id=peer, device_id_type=MESH))
            cps[-1].start()
        for k, (px, py, pc) in enumerate(peers):
            slot = buf.at[4 * px + 2 * py + pc]
            pltpu.make_async_remote_copy(src_ref=slot, dst_ref=slot, send_sem=send.at[k], recv_sem=recv.at[k],
                                         device_id=(x, y, c), device_id_type=MESH).wait_recv()
        for cp in cps:
            cp.wait_send()
        total = buf[0]
        for d in range(1, 8):
            total = total + buf[d]
        out_ref[...] = total

    vm = pl.BlockSpec(memory_space=pltpu.VMEM)
    return pl.pallas_call(
        body, name=name, in_specs=[vm], out_specs=vm,
        out_shape=jax.ShapeDtypeStruct(vec.shape, F32),
        scratch_shapes=[pltpu.VMEM((8, 8, n), F32), pltpu.SemaphoreType.DMA((7,)), pltpu.SemaphoreType.DMA((7,))],
    )(vec)


def _pair_sum(g, buf, ids, name):
    p, r, c = g.shape
    h = r // 2
    tr = _pick(h, (256, 128, 64, 32, 16))
    nh = h // tr

    def body(ids_ref, g_ref, b_ref, sums_ref, own_ref):
        s = g_ref[...] + b_ref[...]
        sums_ref[...] = s.astype(BF16)

        @pl.when(pl.program_id(1) == ids_ref[1])
        def _():
            own_ref[...] = s

    return pl.pallas_call(
        body, name=name,
        grid_spec=pltpu.PrefetchScalarGridSpec(
            num_scalar_prefetch=1, grid=(nh, p),
            in_specs=[pl.BlockSpec((None, tr, c), lambda i, q, ids: (q, ids[0] * nh + i, 0)),
                      pl.BlockSpec((None, tr, c), lambda i, q, ids: (q, i, 0))],
            out_specs=[pl.BlockSpec((None, tr, c), lambda i, q, ids: (q, i, 0)),
                       pl.BlockSpec((tr, c), lambda i, q, ids: (i, 0))]),
        out_shape=[jax.ShapeDtypeStruct((p, h, c), BF16), jax.ShapeDtypeStruct((h, c), F32)],
        compiler_params=_params(("parallel", "arbitrary")),
    )(ids, g, buf)


def _final_sum(own, others, ids, name):
    h, c = own.shape
    tr = _pick(h, (256, 128, 64, 32, 16))
    nh = h // tr

    def body(ids_ref, own_ref, oth_ref, out_ref):
        s = own_ref[...]
        for k in range(3):
            s = s + oth_ref[k].astype(F32)
        out_ref[...] = s

    return pl.pallas_call(
        body, name=name,
        grid_spec=pltpu.PrefetchScalarGridSpec(
            num_scalar_prefetch=1, grid=(nh,),
            in_specs=[pl.BlockSpec((tr, c), lambda i, ids: (i, 0)),
                      pl.BlockSpec((3, tr, c), lambda i, ids: (0, i, 0))],
            out_specs=pl.BlockSpec((tr, c), lambda i, ids: (ids[0] * nh + i, 0))),
        out_shape=jax.ShapeDtypeStruct((2 * h, c), F32),
        compiler_params=_params(("parallel",)),
    )(ids, own, others)


def _adamw(w, g, m, v, name):
    r, c = w.shape
    tm = _pick(r, (256, 128, 64, 32, 16, 8)) if r >= 8 else r

    def fn(rows, _):
        w_, g_, m_, v_ = rows
        m2 = ADAM_B1 * m_ + (1.0 - ADAM_B1) * g_
        v2 = ADAM_B2 * v_ + (1.0 - ADAM_B2) * (g_ * g_)
        m_hat = m2 / (1.0 - ADAM_B1 ** ADAM_STEP)
        v_hat = v2 / (1.0 - ADAM_B2 ** ADAM_STEP)
        delta = -ADAM_LR * (m_hat / (jnp.sqrt(v_hat) + ADAM_EPS) + ADAM_WD * w_)
        return [delta, m2, v2], []

    outs, _ = _rows(fn, [w, g, m, v], [], [(c, F32)] * 3, [], tm=tm, name=name)
    return outs


def kernel(x, attn_norm_w, w_in, lb_logits, sb_norm_w, hg_norm_w, w_out, mlp_norm_w, w_up, w_down, final_norm_w, loss_target, m_attn_norm_w, m_w_in, m_lb_logits, m_sb_norm_w, m_hg_norm_w, m_w_out, m_mlp_norm_w, m_w_up, m_w_down, m_final_norm_w, v_attn_norm_w, v_w_in, v_lb_logits, v_sb_norm_w, v_hg_norm_w, v_w_out, v_mlp_norm_w, v_w_up, v_w_down, v_final_norm_w):
    xs, tgt = x[0], loss_target[0]
    t, d = xs.shape
    width = d // 2
    n_heads = width // HEAD
    hps = min(8, n_heads)
    final_w = final_norm_w.reshape(1, d)
    tm_rows = _pick(t, (256, 128))
    tm = _pick(t, (1024, 512, 256))
    blk = min(ATTN_BLOCK, t)
    ones_a = jnp.ones((blk, blk), F32)
    after_tri = jnp.tril(ones_a, -1).astype(BF16)
    upto_tri = jnp.triu(ones_a).astype(BF16)
    before_tri = jnp.triu(ones_a, 1).astype(BF16)
    ones_c = jnp.ones((HG_CHUNK, HG_CHUNK), F32)
    tri_lo, tri_up = jnp.tril(ones_c).astype(BF16), jnp.triu(ones_c).astype(BF16)
    cx, cy, cc = lax.axis_index("x"), lax.axis_index("y"), lax.axis_index("c")
    ids = jnp.stack([cc, 2 * cx + cy]).astype(jnp.int32)

    shards = [w_in[0], w_out[0], w_up[0], w_down[0]]
    cast = [_cast_to_slot(s, ids, f"cast_w{i}") for i, s in enumerate(shards)]
    g_in, g_out, g_up, g_down = _gather_weights(cast, "gather_weights")
    d_ff = N_CHIPS * w_up.shape[2]
    w_out_all = g_out.reshape(1, d, d)
    w_down_all = g_down.reshape(1, d_ff, d)
    cs_in, cs_up = g_in.shape[2], g_up.shape[2]
    tn_in = _pick(cs_in, (1792, 896, 512, 256, 128))
    tn_up = _pick(cs_up, (1024, 512, 256))
    tn_d = _pick(d, (1024, 512, 256))
    tk_d = _pick(d, (2048, 1024, 512))

    (u,), _ = _rows(lambda r, c_: ([r[0] * _rstd(r[0]) * c_[0]], []), [xs], [attn_norm_w], [(d, BF16)], [],
                    tm=tm_rows, name="norm_in")
    (proj,) = _mm_nn(u, g_in, [F32], tm=_pick(t, (512, 256)), tn=tn_in, tk=tk_d, name="proj_in")
    o_a, mix_a, sb_tot = _attn_fwd(proj, after_tri, sb_norm_w, n_heads, "sb_fwd")
    o_b, mix_b, states = _hgrn_fwd(proj, lb_logits, hg_norm_w, tri_lo, n_heads, hps, "hg_fwd")
    mix = jnp.concatenate([mix_a, mix_b], axis=1)
    (h1,) = _mm_nn(mix, w_out_all, [F32], tm=tm, tn=tn_d, tk=tk_d, name="proj_out",
                   epi=lambda acc, res: (acc + res,), extras=(xs,))
    (mn,), _ = _rows(lambda r, c_: ([r[0] * _rstd(r[0]) * c_[0]], []), [h1], [mlp_norm_w], [(d, BF16)], [],
                     tm=tm_rows, name="norm_mlp")
    up_b, act = _mm_nn(mn, g_up, [BF16, BF16], tm=tm, tn=tn_up, tk=tk_d, name="mlp_up",
                       epi=lambda acc: (acc, jnp.square(jnp.maximum(acc, 0.0))))
    (h2,) = _mm_nn(act, w_down_all, [F32], tm=tm, tn=tn_d, tk=_pick(d_ff, (2048, 1024)), name="mlp_down",
                   epi=lambda acc, res: (acc + res,), extras=(h1,))

    def head(rows, consts):
        hh, tg = rows
        w = consts[0]
        n = hh * _rstd(hh)
        err = n * w - tg
        dhh, dw_rows = _rms_bwd(hh, w, err * (1.0 / d))
        return [dhh, dhh], [_colsum(dw_rows), _colsum(err * err)]

    (dh2, dh2_b), (g_final, loss_cols) = _rows(head, [h2, tgt], [final_w], [(d, F32), (d, BF16)], [d, d],
                                                 tm=tm_rows, name="loss_head")

    (dup,) = _mm_nt(dh2_b, w_down_all, [BF16], tm=tm, tn=_pick(d_ff, (1024, 512)), tk=tk_d, name="mlp_down_dx",
                    epi=lambda acc, upv: (acc * (2.0 * jnp.maximum(upv.astype(F32), 0.0)),), extras=(up_b,))
    gw_down = _mm_tn(act, dh2_b, 1, tm=_pick(d_ff, (1024, 512)), tn=tn_d, tk=_pick(t, (1024, 512, 256)),
                     name="mlp_down_dw")
    (dmn,) = _mm_nt(dup, g_up, [F32], tm=tm, tn=tn_d, tk=_pick(cs_up, (2048, 1024, 512)), name="mlp_up_dx")
    gw_up = _mm_tn(mn, dup, N_CHIPS, tm=tn_d, tn=tn_up, tk=_pick(t, (1024, 512, 256)), name="mlp_up_dw")

    def norm_back(rows, consts):
        xx, dy, skip = rows
        dx, dw_rows = _rms_bwd(xx, consts[0], dy)
        tot = dx + skip
        return [tot, tot], [_colsum(dw_rows)]

    (dh1, dh1_b), (g_mlp_norm,) = _rows(norm_back, [h1, dmn, dh2], [mlp_norm_w], [(d, F32), (d, BF16)], [d],
                                         tm=tm_rows, name="norm_mlp_bwd")

    (dmix,) = _mm_nt(dh1_b, w_out_all, [F32], tm=tm, tn=tn_d, tk=tk_d, name="proj_out_dx")
    gw_out = _mm_tn(mix, dh1_b, 1, tm=tn_d, tn=tn_d, tk=_pick(t, (1024, 512, 256)), name="proj_out_dw")

    def sb_norm_back(rows, consts):
        dx, dw_rows = _heads_map(lambda o, dy: _rms_bwd(o, consts[0], dy), width, *rows)
        dw = sum(_colsum(dw_rows[:, h * HEAD:(h + 1) * HEAD]) for h in range(n_heads))
        return [dx], [dw]

    (do_a,), (g_sb_norm,) = _rows(sb_norm_back, [o_a, (dmix, width, 0)], [sb_norm_w], [(width, F32)], [HEAD],
                                  tm=tm_rows, name="sb_norm_bwd")
    dq_a, dk_a, dv_a = _attn_bwd(proj, sb_tot, do_a, upto_tri, before_tri, n_heads, "sb_bwd")

    def hg_out_back(rows, consts):
        def one(o, gate, dy):
            sg = _sigmoid(gate)
            silu = gate * sg
            n = o * _rstd(o) * consts[0]
            do, dw_rows = _rms_bwd(o, consts[0], dy * silu)
            return do, dy * n * (sg * (1.0 + gate * (1.0 - sg))), dw_rows
        do, dgate, dw_rows = _heads_map(one, width, *rows)
        dw = sum(_colsum(dw_rows[:, h * HEAD:(h + 1) * HEAD]) for h in range(n_heads))
        return [do, dgate], [dw]

    (do_b, dgate), (g_hg_norm,) = _rows(hg_out_back, [o_b, (proj, width, 6), (dmix, width, 1)], [hg_norm_w],
                                         [(width, F32)] * 2, [HEAD], tm=tm_rows, name="hg_out_bwd")
    dhq, dhf, dhi, dlb = _hgrn_bwd(proj, do_b, states, lb_logits, tri_lo, tri_up, n_heads, hps, "hg_bwd")

    (dproj,), _ = _rows(lambda r, _c: ([jnp.concatenate([p.astype(BF16) for p in r], axis=1)], []),
                        [dq_a, dk_a, dv_a, dhq, dhf, dhi, dgate], [], [(7 * width, BF16)], [],
                        tm=tm_rows, name="pack_dproj")
    (du,) = _mm_nt(dproj, g_in, [F32], tm=tm, tn=tn_d, tk=_pick(cs_in, (1792, 896, 512, 256, 128)), name="proj_in_dx")
    gw_in = _mm_tn(u, dproj, N_CHIPS, tm=tn_d, tn=tn_in, tk=_pick(t, (1024, 512, 256)), name="proj_in_dw")
    (dx,), (g_attn_norm,) = _rows(lambda r, c_: (lambda dxx, dwr: ([dxx + r[2]], [_colsum(dwr)]))(
        *_rms_bwd(r[0], c_[0], r[1])), [xs, du, dh1], [attn_norm_w], [(d, F32)], [d], tm=tm_rows, name="norm_in_bwd")

    grads = [gw_in, gw_out.reshape(N_CHIPS, d // N_CHIPS, d), gw_up, gw_down.reshape(N_CHIPS, d_ff // N_CHIPS, d)]
    theirs = _sibling_swap(grads, "grads_to_sibling")
    pair = [_pair_sum(g, b, ids, f"grads_pair_sum{i}") for i, (g, b) in enumerate(zip(grads, theirs))]
    landed = _chip_scatter([p[0] for p in pair], "grads_to_chips")
    halves = [_final_sum(p[1], r, ids, f"grads_final_sum{i}") for i, (p, r) in enumerate(zip(pair, landed))]
    g_w_in, g_w_out, g_w_up, g_w_down = _sibling_join(halves, "grads_join")

    pieces = [g_attn_norm, g_mlp_norm, g_final, g_sb_norm, g_hg_norm, dlb, loss_cols]
    sizes = [p.shape[1] for p in pieces]
    flat = jnp.concatenate(pieces, axis=1)
    n_small = -(-flat.shape[1] // 1024) * 1024
    flat = jnp.pad(flat, ((0, 0), (0, n_small - flat.shape[1]))).reshape(8, n_small // 8)
    flat = _all_sum_small(flat, "small_all_sum").reshape(1, n_small)
    offs = [sum(sizes[:i]) for i in range(len(sizes))]
    g_attn_norm, g_mlp_norm, g_final, g_sb_norm, g_hg_norm, dlb, loss_cols = [
        flat[:, o:o + s] for o, s in zip(offs, sizes)]

    def small_tail(lbl_ref, dlb_ref, loss_ref, glb_ref, out_ref):
        lb = _lower_bound(lbl_ref[...])
        g0 = dlb_ref[...] * lb * (1.0 - lb)
        glb_ref[0:1, :] = g0
        glb_ref[1:2, :] = -g0
        out_ref[...] = jnp.zeros_like(out_ref) + 0.5 * jnp.sum(loss_ref[...]) * (1.0 / d)

    vm = pl.BlockSpec(memory_space=pltpu.VMEM)
    g_lb, loss11 = pl.pallas_call(
        small_tail, name="small_tail", in_specs=[vm, vm, vm], out_specs=[vm, vm],
        out_shape=[jax.ShapeDtypeStruct(lb_logits.shape, F32), jax.ShapeDtypeStruct((1, 128), F32)],
    )(lb_logits, dlb, loss_cols)
    loss = loss11[0, 0]

    names = ["attn_norm_w", "w_in", "lb_logits", "sb_norm_w", "hg_norm_w", "w_out", "mlp_norm_w", "w_up", "w_down",
             "final_norm_w"]
    ws = [attn_norm_w, w_in[0], lb_logits, sb_norm_w, hg_norm_w, w_out[0], mlp_norm_w, w_up[0], w_down[0], final_w]
    gs = [g_attn_norm, g_w_in, g_lb, g_sb_norm, g_hg_norm, g_w_out, g_mlp_norm, g_w_up, g_w_down, g_final]
    ms = [m_attn_norm_w, m_w_in[0], m_lb_logits, m_sb_norm_w, m_hg_norm_w, m_w_out[0], m_mlp_norm_w, m_w_up[0],
          m_w_down[0], m_final_norm_w.reshape(1, d)]
    vs = [v_attn_norm_w, v_w_in[0], v_lb_logits, v_sb_norm_w, v_hg_norm_w, v_w_out[0], v_mlp_norm_w, v_w_up[0],
          v_w_down[0], v_final_norm_w.reshape(1, d)]
    shapes = [attn_norm_w.shape, w_in.shape, lb_logits.shape, sb_norm_w.shape, hg_norm_w.shape, w_out.shape,
              mlp_norm_w.shape, w_up.shape, w_down.shape, final_norm_w.shape]
    deltas, new_ms, new_vs = [], [], []
    for nm, w_, g_, m_, v_ in zip(names, ws, gs, ms, vs):
        dl, m2, v2 = _adamw(w_, g_, m_, v_, "adamw_" + nm)
        deltas.append(dl)
        new_ms.append(m2)
        new_vs.append(v2)

    def shaped(lst):
        return [a.reshape(s) for a, s in zip(lst, shapes)]

    return (loss, dx[None], *shaped(gs), *shaped(deltas), *shaped(new_ms), *shaped(new_vs))
```

```python
import functools

import jax
import jax.numpy as jnp
from jax import lax
from jax.experimental import pallas as pl
from jax.experimental.pallas import tpu as pltpu

F32 = jnp.float32
BF16 = jnp.bfloat16
MESH = pl.DeviceIdType.MESH

HEAD = 128
NORM_EPS = 1e-5
N_CHIPS = 4
ATTN_BLOCK = 256
ATTN_ROWS = 1024
HG_CHUNK = 32
HG_ROWS = 256
VMEM_LIMIT = 56 * 1024 * 1024

ADAM_LR = 0.001
ADAM_B1 = 0.9
ADAM_B2 = 0.999
ADAM_EPS = 1e-08
ADAM_WD = 0.01
ADAM_STEP = 10


def _pick(n, cands):
    for c in cands:
        if n % c == 0:
            return c
    return n


def _params(sem):
    return pltpu.CompilerParams(dimension_semantics=sem, vmem_limit_bytes=VMEM_LIMIT)


def _dot(a, b):
    return jnp.dot(a, b, preferred_element_type=F32)


def _dot_nt(a, b):
    return lax.dot_general(a, b, (((1,), (1,)), ((), ())), preferred_element_type=F32)


def _dot_tn(a, b):
    return lax.dot_general(a, b, (((0,), (0,)), ((), ())), preferred_element_type=F32)


def _hilo(x):
    hi = x.astype(BF16)
    return hi, (x - hi.astype(F32)).astype(BF16)


def _dot_split(tri, x):
    hi, lo = _hilo(x)
    return _dot(tri, hi) + _dot(tri, lo)


def _split_dot(x, tri):
    hi, lo = _hilo(x)
    return _dot(hi, tri) + _dot(lo, tri)


def _dot3(dot, a, b):
    return dot(a[0], b[0]) + (dot(a[0], b[1]) + dot(a[1], b[0]))


def _sigmoid(x):
    return 1.0 / (1.0 + jnp.exp(-x))


def _mm_body(kind, nk, n_extra, n_out, epi):
    dot = {"nn": _dot, "nt": _dot_nt, "tn": _dot_tn}[kind]

    def finish(acc, extra_refs, out_refs):
        res = epi(acc, *[e[...] for e in extra_refs]) if epi is not None else (acc,)
        for o, r in zip(out_refs, res):
            o[...] = r.astype(o.dtype)

    def body(a_ref, b_ref, *rest):
        extra_refs = rest[:n_extra]
        out_refs = rest[n_extra:n_extra + n_out]
        if nk == 1:
            finish(dot(a_ref[...], b_ref[...]), extra_refs, out_refs)
            return
        acc_ref = rest[n_extra + n_out]
        k = pl.program_id(2)

        @pl.when(k == 0)
        def _():
            acc_ref[...] = jnp.zeros_like(acc_ref)

        acc_ref[...] += dot(a_ref[...], b_ref[...])

        @pl.when(k == nk - 1)
        def _():
            finish(acc_ref[...], extra_refs, out_refs)

    return body


def _mm_nn(a, w, out_dtypes, *, tm, tn, tk, name, epi=None, extras=()):
    m, r = a.shape
    p, _, c = w.shape
    npc = c // tn
    nk = r // tk
    body = _mm_body("nn", nk, len(extras), len(out_dtypes), epi)
    tile = pl.BlockSpec((tm, tn), lambda i, j, k: (i, j))
    return pl.pallas_call(
        body, name=name,
        grid=(m // tm, p * npc, nk),
        in_specs=[pl.BlockSpec((tm, tk), lambda i, j, k: (i, k)),
                  pl.BlockSpec((None, tk, tn), lambda i, j, k: (j // npc, k, j % npc))] + [tile] * len(extras),
        out_specs=[tile] * len(out_dtypes),
        out_shape=[jax.ShapeDtypeStruct((m, p * c), d) for d in out_dtypes],
        scratch_shapes=[pltpu.VMEM((tm, tn), F32)] if nk > 1 else [],
        compiler_params=_params(("parallel", "parallel", "arbitrary")),
    )(a, w, *extras)


def _mm_nt(a, w, out_dtypes, *, tm, tn, tk, name, epi=None, extras=()):
    m, _ = a.shape
    p, r, c = w.shape
    kpc = c // tk
    nk = p * kpc
    body = _mm_body("nt", nk, len(extras), len(out_dtypes), epi)
    tile = pl.BlockSpec((tm, tn), lambda i, j, k: (i, j))
    return pl.pallas_call(
        body, name=name,
        grid=(m // tm, r // tn, nk),
        in_specs=[pl.BlockSpec((tm, tk), lambda i, j, k: (i, k)),
                  pl.BlockSpec((None, tn, tk), lambda i, j, k: (k // kpc, j, k % kpc))] + [tile] * len(extras),
        out_specs=[tile] * len(out_dtypes),
        out_shape=[jax.ShapeDtypeStruct((m, r), d) for d in out_dtypes],
        scratch_shapes=[pltpu.VMEM((tm, tn), F32)] if nk > 1 else [],
        compiler_params=_params(("parallel", "parallel", "arbitrary")),
    )(a, w, *extras)


def _mm_tn(a, g, p, *, tm, tn, tk, name):
    t, r = a.shape
    c = g.shape[1] // p
    npc = c // tn
    nk = t // tk
    body = _mm_body("tn", nk, 0, 1, None)
    return pl.pallas_call(
        body, name=name,
        grid=(r // tm, p * npc, nk),
        in_specs=[pl.BlockSpec((tk, tm), lambda i, j, k: (k, i)),
                  pl.BlockSpec((tk, tn), lambda i, j, k: (k, j))],
        out_specs=[pl.BlockSpec((None, tm, tn), lambda i, j, k: (j // npc, i, j % npc))],
        out_shape=[jax.ShapeDtypeStruct((p, r, c), F32)],
        scratch_shapes=[pltpu.VMEM((tm, tn), F32)] if nk > 1 else [],
        compiler_params=_params(("parallel", "parallel", "arbitrary")),
    )(a, g)[0]


def _rows(fn, row_ins, const_ins, row_outs, acc_outs, *, tm, name):
    specs, arrays = [], []
    t = None
    for item in row_ins:
        if isinstance(item, tuple):
            arr, width, cb = item
            specs.append(pl.BlockSpec((tm, width), functools.partial(lambda i, cb: (i, cb), cb=cb)))
        else:
            arr = item
            specs.append(pl.BlockSpec((tm, arr.shape[1]), lambda i: (i, 0)))
        arrays.append(arr)
        t = arr.shape[0]
    for arr in const_ins:
        specs.append(pl.BlockSpec(arr.shape, lambda i: (0, 0)))
        arrays.append(arr)
    n_in, n_row, n_acc = len(arrays), len(row_outs), len(acc_outs)

    def body(*refs):
        ins = [r[...] for r in refs[:n_in]]
        outs = refs[n_in:]
        row_res, acc_res = fn(ins[:len(row_ins)], ins[len(row_ins):])
        for o, r in zip(outs[:n_row], row_res):
            o[...] = r.astype(o.dtype)
        if n_acc:
            i = pl.program_id(0)

            @pl.when(i == 0)
            def _():
                for o in outs[n_row:]:
                    o[...] = jnp.zeros_like(o)

            for o, r in zip(outs[n_row:], acc_res):
                o[...] += r

    res = pl.pallas_call(
        body, name=name,
        grid=(t // tm,),
        in_specs=specs,
        out_specs=[pl.BlockSpec((tm, c), lambda i: (i, 0)) for c, _ in row_outs]
                  + [pl.BlockSpec((1, c), lambda i: (0, 0)) for c in acc_outs],
        out_shape=[jax.ShapeDtypeStruct((t, c), d) for c, d in row_outs]
                  + [jax.ShapeDtypeStruct((1, c), F32) for c in acc_outs],
        compiler_params=_params(("arbitrary",)),
    )(*arrays)
    return res[:n_row], res[n_row:]


def _rstd(x):
    return lax.rsqrt(jnp.mean(x * x, axis=-1, keepdims=True) + NORM_EPS)


def _rms_bwd(x, w, dy):
    r = _rstd(x)
    n = x * r
    dn = dy * w
    dx = r * (dn - n * jnp.mean(dn * n, axis=-1, keepdims=True))
    return dx, dy * n


def _colsum(x):
    return jnp.sum(x, axis=0, keepdims=True)


def _heads_map(fn, width, *tiles):
    outs = None
    for h in range(width // HEAD):
        res = fn(*[t[:, h * HEAD:(h + 1) * HEAD] for t in tiles])
        if outs is None:
            outs = [[] for _ in res]
        for lst, r in zip(outs, res):
            lst.append(r)
    return [jnp.concatenate(lst, axis=1) for lst in outs]


def _log_one_minus_beta(z):
    return -(jnp.maximum(z, 0.0) + jnp.log(1.0 + jnp.exp(-jnp.abs(z))))


def _attn_fwd(proj, after_tri, norm_w, n_heads, name, slots=()):
    t = proj.shape[0]
    blk = min(ATTN_BLOCK, t)
    qb = min(ATTN_ROWS, t)
    ns = qb // blk
    nq = t // qb
    n = len(slots)
    scale = HEAD ** -0.5

    def body(q_ref, k_ref, v_ref, tri_ref, w_ref, *rest):
        o_ref, mix_ref, tot_ref = rest[n:n + 3]
        i = pl.program_id(1)
        if n:
            begin, end = _gather_copies(rest[n + 3:2 * n + 3], *rest[2 * n + 3:])
            pl.when((pl.program_id(0) == 0) & (i == 0))(begin)
        q = (q_ref[...] * scale).astype(BF16)
        tri = tri_ref[...]

        def part(r0, j, acc_l, acc_o, masked):
            sl = pl.ds(pl.multiple_of(j * blk, blk), blk)
            m = qb - r0
            z = _dot_nt(q[r0:, :], k_ref[sl, :].astype(BF16))
            lm = _log_one_minus_beta(z)
            if masked:
                mask = lax.broadcasted_iota(jnp.int32, (m, blk), 1) < lax.broadcasted_iota(jnp.int32, (m, blk), 0)
                lmm = jnp.where(mask, lm, 0.0)
            else:
                lmm = lm
            w = jnp.exp(z + lm + acc_l[r0:, :] + _split_dot(lmm, tri))
            if masked:
                w = jnp.where(mask, w, 0.0)
            new_l = acc_l[r0:, :] + jnp.sum(lmm, axis=1, keepdims=True)
            new_o = acc_o[r0:, :] + _dot(w.astype(BF16), v_ref[sl, :].astype(BF16))
            if r0:
                new_l = jnp.concatenate([acc_l[:r0, :], new_l], axis=0)
                new_o = jnp.concatenate([acc_o[:r0, :], new_o], axis=0)
            return new_l, new_o

        acc = (jnp.zeros((qb, 1), F32), jnp.zeros((qb, HEAD), F32))
        for jr in reversed(range(ns)):
            acc = part(jr * blk, ns * i + jr, *acc, True)
        acc_l, acc_o = lax.fori_loop(0, ns * i, lambda jj, c: part(0, ns * i - 1 - jj, *c, False), acc)
        o_ref[...] = acc_o
        mix_ref[...] = (acc_o * _rstd(acc_o) * w_ref[...]).astype(BF16)
        tot_ref[...] = jnp.broadcast_to(acc_l, (qb, HEAD))
        if n:
            pl.when((pl.program_id(0) == n_heads - 1) & (i == nq - 1))(end)

    width = n_heads * HEAD
    qblk = pl.BlockSpec((qb, HEAD), lambda h, i: (i, h))
    return pl.pallas_call(
        body, name=name,
        grid=(n_heads, nq),
        in_specs=[qblk,
                  pl.BlockSpec((t, HEAD), lambda h, i: (0, n_heads + h)),
                  pl.BlockSpec((t, HEAD), lambda h, i: (0, 2 * n_heads + h)),
                  pl.BlockSpec((blk, blk), lambda h, i: (0, 0)),
                  pl.BlockSpec((1, HEAD), lambda h, i: (0, 0))] + [ANY] * n,
        out_specs=[qblk, qblk, qblk] + [ANY] * n,
        out_shape=[jax.ShapeDtypeStruct((t, width), F32), jax.ShapeDtypeStruct((t, width), BF16),
                   jax.ShapeDtypeStruct((t, width), F32)] + [jax.ShapeDtypeStruct(s.shape, s.dtype) for s in slots],
        input_output_aliases={5 + w: 3 + w for w in range(n)},
        scratch_shapes=[pltpu.SemaphoreType.DMA((6 * n,))] * 2 if n else [],
        compiler_params=_params(("arbitrary", "arbitrary")),
    )(proj, proj, proj, after_tri, norm_w, *slots)


def _attn_bwd(proj, tot, do, upto_tri, before_tri, n_heads, name, sums=()):
    t = proj.shape[0]
    blk = min(ATTN_BLOCK, t)
    qb = min(ATTN_ROWS, t)
    ns = qb // blk
    nq = t // qb
    n = len(sums)
    scale = HEAD ** -0.5

    def body(q_ref, k_ref, v_ref, tot_ref, do_ref, upto_ref, before_ref, *rest):
        dq_ref, dk_ref, dv_ref = rest[n:n + 3]
        i = pl.program_id(1)
        if n:
            begin, end = _scatter_copies(rest[:n], rest[n + 3:2 * n + 3], *rest[2 * n + 3:])
            pl.when((pl.program_id(0) == 0) & (i == 0))(begin)

        @pl.when(i == 0)
        def _():
            dk_ref[...] = jnp.zeros_like(dk_ref)
            dv_ref[...] = jnp.zeros_like(dv_ref)

        q = (q_ref[...] * scale).astype(BF16)
        dob = do_ref[...].astype(BF16)
        total = tot_ref[:, 0:1]
        upto = upto_ref[...]
        before = before_ref[...]

        def part(r0, j, seen_l, seen_g, dq, masked):
            sl = pl.ds(pl.multiple_of(j * blk, blk), blk)
            m = qb - r0
            qq, dd = q[r0:, :], dob[r0:, :]
            kb = k_ref[sl, :].astype(BF16)
            z = _dot_nt(qq, kb)
            lm = _log_one_minus_beta(z)
            if masked:
                mask = lax.broadcasted_iota(jnp.int32, (m, blk), 1) < lax.broadcasted_iota(jnp.int32, (m, blk), 0)
                lmm = jnp.where(mask, lm, 0.0)
            else:
                lmm = lm
            after = (total[r0:, :] - seen_l[r0:, :]) - _split_dot(lmm, upto)
            w = jnp.exp(z + lm + after)
            if masked:
                w = jnp.where(mask, w, 0.0)
            sig = jnp.exp(z + lm)
            g = w * _dot_nt(dd, v_ref[sl, :].astype(BF16))
            g_before = seen_g[r0:, :] + _split_dot(g, before)
            dz = g * (1.0 - sig) - g_before * sig
            if masked:
                dz = jnp.where(mask, dz, 0.0)
            dzb = dz.astype(BF16)
            dk_ref[sl, :] += _dot_tn(dzb, qq)
            dv_ref[sl, :] += _dot_tn(w.astype(BF16), dd)
            new = (seen_l[r0:, :] + jnp.sum(lmm, axis=1, keepdims=True),
                   seen_g[r0:, :] + jnp.sum(g, axis=1, keepdims=True), dq[r0:, :] + _dot(dzb, kb))
            if r0:
                new = tuple(jnp.concatenate([old[:r0, :], n], axis=0) for old, n in zip((seen_l, seen_g, dq), new))
            return new

        zero = jnp.zeros((qb, 1), F32)
        carry = lax.fori_loop(0, ns * i, lambda j, c: part(0, j, *c, False), (zero, zero, jnp.zeros((qb, HEAD), F32)))
        for jr in range(ns):
            carry = part(jr * blk, ns * i + jr, *carry, True)
        dq_ref[...] = carry[2] * scale
        if n:
            pl.when((pl.program_id(0) == n_heads - 1) & (i == nq - 1))(end)

    width = n_heads * HEAD
    qblk = pl.BlockSpec((qb, HEAD), lambda h, i: (i, h))
    full = pl.BlockSpec((t, HEAD), lambda h, i: (0, h))
    tri = pl.BlockSpec((blk, blk), lambda h, i: (0, 0))
    return pl.pallas_call(
        body, name=name,
        grid=(n_heads, nq),
        in_specs=[qblk,
                  pl.BlockSpec((t, HEAD), lambda h, i: (0, n_heads + h)),
                  pl.BlockSpec((t, HEAD), lambda h, i: (0, 2 * n_heads + h)),
                  qblk, qblk, tri, tri] + [ANY] * n,
        out_specs=[qblk, full, full] + [ANY] * n,
        out_shape=[jax.ShapeDtypeStruct((t, width), F32)] * 3
                  + [jax.ShapeDtypeStruct((3,) + s.shape[1:], s.dtype) for s in sums],
        scratch_shapes=[pltpu.SemaphoreType.DMA((3 * n,))] * 2 if n else [],
        compiler_params=_params(("arbitrary", "arbitrary")),
    )(proj, proj, proj, tot, do, upto_tri, before_tri, *sums)


def _lower_bound(logits):
    l0, l1 = logits[0:1, :], logits[1:2, :]
    mx = jnp.maximum(l0, l1)
    e0, e1 = jnp.exp(l0 - mx), jnp.exp(l1 - mx)
    return e0 / (e0 + e1)


def _hg_chunk(qc, kc, gc, tri_lo):
    c = qc.shape[0]
    cum = _dot_split(tri_lo, gc)
    mid = cum[c // 2 - 1:c // 2, :]
    last = cum[c - 1:c, :]
    qt = qc * jnp.exp(cum - mid)
    kt = kc * jnp.exp(mid - cum)
    qe = qc * jnp.exp(cum)
    kd = kc * jnp.exp(last - cum)
    return cum, mid, last, qt, kt, qe, kd


def _hgrn_fwd(proj, lb_logits, norm_w, tri_lo, n_heads, heads_per_step, name):
    t = proj.shape[0]
    bt = min(HG_ROWS, t)
    c = HG_CHUNK
    nc = bt // c
    hw = heads_per_step * HEAD
    width = n_heads * HEAD
    col0 = 3 * width // hw

    def body(hq_ref, hf_ref, hi_ref, hgate_ref, lbl_ref, w_ref, tri_ref, o_ref, mix_ref, st_ref,
             state, q_scr, k_scr, g_scr):
        @pl.when(pl.program_id(1) == 0)
        def _():
            state[...] = jnp.zeros_like(state)

        lb = _lower_bound(lbl_ref[...])
        f = hf_ref[...]
        g_scr[...] = jnp.log(lb + (1.0 - lb) * _sigmoid(f))
        k_scr[...] = (1.0 - lb) * _sigmoid(-f)
        hq = hq_ref[...]
        q_scr[...] = hq * _sigmoid(hq)
        tri = tri_ref[...]
        causal = lax.broadcasted_iota(jnp.int32, (c, c), 1) <= lax.broadcasted_iota(jnp.int32, (c, c), 0)

        def chunk(ci, carry):
            r = pl.ds(pl.multiple_of(ci * c, c), c)
            vc = hi_ref[r, :].astype(BF16)
            _, _, last, qt, kt, qe, kd = _hg_chunk(q_scr[r, :], k_scr[r, :], g_scr[r, :], tri)
            qt, kt, qe, kd = qt.astype(BF16), kt.astype(BF16), qe.astype(BF16), kd.astype(BF16)
            e_last = jnp.exp(last)
            old = [state[h] for h in range(heads_per_step)]
            outs, new = [], []
            for h in range(heads_per_step):
                cs = slice(h * HEAD, (h + 1) * HEAD)
                a = jnp.where(causal, _dot_nt(qt[:, cs], kt[:, cs]), 0.0)
                outs.append(_dot(a.astype(BF16), vc[:, cs]) + _dot_nt(qe[:, cs], old[h].astype(BF16)))
                new.append(old[h] * e_last[:, cs] + _dot_tn(vc[:, cs], kd[:, cs]))
            for h in range(heads_per_step):
                st_ref[ci, :, h * HEAD:(h + 1) * HEAD] = old[h]
                state[h] = new[h]
            o_ref[r, :] = jnp.concatenate(outs, axis=1)
            return carry

        lax.fori_loop(0, nc, chunk, 0)

        def finish(o, gate):
            return ((o * _rstd(o) * w_ref[...]) * (gate * _sigmoid(gate)),)

        mix_ref[...] = _heads_map(finish, hw, o_ref[...], hgate_ref[...])[0].astype(BF16)

    def col(group):
        return pl.BlockSpec((bt, hw), functools.partial(lambda hp, tb, g: (tb, col0 + g * (width // hw) + hp), g=group))

    blk = pl.BlockSpec((bt, hw), lambda hp, tb: (tb, hp))
    return pl.pallas_call(
        body, name=name,
        grid=(n_heads // heads_per_step, t // bt),
        in_specs=[col(0), col(1), col(2), col(3),
                  pl.BlockSpec((2, hw), lambda hp, tb: (0, hp)),
                  pl.BlockSpec((1, HEAD), lambda hp, tb: (0, 0)),
                  pl.BlockSpec((c, c), lambda hp, tb: (0, 0))],
        out_specs=[blk, blk, pl.BlockSpec((nc, HEAD, hw), lambda hp, tb: (tb, 0, hp))],
        out_shape=[jax.ShapeDtypeStruct((t, width), F32), jax.ShapeDtypeStruct((t, width), BF16),
                   jax.ShapeDtypeStruct((t // c, HEAD, width), F32)],
        scratch_shapes=[pltpu.VMEM((heads_per_step, HEAD, HEAD), F32)] + [pltpu.VMEM((bt, hw), F32)] * 3,
        compiler_params=_params(("parallel", "arbitrary")),
    )(proj, proj, proj, proj, lb_logits, norm_w, tri_lo)


def _hgrn_bwd(proj, do, states, lb_logits, tri_lo, tri_up, n_heads, heads_per_step, name):
    t = proj.shape[0]
    bt = min(HG_ROWS, t)
    c = HG_CHUNK
    nc = bt // c
    nb = t // bt
    hw = heads_per_step * HEAD
    width = n_heads * HEAD
    col0 = 3 * width // hw

    def body(hq_ref, hf_ref, hi_ref, do_ref, st_ref, lbl_ref, lo_ref, up_ref, dq_ref, df_ref, di_ref, dlb_ref,
             dstate, q_scr, k_scr, g_scr, dk_scr, dg_scr):
        @pl.when(pl.program_id(1) == 0)
        def _():
            dstate[...] = jnp.zeros_like(dstate)
            dlb_ref[...] = jnp.zeros_like(dlb_ref)

        lb = _lower_bound(lbl_ref[...])
        f = hf_ref[...]
        sg = _sigmoid(f)
        sgn = _sigmoid(-f)
        den = lb + (1.0 - lb) * sg
        g_scr[...] = jnp.log(den)
        k_scr[...] = (1.0 - lb) * sgn
        hq = hq_ref[...]
        sq = _sigmoid(hq)
        q_scr[...] = hq * sq
        tri_lo_v = lo_ref[...]
        tri_up_v = up_ref[...]
        causal = lax.broadcasted_iota(jnp.int32, (c, c), 1) <= lax.broadcasted_iota(jnp.int32, (c, c), 0)
        last_row = lax.broadcasted_iota(jnp.int32, (c, hw), 0) == c - 1

        def chunk(cc, carry):
            ci = nc - 1 - cc
            r = pl.ds(pl.multiple_of(ci * c, c), c)
            qc, kc = q_scr[r, :], k_scr[r, :]
            cum, mid, last, qt, kt, qe, kd = _hg_chunk(qc, kc, g_scr[r, :], tri_lo_v)
            qt, kt, qe, kd, doc, vc = [_hilo(v) for v in (qt, kt, qe, kd, do_ref[r, :], hi_ref[r, :])]
            e_last = jnp.exp(last)
            sts = [st_ref[ci, :, h * HEAD:(h + 1) * HEAD] for h in range(heads_per_step)]
            dsts = [dstate[h] for h in range(heads_per_step)]
            di, dq_inter, dk_inter, dq_intra, dk_intra, st_sums, new = [], [], [], [], [], [], []
            for h in range(heads_per_step):
                cs = slice(h * HEAD, (h + 1) * HEAD)

                def head(pair):
                    return pair[0][:, cs], pair[1][:, cs]

                st, dst = _hilo(sts[h]), _hilo(dsts[h])
                a = _hilo(jnp.where(causal, _dot3(_dot_nt, head(qt), head(kt)), 0.0))
                da = _hilo(jnp.where(causal, _dot3(_dot_nt, head(doc), head(vc)), 0.0))
                di.append(_dot3(_dot_tn, a, head(doc)) + _dot3(_dot_nt, head(kd), dst))
                dq_inter.append(_dot3(_dot, head(doc), st))
                dk_inter.append(_dot3(_dot, head(vc), dst))
                dq_intra.append(_dot3(_dot, da, head(kt)))
                dk_intra.append(_dot3(_dot_tn, da, head(qt)))
                st_sums.append(_colsum(dsts[h] * sts[h]))
                new.append(dsts[h] * e_last[:, cs] + _dot3(_dot_tn, head(doc), head(qe)))

            def wide(parts):
                return jnp.concatenate(parts, axis=1)

            dq_inter = wide(dq_inter) * jnp.exp(cum)
            dk_inter = wide(dk_inter) * jnp.exp(last - cum)
            dq = wide(dq_intra) * jnp.exp(cum - mid) + dq_inter
            dk = wide(dk_intra) * jnp.exp(mid - cum) + dk_inter
            d_last = _colsum(kc * dk_inter) + e_last * wide(st_sums)
            dcum = qc * dq - kc * dk + jnp.where(last_row, d_last, 0.0)
            for h in range(heads_per_step):
                dstate[h] = new[h]
            di_ref[r, :] = wide(di)
            dq_ref[r, :] = dq
            dk_scr[r, :] = dk
            dg_scr[r, :] = _dot_split(tri_up_v, dcum)
            return carry

        lax.fori_loop(0, nc, chunk, 0)

        e = (dg_scr[...] / den - dk_scr[...]) * sgn
        df_ref[...] = e * (1.0 - lb) * sg
        dlb_ref[...] += _colsum(e)
        dq_ref[...] = dq_ref[...] * (sq * (1.0 + hq * (1.0 - sq)))

    def col(group):
        return pl.BlockSpec((bt, hw), functools.partial(
            lambda hp, tb, g: (nb - 1 - tb, col0 + g * (width // hw) + hp), g=group))

    blk = pl.BlockSpec((bt, hw), lambda hp, tb: (nb - 1 - tb, hp))
    tri = pl.BlockSpec((c, c), lambda hp, tb: (0, 0))
    return pl.pallas_call(
        body, name=name,
        grid=(n_heads // heads_per_step, nb),
        in_specs=[col(0), col(1), col(2), blk,
                  pl.BlockSpec((nc, HEAD, hw), lambda hp, tb: (nb - 1 - tb, 0, hp)),
                  pl.BlockSpec((2, hw), lambda hp, tb: (0, hp)), tri, tri],
        out_specs=[blk, blk, blk, pl.BlockSpec((1, hw), lambda hp, tb: (0, hp))],
        out_shape=[jax.ShapeDtypeStruct((t, width), F32)] * 3 + [jax.ShapeDtypeStruct((1, width), F32)],
        scratch_shapes=[pltpu.VMEM((heads_per_step, HEAD, HEAD), F32)] + [pltpu.VMEM((bt, hw), F32)] * 5,
        compiler_params=_params(("parallel", "arbitrary")),
    )(proj, proj, proj, do, states, lb_logits, tri_lo, tri_up)


def _place():
    x, y, c = lax.axis_index("x"), lax.axis_index("y"), lax.axis_index("c")
    chips = [(1 - x, y), (x, 1 - y), (1 - x, 1 - y)]
    return x, y, c, chips


ANY = pl.BlockSpec(memory_space=pl.ANY)


def _cast_to_slot(shard, ids, name):
    r, c = shard.shape
    tm = _pick(r, (256, 128, 64, 32, 16))

    def body(ids_ref, s_ref, o_ref):
        o_ref[...] = s_ref[...].astype(BF16)

    return pl.pallas_call(
        body, name=name,
        grid_spec=pltpu.PrefetchScalarGridSpec(
            num_scalar_prefetch=1, grid=(r // tm,),
            in_specs=[pl.BlockSpec((tm, c), lambda i, ids: (i, 0))],
            out_specs=pl.BlockSpec((None, tm, c), lambda i, ids: (ids[1], i, 0))),
        out_shape=jax.ShapeDtypeStruct((N_CHIPS, r, c), BF16),
        compiler_params=_params(("parallel",)),
    )(ids, shard)


def _gather_copies(bufs, send, recv):
    x, y, c, chips = _place()
    mine = 2 * x + y

    def half(ref, who, core):
        h = ref.shape[-2] // 2
        return ref.at[who, pl.ds(core * h, h), :]

    def copy(w, k, rows, to):
        return pltpu.make_async_remote_copy(src_ref=rows, dst_ref=rows, send_sem=send.at[6 * w + k],
                                            recv_sem=recv.at[6 * w + k], device_id=to, device_id_type=MESH)

    def to_chips(w):
        return [copy(w, k, half(bufs[w], mine, c), (qx, qy, c)) for k, (qx, qy) in enumerate(chips)]

    def to_sibling(w):
        return [copy(w, 3 + k, half(bufs[w], 2 * qx + qy, c), (x, y, 1 - c)) for k, (qx, qy) in enumerate(chips)]

    def begin():
        for w in range(len(bufs)):
            for cp in to_chips(w):
                cp.start()

    def end():
        for w in range(len(bufs)):
            for k, (qx, qy) in enumerate(chips):
                copy(w, k, half(bufs[w], 2 * qx + qy, c), (x, y, c)).wait_recv()
                to_sibling(w)[k].start()
        for w in range(len(bufs)):
            for k, (qx, qy) in enumerate(chips):
                copy(w, 3 + k, half(bufs[w], 2 * qx + qy, 1 - c), (x, y, c)).wait_recv()
        for w in range(len(bufs)):
            for cp in to_chips(w) + to_sibling(w):
                cp.wait_send()

    return begin, end


def _scatter_copies(ins, outs, send, recv):
    _, _, c, chips = _place()

    def copies():
        return [pltpu.make_async_remote_copy(
            src_ref=ins[w].at[2 * qx + qy], dst_ref=outs[w].at[k], send_sem=send.at[3 * w + k],
            recv_sem=recv.at[3 * w + k], device_id=(qx, qy, c), device_id_type=MESH)
            for w in range(len(ins)) for k, (qx, qy) in enumerate(chips)]

    def begin():
        for cp in copies():
            cp.start()

    def end():
        for cp in copies():
            cp.wait()

    return begin, end


def _gather_weights(slots, name):
    n = len(slots)

    def body(*refs):
        begin, end = _gather_copies(refs[n:2 * n], *refs[2 * n:])
        begin()
        end()

    return pl.pallas_call(
        body, name=name,
        in_specs=[ANY] * n, out_specs=[ANY] * n,
        out_shape=[jax.ShapeDtypeStruct(s.shape, s.dtype) for s in slots],
        input_output_aliases={w: w for w in range(n)},
        scratch_shapes=[pltpu.SemaphoreType.DMA((6 * n,)), pltpu.SemaphoreType.DMA((6 * n,))],
    )(*slots)


def _sibling_swap(grads, name):
    n = len(grads)

    def body(*refs):
        ins, outs = refs[:n], refs[n:2 * n]
        send, recv = refs[2 * n:]
        x, y, c, _ = _place()
        cps = []
        for w in range(n):
            h = ins[w].shape[1] // 2
            cps.append(pltpu.make_async_remote_copy(
                src_ref=ins[w].at[:, pl.ds((1 - c) * h, h), :], dst_ref=outs[w], send_sem=send.at[w],
                recv_sem=recv.at[w], device_id=(x, y, 1 - c), device_id_type=MESH))
            cps[-1].start()
        for cp in cps:
            cp.wait()

    return pl.pallas_call(
        body, name=name, in_specs=[ANY] * n, out_specs=[ANY] * n,
        out_shape=[jax.ShapeDtypeStruct((g.shape[0], g.shape[1] // 2, g.shape[2]), g.dtype) for g in grads],
        scratch_shapes=[pltpu.SemaphoreType.DMA((n,)), pltpu.SemaphoreType.DMA((n,))],
    )(*grads)


def _chip_scatter(sums, name):
    n = len(sums)

    def body(*refs):
        begin, end = _scatter_copies(refs[:n], refs[n:2 * n], *refs[2 * n:])
        begin()
        end()

    return pl.pallas_call(
        body, name=name, in_specs=[ANY] * n, out_specs=[ANY] * n,
        out_shape=[jax.ShapeDtypeStruct((3,) + s.shape[1:], s.dtype) for s in sums],
        scratch_shapes=[pltpu.SemaphoreType.DMA((3 * n,)), pltpu.SemaphoreType.DMA((3 * n,))],
    )(*sums)


def _sibling_join(fulls, name):
    n = len(fulls)

    def body(*refs):
        bufs = refs[n:2 * n]
        send, recv = refs[2 * n:]
        x, y, c, _ = _place()
        cps = []
        for w in range(n):
            h = bufs[w].shape[0] // 2
            rows = bufs[w].at[pl.ds(c * h, h), :]
            cps.append(pltpu.make_async_remote_copy(
                src_ref=rows, dst_ref=rows, send_sem=send.at[w], recv_sem=recv.at[w],
                device_id=(x, y, 1 - c), device_id_type=MESH))
            cps[-1].start()
        for w in range(n):
            h = bufs[w].shape[0] // 2
            theirs = bufs[w].at[pl.ds((1 - c) * h, h), :]
            pltpu.make_async_remote_copy(src_ref=theirs, dst_ref=theirs, send_sem=send.at[w], recv_sem=recv.at[w],
                                         device_id=(x, y, c), device_id_type=MESH).wait_recv()
        for cp in cps:
            cp.wait_send()

    return pl.pallas_call(
        body, name=name, in_specs=[ANY] * n, out_specs=[ANY] * n,
        out_shape=[jax.ShapeDtypeStruct(s.shape, s.dtype) for s in fulls],
        input_output_aliases={w: w for w in range(n)},
        scratch_shapes=[pltpu.SemaphoreType.DMA((n,)), pltpu.SemaphoreType.DMA((n,))],
    )(*fulls)


def _all_sum_small(vec, name):
    n = vec.shape[1]

    def body(v_ref, out_ref, buf, send, recv):
        x, y, c, _ = _place()
        me = 4 * x + 2 * y + c
        buf[me] = v_ref[...]
        peers = []
        for mask in range(1, 8):
            px = 1 - x if mask & 4 else x
            py = 1 - y if mask & 2 else y
            pc = 1 - c if mask & 1 else c
            peers.append((px, py, pc))
        cps = []
        for k, peer in enumerate(peers):
            cps.append(pltpu.make_async_remote_copy(src_ref=buf.at[me], dst_ref=buf.at[me], send_sem=send.at[k],
                                                    recv_sem=recv.at[k], device_id=peer, device_id_type=MESH))
            cps[-1].start()
        for k, (px, py, pc) in enumerate(peers):
            slot = buf.at[4 * px + 2 * py + pc]
            pltpu.make_async_remote_copy(src_ref=slot, dst_ref=slot, send_sem=send.at[k], recv_sem=recv.at[k],
                                         device_id=(x, y, c), device_id_type=MESH).wait_recv()
        for cp in cps:
            cp.wait_send()
        total = buf[0]
        for d in range(1, 8):
            total = total + buf[d]
        out_ref[...] = total

    vm = pl.BlockSpec(memory_space=pltpu.VMEM)
    return pl.pallas_call(
        body, name=name, in_specs=[vm], out_specs=vm,
        out_shape=jax.ShapeDtypeStruct(vec.shape, F32),
        scratch_shapes=[pltpu.VMEM((8, 8, n), F32), pltpu.SemaphoreType.DMA((7,)), pltpu.SemaphoreType.DMA((7,))],
    )(vec)


def _pair_sum(g, buf, ids, name):
    p, r, c = g.shape
    h = r // 2
    tr = _pick(h, (256, 128, 64, 32, 16))
    nh = h // tr

    def body(ids_ref, g_ref, b_ref, sums_ref, own_ref):
        s = g_ref[...] + b_ref[...]
        sums_ref[...] = s.astype(BF16)

        @pl.when(pl.program_id(1) == ids_ref[1])
        def _():
            own_ref[...] = s

    return pl.pallas_call(
        body, name=name,
        grid_spec=pltpu.PrefetchScalarGridSpec(
            num_scalar_prefetch=1, grid=(nh, p),
            in_specs=[pl.BlockSpec((None, tr, c), lambda i, q, ids: (q, ids[0] * nh + i, 0)),
                      pl.BlockSpec((None, tr, c), lambda i, q, ids: (q, i, 0))],
            out_specs=[pl.BlockSpec((None, tr, c), lambda i, q, ids: (q, i, 0)),
                       pl.BlockSpec((tr, c), lambda i, q, ids: (i, 0))]),
        out_shape=[jax.ShapeDtypeStruct((p, h, c), BF16), jax.ShapeDtypeStruct((h, c), F32)],
        compiler_params=_params(("parallel", "arbitrary")),
    )(ids, g, buf)


def _final_sum(own, others, ids, name):
    h, c = own.shape
    tr = _pick(h, (256, 128, 64, 32, 16))
    nh = h // tr

    def body(ids_ref, own_ref, oth_ref, out_ref):
        s = own_ref[...]
        for k in range(3):
            s = s + oth_ref[k].astype(F32)
        out_ref[...] = s

    return pl.pallas_call(
        body, name=name,
        grid_spec=pltpu.PrefetchScalarGridSpec(
            num_scalar_prefetch=1, grid=(nh,),
            in_specs=[pl.BlockSpec((tr, c), lambda i, ids: (i, 0)),
                      pl.BlockSpec((3, tr, c), lambda i, ids: (0, i, 0))],
            out_specs=pl.BlockSpec((tr, c), lambda i, ids: (ids[0] * nh + i, 0))),
        out_shape=jax.ShapeDtypeStruct((2 * h, c), F32),
        compiler_params=_params(("parallel",)),
    )(ids, own, others)


def _adamw(w, g, m, v, name):
    r, c = w.shape
    tm = _pick(r, (256, 128, 64, 32, 16, 8)) if r >= 8 else r

    def fn(rows, _):
        w_, g_, m_, v_ = rows
        m2 = ADAM_B1 * m_ + (1.0 - ADAM_B1) * g_
        v2 = ADAM_B2 * v_ + (1.0 - ADAM_B2) * (g_ * g_)
        m_hat = m2 / (1.0 - ADAM_B1 ** ADAM_STEP)
        v_hat = v2 / (1.0 - ADAM_B2 ** ADAM_STEP)
        delta = -ADAM_LR * (m_hat / (jnp.sqrt(v_hat) + ADAM_EPS) + ADAM_WD * w_)
        return [delta, m2, v2], []

    outs, _ = _rows(fn, [w, g, m, v], [], [(c, F32)] * 3, [], tm=tm, name=name)
    return outs


def kernel(x, attn_norm_w, w_in, lb_logits, sb_norm_w, hg_norm_w, w_out, mlp_norm_w, w_up, w_down, final_norm_w, loss_target, m_attn_norm_w, m_w_in, m_lb_logits, m_sb_norm_w, m_hg_norm_w, m_w_out, m_mlp_norm_w, m_w_up, m_w_down, m_final_norm_w, v_attn_norm_w, v_w_in, v_lb_logits, v_sb_norm_w, v_hg_norm_w, v_w_out, v_mlp_norm_w, v_w_up, v_w_down, v_final_norm_w):
    xs, tgt = x[0], loss_target[0]
    t, d = xs.shape
    width = d // 2
    n_heads = width // HEAD
    hps = min(8, n_heads)
    final_w = final_norm_w.reshape(1, d)
    tm_rows = _pick(t, (256, 128))
    tm = _pick(t, (1024, 512, 256))
    blk = min(ATTN_BLOCK, t)
    ones_a = jnp.ones((blk, blk), F32)
    after_tri = jnp.tril(ones_a, -1).astype(BF16)
    upto_tri = jnp.triu(ones_a).astype(BF16)
    before_tri = jnp.triu(ones_a, 1).astype(BF16)
    ones_c = jnp.ones((HG_CHUNK, HG_CHUNK), F32)
    tri_lo, tri_up = jnp.tril(ones_c).astype(BF16), jnp.triu(ones_c).astype(BF16)
    cx, cy, cc = lax.axis_index("x"), lax.axis_index("y"), lax.axis_index("c")
    ids = jnp.stack([cc, 2 * cx + cy]).astype(jnp.int32)

    shards = [w_in[0], w_out[0], w_up[0], w_down[0]]
    cast = [_cast_to_slot(s, ids, f"cast_w{i}") for i, s in enumerate(shards)]
    (g_in,) = _gather_weights(cast[:1], "gather_w_in")
    d_ff = N_CHIPS * w_up.shape[2]
    cs_in, cs_up = g_in.shape[2], w_up.shape[2]
    tn_in = _pick(cs_in, (1792, 896, 512, 256, 128))
    tn_up = _pick(cs_up, (1024, 512, 256))
    tn_d = _pick(d, (1024, 512, 256))
    tk_d = _pick(d, (2048, 1024, 512))

    (u,), _ = _rows(lambda r, c_: ([r[0] * _rstd(r[0]) * c_[0]], []), [xs], [attn_norm_w], [(d, BF16)], [],
                    tm=tm_rows, name="norm_in")
    (proj,) = _mm_nn(u, g_in, [F32], tm=_pick(t, (512, 256)), tn=tn_in, tk=tk_d, name="proj_in")
    o_a, mix_a, sb_tot, g_out, g_up, g_down = _attn_fwd(proj, after_tri, sb_norm_w, n_heads, "sb_fwd", slots=cast[1:])
    w_out_all = g_out.reshape(1, d, d)
    w_down_all = g_down.reshape(1, d_ff, d)
    o_b, mix_b, states = _hgrn_fwd(proj, lb_logits, hg_norm_w, tri_lo, n_heads, hps, "hg_fwd")
    mix = jnp.concatenate([mix_a, mix_b], axis=1)
    (h1,) = _mm_nn(mix, w_out_all, [F32], tm=tm, tn=tn_d, tk=tk_d, name="proj_out",
                   epi=lambda acc, res: (acc + res,), extras=(xs,))
    (mn,), _ = _rows(lambda r, c_: ([r[0] * _rstd(r[0]) * c_[0]], []), [h1], [mlp_norm_w], [(d, BF16)], [],
                     tm=tm_rows, name="norm_mlp")
    up_b, act = _mm_nn(mn, g_up, [BF16, BF16], tm=tm, tn=tn_up, tk=tk_d, name="mlp_up",
                       epi=lambda acc: (acc, jnp.square(jnp.maximum(acc, 0.0))))
    (h2,) = _mm_nn(act, w_down_all, [F32], tm=tm, tn=tn_d, tk=_pick(d_ff, (2048, 1024)), name="mlp_down",
                   epi=lambda acc, res: (acc + res,), extras=(h1,))

    def head(rows, consts):
        hh, tg = rows
        w = consts[0]
        n = hh * _rstd(hh)
        err = n * w - tg
        dhh, dw_rows = _rms_bwd(hh, w, err * (1.0 / d))
        return [dhh, dhh], [_colsum(dw_rows), _colsum(err * err)]

    (dh2, dh2_b), (g_final, loss_cols) = _rows(head, [h2, tgt], [final_w], [(d, F32), (d, BF16)], [d, d],
                                                 tm=tm_rows, name="loss_head")

    (dup,) = _mm_nt(dh2_b, w_down_all, [BF16], tm=tm, tn=_pick(d_ff, (1024, 512)), tk=tk_d, name="mlp_down_dx",
                    epi=lambda acc, upv: (acc * (2.0 * jnp.maximum(upv.astype(F32), 0.0)),), extras=(up_b,))
    gw_down = _mm_tn(act, dh2_b, 1, tm=_pick(d_ff, (1024, 512)), tn=tn_d, tk=_pick(t, (1024, 512, 256)),
                     name="mlp_down_dw")
    (dmn,) = _mm_nt(dup, g_up, [F32], tm=tm, tn=tn_d, tk=_pick(cs_up, (2048, 1024, 512)), name="mlp_up_dx")
    gw_up = _mm_tn(mn, dup, N_CHIPS, tm=tn_d, tn=tn_up, tk=_pick(t, (1024, 512, 256)), name="mlp_up_dw")

    def pair_sums(grads, tag):
        theirs = _sibling_swap(grads, "grads_to_sibling_" + tag)
        return [_pair_sum(g, b, ids, f"grads_pair_sum_{tag}{i}") for i, (g, b) in enumerate(zip(grads, theirs))]

    pair_mlp = pair_sums([gw_up, gw_down.reshape(N_CHIPS, d_ff // N_CHIPS, d)], "mlp")

    def norm_back(rows, consts):
        xx, dy, skip = rows
        dx, dw_rows = _rms_bwd(xx, consts[0], dy)
        tot = dx + skip
        return [tot, tot], [_colsum(dw_rows)]

    (dh1, dh1_b), (g_mlp_norm,) = _rows(norm_back, [h1, dmn, dh2], [mlp_norm_w], [(d, F32), (d, BF16)], [d],
                                         tm=tm_rows, name="norm_mlp_bwd")

    (dmix,) = _mm_nt(dh1_b, w_out_all, [F32], tm=tm, tn=tn_d, tk=tk_d, name="proj_out_dx")
    gw_out = _mm_tn(mix, dh1_b, 1, tm=tn_d, tn=tn_d, tk=_pick(t, (1024, 512, 256)), name="proj_out_dw")

    def sb_norm_back(rows, consts):
        dx, dw_rows = _heads_map(lambda o, dy: _rms_bwd(o, consts[0], dy), width, *rows)
        dw = sum(_colsum(dw_rows[:, h * HEAD:(h + 1) * HEAD]) for h in range(n_heads))
        return [dx], [dw]

    (do_a,), (g_sb_norm,) = _rows(sb_norm_back, [o_a, (dmix, width, 0)], [sb_norm_w], [(width, F32)], [HEAD],
                                  tm=tm_rows, name="sb_norm_bwd")
    dq_a, dk_a, dv_a, *landed_mlp = _attn_bwd(proj, sb_tot, do_a, upto_tri, before_tri, n_heads, "sb_bwd",
                                              sums=[p[0] for p in pair_mlp])

    def hg_out_back(rows, consts):
        def one(o, gate, dy):
            sg = _sigmoid(gate)
            silu = gate * sg
            n = o * _rstd(o) * consts[0]
            do, dw_rows = _rms_bwd(o, consts[0], dy * silu)
            return do, dy * n * (sg * (1.0 + gate * (1.0 - sg))), dw_rows
        do, dgate, dw_rows = _heads_map(one, width, *rows)
        dw = sum(_colsum(dw_rows[:, h * HEAD:(h + 1) * HEAD]) for h in range(n_heads))
        return [do, dgate], [dw]

    (do_b, dgate), (g_hg_norm,) = _rows(hg_out_back, [o_b, (proj, width, 6), (dmix, width, 1)], [hg_norm_w],
                                         [(width, F32)] * 2, [HEAD], tm=tm_rows, name="hg_out_bwd")
    dhq, dhf, dhi, dlb = _hgrn_bwd(proj, do_b, states, lb_logits, tri_lo, tri_up, n_heads, hps, "hg_bwd")

    (dproj,), _ = _rows(lambda r, _c: ([jnp.concatenate([p.astype(BF16) for p in r], axis=1)], []),
                        [dq_a, dk_a, dv_a, dhq, dhf, dhi, dgate], [], [(7 * width, BF16)], [],
                        tm=tm_rows, name="pack_dproj")
    (du,) = _mm_nt(dproj, g_in, [F32], tm=tm, tn=tn_d, tk=_pick(cs_in, (1792, 896, 512, 256, 128)), name="proj_in_dx")
    gw_in = _mm_tn(u, dproj, N_CHIPS, tm=tn_d, tn=tn_in, tk=_pick(t, (1024, 512, 256)), name="proj_in_dw")
    (dx,), (g_attn_norm,) = _rows(lambda r, c_: (lambda dxx, dwr: ([dxx + r[2]], [_colsum(dwr)]))(
        *_rms_bwd(r[0], c_[0], r[1])), [xs, du, dh1], [attn_norm_w], [(d, F32)], [d], tm=tm_rows, name="norm_in_bwd")

    pair_mix = pair_sums([gw_in, gw_out.reshape(N_CHIPS, d // N_CHIPS, d)], "mix")
    landed_mix = _chip_scatter([p[0] for p in pair_mix], "grads_to_chips_mix")
    halves = [_final_sum(p[1], r, ids, f"grads_final_sum{i}")
              for i, (p, r) in enumerate(zip(pair_mix + pair_mlp, list(landed_mix) + list(landed_mlp)))]
    g_w_in, g_w_out, g_w_up, g_w_down = _sibling_join(halves, "grads_join")

    pieces = [g_attn_norm, g_mlp_norm, g_final, g_sb_norm, g_hg_norm, dlb, loss_cols]
    sizes = [p.shape[1] for p in pieces]
    flat = jnp.concatenate(pieces, axis=1)
    n_small = -(-flat.shape[1] // 1024) * 1024
    flat = jnp.pad(flat, ((0, 0), (0, n_small - flat.shape[1]))).reshape(8, n_small // 8)
    flat = _all_sum_small(flat, "small_all_sum").reshape(1, n_small)
    offs = [sum(sizes[:i]) for i in range(len(sizes))]
    g_attn_norm, g_mlp_norm, g_final, g_sb_norm, g_hg_norm, dlb, loss_cols = [
        flat[:, o:o + s] for o, s in zip(offs, sizes)]

    def small_tail(lbl_ref, dlb_ref, loss_ref, glb_ref, out_ref):
        lb = _lower_bound(lbl_ref[...])
        g0 = dlb_ref[...] * lb * (1.0 - lb)
        glb_ref[0:1, :] = g0
        glb_ref[1:2, :] = -g0
        out_ref[...] = jnp.zeros_like(out_ref) + 0.5 * jnp.sum(loss_ref[...]) * (1.0 / d)

    vm = pl.BlockSpec(memory_space=pltpu.VMEM)
    g_lb, loss11 = pl.pallas_call(
        small_tail, name="small_tail", in_specs=[vm, vm, vm], out_specs=[vm, vm],
        out_shape=[jax.ShapeDtypeStruct(lb_logits.shape, F32), jax.ShapeDtypeStruct((1, 128), F32)],
    )(lb_logits, dlb, loss_cols)
    loss = loss11[0, 0]

    names = ["attn_norm_w", "w_in", "lb_logits", "sb_norm_w", "hg_norm_w", "w_out", "mlp_norm_w", "w_up", "w_down",
             "final_norm_w"]
    ws = [attn_norm_w, w_in[0], lb_logits, sb_norm_w, hg_norm_w, w_out[0], mlp_norm_w, w_up[0], w_down[0], final_w]
    gs = [g_attn_norm, g_w_in, g_lb, g_sb_norm, g_hg_norm, g_w_out, g_mlp_norm, g_w_up, g_w_down, g_final]
    ms = [m_attn_norm_w, m_w_in[0], m_lb_logits, m_sb_norm_w, m_hg_norm_w, m_w_out[0], m_mlp_norm_w, m_w_up[0],
          m_w_down[0], m_final_norm_w.reshape(1, d)]
    vs = [v_attn_norm_w, v_w_in[0], v_lb_logits, v_sb_norm_w, v_hg_norm_w, v_w_out[0], v_mlp_norm_w, v_w_up[0],
          v_w_down[0], v_final_norm_w.reshape(1, d)]
    shapes = [attn_norm_w.shape, w_in.shape, lb_logits.shape, sb_norm_w.shape, hg_norm_w.shape, w_out.shape,
              mlp_norm_w.shape, w_up.shape, w_down.shape, final_norm_w.shape]
    deltas, new_ms, new_vs = [], [], []
    for nm, w_, g_, m_, v_ in zip(names, ws, gs, ms, vs):
        dl, m2, v2 = _adamw(w_, g_, m_, v_, "adamw_" + nm)
        deltas.append(dl)
        new_ms.append(m2)
        new_vs.append(v2)

    def shaped(lst):
        return [a.reshape(s) for a, s in zip(lst, shapes)]

    return (loss, dx[None], *shaped(gs), *shaped(deltas), *shaped(new_ms), *shaped(new_vs))
```

```python
import functools

import jax
import jax.numpy as jnp
from jax import lax
from jax.experimental import pallas as pl
from jax.experimental.pallas import tpu as pltpu

F32 = jnp.float32
BF16 = jnp.bfloat16
MESH = pl.DeviceIdType.MESH

HEAD = 128
NORM_EPS = 1e-5
N_CHIPS = 4
ATTN_BLOCK = 256
ATTN_ROWS = 1024
HG_CHUNK = 32
HG_ROWS = 256
VMEM_LIMIT = 56 * 1024 * 1024

ADAM_LR = 0.001
ADAM_B1 = 0.9
ADAM_B2 = 0.999
ADAM_EPS = 1e-08
ADAM_WD = 0.01
ADAM_STEP = 10


def _pick(n, cands):
    for c in cands:
        if n % c == 0:
            return c
    return n


def _params(sem):
    return pltpu.CompilerParams(dimension_semantics=sem, vmem_limit_bytes=VMEM_LIMIT)


def _dot(a, b):
    return jnp.dot(a, b, preferred_element_type=F32)


def _dot_nt(a, b):
    return lax.dot_general(a, b, (((1,), (1,)), ((), ())), preferred_element_type=F32)


def _dot_tn(a, b):
    return lax.dot_general(a, b, (((0,), (0,)), ((), ())), preferred_element_type=F32)


def _hilo(x):
    hi = x.astype(BF16)
    return hi, (x - hi.astype(F32)).astype(BF16)


def _dot_split(tri, x):
    hi, lo = _hilo(x)
    return _dot(tri, hi) + _dot(tri, lo)


def _dot3(dot, a, b):
    return dot(a[0], b[0]) + (dot(a[0], b[1]) + dot(a[1], b[0]))


def _sigmoid(x):
    return 1.0 / (1.0 + jnp.exp(-x))


def _mm_body(kind, nk, n_extra, n_out, epi):
    dot = {"nn": _dot, "nt": _dot_nt, "tn": _dot_tn}[kind]

    def finish(acc, extra_refs, out_refs):
        res = epi(acc, *[e[...] for e in extra_refs]) if epi is not None else (acc,)
        for o, r in zip(out_refs, res):
            o[...] = r.astype(o.dtype)

    def body(a_ref, b_ref, *rest):
        extra_refs = rest[:n_extra]
        out_refs = rest[n_extra:n_extra + n_out]
        if nk == 1:
            finish(dot(a_ref[...], b_ref[...]), extra_refs, out_refs)
            return
        acc_ref = rest[n_extra + n_out]
        k = pl.program_id(2)

        @pl.when(k == 0)
        def _():
            acc_ref[...] = jnp.zeros_like(acc_ref)

        acc_ref[...] += dot(a_ref[...], b_ref[...])

        @pl.when(k == nk - 1)
        def _():
            finish(acc_ref[...], extra_refs, out_refs)

    return body


def _mm_nn(a, w, out_dtypes, *, tm, tn, tk, name, epi=None, extras=()):
    m, r = a.shape
    p, _, c = w.shape
    npc = c // tn
    nk = r // tk
    body = _mm_body("nn", nk, len(extras), len(out_dtypes), epi)
    tile = pl.BlockSpec((tm, tn), lambda i, j, k: (i, j))
    return pl.pallas_call(
        body, name=name,
        grid=(m // tm, p * npc, nk),
        in_specs=[pl.BlockSpec((tm, tk), lambda i, j, k: (i, k)),
                  pl.BlockSpec((None, tk, tn), lambda i, j, k: (j // npc, k, j % npc))] + [tile] * len(extras),
        out_specs=[tile] * len(out_dtypes),
        out_shape=[jax.ShapeDtypeStruct((m, p * c), d) for d in out_dtypes],
        scratch_shapes=[pltpu.VMEM((tm, tn), F32)] if nk > 1 else [],
        compiler_params=_params(("parallel", "parallel", "arbitrary")),
    )(a, w, *extras)


def _mm_nt(a, w, out_dtypes, *, tm, tn, tk, name, epi=None, extras=(), sums=()):
    m, _ = a.shape
    p, r, c = w.shape
    kpc = c // tk
    nk = p * kpc
    n_e, n_o, n_s = len(extras), len(out_dtypes), len(sums)
    grid = (m // tm, r // tn, nk)
    inner = _mm_body("nt", nk, n_e, n_o, epi)

    def body(a_ref, b_ref, *rest):
        extra, sum_refs = rest[:n_e], rest[n_e:n_e + n_s]
        outs, landed = rest[n_e + n_s:n_e + n_s + n_o], rest[n_e + n_s + n_o:n_e + 2 * n_s + n_o]
        scratch = rest[n_e + 2 * n_s + n_o:]
        acc = scratch[:1] if nk > 1 else ()
        if n_s:
            ids = [pl.program_id(ax) for ax in range(3)]
            begin, end = _scatter_copies(sum_refs, landed, *scratch[len(acc):])
            pl.when((ids[0] == 0) & (ids[1] == 0) & (ids[2] == 0))(begin)
        inner(a_ref, b_ref, *extra, *outs, *acc)
        if n_s:
            pl.when((ids[0] == grid[0] - 1) & (ids[1] == grid[1] - 1) & (ids[2] == grid[2] - 1))(end)

    tile = pl.BlockSpec((tm, tn), lambda i, j, k: (i, j))
    return pl.pallas_call(
        body, name=name,
        grid=grid,
        in_specs=[pl.BlockSpec((tm, tk), lambda i, j, k: (i, k)),
                  pl.BlockSpec((None, tn, tk), lambda i, j, k: (k // kpc, j, k % kpc))] + [tile] * n_e + [ANY] * n_s,
        out_specs=[tile] * n_o + [ANY] * n_s,
        out_shape=[jax.ShapeDtypeStruct((m, r), d) for d in out_dtypes]
                  + [jax.ShapeDtypeStruct((3,) + s.shape[1:], s.dtype) for s in sums],
        scratch_shapes=([pltpu.VMEM((tm, tn), F32)] if nk > 1 else [])
                       + ([pltpu.SemaphoreType.DMA((3 * n_s,))] * 2 if n_s else []),
        compiler_params=_params(("arbitrary",) * 3 if n_s else ("parallel", "parallel", "arbitrary")),
    )(a, w, *extras, *sums)


def _mm_tn(a, g, p, *, tm, tn, tk, name):
    t, r = a.shape
    c = g.shape[1] // p
    npc = c // tn
    nk = t // tk
    body = _mm_body("tn", nk, 0, 1, None)
    return pl.pallas_call(
        body, name=name,
        grid=(r // tm, p * npc, nk),
        in_specs=[pl.BlockSpec((tk, tm), lambda i, j, k: (k, i)),
                  pl.BlockSpec((tk, tn), lambda i, j, k: (k, j))],
        out_specs=[pl.BlockSpec((None, tm, tn), lambda i, j, k: (j // npc, i, j % npc))],
        out_shape=[jax.ShapeDtypeStruct((p, r, c), F32)],
        scratch_shapes=[pltpu.VMEM((tm, tn), F32)] if nk > 1 else [],
        compiler_params=_params(("parallel", "parallel", "arbitrary")),
    )(a, g)[0]


def _rows(fn, row_ins, const_ins, row_outs, acc_outs, *, tm, name):
    specs, arrays = [], []
    t = None
    for item in row_ins:
        if isinstance(item, tuple):
            arr, width, cb = item
            specs.append(pl.BlockSpec((tm, width), functools.partial(lambda i, cb: (i, cb), cb=cb)))
        else:
            arr = item
            specs.append(pl.BlockSpec((tm, arr.shape[1]), lambda i: (i, 0)))
        arrays.append(arr)
        t = arr.shape[0]
    for arr in const_ins:
        specs.append(pl.BlockSpec(arr.shape, lambda i: (0, 0)))
        arrays.append(arr)
    n_in, n_row, n_acc = len(arrays), len(row_outs), len(acc_outs)

    def body(*refs):
        ins = [r[...] for r in refs[:n_in]]
        outs = refs[n_in:]
        row_res, acc_res = fn(ins[:len(row_ins)], ins[len(row_ins):])
        for o, r in zip(outs[:n_row], row_res):
            o[...] = r.astype(o.dtype)
        if n_acc:
            i = pl.program_id(0)

            @pl.when(i == 0)
            def _():
                for o in outs[n_row:]:
                    o[...] = jnp.zeros_like(o)

            for o, r in zip(outs[n_row:], acc_res):
                o[...] += r

    res = pl.pallas_call(
        body, name=name,
        grid=(t // tm,),
        in_specs=specs,
        out_specs=[pl.BlockSpec((tm, c), lambda i: (i, 0)) for c, _ in row_outs]
                  + [pl.BlockSpec((1, c), lambda i: (0, 0)) for c in acc_outs],
        out_shape=[jax.ShapeDtypeStruct((t, c), d) for c, d in row_outs]
                  + [jax.ShapeDtypeStruct((1, c), F32) for c in acc_outs],
        compiler_params=_params(("arbitrary",)),
    )(*arrays)
    return res[:n_row], res[n_row:]


def _rstd(x):
    return lax.rsqrt(jnp.mean(x * x, axis=-1, keepdims=True) + NORM_EPS)


def _rms_bwd(x, w, dy):
    r = _rstd(x)
    n = x * r
    dn = dy * w
    dx = r * (dn - n * jnp.mean(dn * n, axis=-1, keepdims=True))
    return dx, dy * n


def _colsum(x):
    return jnp.sum(x, axis=0, keepdims=True)


def _heads_map(fn, width, *tiles):
    outs = None
    for h in range(width // HEAD):
        res = fn(*[t[:, h * HEAD:(h + 1) * HEAD] for t in tiles])
        if outs is None:
            outs = [[] for _ in res]
        for lst, r in zip(outs, res):
            lst.append(r)
    return [jnp.concatenate(lst, axis=1) for lst in outs]


def _log_one_minus_beta(z):
    return -(jnp.maximum(z, 0.0) + jnp.log(1.0 + jnp.exp(-jnp.abs(z))))


def _attn_fwd(proj, after_tri, norm_w, n_heads, name, slots=()):
    t = proj.shape[0]
    blk = min(ATTN_BLOCK, t)
    qb = min(ATTN_ROWS, t)
    ns = qb // blk
    nq = t // qb
    n = len(slots)
    scale = HEAD ** -0.5

    def body(q_ref, k_ref, v_ref, tri_ref, w_ref, *rest):
        o_ref, mix_ref, tot_ref = rest[n:n + 3]
        i = pl.program_id(1)
        if n:
            begin, end = _gather_copies(rest[n + 3:2 * n + 3], *rest[2 * n + 3:])
            pl.when((pl.program_id(0) == 0) & (i == 0))(begin)
        q = (q_ref[...] * scale).astype(BF16)
        tri = tri_ref[...]

        def part(r0, j, acc_l, acc_o, masked):
            sl = pl.ds(pl.multiple_of(j * blk, blk), blk)
            m = qb - r0
            z = _dot_nt(q[r0:, :], k_ref[sl, :].astype(BF16))
            lm = _log_one_minus_beta(z)
            if masked:
                mask = lax.broadcasted_iota(jnp.int32, (m, blk), 1) < lax.broadcasted_iota(jnp.int32, (m, blk), 0)
                lmm = jnp.where(mask, lm, 0.0)
            else:
                lmm = lm
            w = jnp.exp(z + lm + acc_l[r0:, :] + _dot(lmm.astype(BF16), tri))
            if masked:
                w = jnp.where(mask, w, 0.0)
            new_l = acc_l[r0:, :] + jnp.sum(lmm, axis=1, keepdims=True)
            new_o = acc_o[r0:, :] + _dot(w.astype(BF16), v_ref[sl, :].astype(BF16))
            if r0:
                new_l = jnp.concatenate([acc_l[:r0, :], new_l], axis=0)
                new_o = jnp.concatenate([acc_o[:r0, :], new_o], axis=0)
            return new_l, new_o

        acc = (jnp.zeros((qb, 1), F32), jnp.zeros((qb, HEAD), F32))
        for jr in reversed(range(ns)):
            acc = part(jr * blk, ns * i + jr, *acc, True)
        acc_l, acc_o = lax.fori_loop(0, ns * i, lambda jj, c: part(0, ns * i - 1 - jj, *c, False), acc)
        o_ref[...] = acc_o
        mix_ref[...] = (acc_o * _rstd(acc_o) * w_ref[...]).astype(BF16)
        tot_ref[...] = jnp.broadcast_to(acc_l, (qb, HEAD))
        if n:
            pl.when((pl.program_id(0) == n_heads - 1) & (i == nq - 1))(end)

    width = n_heads * HEAD
    qblk = pl.BlockSpec((qb, HEAD), lambda h, i: (i, h))
    return pl.pallas_call(
        body, name=name,
        grid=(n_heads, nq),
        in_specs=[qblk,
                  pl.BlockSpec((t, HEAD), lambda h, i: (0, n_heads + h)),
                  pl.BlockSpec((t, HEAD), lambda h, i: (0, 2 * n_heads + h)),
                  pl.BlockSpec((blk, blk), lambda h, i: (0, 0)),
                  pl.BlockSpec((1, HEAD), lambda h, i: (0, 0))] + [ANY] * n,
        out_specs=[qblk, qblk, qblk] + [ANY] * n,
        out_shape=[jax.ShapeDtypeStruct((t, width), F32), jax.ShapeDtypeStruct((t, width), BF16),
                   jax.ShapeDtypeStruct((t, width), F32)] + [jax.ShapeDtypeStruct(s.shape, s.dtype) for s in slots],
        input_output_aliases={5 + w: 3 + w for w in range(n)},
        scratch_shapes=[pltpu.SemaphoreType.DMA((6 * n,))] * 2 if n else [],
        compiler_params=_params(("arbitrary", "arbitrary")),
    )(proj, proj, proj, after_tri, norm_w, *slots)


def _attn_bwd(proj, tot, do, after_tri, before_tri, n_heads, name, sums=()):
    t = proj.shape[0]
    blk = min(ATTN_BLOCK, t)
    qb = min(ATTN_ROWS, t)
    ns = qb // blk
    nq = t // qb
    n = len(sums)
    scale = HEAD ** -0.5

    def body(q_ref, k_ref, v_ref, tot_ref, do_ref, after_ref, before_ref, *rest):
        dq_ref, dk_ref, dv_ref = rest[n:n + 3]
        i = pl.program_id(1)
        if n:
            begin, end = _scatter_copies(rest[:n], rest[n + 3:2 * n + 3], *rest[2 * n + 3:])
            pl.when((pl.program_id(0) == 0) & (i == 0))(begin)

        @pl.when(i == 0)
        def _():
            dk_ref[...] = jnp.zeros_like(dk_ref)
            dv_ref[...] = jnp.zeros_like(dv_ref)

        q = (q_ref[...] * scale).astype(BF16)
        dob = do_ref[...].astype(BF16)
        total = tot_ref[:, 0:1]
        after_tri = after_ref[...]
        before = before_ref[...]

        def part(r0, j, seen_l, seen_g, dq, masked):
            sl = pl.ds(pl.multiple_of(j * blk, blk), blk)
            m = qb - r0
            qq, dd = q[r0:, :], dob[r0:, :]
            kb = k_ref[sl, :].astype(BF16)
            z = _dot_nt(qq, kb)
            lm = _log_one_minus_beta(z)
            if masked:
                mask = lax.broadcasted_iota(jnp.int32, (m, blk), 1) < lax.broadcasted_iota(jnp.int32, (m, blk), 0)
                lmm = jnp.where(mask, lm, 0.0)
            else:
                lmm = lm
            row_l = jnp.sum(lmm, axis=1, keepdims=True)
            after = (total[r0:, :] - seen_l[r0:, :] - row_l) + _dot(lmm.astype(BF16), after_tri)
            w = jnp.exp(z + lm + after)
            if masked:
                w = jnp.where(mask, w, 0.0)
            sig = jnp.exp(z + lm)
            g = w * _dot_nt(dd, v_ref[sl, :].astype(BF16))
            g_before = seen_g[r0:, :] + _dot(g.astype(BF16), before)
            dz = g * (1.0 - sig) - g_before * sig
            if masked:
                dz = jnp.where(mask, dz, 0.0)
            dzb = dz.astype(BF16)
            dk_ref[sl, :] += _dot_tn(dzb, qq)
            dv_ref[sl, :] += _dot_tn(w.astype(BF16), dd)
            new = (seen_l[r0:, :] + row_l,
                   seen_g[r0:, :] + jnp.sum(g, axis=1, keepdims=True), dq[r0:, :] + _dot(dzb, kb))
            if r0:
                new = tuple(jnp.concatenate([old[:r0, :], n], axis=0) for old, n in zip((seen_l, seen_g, dq), new))
            return new

        zero = jnp.zeros((qb, 1), F32)
        carry = lax.fori_loop(0, ns * i, lambda j, c: part(0, j, *c, False), (zero, zero, jnp.zeros((qb, HEAD), F32)))
        for jr in range(ns):
            carry = part(jr * blk, ns * i + jr, *carry, True)
        dq_ref[...] = carry[2] * scale
        if n:
            pl.when((pl.program_id(0) == n_heads - 1) & (i == nq - 1))(end)

    width = n_heads * HEAD
    qblk = pl.BlockSpec((qb, HEAD), lambda h, i: (i, h))
    full = pl.BlockSpec((t, HEAD), lambda h, i: (0, h))
    tri = pl.BlockSpec((blk, blk), lambda h, i: (0, 0))
    return pl.pallas_call(
        body, name=name,
        grid=(n_heads, nq),
        in_specs=[qblk,
                  pl.BlockSpec((t, HEAD), lambda h, i: (0, n_heads + h)),
                  pl.BlockSpec((t, HEAD), lambda h, i: (0, 2 * n_heads + h)),
                  qblk, qblk, tri, tri] + [ANY] * n,
        out_specs=[qblk, full, full] + [ANY] * n,
        out_shape=[jax.ShapeDtypeStruct((t, width), F32)] * 3
                  + [jax.ShapeDtypeStruct((3,) + s.shape[1:], s.dtype) for s in sums],
        scratch_shapes=[pltpu.SemaphoreType.DMA((3 * n,))] * 2 if n else [],
        compiler_params=_params(("arbitrary", "arbitrary")),
    )(proj, proj, proj, tot, do, after_tri, before_tri, *sums)


def _lower_bound(logits):
    l0, l1 = logits[0:1, :], logits[1:2, :]
    mx = jnp.maximum(l0, l1)
    e0, e1 = jnp.exp(l0 - mx), jnp.exp(l1 - mx)
    return e0 / (e0 + e1)


def _hg_chunk(qc, kc, gc, tri_lo):
    c = qc.shape[0]
    cum = _dot_split(tri_lo, gc)
    mid = cum[c // 2 - 1:c // 2, :]
    last = cum[c - 1:c, :]
    qt = qc * jnp.exp(cum - mid)
    kt = kc * jnp.exp(mid - cum)
    qe = qc * jnp.exp(cum)
    kd = kc * jnp.exp(last - cum)
    return cum, mid, last, qt, kt, qe, kd


def _hgrn_fwd(proj, lb_logits, norm_w, tri_lo, n_heads, heads_per_step, name):
    t = proj.shape[0]
    bt = min(HG_ROWS, t)
    c = HG_CHUNK
    nc = bt // c
    hw = heads_per_step * HEAD
    width = n_heads * HEAD
    col0 = 3 * width // hw

    def body(hq_ref, hf_ref, hi_ref, hgate_ref, lbl_ref, w_ref, tri_ref, o_ref, mix_ref, st_ref,
             state, q_scr, k_scr, g_scr):
        @pl.when(pl.program_id(1) == 0)
        def _():
            state[...] = jnp.zeros_like(state)

        lb = _lower_bound(lbl_ref[...])
        f = hf_ref[...]
        g_scr[...] = jnp.log(lb + (1.0 - lb) * _sigmoid(f))
        k_scr[...] = (1.0 - lb) * _sigmoid(-f)
        hq = hq_ref[...]
        q_scr[...] = hq * _sigmoid(hq)
        tri = tri_ref[...]
        causal = lax.broadcasted_iota(jnp.int32, (c, c), 1) <= lax.broadcasted_iota(jnp.int32, (c, c), 0)

        def chunk(ci, carry):
            r = pl.ds(pl.multiple_of(ci * c, c), c)
            vc = hi_ref[r, :].astype(BF16)
            _, _, last, qt, kt, qe, kd = _hg_chunk(q_scr[r, :], k_scr[r, :], g_scr[r, :], tri)
            qt, kt, qe, kd = qt.astype(BF16), kt.astype(BF16), qe.astype(BF16), kd.astype(BF16)
            e_last = jnp.exp(last)
            old = [state[h] for h in range(heads_per_step)]
            outs, new = [], []
            for h in range(heads_per_step):
                cs = slice(h * HEAD, (h + 1) * HEAD)
                a = jnp.where(causal, _dot_nt(qt[:, cs], kt[:, cs]), 0.0)
                outs.append(_dot(a.astype(BF16), vc[:, cs]) + _dot_nt(qe[:, cs], old[h].astype(BF16)))
                new.append(old[h] * e_last[:, cs] + _dot_tn(vc[:, cs], kd[:, cs]))
            for h in range(heads_per_step):
                st_ref[ci, :, h * HEAD:(h + 1) * HEAD] = old[h]
                state[h] = new[h]
            o_ref[r, :] = jnp.concatenate(outs, axis=1)
            return carry

        lax.fori_loop(0, nc, chunk, 0)

        def finish(o, gate):
            return ((o * _rstd(o) * w_ref[...]) * (gate * _sigmoid(gate)),)

        mix_ref[...] = _heads_map(finish, hw, o_ref[...], hgate_ref[...])[0].astype(BF16)

    def col(group):
        return pl.BlockSpec((bt, hw), functools.partial(lambda hp, tb, g: (tb, col0 + g * (width // hw) + hp), g=group))

    blk = pl.BlockSpec((bt, hw), lambda hp, tb: (tb, hp))
    return pl.pallas_call(
        body, name=name,
        grid=(n_heads // heads_per_step, t // bt),
        in_specs=[col(0), col(1), col(2), col(3),
                  pl.BlockSpec((2, hw), lambda hp, tb: (0, hp)),
                  pl.BlockSpec((1, HEAD), lambda hp, tb: (0, 0)),
                  pl.BlockSpec((c, c), lambda hp, tb: (0, 0))],
        out_specs=[blk, blk, pl.BlockSpec((nc, HEAD, hw), lambda hp, tb: (tb, 0, hp))],
        out_shape=[jax.ShapeDtypeStruct((t, width), F32), jax.ShapeDtypeStruct((t, width), BF16),
                   jax.ShapeDtypeStruct((t // c, HEAD, width), F32)],
        scratch_shapes=[pltpu.VMEM((heads_per_step, HEAD, HEAD), F32)] + [pltpu.VMEM((bt, hw), F32)] * 3,
        compiler_params=_params(("parallel", "arbitrary")),
    )(proj, proj, proj, proj, lb_logits, norm_w, tri_lo)


def _hgrn_bwd(proj, do, states, lb_logits, tri_lo, tri_up, n_heads, heads_per_step, name):
    t = proj.shape[0]
    bt = min(HG_ROWS, t)
    c = HG_CHUNK
    nc = bt // c
    nb = t // bt
    hw = heads_per_step * HEAD
    width = n_heads * HEAD
    col0 = 3 * width // hw

    def body(hq_ref, hf_ref, hi_ref, do_ref, st_ref, lbl_ref, lo_ref, up_ref, dq_ref, df_ref, di_ref, dlb_ref,
             dstate, q_scr, k_scr, g_scr, dk_scr, dg_scr):
        @pl.when(pl.program_id(1) == 0)
        def _():
            dstate[...] = jnp.zeros_like(dstate)
            dlb_ref[...] = jnp.zeros_like(dlb_ref)

        lb = _lower_bound(lbl_ref[...])
        f = hf_ref[...]
        sg = _sigmoid(f)
        sgn = _sigmoid(-f)
        den = lb + (1.0 - lb) * sg
        g_scr[...] = jnp.log(den)
        k_scr[...] = (1.0 - lb) * sgn
        hq = hq_ref[...]
        sq = _sigmoid(hq)
        q_scr[...] = hq * sq
        tri_lo_v = lo_ref[...]
        tri_up_v = up_ref[...]
        causal = lax.broadcasted_iota(jnp.int32, (c, c), 1) <= lax.broadcasted_iota(jnp.int32, (c, c), 0)
        last_row = lax.broadcasted_iota(jnp.int32, (c, hw), 0) == c - 1

        def chunk(cc, carry):
            ci = nc - 1 - cc
            r = pl.ds(pl.multiple_of(ci * c, c), c)
            qc, kc = q_scr[r, :], k_scr[r, :]
            cum, mid, last, qt, kt, qe, kd = _hg_chunk(qc, kc, g_scr[r, :], tri_lo_v)
            qt, kt, qe, kd, doc, vc = [_hilo(v) for v in (qt, kt, qe, kd, do_ref[r, :], hi_ref[r, :])]
            e_last = jnp.exp(last)
            sts = [st_ref[ci, :, h * HEAD:(h + 1) * HEAD] for h in range(heads_per_step)]
            dsts = [dstate[h] for h in range(heads_per_step)]
            di, dq_inter, dk_inter, dq_intra, dk_intra, st_sums, new = [], [], [], [], [], [], []
            for h in range(heads_per_step):
                cs = slice(h * HEAD, (h + 1) * HEAD)

                def head(pair):
                    return pair[0][:, cs], pair[1][:, cs]

                st, dst = _hilo(sts[h]), _hilo(dsts[h])
                a = _hilo(jnp.where(causal, _dot3(_dot_nt, head(qt), head(kt)), 0.0))
                da = _hilo(jnp.where(causal, _dot3(_dot_nt, head(doc), head(vc)), 0.0))
                di.append(_dot3(_dot_tn, a, head(doc)) + _dot3(_dot_nt, head(kd), dst))
                dq_inter.append(_dot3(_dot, head(doc), st))
                dk_inter.append(_dot3(_dot, head(vc), dst))
                dq_intra.append(_dot3(_dot, da, head(kt)))
                dk_intra.append(_dot3(_dot_tn, da, head(qt)))
                st_sums.append(_colsum(dsts[h] * sts[h]))
                new.append(dsts[h] * e_last[:, cs] + _dot3(_dot_tn, head(doc), head(qe)))

            def wide(parts):
                return jnp.concatenate(parts, axis=1)

            dq_inter = wide(dq_inter) * jnp.exp(cum)
            dk_inter = wide(dk_inter) * jnp.exp(last - cum)
            dq = wide(dq_intra) * jnp.exp(cum - mid) + dq_inter
            dk = wide(dk_intra) * jnp.exp(mid - cum) + dk_inter
            d_last = _colsum(kc * dk_inter) + e_last * wide(st_sums)
            dcum = qc * dq - kc * dk + jnp.where(last_row, d_last, 0.0)
            for h in range(heads_per_step):
                dstate[h] = new[h]
            di_ref[r, :] = wide(di)
            dq_ref[r, :] = dq
            dk_scr[r, :] = dk
            dg_scr[r, :] = _dot_split(tri_up_v, dcum)
            return carry

        lax.fori_loop(0, nc, chunk, 0)

        e = (dg_scr[...] / den - dk_scr[...]) * sgn
        df_ref[...] = e * (1.0 - lb) * sg
        dlb_ref[...] += _colsum(e)
        dq_ref[...] = dq_ref[...] * (sq * (1.0 + hq * (1.0 - sq)))

    def col(group):
        return pl.BlockSpec((bt, hw), functools.partial(
            lambda hp, tb, g: (nb - 1 - tb, col0 + g * (width // hw) + hp), g=group))

    blk = pl.BlockSpec((bt, hw), lambda hp, tb: (nb - 1 - tb, hp))
    tri = pl.BlockSpec((c, c), lambda hp, tb: (0, 0))
    return pl.pallas_call(
        body, name=name,
        grid=(n_heads // heads_per_step, nb),
        in_specs=[col(0), col(1), col(2), blk,
                  pl.BlockSpec((nc, HEAD, hw), lambda hp, tb: (nb - 1 - tb, 0, hp)),
                  pl.BlockSpec((2, hw), lambda hp, tb: (0, hp)), tri, tri],
        out_specs=[blk, blk, blk, pl.BlockSpec((1, hw), lambda hp, tb: (0, hp))],
        out_shape=[jax.ShapeDtypeStruct((t, width), F32)] * 3 + [jax.ShapeDtypeStruct((1, width), F32)],
        scratch_shapes=[pltpu.VMEM((heads_per_step, HEAD, HEAD), F32)] + [pltpu.VMEM((bt, hw), F32)] * 5,
        compiler_params=_params(("parallel", "arbitrary")),
    )(proj, proj, proj, do, states, lb_logits, tri_lo, tri_up)


def _place():
    x, y, c = lax.axis_index("x"), lax.axis_index("y"), lax.axis_index("c")
    chips = [(1 - x, y), (x, 1 - y), (1 - x, 1 - y)]
    return x, y, c, chips


ANY = pl.BlockSpec(memory_space=pl.ANY)


def _cast_to_slot(shard, ids, name):
    r, c = shard.shape
    tm = _pick(r, (256, 128, 64, 32, 16))

    def body(ids_ref, s_ref, o_ref):
        o_ref[...] = s_ref[...].astype(BF16)

    return pl.pallas_call(
        body, name=name,
        grid_spec=pltpu.PrefetchScalarGridSpec(
            num_scalar_prefetch=1, grid=(r // tm,),
            in_specs=[pl.BlockSpec((tm, c), lambda i, ids: (i, 0))],
            out_specs=pl.BlockSpec((None, tm, c), lambda i, ids: (ids[1], i, 0))),
        out_shape=jax.ShapeDtypeStruct((N_CHIPS, r, c), BF16),
        compiler_params=_params(("parallel",)),
    )(ids, shard)


def _gather_copies(bufs, send, recv):
    x, y, c, chips = _place()
    mine = 2 * x + y

    def half(ref, who, core):
        h = ref.shape[-2] // 2
        return ref.at[who, pl.ds(core * h, h), :]

    def copy(w, k, rows, to):
        return pltpu.make_async_remote_copy(src_ref=rows, dst_ref=rows, send_sem=send.at[6 * w + k],
                                            recv_sem=recv.at[6 * w + k], device_id=to, device_id_type=MESH)

    def to_chips(w):
        return [copy(w, k, half(bufs[w], mine, c), (qx, qy, c)) for k, (qx, qy) in enumerate(chips)]

    def to_sibling(w):
        return [copy(w, 3 + k, half(bufs[w], 2 * qx + qy, c), (x, y, 1 - c)) for k, (qx, qy) in enumerate(chips)]

    def begin():
        for w in range(len(bufs)):
            for cp in to_chips(w):
                cp.start()

    def end():
        for w in range(len(bufs)):
            for k, (qx, qy) in enumerate(chips):
                copy(w, k, half(bufs[w], 2 * qx + qy, c), (x, y, c)).wait_recv()
                to_sibling(w)[k].start()
        for w in range(len(bufs)):
            for k, (qx, qy) in enumerate(chips):
                copy(w, 3 + k, half(bufs[w], 2 * qx + qy, 1 - c), (x, y, c)).wait_recv()
        for w in range(len(bufs)):
            for cp in to_chips(w) + to_sibling(w):
                cp.wait_send()

    return begin, end


def _scatter_copies(ins, outs, send, recv):
    _, _, c, chips = _place()

    def copies():
        return [pltpu.make_async_remote_copy(
            src_ref=ins[w].at[2 * qx + qy], dst_ref=outs[w].at[k], send_sem=send.at[3 * w + k],
            recv_sem=recv.at[3 * w + k], device_id=(qx, qy, c), device_id_type=MESH)
            for w in range(len(ins)) for k, (qx, qy) in enumerate(chips)]

    def begin():
        for cp in copies():
            cp.start()

    def end():
        for cp in copies():
            cp.wait()

    return begin, end


def _gather_weights(slots, name):
    n = len(slots)

    def body(*refs):
        begin, end = _gather_copies(refs[n:2 * n], *refs[2 * n:])
        begin()
        end()

    return pl.pallas_call(
        body, name=name,
        in_specs=[ANY] * n, out_specs=[ANY] * n,
        out_shape=[jax.ShapeDtypeStruct(s.shape, s.dtype) for s in slots],
        input_output_aliases={w: w for w in range(n)},
        scratch_shapes=[pltpu.SemaphoreType.DMA((6 * n,)), pltpu.SemaphoreType.DMA((6 * n,))],
    )(*slots)


def _sibling_swap(grads, name):
    n = len(grads)

    def body(*refs):
        ins, outs = refs[:n], refs[n:2 * n]
        send, recv = refs[2 * n:]
        x, y, c, _ = _place()
        cps = []
        for w in range(n):
            h = ins[w].shape[1] // 2
            cps.append(pltpu.make_async_remote_copy(
                src_ref=ins[w].at[:, pl.ds((1 - c) * h, h), :], dst_ref=outs[w], send_sem=send.at[w],
                recv_sem=recv.at[w], device_id=(x, y, 1 - c), device_id_type=MESH))
            cps[-1].start()
        for cp in cps:
            cp.wait()

    return pl.pallas_call(
        body, name=name, in_specs=[ANY] * n, out_specs=[ANY] * n,
        out_shape=[jax.ShapeDtypeStruct((g.shape[0], g.shape[1] // 2, g.shape[2]), g.dtype) for g in grads],
        scratch_shapes=[pltpu.SemaphoreType.DMA((n,)), pltpu.SemaphoreType.DMA((n,))],
    )(*grads)


def _sibling_join(fulls, name):
    n = len(fulls)

    def body(*refs):
        bufs = refs[n:2 * n]
        send, recv = refs[2 * n:]
        x, y, c, _ = _place()
        cps = []
        for w in range(n):
            h = bufs[w].shape[0] // 2
            rows = bufs[w].at[pl.ds(c * h, h), :]
            cps.append(pltpu.make_async_remote_copy(
                src_ref=rows, dst_ref=rows, send_sem=send.at[w], recv_sem=recv.at[w],
                device_id=(x, y, 1 - c), device_id_type=MESH))
            cps[-1].start()
        for w in range(n):
            h = bufs[w].shape[0] // 2
            theirs = bufs[w].at[pl.ds((1 - c) * h, h), :]
            pltpu.make_async_remote_copy(src_ref=theirs, dst_ref=theirs, send_sem=send.at[w], recv_sem=recv.at[w],
                                         device_id=(x, y, c), device_id_type=MESH).wait_recv()
        for cp in cps:
            cp.wait_send()

    return pl.pallas_call(
        body, name=name, in_specs=[ANY] * n, out_specs=[ANY] * n,
        out_shape=[jax.ShapeDtypeStruct(s.shape, s.dtype) for s in fulls],
        input_output_aliases={w: w for w in range(n)},
        scratch_shapes=[pltpu.SemaphoreType.DMA((n,)), pltpu.SemaphoreType.DMA((n,))],
    )(*fulls)


def _all_sum_small(vec, name):
    n = vec.shape[1]

    def body(v_ref, out_ref, buf, send, recv):
        x, y, c, _ = _place()
        me = 4 * x + 2 * y + c
        buf[me] = v_ref[...]
        peers = []
        for mask in range(1, 8):
            px = 1 - x if mask & 4 else x
            py = 1 - y if mask & 2 else y
            pc = 1 - c if mask & 1 else c
            peers.append((px, py, pc))
        cps = []
        for k, peer in enumerate(peers):
            cps.append(pltpu.make_async_remote_copy(src_ref=buf.at[me], dst_ref=buf.at[me], send_sem=send.at[k],
                                                    recv_sem=recv.at[k], device_id=peer, device_id_type=MESH))
            cps[-1].start()
        for k, (px, py, pc) in enumerate(peers):
            slot = buf.at[4 * px + 2 * py + pc]
            pltpu.make_async_remote_copy(src_ref=slot, dst_ref=slot, send_sem=send.at[k], recv_sem=recv.at[k],
                                         device_id=(x, y, c), device_id_type=MESH).wait_recv()
        for cp in cps:
            cp.wait_send()
        total = buf[0]
        for d in range(1, 8):
            total = total + buf[d]
        out_ref[...] = total

    vm = pl.BlockSpec(memory_space=pltpu.VMEM)
    return pl.pallas_call(
        body, name=name, in_specs=[vm], out_specs=vm,
        out_shape=jax.ShapeDtypeStruct(vec.shape, F32),
        scratch_shapes=[pltpu.VMEM((8, 8, n), F32), pltpu.SemaphoreType.DMA((7,)), pltpu.SemaphoreType.DMA((7,))],
    )(vec)


def _pair_sum(g, buf, ids, name):
    p, r, c = g.shape
    h = r // 2
    tr = _pick(h, (256, 128, 64, 32, 16))
    nh = h // tr

    def body(ids_ref, g_ref, b_ref, sums_ref, own_ref):
        s = g_ref[...] + b_ref[...]
        sums_ref[...] = s.astype(BF16)

        @pl.when(pl.program_id(1) == ids_ref[1])
        def _():
            own_ref[...] = s

    return pl.pallas_call(
        body, name=name,
        grid_spec=pltpu.PrefetchScalarGridSpec(
            num_scalar_prefetch=1, grid=(nh, p),
            in_specs=[pl.BlockSpec((None, tr, c), lambda i, q, ids: (q, ids[0] * nh + i, 0)),
                      pl.BlockSpec((None, tr, c), lambda i, q, ids: (q, i, 0))],
            out_specs=[pl.BlockSpec((None, tr, c), lambda i, q, ids: (q, i, 0)),
                       pl.BlockSpec((tr, c), lambda i, q, ids: (i, 0))]),
        out_shape=[jax.ShapeDtypeStruct((p, h, c), BF16), jax.ShapeDtypeStruct((h, c), F32)],
        compiler_params=_params(("parallel", "arbitrary")),
    )(ids, g, buf)


def _final_sum(own, others, ids, name):
    h, c = own.shape
    tr = _pick(h, (256, 128, 64, 32, 16))
    nh = h // tr

    def body(ids_ref, own_ref, oth_ref, out_ref):
        s = own_ref[...]
        for k in range(3):
            s = s + oth_ref[k].astype(F32)
        out_ref[...] = s

    return pl.pallas_call(
        body, name=name,
        grid_spec=pltpu.PrefetchScalarGridSpec(
            num_scalar_prefetch=1, grid=(nh,),
            in_specs=[pl.BlockSpec((tr, c), lambda i, ids: (i, 0)),
                      pl.BlockSpec((3, tr, c), lambda i, ids: (0, i, 0))],
            out_specs=pl.BlockSpec((tr, c), lambda i, ids: (ids[0] * nh + i, 0))),
        out_shape=jax.ShapeDtypeStruct((2 * h, c), F32),
        compiler_params=_params(("parallel",)),
    )(ids, own, others)


def _adamw(w, g, m, v, name):
    r, c = w.shape
    tm = _pick(r, (256, 128, 64, 32, 16, 8)) if r >= 8 else r

    def fn(rows, _):
        w_, g_, m_, v_ = rows
        m2 = ADAM_B1 * m_ + (1.0 - ADAM_B1) * g_
        v2 = ADAM_B2 * v_ + (1.0 - ADAM_B2) * (g_ * g_)
        m_hat = m2 / (1.0 - ADAM_B1 ** ADAM_STEP)
        v_hat = v2 / (1.0 - ADAM_B2 ** ADAM_STEP)
        delta = -ADAM_LR * (m_hat / (jnp.sqrt(v_hat) + ADAM_EPS) + ADAM_WD * w_)
        return [delta, m2, v2], []

    outs, _ = _rows(fn, [w, g, m, v], [], [(c, F32)] * 3, [], tm=tm, name=name)
    return outs


def kernel(x, attn_norm_w, w_in, lb_logits, sb_norm_w, hg_norm_w, w_out, mlp_norm_w, w_up, w_down, final_norm_w, loss_target, m_attn_norm_w, m_w_in, m_lb_logits, m_sb_norm_w, m_hg_norm_w, m_w_out, m_mlp_norm_w, m_w_up, m_w_down, m_final_norm_w, v_attn_norm_w, v_w_in, v_lb_logits, v_sb_norm_w, v_hg_norm_w, v_w_out, v_mlp_norm_w, v_w_up, v_w_down, v_final_norm_w):
    xs, tgt = x[0], loss_target[0]
    t, d = xs.shape
    width = d // 2
    n_heads = width // HEAD
    hps = min(8, n_heads)
    final_w = final_norm_w.reshape(1, d)
    tm_rows = _pick(t, (256, 128))
    tm = _pick(t, (1024, 512, 256))
    blk = min(ATTN_BLOCK, t)
    ones_a = jnp.ones((blk, blk), F32)
    after_tri = jnp.tril(ones_a, -1).astype(BF16)
    before_tri = jnp.triu(ones_a, 1).astype(BF16)
    ones_c = jnp.ones((HG_CHUNK, HG_CHUNK), F32)
    tri_lo, tri_up = jnp.tril(ones_c).astype(BF16), jnp.triu(ones_c).astype(BF16)
    cx, cy, cc = lax.axis_index("x"), lax.axis_index("y"), lax.axis_index("c")
    ids = jnp.stack([cc, 2 * cx + cy]).astype(jnp.int32)

    shards = [w_in[0], w_out[0], w_up[0], w_down[0]]
    cast = [_cast_to_slot(s, ids, f"cast_w{i}") for i, s in enumerate(shards)]
    (g_in,) = _gather_weights(cast[:1], "gather_w_in")
    d_ff = N_CHIPS * w_up.shape[2]
    cs_in, cs_up = g_in.shape[2], w_up.shape[2]
    tn_in = _pick(cs_in, (1792, 896, 512, 256, 128))
    tn_up = _pick(cs_up, (1024, 512, 256))
    tn_d = _pick(d, (1024, 512, 256))
    tk_d = _pick(d, (2048, 1024, 512))

    (u,), _ = _rows(lambda r, c_: ([r[0] * _rstd(r[0]) * c_[0]], []), [xs], [attn_norm_w], [(d, BF16)], [],
                    tm=tm_rows, name="norm_in")
    (proj,) = _mm_nn(u, g_in, [F32], tm=_pick(t, (512, 256)), tn=tn_in, tk=tk_d, name="proj_in")
    o_a, mix_a, sb_tot, g_out, g_up, g_down = _attn_fwd(proj, after_tri, sb_norm_w, n_heads, "sb_fwd", slots=cast[1:])
    w_out_all = g_out.reshape(1, d, d)
    w_down_all = g_down.reshape(1, d_ff, d)
    o_b, mix_b, states = _hgrn_fwd(proj, lb_logits, hg_norm_w, tri_lo, n_heads, hps, "hg_fwd")
    mix = jnp.concatenate([mix_a, mix_b], axis=1)
    (h1,) = _mm_nn(mix, w_out_all, [F32], tm=tm, tn=tn_d, tk=tk_d, name="proj_out",
                   epi=lambda acc, res: (acc + res,), extras=(xs,))
    (mn,), _ = _rows(lambda r, c_: ([r[0] * _rstd(r[0]) * c_[0]], []), [h1], [mlp_norm_w], [(d, BF16)], [],
                     tm=tm_rows, name="norm_mlp")
    up_b, act = _mm_nn(mn, g_up, [BF16, BF16], tm=tm, tn=tn_up, tk=tk_d, name="mlp_up",
                       epi=lambda acc: (acc, jnp.square(jnp.maximum(acc, 0.0))))
    (h2,) = _mm_nn(act, w_down_all, [F32], tm=tm, tn=tn_d, tk=_pick(d_ff, (2048, 1024)), name="mlp_down",
                   epi=lambda acc, res: (acc + res,), extras=(h1,))

    def head(rows, consts):
        hh, tg = rows
        w = consts[0]
        n = hh * _rstd(hh)
        err = n * w - tg
        dhh, dw_rows = _rms_bwd(hh, w, err * (1.0 / d))
        return [dhh, dhh], [_colsum(dw_rows), _colsum(err * err)]

    (dh2, dh2_b), (g_final, loss_cols) = _rows(head, [h2, tgt], [final_w], [(d, F32), (d, BF16)], [d, d],
                                                 tm=tm_rows, name="loss_head")

    (dup,) = _mm_nt(dh2_b, w_down_all, [BF16], tm=tm, tn=_pick(d_ff, (1024, 512)), tk=tk_d, name="mlp_down_dx",
                    epi=lambda acc, upv: (acc * (2.0 * jnp.maximum(upv.astype(F32), 0.0)),), extras=(up_b,))
    gw_down = _mm_tn(act, dh2_b, 1, tm=_pick(d_ff, (1024, 512)), tn=tn_d, tk=_pick(t, (1024, 512, 256)),
                     name="mlp_down_dw")
    (dmn,) = _mm_nt(dup, g_up, [F32], tm=tm, tn=tn_d, tk=_pick(cs_up, (2048, 1024, 512)), name="mlp_up_dx")
    gw_up = _mm_tn(mn, dup, N_CHIPS, tm=tn_d, tn=tn_up, tk=_pick(t, (1024, 512, 256)), name="mlp_up_dw")

    def pair_sums(grads, tag):
        theirs = _sibling_swap(grads, "grads_to_sibling_" + tag)
        return [_pair_sum(g, b, ids, f"grads_pair_sum_{tag}{i}") for i, (g, b) in enumerate(zip(grads, theirs))]

    pair_mlp = pair_sums([gw_up, gw_down.reshape(N_CHIPS, d_ff // N_CHIPS, d)], "mlp")

    def norm_back(rows, consts):
        xx, dy, skip = rows
        dx, dw_rows = _rms_bwd(xx, consts[0], dy)
        tot = dx + skip
        return [tot, tot], [_colsum(dw_rows)]

    (dh1, dh1_b), (g_mlp_norm,) = _rows(norm_back, [h1, dmn, dh2], [mlp_norm_w], [(d, F32), (d, BF16)], [d],
                                         tm=tm_rows, name="norm_mlp_bwd")

    (dmix,) = _mm_nt(dh1_b, w_out_all, [F32], tm=tm, tn=tn_d, tk=tk_d, name="proj_out_dx")
    gw_out = _mm_tn(mix, dh1_b, 1, tm=tn_d, tn=tn_d, tk=_pick(t, (1024, 512, 256)), name="proj_out_dw")

    def sb_norm_back(rows, consts):
        dx, dw_rows = _heads_map(lambda o, dy: _rms_bwd(o, consts[0], dy), width, *rows)
        dw = sum(_colsum(dw_rows[:, h * HEAD:(h + 1) * HEAD]) for h in range(n_heads))
        return [dx], [dw]

    (do_a,), (g_sb_norm,) = _rows(sb_norm_back, [o_a, (dmix, width, 0)], [sb_norm_w], [(width, F32)], [HEAD],
                                  tm=tm_rows, name="sb_norm_bwd")
    dq_a, dk_a, dv_a, *landed_mlp = _attn_bwd(proj, sb_tot, do_a, after_tri, before_tri, n_heads, "sb_bwd",
                                              sums=[p[0] for p in pair_mlp])

    def hg_out_back(rows, consts):
        def one(o, gate, dy):
            sg = _sigmoid(gate)
            silu = gate * sg
            n = o * _rstd(o) * consts[0]
            do, dw_rows = _rms_bwd(o, consts[0], dy * silu)
            return do, dy * n * (sg * (1.0 + gate * (1.0 - sg))), dw_rows
        do, dgate, dw_rows = _heads_map(one, width, *rows)
        dw = sum(_colsum(dw_rows[:, h * HEAD:(h + 1) * HEAD]) for h in range(n_heads))
        return [do, dgate], [dw]

    (do_b, dgate), (g_hg_norm,) = _rows(hg_out_back, [o_b, (proj, width, 6), (dmix, width, 1)], [hg_norm_w],
                                         [(width, F32)] * 2, [HEAD], tm=tm_rows, name="hg_out_bwd")
    dhq, dhf, dhi, dlb = _hgrn_bwd(proj, do_b, states, lb_logits, tri_lo, tri_up, n_heads, hps, "hg_bwd")

    (dproj,), _ = _rows(lambda r, _c: ([jnp.concatenate([p.astype(BF16) for p in r], axis=1)], []),
                        [dq_a, dk_a, dv_a, dhq, dhf, dhi, dgate], [], [(7 * width, BF16)], [],
                        tm=tm_rows, name="pack_dproj")
    gw_in = _mm_tn(u, dproj, N_CHIPS, tm=tn_d, tn=tn_in, tk=_pick(t, (1024, 512, 256)), name="proj_in_dw")
    pair_mix = pair_sums([gw_in, gw_out.reshape(N_CHIPS, d // N_CHIPS, d)], "mix")
    du, *landed_mix = _mm_nt(dproj, g_in, [F32], tm=tm, tn=tn_d, tk=_pick(cs_in, (1792, 896, 512, 256, 128)),
                             name="proj_in_dx", sums=[p[0] for p in pair_mix])
    (dx,), (g_attn_norm,) = _rows(lambda r, c_: (lambda dxx, dwr: ([dxx + r[2]], [_colsum(dwr)]))(
        *_rms_bwd(r[0], c_[0], r[1])), [xs, du, dh1], [attn_norm_w], [(d, F32)], [d], tm=tm_rows, name="norm_in_bwd")

    halves = [_final_sum(p[1], r, ids, f"grads_final_sum{i}")
              for i, (p, r) in enumerate(zip(pair_mix + pair_mlp, list(landed_mix) + list(landed_mlp)))]
    g_w_in, g_w_out, g_w_up, g_w_down = _sibling_join(halves, "grads_join")

    pieces = [g_attn_norm, g_mlp_norm, g_final, g_sb_norm, g_hg_norm, dlb, loss_cols]
    sizes = [p.shape[1] for p in pieces]
    flat = jnp.concatenate(pieces, axis=1)
    n_small = -(-flat.shape[1] // 1024) * 1024
    flat = jnp.pad(flat, ((0, 0), (0, n_small - flat.shape[1]))).reshape(8, n_small // 8)
    flat = _all_sum_small(flat, "small_all_sum").reshape(1, n_small)
    offs = [sum(sizes[:i]) for i in range(len(sizes))]
    g_attn_norm, g_mlp_norm, g_final, g_sb_norm, g_hg_norm, dlb, loss_cols = [
        flat[:, o:o + s] for o, s in zip(offs, sizes)]

    def small_tail(lbl_ref, dlb_ref, loss_ref, glb_ref, out_ref):
        lb = _lower_bound(lbl_ref[...])
        g0 = dlb_ref[...] * lb * (1.0 - lb)
        glb_ref[0:1, :] = g0
        glb_ref[1:2, :] = -g0
        out_ref[...] = jnp.zeros_like(out_ref) + 0.5 * jnp.sum(loss_ref[...]) * (1.0 / d)

    vm = pl.BlockSpec(memory_space=pltpu.VMEM)
    g_lb, loss11 = pl.pallas_call(
        small_tail, name="small_tail", in_specs=[vm, vm, vm], out_specs=[vm, vm],
        out_shape=[jax.ShapeDtypeStruct(lb_logits.shape, F32), jax.ShapeDtypeStruct((1, 128), F32)],
    )(lb_logits, dlb, loss_cols)
    loss = loss11[0, 0]

    names = ["attn_norm_w", "w_in", "lb_logits", "sb_norm_w", "hg_norm_w", "w_out", "mlp_norm_w", "w_up", "w_down",
             "final_norm_w"]
    ws = [attn_norm_w, w_in[0], lb_logits, sb_norm_w, hg_norm_w, w_out[0], mlp_norm_w, w_up[0], w_down[0], final_w]
    gs = [g_attn_norm, g_w_in, g_lb, g_sb_norm, g_hg_norm, g_w_out, g_mlp_norm, g_w_up, g_w_down, g_final]
    ms = [m_attn_norm_w, m_w_in[0], m_lb_logits, m_sb_norm_w, m_hg_norm_w, m_w_out[0], m_mlp_norm_w, m_w_up[0],
          m_w_down[0], m_final_norm_w.reshape(1, d)]
    vs = [v_attn_norm_w, v_w_in[0], v_lb_logits, v_sb_norm_w, v_hg_norm_w, v_w_out[0], v_mlp_norm_w, v_w_up[0],
          v_w_down[0], v_final_norm_w.reshape(1, d)]
    shapes = [attn_norm_w.shape, w_in.shape, lb_logits.shape, sb_norm_w.shape, hg_norm_w.shape, w_out.shape,
              mlp_norm_w.shape, w_up.shape, w_down.shape, final_norm_w.shape]
    deltas, new_ms, new_vs = [], [], []
    for nm, w_, g_, m_, v_ in zip(names, ws, gs, ms, vs):
        dl, m2, v2 = _adamw(w_, g_, m_, v_, "adamw_" + nm)
        deltas.append(dl)
        new_ms.append(m2)
        new_vs.append(v2)

    def shaped(lst):
        return [a.reshape(s) for a, s in zip(lst, shapes)]

    return (loss, dx[None], *shaped(gs), *shaped(deltas), *shaped(new_ms), *shaped(new_vs))
```

```python
import functools

import jax
import jax.numpy as jnp
from jax import lax
from jax.experimental import pallas as pl
from jax.experimental.pallas import tpu as pltpu

F32 = jnp.float32
BF16 = jnp.bfloat16
MESH = pl.DeviceIdType.MESH

HEAD = 128
NORM_EPS = 1e-5
N_CHIPS = 4
ATTN_BLOCK = 256
ATTN_ROWS = 1024
ATTN_DEAD = -110.0
HG_CHUNK = 32
HG_ROWS = 256
HG_UNROLL = 2
VMEM_LIMIT = 56 * 1024 * 1024

ADAM_LR = 0.001
ADAM_B1 = 0.9
ADAM_B2 = 0.999
ADAM_EPS = 1e-08
ADAM_WD = 0.01
ADAM_STEP = 10


def _pick(n, cands):
    for c in cands:
        if n % c == 0:
            return c
    return n


def _params(sem):
    return pltpu.CompilerParams(dimension_semantics=sem, vmem_limit_bytes=VMEM_LIMIT)


def _dot(a, b):
    return jnp.dot(a, b, preferred_element_type=F32)


def _dot_nt(a, b):
    return lax.dot_general(a, b, (((1,), (1,)), ((), ())), preferred_element_type=F32)


def _dot_tn(a, b):
    return lax.dot_general(a, b, (((0,), (0,)), ((), ())), preferred_element_type=F32)


def _hilo(x):
    hi = x.astype(BF16)
    return hi, (x - hi.astype(F32)).astype(BF16)


def _dot_split(tri, x):
    hi, lo = _hilo(x)
    return _dot(tri, hi) + _dot(tri, lo)


def _dot3(dot, a, b):
    return dot(a[0], b[0]) + (dot(a[0], b[1]) + dot(a[1], b[0]))


def _sigmoid(x):
    return 1.0 / (1.0 + jnp.exp(-x))


def _mm_body(kind, nk, n_extra, n_out, epi):
    dot = {"nn": _dot, "nt": _dot_nt, "tn": _dot_tn}[kind]

    def finish(acc, extra_refs, out_refs):
        res = epi(acc, *[e[...] for e in extra_refs]) if epi is not None else (acc,)
        for o, r in zip(out_refs, res):
            o[...] = r.astype(o.dtype)

    def body(a_ref, b_ref, *rest):
        extra_refs = rest[:n_extra]
        out_refs = rest[n_extra:n_extra + n_out]
        if nk == 1:
            finish(dot(a_ref[...], b_ref[...]), extra_refs, out_refs)
            return
        acc_ref = rest[n_extra + n_out]
        k = pl.program_id(2)

        @pl.when(k == 0)
        def _():
            acc_ref[...] = jnp.zeros_like(acc_ref)

        acc_ref[...] += dot(a_ref[...], b_ref[...])

        @pl.when(k == nk - 1)
        def _():
            finish(acc_ref[...], extra_refs, out_refs)

    return body


def _mm_nn(a, w, out_dtypes, *, tm, tn, tk, name, epi=None, extras=()):
    m, r = a.shape
    p, _, c = w.shape
    npc = c // tn
    nk = r // tk
    body = _mm_body("nn", nk, len(extras), len(out_dtypes), epi)
    tile = pl.BlockSpec((tm, tn), lambda i, j, k: (i, j))
    return pl.pallas_call(
        body, name=name,
        grid=(m // tm, p * npc, nk),
        in_specs=[pl.BlockSpec((tm, tk), lambda i, j, k: (i, k)),
                  pl.BlockSpec((None, tk, tn), lambda i, j, k: (j // npc, k, j % npc))] + [tile] * len(extras),
        out_specs=[tile] * len(out_dtypes),
        out_shape=[jax.ShapeDtypeStruct((m, p * c), d) for d in out_dtypes],
        scratch_shapes=[pltpu.VMEM((tm, tn), F32)] if nk > 1 else [],
        compiler_params=_params(("parallel", "parallel", "arbitrary")),
    )(a, w, *extras)


def _mm_nt(a, w, out_dtypes, *, tm, tn, tk, name, epi=None, extras=(), sums=()):
    m, _ = a.shape
    p, r, c = w.shape
    kpc = c // tk
    nk = p * kpc
    n_e, n_o, n_s = len(extras), len(out_dtypes), len(sums)
    grid = (m // tm, r // tn, nk)
    inner = _mm_body("nt", nk, n_e, n_o, epi)

    def body(a_ref, b_ref, *rest):
        extra, sum_refs = rest[:n_e], rest[n_e:n_e + n_s]
        outs, landed = rest[n_e + n_s:n_e + n_s + n_o], rest[n_e + n_s + n_o:n_e + 2 * n_s + n_o]
        scratch = rest[n_e + 2 * n_s + n_o:]
        acc = scratch[:1] if nk > 1 else ()
        if n_s:
            ids = [pl.program_id(ax) for ax in range(3)]
            begin, end = _scatter_copies(sum_refs, landed, *scratch[len(acc):])
            pl.when((ids[0] == 0) & (ids[1] == 0) & (ids[2] == 0))(begin)
        inner(a_ref, b_ref, *extra, *outs, *acc)
        if n_s:
            pl.when((ids[0] == grid[0] - 1) & (ids[1] == grid[1] - 1) & (ids[2] == grid[2] - 1))(end)

    tile = pl.BlockSpec((tm, tn), lambda i, j, k: (i, j))
    return pl.pallas_call(
        body, name=name,
        grid=grid,
        in_specs=[pl.BlockSpec((tm, tk), lambda i, j, k: (i, k)),
                  pl.BlockSpec((None, tn, tk), lambda i, j, k: (k // kpc, j, k % kpc))] + [tile] * n_e + [ANY] * n_s,
        out_specs=[tile] * n_o + [ANY] * n_s,
        out_shape=[jax.ShapeDtypeStruct((m, r), d) for d in out_dtypes]
                  + [jax.ShapeDtypeStruct((3,) + s.shape[1:], s.dtype) for s in sums],
        scratch_shapes=([pltpu.VMEM((tm, tn), F32)] if nk > 1 else [])
                       + ([pltpu.SemaphoreType.DMA((3 * n_s,))] * 2 if n_s else []),
        compiler_params=_params(("arbitrary",) * 3 if n_s else ("parallel", "parallel", "arbitrary")),
    )(a, w, *extras, *sums)


def _mm_tn(a, g, p, *, tm, tn, tk, name):
    t, r = a.shape
    c = g.shape[1] // p
    npc = c // tn
    nk = t // tk
    body = _mm_body("tn", nk, 0, 1, None)
    return pl.pallas_call(
        body, name=name,
        grid=(r // tm, p * npc, nk),
        in_specs=[pl.BlockSpec((tk, tm), lambda i, j, k: (k, i)),
                  pl.BlockSpec((tk, tn), lambda i, j, k: (k, j))],
        out_specs=[pl.BlockSpec((None, tm, tn), lambda i, j, k: (j // npc, i, j % npc))],
        out_shape=[jax.ShapeDtypeStruct((p, r, c), F32)],
        scratch_shapes=[pltpu.VMEM((tm, tn), F32)] if nk > 1 else [],
        compiler_params=_params(("parallel", "parallel", "arbitrary")),
    )(a, g)[0]


def _rows(fn, row_ins, const_ins, row_outs, acc_outs, *, tm, name):
    specs, arrays = [], []
    t = None
    for item in row_ins:
        if isinstance(item, tuple):
            arr, width, cb = item
            specs.append(pl.BlockSpec((tm, width), functools.partial(lambda i, cb: (i, cb), cb=cb)))
        else:
            arr = item
            specs.append(pl.BlockSpec((tm, arr.shape[1]), lambda i: (i, 0)))
        arrays.append(arr)
        t = arr.shape[0]
    for arr in const_ins:
        specs.append(pl.BlockSpec(arr.shape, lambda i: (0, 0)))
        arrays.append(arr)
    n_in, n_row, n_acc = len(arrays), len(row_outs), len(acc_outs)

    def body(*refs):
        ins = [r[...] for r in refs[:n_in]]
        outs = refs[n_in:]
        row_res, acc_res = fn(ins[:len(row_ins)], ins[len(row_ins):])
        for o, r in zip(outs[:n_row], row_res):
            o[...] = r.astype(o.dtype)
        if n_acc:
            i = pl.program_id(0)

            @pl.when(i == 0)
            def _():
                for o in outs[n_row:]:
                    o[...] = jnp.zeros_like(o)

            for o, r in zip(outs[n_row:], acc_res):
                o[...] += r

    res = pl.pallas_call(
        body, name=name,
        grid=(t // tm,),
        in_specs=specs,
        out_specs=[pl.BlockSpec((tm, c), lambda i: (i, 0)) for c, _ in row_outs]
                  + [pl.BlockSpec((1, c), lambda i: (0, 0)) for c in acc_outs],
        out_shape=[jax.ShapeDtypeStruct((t, c), d) for c, d in row_outs]
                  + [jax.ShapeDtypeStruct((1, c), F32) for c in acc_outs],
        compiler_params=_params(("arbitrary",)),
    )(*arrays)
    return res[:n_row], res[n_row:]


def _rstd(x):
    return lax.rsqrt(jnp.mean(x * x, axis=-1, keepdims=True) + NORM_EPS)


def _rms_bwd(x, w, dy):
    r = _rstd(x)
    n = x * r
    dn = dy * w
    dx = r * (dn - n * jnp.mean(dn * n, axis=-1, keepdims=True))
    return dx, dy * n


def _colsum(x):
    return jnp.sum(x, axis=0, keepdims=True)


def _heads_map(fn, width, *tiles):
    outs = None
    for h in range(width // HEAD):
        res = fn(*[t[:, h * HEAD:(h + 1) * HEAD] for t in tiles])
        if outs is None:
            outs = [[] for _ in res]
        for lst, r in zip(outs, res):
            lst.append(r)
    return [jnp.concatenate(lst, axis=1) for lst in outs]


def _log_one_minus_beta(z):
    return -(jnp.maximum(z, 0.0) + jnp.log(1.0 + jnp.exp(-jnp.abs(z))))


def _attn_fwd(proj, after_tri, norm_w, n_heads, name, slots=()):
    t = proj.shape[0]
    blk = min(ATTN_BLOCK, t)
    qb = min(ATTN_ROWS, t)
    ns = qb // blk
    nq = t // qb
    n = len(slots)
    scale = HEAD ** -0.5

    def body(q_ref, k_ref, v_ref, tri_ref, w_ref, *rest):
        o_ref, mix_ref, tot_ref = rest[n:n + 3]
        i = pl.program_id(1)
        if n:
            begin, end = _gather_copies(rest[n + 3:2 * n + 3], *rest[2 * n + 3:])
            pl.when((pl.program_id(0) == 0) & (i == 0))(begin)
        q = (q_ref[...] * scale).astype(BF16)
        tri = tri_ref[...]

        def part(r0, j, acc_l, acc_o, masked):
            sl = pl.ds(pl.multiple_of(j * blk, blk), blk)
            m = qb - r0
            z = _dot_nt(q[r0:, :], k_ref[sl, :].astype(BF16))
            lm = _log_one_minus_beta(z)
            if masked:
                mask = lax.broadcasted_iota(jnp.int32, (m, blk), 1) < lax.broadcasted_iota(jnp.int32, (m, blk), 0)
                lmm = jnp.where(mask, lm, 0.0)
            else:
                lmm = lm
            w = jnp.exp(z + lm + acc_l[r0:, :] + _dot(lmm.astype(BF16), tri))
            if masked:
                w = jnp.where(mask, w, 0.0)
            new_l = acc_l[r0:, :] + jnp.sum(lmm, axis=1, keepdims=True)
            new_o = acc_o[r0:, :] + _dot(w.astype(BF16), v_ref[sl, :].astype(BF16))
            if r0:
                new_l = jnp.concatenate([acc_l[:r0, :], new_l], axis=0)
                new_o = jnp.concatenate([acc_o[:r0, :], new_o], axis=0)
            return new_l, new_o

        acc = (jnp.zeros((qb, 1), F32), jnp.zeros((qb, HEAD), F32))
        for jr in reversed(range(ns)):
            acc = part(jr * blk, ns * i + jr, *acc, True)

        def more(c):
            return (c[0] < ns * i) & (jnp.max(c[1]) > ATTN_DEAD)

        def step(c):
            return (c[0] + 1,) + part(0, ns * i - 1 - c[0], c[1], c[2], False)

        swept, acc_l, acc_o = lax.while_loop(more, step, (jnp.int32(0),) + acc)
        o_ref[...] = acc_o
        mix_ref[...] = (acc_o * _rstd(acc_o) * w_ref[...]).astype(BF16)
        first = (ns * i - swept).astype(F32)
        tot_ref[...] = jnp.where(lax.broadcasted_iota(jnp.int32, (qb, HEAD), 1) == 1, first, acc_l)
        if n:
            pl.when((pl.program_id(0) == n_heads - 1) & (i == nq - 1))(end)

    width = n_heads * HEAD
    qblk = pl.BlockSpec((qb, HEAD), lambda h, i: (i, h))
    return pl.pallas_call(
        body, name=name,
        grid=(n_heads, nq),
        in_specs=[qblk,
                  pl.BlockSpec((t, HEAD), lambda h, i: (0, n_heads + h)),
                  pl.BlockSpec((t, HEAD), lambda h, i: (0, 2 * n_heads + h)),
                  pl.BlockSpec((blk, blk), lambda h, i: (0, 0)),
                  pl.BlockSpec((1, HEAD), lambda h, i: (0, 0))] + [ANY] * n,
        out_specs=[qblk, qblk, qblk] + [ANY] * n,
        out_shape=[jax.ShapeDtypeStruct((t, width), F32), jax.ShapeDtypeStruct((t, width), BF16),
                   jax.ShapeDtypeStruct((t, width), F32)] + [jax.ShapeDtypeStruct(s.shape, s.dtype) for s in slots],
        input_output_aliases={5 + w: 3 + w for w in range(n)},
        scratch_shapes=[pltpu.SemaphoreType.DMA((6 * n,))] * 2 if n else [],
        compiler_params=_params(("arbitrary", "arbitrary")),
    )(proj, proj, proj, after_tri, norm_w, *slots)


def _attn_bwd(proj, tot, do, after_tri, before_tri, n_heads, name, sums=()):
    t = proj.shape[0]
    blk = min(ATTN_BLOCK, t)
    qb = min(ATTN_ROWS, t)
    ns = qb // blk
    nq = t // qb
    n = len(sums)
    scale = HEAD ** -0.5

    def body(q_ref, k_ref, v_ref, tot_ref, do_ref, after_ref, before_ref, *rest):
        dq_ref, dk_ref, dv_ref = rest[n:n + 3]
        i = pl.program_id(1)
        if n:
            begin, end = _scatter_copies(rest[:n], rest[n + 3:2 * n + 3], *rest[2 * n + 3:])
            pl.when((pl.program_id(0) == 0) & (i == 0))(begin)

        @pl.when(i == 0)
        def _():
            dk_ref[...] = jnp.zeros_like(dk_ref)
            dv_ref[...] = jnp.zeros_like(dv_ref)

        q = (q_ref[...] * scale).astype(BF16)
        dob = do_ref[...].astype(BF16)
        total = tot_ref[:, 0:1]
        after_tri = after_ref[...]
        before = before_ref[...]

        def part(r0, j, seen_l, seen_g, dq, masked):
            sl = pl.ds(pl.multiple_of(j * blk, blk), blk)
            m = qb - r0
            qq, dd = q[r0:, :], dob[r0:, :]
            kb = k_ref[sl, :].astype(BF16)
            z = _dot_nt(qq, kb)
            lm = _log_one_minus_beta(z)
            if masked:
                mask = lax.broadcasted_iota(jnp.int32, (m, blk), 1) < lax.broadcasted_iota(jnp.int32, (m, blk), 0)
                lmm = jnp.where(mask, lm, 0.0)
            else:
                lmm = lm
            row_l = jnp.sum(lmm, axis=1, keepdims=True)
            after = (total[r0:, :] - seen_l[r0:, :] - row_l) + _dot(lmm.astype(BF16), after_tri)
            w = jnp.exp(z + lm + after)
            if masked:
                w = jnp.where(mask, w, 0.0)
            sig = jnp.exp(z + lm)
            g = w * _dot_nt(dd, v_ref[sl, :].astype(BF16))
            g_before = seen_g[r0:, :] + _dot(g.astype(BF16), before)
            dz = g * (1.0 - sig) - g_before * sig
            if masked:
                dz = jnp.where(mask, dz, 0.0)
            dzb = dz.astype(BF16)
            dk_ref[sl, :] += _dot_tn(dzb, qq)
            dv_ref[sl, :] += _dot_tn(w.astype(BF16), dd)
            new = (seen_l[r0:, :] + row_l,
                   seen_g[r0:, :] + jnp.sum(g, axis=1, keepdims=True), dq[r0:, :] + _dot(dzb, kb))
            if r0:
                new = tuple(jnp.concatenate([old[:r0, :], n], axis=0) for old, n in zip((seen_l, seen_g, dq), new))
            return new

        zero = jnp.zeros((qb, 1), F32)
        first = jnp.max(tot_ref[0:8, 1:2]).astype(jnp.int32)
        carry = lax.fori_loop(first, ns * i, lambda j, c: part(0, j, *c, False),
                              (zero, zero, jnp.zeros((qb, HEAD), F32)))
        for jr in range(ns):
            carry = part(jr * blk, ns * i + jr, *carry, True)
        dq_ref[...] = carry[2] * scale
        if n:
            pl.when((pl.program_id(0) == n_heads - 1) & (i == nq - 1))(end)

    width = n_heads * HEAD
    qblk = pl.BlockSpec((qb, HEAD), lambda h, i: (i, h))
    full = pl.BlockSpec((t, HEAD), lambda h, i: (0, h))
    tri = pl.BlockSpec((blk, blk), lambda h, i: (0, 0))
    return pl.pallas_call(
        body, name=name,
        grid=(n_heads, nq),
        in_specs=[qblk,
                  pl.BlockSpec((t, HEAD), lambda h, i: (0, n_heads + h)),
                  pl.BlockSpec((t, HEAD), lambda h, i: (0, 2 * n_heads + h)),
                  qblk, qblk, tri, tri] + [ANY] * n,
        out_specs=[qblk, full, full] + [ANY] * n,
        out_shape=[jax.ShapeDtypeStruct((t, width), F32)] * 3
                  + [jax.ShapeDtypeStruct((3,) + s.shape[1:], s.dtype) for s in sums],
        scratch_shapes=[pltpu.SemaphoreType.DMA((3 * n,))] * 2 if n else [],
        compiler_params=_params(("arbitrary", "arbitrary")),
    )(proj, proj, proj, tot, do, after_tri, before_tri, *sums)


def _lower_bound(logits):
    l0, l1 = logits[0:1, :], logits[1:2, :]
    mx = jnp.maximum(l0, l1)
    e0, e1 = jnp.exp(l0 - mx), jnp.exp(l1 - mx)
    return e0 / (e0 + e1)


def _hg_chunk(qc, kc, gc, tri_lo):
    c = qc.shape[0]
    cum = _dot_split(tri_lo, gc)
    mid = cum[c // 2 - 1:c // 2, :]
    last = cum[c - 1:c, :]
    qt = qc * jnp.exp(cum - mid)
    kt = kc * jnp.exp(mid - cum)
    qe = qc * jnp.exp(cum)
    kd = kc * jnp.exp(last - cum)
    return cum, mid, last, qt, kt, qe, kd


def _hgrn_fwd(proj, lb_logits, norm_w, tri_lo, n_heads, heads_per_step, name):
    t = proj.shape[0]
    bt = min(HG_ROWS, t)
    c = HG_CHUNK
    nc = bt // c
    hw = heads_per_step * HEAD
    width = n_heads * HEAD
    col0 = 3 * width // hw

    def body(hq_ref, hf_ref, hi_ref, hgate_ref, lbl_ref, w_ref, tri_ref, o_ref, mix_ref, st_ref,
             state, q_scr, k_scr, g_scr):
        @pl.when(pl.program_id(1) == 0)
        def _():
            state[...] = jnp.zeros_like(state)

        lb = _lower_bound(lbl_ref[...])
        f = hf_ref[...]
        g_scr[...] = jnp.log(lb + (1.0 - lb) * _sigmoid(f))
        k_scr[...] = (1.0 - lb) * _sigmoid(-f)
        hq = hq_ref[...]
        q_scr[...] = hq * _sigmoid(hq)
        tri = tri_ref[...]
        causal = lax.broadcasted_iota(jnp.int32, (c, c), 1) <= lax.broadcasted_iota(jnp.int32, (c, c), 0)

        def chunk(ci, carry):
            r = pl.ds(pl.multiple_of(ci * c, c), c)
            vc = hi_ref[r, :].astype(BF16)
            _, _, last, qt, kt, qe, kd = _hg_chunk(q_scr[r, :], k_scr[r, :], g_scr[r, :], tri)
            qt, kt, qe, kd = qt.astype(BF16), kt.astype(BF16), qe.astype(BF16), kd.astype(BF16)
            e_last = jnp.exp(last)
            old = [state[h] for h in range(heads_per_step)]
            outs, new = [], []
            for h in range(heads_per_step):
                cs = slice(h * HEAD, (h + 1) * HEAD)
                a = jnp.where(causal, _dot_nt(qt[:, cs], kt[:, cs]), 0.0)
                outs.append(_dot(a.astype(BF16), vc[:, cs]) + _dot_nt(qe[:, cs], old[h].astype(BF16)))
                new.append(old[h] * e_last[:, cs] + _dot_tn(vc[:, cs], kd[:, cs]))
            for h in range(heads_per_step):
                st_ref[ci, :, h * HEAD:(h + 1) * HEAD] = old[h]
                state[h] = new[h]
            o_ref[r, :] = jnp.concatenate(outs, axis=1)
            return carry

        lax.fori_loop(0, nc, chunk, 0, unroll=HG_UNROLL)

        def finish(o, gate):
            return ((o * _rstd(o) * w_ref[...]) * (gate * _sigmoid(gate)),)

        mix_ref[...] = _heads_map(finish, hw, o_ref[...], hgate_ref[...])[0].astype(BF16)

    def col(group):
        return pl.BlockSpec((bt, hw), functools.partial(lambda hp, tb, g: (tb, col0 + g * (width // hw) + hp), g=group))

    blk = pl.BlockSpec((bt, hw), lambda hp, tb: (tb, hp))
    return pl.pallas_call(
        body, name=name,
        grid=(n_heads // heads_per_step, t // bt),
        in_specs=[col(0), col(1), col(2), col(3),
                  pl.BlockSpec((2, hw), lambda hp, tb: (0, hp)),
                  pl.BlockSpec((1, HEAD), lambda hp, tb: (0, 0)),
                  pl.BlockSpec((c, c), lambda hp, tb: (0, 0))],
        out_specs=[blk, blk, pl.BlockSpec((nc, HEAD, hw), lambda hp, tb: (tb, 0, hp))],
        out_shape=[jax.ShapeDtypeStruct((t, width), F32), jax.ShapeDtypeStruct((t, width), BF16),
                   jax.ShapeDtypeStruct((t // c, HEAD, width), F32)],
        scratch_shapes=[pltpu.VMEM((heads_per_step, HEAD, HEAD), F32)] + [pltpu.VMEM((bt, hw), F32)] * 3,
        compiler_params=_params(("parallel", "arbitrary")),
    )(proj, proj, proj, proj, lb_logits, norm_w, tri_lo)


def _hgrn_bwd(proj, do, states, lb_logits, tri_lo, tri_up, n_heads, heads_per_step, name):
    t = proj.shape[0]
    bt = min(HG_ROWS, t)
    c = HG_CHUNK
    nc = bt // c
    nb = t // bt
    hw = heads_per_step * HEAD
    width = n_heads * HEAD
    col0 = 3 * width // hw

    def body(hq_ref, hf_ref, hi_ref, do_ref, st_ref, lbl_ref, lo_ref, up_ref, dq_ref, df_ref, di_ref, dlb_ref,
             dstate, q_scr, k_scr, g_scr, dk_scr, dg_scr):
        @pl.when(pl.program_id(1) == 0)
        def _():
            dstate[...] = jnp.zeros_like(dstate)
            dlb_ref[...] = jnp.zeros_like(dlb_ref)

        lb = _lower_bound(lbl_ref[...])
        f = hf_ref[...]
        sg = _sigmoid(f)
        sgn = _sigmoid(-f)
        den = lb + (1.0 - lb) * sg
        g_scr[...] = jnp.log(den)
        k_scr[...] = (1.0 - lb) * sgn
        hq = hq_ref[...]
        sq = _sigmoid(hq)
        q_scr[...] = hq * sq
        tri_lo_v = lo_ref[...]
        tri_up_v = up_ref[...]
        causal = lax.broadcasted_iota(jnp.int32, (c, c), 1) <= lax.broadcasted_iota(jnp.int32, (c, c), 0)
        last_row = lax.broadcasted_iota(jnp.int32, (c, hw), 0) == c - 1

        def chunk(cc, carry):
            ci = nc - 1 - cc
            r = pl.ds(pl.multiple_of(ci * c, c), c)
            qc, kc = q_scr[r, :], k_scr[r, :]
            cum, mid, last, qt, kt, qe, kd = _hg_chunk(qc, kc, g_scr[r, :], tri_lo_v)
            qt, kt, qe, kd, doc, vc = [_hilo(v) for v in (qt, kt, qe, kd, do_ref[r, :], hi_ref[r, :])]
            e_last = jnp.exp(last)
            sts = [st_ref[ci, :, h * HEAD:(h + 1) * HEAD] for h in range(heads_per_step)]
            dsts = [dstate[h] for h in range(heads_per_step)]
            di, dq_inter, dk_inter, dq_intra, dk_intra, st_sums, new = [], [], [], [], [], [], []
            for h in range(heads_per_step):
                cs = slice(h * HEAD, (h + 1) * HEAD)

                def head(pair):
                    return pair[0][:, cs], pair[1][:, cs]

                st, dst = _hilo(sts[h]), _hilo(dsts[h])
                a = _hilo(jnp.where(causal, _dot3(_dot_nt, head(qt), head(kt)), 0.0))
                da = _hilo(jnp.where(causal, _dot3(_dot_nt, head(doc), head(vc)), 0.0))
                di.append(_dot3(_dot_tn, a, head(doc)) + _dot3(_dot_nt, head(kd), dst))
                dq_inter.append(_dot3(_dot, head(doc), st))
                dk_inter.append(_dot3(_dot, head(vc), dst))
                dq_intra.append(_dot3(_dot, da, head(kt)))
                dk_intra.append(_dot3(_dot_tn, da, head(qt)))
                st_sums.append(_colsum(dsts[h] * sts[h]))
                new.append(dsts[h] * e_last[:, cs] + _dot3(_dot_tn, head(doc), head(qe)))

            def wide(parts):
                return jnp.concatenate(parts, axis=1)

            dq_inter = wide(dq_inter) * jnp.exp(cum)
            dk_inter = wide(dk_inter) * jnp.exp(last - cum)
            dq = wide(dq_intra) * jnp.exp(cum - mid) + dq_inter
            dk = wide(dk_intra) * jnp.exp(mid - cum) + dk_inter
            d_last = _colsum(kc * dk_inter) + e_last * wide(st_sums)
            dcum = qc * dq - kc * dk + jnp.where(last_row, d_last, 0.0)
            for h in range(heads_per_step):
                dstate[h] = new[h]
            di_ref[r, :] = wide(di)
            dq_ref[r, :] = dq
            dk_scr[r, :] = dk
            dg_scr[r, :] = _dot_split(tri_up_v, dcum)
            return carry

        lax.fori_loop(0, nc, chunk, 0, unroll=HG_UNROLL)

        e = (dg_scr[...] / den - dk_scr[...]) * sgn
        df_ref[...] = e * (1.0 - lb) * sg
        dlb_ref[...] += _colsum(e)
        dq_ref[...] = dq_ref[...] * (sq * (1.0 + hq * (1.0 - sq)))

    def col(group):
        return pl.BlockSpec((bt, hw), functools.partial(
            lambda hp, tb, g: (nb - 1 - tb, col0 + g * (width // hw) + hp), g=group))

    blk = pl.BlockSpec((bt, hw), lambda hp, tb: (nb - 1 - tb, hp))
    tri = pl.BlockSpec((c, c), lambda hp, tb: (0, 0))
    return pl.pallas_call(
        body, name=name,
        grid=(n_heads // heads_per_step, nb),
        in_specs=[col(0), col(1), col(2), blk,
                  pl.BlockSpec((nc, HEAD, hw), lambda hp, tb: (nb - 1 - tb, 0, hp)),
                  pl.BlockSpec((2, hw), lambda hp, tb: (0, hp)), tri, tri],
        out_specs=[blk, blk, blk, pl.BlockSpec((1, hw), lambda hp, tb: (0, hp))],
        out_shape=[jax.ShapeDtypeStruct((t, width), F32)] * 3 + [jax.ShapeDtypeStruct((1, width), F32)],
        scratch_shapes=[pltpu.VMEM((heads_per_step, HEAD, HEAD), F32)] + [pltpu.VMEM((bt, hw), F32)] * 5,
        compiler_params=_params(("parallel", "arbitrary")),
    )(proj, proj, proj, do, states, lb_logits, tri_lo, tri_up)


def _place():
    x, y, c = lax.axis_index("x"), lax.axis_index("y"), lax.axis_index("c")
    chips = [(1 - x, y), (x, 1 - y), (1 - x, 1 - y)]
    return x, y, c, chips


ANY = pl.BlockSpec(memory_space=pl.ANY)


def _cast_to_slot(shard, ids, name):
    r, c = shard.shape
    tm = _pick(r, (256, 128, 64, 32, 16))

    def body(ids_ref, s_ref, o_ref):
        o_ref[...] = s_ref[...].astype(BF16)

    return pl.pallas_call(
        body, name=name,
        grid_spec=pltpu.PrefetchScalarGridSpec(
            num_scalar_prefetch=1, grid=(r // tm,),
            in_specs=[pl.BlockSpec((tm, c), lambda i, ids: (i, 0))],
            out_specs=pl.BlockSpec((None, tm, c), lambda i, ids: (ids[1], i, 0))),
        out_shape=jax.ShapeDtypeStruct((N_CHIPS, r, c), BF16),
        compiler_params=_params(("parallel",)),
    )(ids, shard)


def _gather_copies(bufs, send, recv):
    x, y, c, chips = _place()
    mine = 2 * x + y

    def half(ref, who, core):
        h = ref.shape[-2] // 2
        return ref.at[who, pl.ds(core * h, h), :]

    def copy(w, k, rows, to):
        return pltpu.make_async_remote_copy(src_ref=rows, dst_ref=rows, send_sem=send.at[6 * w + k],
                                            recv_sem=recv.at[6 * w + k], device_id=to, device_id_type=MESH)

    def to_chips(w):
        return [copy(w, k, half(bufs[w], mine, c), (qx, qy, c)) for k, (qx, qy) in enumerate(chips)]

    def to_sibling(w):
        return [copy(w, 3 + k, half(bufs[w], 2 * qx + qy, c), (x, y, 1 - c)) for k, (qx, qy) in enumerate(chips)]

    def begin():
        for w in range(len(bufs)):
            for cp in to_chips(w):
                cp.start()

    def end():
        for w in range(len(bufs)):
            for k, (qx, qy) in enumerate(chips):
                copy(w, k, half(bufs[w], 2 * qx + qy, c), (x, y, c)).wait_recv()
                to_sibling(w)[k].start()
        for w in range(len(bufs)):
            for k, (qx, qy) in enumerate(chips):
                copy(w, 3 + k, half(bufs[w], 2 * qx + qy, 1 - c), (x, y, c)).wait_recv()
        for w in range(len(bufs)):
            for cp in to_chips(w) + to_sibling(w):
                cp.wait_send()

    return begin, end


def _scatter_copies(ins, outs, send, recv):
    _, _, c, chips = _place()

    def copies():
        return [pltpu.make_async_remote_copy(
            src_ref=ins[w].at[2 * qx + qy], dst_ref=outs[w].at[k], send_sem=send.at[3 * w + k],
            recv_sem=recv.at[3 * w + k], device_id=(qx, qy, c), device_id_type=MESH)
            for w in range(len(ins)) for k, (qx, qy) in enumerate(chips)]

    def begin():
        for cp in copies():
            cp.start()

    def end():
        for cp in copies():
            cp.wait()

    return begin, end


def _gather_weights(slots, name):
    n = len(slots)

    def body(*refs):
        begin, end = _gather_copies(refs[n:2 * n], *refs[2 * n:])
        begin()
        end()

    return pl.pallas_call(
        body, name=name,
        in_specs=[ANY] * n, out_specs=[ANY] * n,
        out_shape=[jax.ShapeDtypeStruct(s.shape, s.dtype) for s in slots],
        input_output_aliases={w: w for w in range(n)},
        scratch_shapes=[pltpu.SemaphoreType.DMA((6 * n,)), pltpu.SemaphoreType.DMA((6 * n,))],
    )(*slots)


def _sibling_swap(grads, name):
    n = len(grads)

    def body(*refs):
        ins, outs = refs[:n], refs[n:2 * n]
        send, recv = refs[2 * n:]
        x, y, c, _ = _place()
        cps = []
        for w in range(n):
            h = ins[w].shape[1] // 2
            cps.append(pltpu.make_async_remote_copy(
                src_ref=ins[w].at[:, pl.ds((1 - c) * h, h), :], dst_ref=outs[w], send_sem=send.at[w],
                recv_sem=recv.at[w], device_id=(x, y, 1 - c), device_id_type=MESH))
            cps[-1].start()
        for cp in cps:
            cp.wait()

    return pl.pallas_call(
        body, name=name, in_specs=[ANY] * n, out_specs=[ANY] * n,
        out_shape=[jax.ShapeDtypeStruct((g.shape[0], g.shape[1] // 2, g.shape[2]), g.dtype) for g in grads],
        scratch_shapes=[pltpu.SemaphoreType.DMA((n,)), pltpu.SemaphoreType.DMA((n,))],
    )(*grads)


def _sibling_join(fulls, name):
    n = len(fulls)

    def body(*refs):
        bufs = refs[n:2 * n]
        send, recv = refs[2 * n:]
        x, y, c, _ = _place()
        cps = []
        for w in range(n):
            h = bufs[w].shape[0] // 2
            rows = bufs[w].at[pl.ds(c * h, h), :]
            cps.append(pltpu.make_async_remote_copy(
                src_ref=rows, dst_ref=rows, send_sem=send.at[w], recv_sem=recv.at[w],
                device_id=(x, y, 1 - c), device_id_type=MESH))
            cps[-1].start()
        for w in range(n):
            h = bufs[w].shape[0] // 2
            theirs = bufs[w].at[pl.ds((1 - c) * h, h), :]
            pltpu.make_async_remote_copy(src_ref=theirs, dst_ref=theirs, send_sem=send.at[w], recv_sem=recv.at[w],
                                         device_id=(x, y, c), device_id_type=MESH).wait_recv()
        for cp in cps:
            cp.wait_send()

    return pl.pallas_call(
        body, name=name, in_specs=[ANY] * n, out_specs=[ANY] * n,
        out_shape=[jax.ShapeDtypeStruct(s.shape, s.dtype) for s in fulls],
        input_output_aliases={w: w for w in range(n)},
        scratch_shapes=[pltpu.SemaphoreType.DMA((n,)), pltpu.SemaphoreType.DMA((n,))],
    )(*fulls)


def _all_sum_small(vec, name):
    n = vec.shape[1]

    def body(v_ref, out_ref, buf, send, recv):
        x, y, c, _ = _place()
        me = 4 * x + 2 * y + c
        buf[me] = v_ref[...]
        peers = []
        for mask in range(1, 8):
            px = 1 - x if mask & 4 else x
            py = 1 - y if mask & 2 else y
            pc = 1 - c if mask & 1 else c
            peers.append((px, py, pc))
        cps = []
        for k, peer in enumerate(peers):
            cps.append(pltpu.make_async_remote_copy(src_ref=buf.at[me], dst_ref=buf.at[me], send_sem=send.at[k],
                                                    recv_sem=recv.at[k], device_id=peer, device_id_type=MESH))
            cps[-1].start()
        for k, (px, py, pc) in enumerate(peers):
            slot = buf.at[4 * px + 2 * py + pc]
            pltpu.make_async_remote_copy(src_ref=slot, dst_ref=slot, send_sem=send.at[k], recv_sem=recv.at[k],
                                         device_id=(x, y, c), device_id_type=MESH).wait_recv()
        for cp in cps:
            cp.wait_send()
        total = buf[0]
        for d in range(1, 8):
            total = total + buf[d]
        out_ref[...] = total

    vm = pl.BlockSpec(memory_space=pltpu.VMEM)
    return pl.pallas_call(
        body, name=name, in_specs=[vm], out_specs=vm,
        out_shape=jax.ShapeDtypeStruct(vec.shape, F32),
        scratch_shapes=[pltpu.VMEM((8, 8, n), F32), pltpu.SemaphoreType.DMA((7,)), pltpu.SemaphoreType.DMA((7,))],
    )(vec)


def _pair_sum(g, buf, ids, name):
    p, r, c = g.shape
    h = r // 2
    tr = _pick(h, (256, 128, 64, 32, 16))
    nh = h // tr

    def body(ids_ref, g_ref, b_ref, sums_ref, own_ref):
        s = g_ref[...] + b_ref[...]
        sums_ref[...] = s.astype(BF16)

        @pl.when(pl.program_id(1) == ids_ref[1])
        def _():
            own_ref[...] = s

    return pl.pallas_call(
        body, name=name,
        grid_spec=pltpu.PrefetchScalarGridSpec(
            num_scalar_prefetch=1, grid=(nh, p),
            in_specs=[pl.BlockSpec((None, tr, c), lambda i, q, ids: (q, ids[0] * nh + i, 0)),
                      pl.BlockSpec((None, tr, c), lambda i, q, ids: (q, i, 0))],
            out_specs=[pl.BlockSpec((None, tr, c), lambda i, q, ids: (q, i, 0)),
                       pl.BlockSpec((tr, c), lambda i, q, ids: (i, 0))]),
        out_shape=[jax.ShapeDtypeStruct((p, h, c), BF16), jax.ShapeDtypeStruct((h, c), F32)],
        compiler_params=_params(("parallel", "arbitrary")),
    )(ids, g, buf)


def _final_sum(own, others, ids, name):
    h, c = own.shape
    tr = _pick(h, (256, 128, 64, 32, 16))
    nh = h // tr

    def body(ids_ref, own_ref, oth_ref, out_ref):
        s = own_ref[...]
        for k in range(3):
            s = s + oth_ref[k].astype(F32)
        out_ref[...] = s

    return pl.pallas_call(
        body, name=name,
        grid_spec=pltpu.PrefetchScalarGridSpec(
            num_scalar_prefetch=1, grid=(nh,),
            in_specs=[pl.BlockSpec((tr, c), lambda i, ids: (i, 0)),
                      pl.BlockSpec((3, tr, c), lambda i, ids: (0, i, 0))],
            out_specs=pl.BlockSpec((tr, c), lambda i, ids: (ids[0] * nh + i, 0))),
        out_shape=jax.ShapeDtypeStruct((2 * h, c), F32),
        compiler_params=_params(("parallel",)),
    )(ids, own, others)


def _adamw(w, g, m, v, name):
    r, c = w.shape
    tm = _pick(r, (256, 128, 64, 32, 16, 8)) if r >= 8 else r

    def fn(rows, _):
        w_, g_, m_, v_ = rows
        m2 = ADAM_B1 * m_ + (1.0 - ADAM_B1) * g_
        v2 = ADAM_B2 * v_ + (1.0 - ADAM_B2) * (g_ * g_)
        m_hat = m2 / (1.0 - ADAM_B1 ** ADAM_STEP)
        v_hat = v2 / (1.0 - ADAM_B2 ** ADAM_STEP)
        delta = -ADAM_LR * (m_hat / (jnp.sqrt(v_hat) + ADAM_EPS) + ADAM_WD * w_)
        return [delta, m2, v2], []

    outs, _ = _rows(fn, [w, g, m, v], [], [(c, F32)] * 3, [], tm=tm, name=name)
    return outs


def kernel(x, attn_norm_w, w_in, lb_logits, sb_norm_w, hg_norm_w, w_out, mlp_norm_w, w_up, w_down, final_norm_w, loss_target, m_attn_norm_w, m_w_in, m_lb_logits, m_sb_norm_w, m_hg_norm_w, m_w_out, m_mlp_norm_w, m_w_up, m_w_down, m_final_norm_w, v_attn_norm_w, v_w_in, v_lb_logits, v_sb_norm_w, v_hg_norm_w, v_w_out, v_mlp_norm_w, v_w_up, v_w_down, v_final_norm_w):
    xs, tgt = x[0], loss_target[0]
    t, d = xs.shape
    width = d // 2
    n_heads = width // HEAD
    hps = min(8, n_heads)
    final_w = final_norm_w.reshape(1, d)
    tm_rows = _pick(t, (256, 128))
    tm = _pick(t, (1024, 512, 256))
    blk = min(ATTN_BLOCK, t)
    ones_a = jnp.ones((blk, blk), F32)
    after_tri = jnp.tril(ones_a, -1).astype(BF16)
    before_tri = jnp.triu(ones_a, 1).astype(BF16)
    ones_c = jnp.ones((HG_CHUNK, HG_CHUNK), F32)
    tri_lo, tri_up = jnp.tril(ones_c).astype(BF16), jnp.triu(ones_c).astype(BF16)
    cx, cy, cc = lax.axis_index("x"), lax.axis_index("y"), lax.axis_index("c")
    ids = jnp.stack([cc, 2 * cx + cy]).astype(jnp.int32)

    shards = [w_in[0], w_out[0], w_up[0], w_down[0]]
    cast = [_cast_to_slot(s, ids, f"cast_w{i}") for i, s in enumerate(shards)]
    (g_in,) = _gather_weights(cast[:1], "gather_w_in")
    d_ff = N_CHIPS * w_up.shape[2]
    cs_in, cs_up = g_in.shape[2], w_up.shape[2]
    tn_in = _pick(cs_in, (1792, 896, 512, 256, 128))
    tn_up = _pick(cs_up, (1024, 512, 256))
    tn_d = _pick(d, (1024, 512, 256))
    tk_d = _pick(d, (2048, 1024, 512))

    (u,), _ = _rows(lambda r, c_: ([r[0] * _rstd(r[0]) * c_[0]], []), [xs], [attn_norm_w], [(d, BF16)], [],
                    tm=tm_rows, name="norm_in")
    (proj,) = _mm_nn(u, g_in, [F32], tm=_pick(t, (512, 256)), tn=tn_in, tk=tk_d, name="proj_in")
    o_a, mix_a, sb_tot, g_out, g_up, g_down = _attn_fwd(proj, after_tri, sb_norm_w, n_heads, "sb_fwd", slots=cast[1:])
    w_out_all = g_out.reshape(1, d, d)
    w_down_all = g_down.reshape(1, d_ff, d)
    o_b, mix_b, states = _hgrn_fwd(proj, lb_logits, hg_norm_w, tri_lo, n_heads, hps, "hg_fwd")
    mix = jnp.concatenate([mix_a, mix_b], axis=1)
    (h1,) = _mm_nn(mix, w_out_all, [F32], tm=tm, tn=tn_d, tk=tk_d, name="proj_out",
                   epi=lambda acc, res: (acc + res,), extras=(xs,))
    (mn,), _ = _rows(lambda r, c_: ([r[0] * _rstd(r[0]) * c_[0]], []), [h1], [mlp_norm_w], [(d, BF16)], [],
                     tm=tm_rows, name="norm_mlp")
    up_b, act = _mm_nn(mn, g_up, [BF16, BF16], tm=tm, tn=tn_up, tk=tk_d, name="mlp_up",
                       epi=lambda acc: (acc, jnp.square(jnp.maximum(acc, 0.0))))
    (h2,) = _mm_nn(act, w_down_all, [F32], tm=tm, tn=tn_d, tk=_pick(d_ff, (2048, 1024)), name="mlp_down",
                   epi=lambda acc, res: (acc + res,), extras=(h1,))

    def head(rows, consts):
        hh, tg = rows
        w = consts[0]
        n = hh * _rstd(hh)
        err = n * w - tg
        dhh, dw_rows = _rms_bwd(hh, w, err * (1.0 / d))
        return [dhh, dhh], [_colsum(dw_rows), _colsum(err * err)]

    (dh2, dh2_b), (g_final, loss_cols) = _rows(head, [h2, tgt], [final_w], [(d, F32), (d, BF16)], [d, d],
                                                 tm=tm_rows, name="loss_head")

    (dup,) = _mm_nt(dh2_b, w_down_all, [BF16], tm=tm, tn=_pick(d_ff, (1024, 512)), tk=tk_d, name="mlp_down_dx",
                    epi=lambda acc, upv: (acc * (2.0 * jnp.maximum(upv.astype(F32), 0.0)),), extras=(up_b,))
    gw_down = _mm_tn(act, dh2_b, 1, tm=_pick(d_ff, (1024, 512)), tn=tn_d, tk=_pick(t, (1024, 512, 256)),
                     name="mlp_down_dw")
    (dmn,) = _mm_nt(dup, g_up, [F32], tm=tm, tn=tn_d, tk=_pick(cs_up, (2048, 1024, 512)), name="mlp_up_dx")
    gw_up = _mm_tn(mn, dup, N_CHIPS, tm=tn_d, tn=tn_up, tk=_pick(t, (1024, 512, 256)), name="mlp_up_dw")

    def pair_sums(grads, tag):
        theirs = _sibling_swap(grads, "grads_to_sibling_" + tag)
        return [_pair_sum(g, b, ids, f"grads_pair_sum_{tag}{i}") for i, (g, b) in enumerate(zip(grads, theirs))]

    pair_mlp = pair_sums([gw_up, gw_down.reshape(N_CHIPS, d_ff // N_CHIPS, d)], "mlp")

    def norm_back(rows, consts):
        xx, dy, skip = rows
        dx, dw_rows = _rms_bwd(xx, consts[0], dy)
        tot = dx + skip
        return [tot, tot], [_colsum(dw_rows)]

    (dh1, dh1_b), (g_mlp_norm,) = _rows(norm_back, [h1, dmn, dh2], [mlp_norm_w], [(d, F32), (d, BF16)], [d],
                                         tm=tm_rows, name="norm_mlp_bwd")

    (dmix,) = _mm_nt(dh1_b, w_out_all, [F32], tm=tm, tn=tn_d, tk=tk_d, name="proj_out_dx")
    gw_out = _mm_tn(mix, dh1_b, 1, tm=tn_d, tn=tn_d, tk=_pick(t, (1024, 512, 256)), name="proj_out_dw")

    def sb_norm_back(rows, consts):
        dx, dw_rows = _heads_map(lambda o, dy: _rms_bwd(o, consts[0], dy), width, *rows)
        dw = sum(_colsum(dw_rows[:, h * HEAD:(h + 1) * HEAD]) for h in range(n_heads))
        return [dx], [dw]

    (do_a,), (g_sb_norm,) = _rows(sb_norm_back, [o_a, (dmix, width, 0)], [sb_norm_w], [(width, F32)], [HEAD],
                                  tm=tm_rows, name="sb_norm_bwd")
    dq_a, dk_a, dv_a, *landed_mlp = _attn_bwd(proj, sb_tot, do_a, after_tri, before_tri, n_heads, "sb_bwd",
                                              sums=[p[0] for p in pair_mlp])

    def hg_out_back(rows, consts):
        def one(o, gate, dy):
            sg = _sigmoid(gate)
            silu = gate * sg
            n = o * _rstd(o) * consts[0]
            do, dw_rows = _rms_bwd(o, consts[0], dy * silu)
            return do, dy * n * (sg * (1.0 + gate * (1.0 - sg))), dw_rows
        do, dgate, dw_rows = _heads_map(one, width, *rows)
        dw = sum(_colsum(dw_rows[:, h * HEAD:(h + 1) * HEAD]) for h in range(n_heads))
        return [do, dgate], [dw]

    (do_b, dgate), (g_hg_norm,) = _rows(hg_out_back, [o_b, (proj, width, 6), (dmix, width, 1)], [hg_norm_w],
                                         [(width, F32)] * 2, [HEAD], tm=tm_rows, name="hg_out_bwd")
    dhq, dhf, dhi, dlb = _hgrn_bwd(proj, do_b, states, lb_logits, tri_lo, tri_up, n_heads, hps, "hg_bwd")

    (dproj,), _ = _rows(lambda r, _c: ([jnp.concatenate([p.astype(BF16) for p in r], axis=1)], []),
                        [dq_a, dk_a, dv_a, dhq, dhf, dhi, dgate], [], [(7 * width, BF16)], [],
                        tm=tm_rows, name="pack_dproj")
    gw_in = _mm_tn(u, dproj, N_CHIPS, tm=tn_d, tn=tn_in, tk=_pick(t, (1024, 512, 256)), name="proj_in_dw")
    pair_mix = pair_sums([gw_in, gw_out.reshape(N_CHIPS, d // N_CHIPS, d)], "mix")
    du, *landed_mix = _mm_nt(dproj, g_in, [F32], tm=tm, tn=tn_d, tk=_pick(cs_in, (1792, 896, 512, 256, 128)),
                             name="proj_in_dx", sums=[p[0] for p in pair_mix])
    (dx,), (g_attn_norm,) = _rows(lambda r, c_: (lambda dxx, dwr: ([dxx + r[2]], [_colsum(dwr)]))(
        *_rms_bwd(r[0], c_[0], r[1])), [xs, du, dh1], [attn_norm_w], [(d, F32)], [d], tm=tm_rows, name="norm_in_bwd")

    halves = [_final_sum(p[1], r, ids, f"grads_final_sum{i}")
              for i, (p, r) in enumerate(zip(pair_mix + pair_mlp, list(landed_mix) + list(landed_mlp)))]
    g_w_in, g_w_out, g_w_up, g_w_down = _sibling_join(halves, "grads_join")

    pieces = [g_attn_norm, g_mlp_norm, g_final, g_sb_norm, g_hg_norm, dlb, loss_cols]
    sizes = [p.shape[1] for p in pieces]
    flat = jnp.concatenate(pieces, axis=1)
    n_small = -(-flat.shape[1] // 1024) * 1024
    flat = jnp.pad(flat, ((0, 0), (0, n_small - flat.shape[1]))).reshape(8, n_small // 8)
    flat = _all_sum_small(flat, "small_all_sum").reshape(1, n_small)
    offs = [sum(sizes[:i]) for i in range(len(sizes))]
    g_attn_norm, g_mlp_norm, g_final, g_sb_norm, g_hg_norm, dlb, loss_cols = [
        flat[:, o:o + s] for o, s in zip(offs, sizes)]

    def small_tail(lbl_ref, dlb_ref, loss_ref, glb_ref, out_ref):
        lb = _lower_bound(lbl_ref[...])
        g0 = dlb_ref[...] * lb * (1.0 - lb)
        glb_ref[0:1, :] = g0
        glb_ref[1:2, :] = -g0
        out_ref[...] = jnp.zeros_like(out_ref) + 0.5 * jnp.sum(loss_ref[...]) * (1.0 / d)

    vm = pl.BlockSpec(memory_space=pltpu.VMEM)
    g_lb, loss11 = pl.pallas_call(
        small_tail, name="small_tail", in_specs=[vm, vm, vm], out_specs=[vm, vm],
        out_shape=[jax.ShapeDtypeStruct(lb_logits.shape, F32), jax.ShapeDtypeStruct((1, 128), F32)],
    )(lb_logits, dlb, loss_cols)
    loss = loss11[0, 0]

    names = ["attn_norm_w", "w_in", "lb_logits", "sb_norm_w", "hg_norm_w", "w_out", "mlp_norm_w", "w_up", "w_down",
             "final_norm_w"]
    ws = [attn_norm_w, w_in[0], lb_logits, sb_norm_w, hg_norm_w, w_out[0], mlp_norm_w, w_up[0], w_down[0], final_w]
    gs = [g_attn_norm, g_w_in, g_lb, g_sb_norm, g_hg_norm, g_w_out, g_mlp_norm, g_w_up, g_w_down, g_final]
    ms = [m_attn_norm_w, m_w_in[0], m_lb_logits, m_sb_norm_w, m_hg_norm_w, m_w_out[0], m_mlp_norm_w, m_w_up[0],
          m_w_down[0], m_final_norm_w.reshape(1, d)]
    vs = [v_attn_norm_w, v_w_in[0], v_lb_logits, v_sb_norm_w, v_hg_norm_w, v_w_out[0], v_mlp_norm_w, v_w_up[0],
          v_w_down[0], v_final_norm_w.reshape(1, d)]
    shapes = [attn_norm_w.shape, w_in.shape, lb_logits.shape, sb_norm_w.shape, hg_norm_w.shape, w_out.shape,
              mlp_norm_w.shape, w_up.shape, w_down.shape, final_norm_w.shape]
    deltas, new_ms, new_vs = [], [], []
    for nm, w_, g_, m_, v_ in zip(names, ws, gs, ms, vs):
        dl, m2, v2 = _adamw(w_, g_, m_, v_, "adamw_" + nm)
        deltas.append(dl)
        new_ms.append(m2)
        new_vs.append(v2)

    def shaped(lst):
        return [a.reshape(s) for a, s in zip(lst, shapes)]

    return (loss, dx[None], *shaped(gs), *shaped(deltas), *shaped(new_ms), *shaped(new_vs))
```

```python
import functools

import jax
import jax.numpy as jnp
from jax import lax
from jax.experimental import pallas as pl
from jax.experimental.pallas import tpu as pltpu

F32 = jnp.float32
BF16 = jnp.bfloat16
MESH = pl.DeviceIdType.MESH

HEAD = 128
NORM_EPS = 1e-5
N_CHIPS = 4
ATTN_BLOCK = 256
ATTN_ROWS = 1024
ATTN_DEAD = -110.0
HG_CHUNK = 32
HG_ROWS = 256
HG_UNROLL = 2
VMEM_LIMIT = 56 * 1024 * 1024

ADAM_LR = 0.001
ADAM_B1 = 0.9
ADAM_B2 = 0.999
ADAM_EPS = 1e-08
ADAM_WD = 0.01
ADAM_STEP = 10


def _pick(n, cands):
    for c in cands:
        if n % c == 0:
            return c
    return n


def _params(sem):
    return pltpu.CompilerParams(dimension_semantics=sem, vmem_limit_bytes=VMEM_LIMIT)


def _dot(a, b):
    return jnp.dot(a, b, preferred_element_type=F32)


def _dot_nt(a, b):
    return lax.dot_general(a, b, (((1,), (1,)), ((), ())), preferred_element_type=F32)


def _dot_tn(a, b):
    return lax.dot_general(a, b, (((0,), (0,)), ((), ())), preferred_element_type=F32)


def _hilo(x):
    hi = x.astype(BF16)
    return hi, (x - hi.astype(F32)).astype(BF16)


def _dot_split(tri, x):
    hi, lo = _hilo(x)
    return _dot(tri, hi) + _dot(tri, lo)


def _dot3(dot, a, b):
    return dot(a[0], b[0]) + (dot(a[0], b[1]) + dot(a[1], b[0]))


def _sigmoid(x):
    return 1.0 / (1.0 + jnp.exp(-x))


def _hosted_call(inner, grid, in_specs, out_specs, out_shape, scratch, args, semantics, name, comm=None):
    kind, arrays = comm if comm else (None, ())
    n_i, n_o, n_s, n_c = len(in_specs), len(out_specs), len(scratch), len(arrays)

    def body(*refs):
        c_in = refs[n_i:n_i + n_c]
        c_out = refs[n_i + n_c + n_o:n_i + 2 * n_c + n_o]
        scr = refs[n_i + 2 * n_c + n_o:]
        if n_c:
            ids = [pl.program_id(ax) for ax in range(len(grid))]
            first, last = ids[0] == 0, ids[0] == grid[0] - 1
            for ax in range(1, len(grid)):
                first, last = first & (ids[ax] == 0), last & (ids[ax] == grid[ax] - 1)
            sems = scr[n_s:]
            copies = {"scatter": _scatter_copies, "swap": _swap_copies}
            begin, end = _gather_copies(c_out, *sems) if kind == "gather" else copies[kind](c_in, c_out, *sems)
            pl.when(first)(begin)
        inner(*refs[:n_i], *refs[n_i + n_c:n_i + n_c + n_o], *scr[:n_s])
        if n_c:
            pl.when(last)(end)

    gather = kind == "gather"
    shape = {"gather": lambda a: a.shape, "scatter": lambda a: (3,) + a.shape[1:],
             "swap": lambda a: (a.shape[0], a.shape[1] // 2, a.shape[2])}
    landed = [jax.ShapeDtypeStruct(shape[kind](a), a.dtype) for a in arrays]
    return pl.pallas_call(
        body, name=name, grid=grid,
        in_specs=list(in_specs) + [ANY] * n_c, out_specs=list(out_specs) + [ANY] * n_c,
        out_shape=list(out_shape) + landed,
        input_output_aliases={n_i + w: n_o + w for w in range(n_c)} if gather else {},
        scratch_shapes=list(scratch) + ([pltpu.SemaphoreType.DMA(
            ({"gather": 6, "scatter": 3, "swap": 1}[kind] * n_c,))] * 2 if n_c else []),
        compiler_params=_params(("arbitrary",) * len(grid) if n_c else semantics),
    )(*args, *arrays)


def _mm_body(kind, nk, n_extra, n_out, epi):
    dot = {"nn": _dot, "nt": _dot_nt, "tn": _dot_tn}[kind]

    def finish(acc, extra_refs, out_refs):
        res = epi(acc, *[e[...] for e in extra_refs]) if epi is not None else (acc,)
        for o, r in zip(out_refs, res):
            o[...] = r.astype(o.dtype)

    def body(a_ref, b_ref, *rest):
        extra_refs = rest[:n_extra]
        out_refs = rest[n_extra:n_extra + n_out]
        if nk == 1:
            finish(dot(a_ref[...], b_ref[...]), extra_refs, out_refs)
            return
        acc_ref = rest[n_extra + n_out]
        k = pl.program_id(2)

        @pl.when(k == 0)
        def _():
            acc_ref[...] = jnp.zeros_like(acc_ref)

        acc_ref[...] += dot(a_ref[...], b_ref[...])

        @pl.when(k == nk - 1)
        def _():
            finish(acc_ref[...], extra_refs, out_refs)

    return body


def _mm_nn(a, w, out_dtypes, *, tm, tn, tk, name, epi=None, extras=(), comm=None):
    m, r = a.shape
    p, _, c = w.shape
    npc = c // tn
    nk = r // tk
    tile = pl.BlockSpec((tm, tn), lambda i, j, k: (i, j))
    return _hosted_call(
        _mm_body("nn", nk, len(extras), len(out_dtypes), epi), (m // tm, p * npc, nk),
        [pl.BlockSpec((tm, tk), lambda i, j, k: (i, k)),
         pl.BlockSpec((None, tk, tn), lambda i, j, k: (j // npc, k, j % npc))] + [tile] * len(extras),
        [tile] * len(out_dtypes), [jax.ShapeDtypeStruct((m, p * c), d) for d in out_dtypes],
        [pltpu.VMEM((tm, tn), F32)] if nk > 1 else [], (a, w, *extras),
        ("parallel", "parallel", "arbitrary"), name, comm)


def _mm_nt(a, w, out_dtypes, *, tm, tn, tk, name, epi=None, extras=(), comm=None):
    m, _ = a.shape
    p, r, c = w.shape
    kpc = c // tk
    nk = p * kpc
    tile = pl.BlockSpec((tm, tn), lambda i, j, k: (i, j))
    return _hosted_call(
        _mm_body("nt", nk, len(extras), len(out_dtypes), epi), (m // tm, r // tn, nk),
        [pl.BlockSpec((tm, tk), lambda i, j, k: (i, k)),
         pl.BlockSpec((None, tn, tk), lambda i, j, k: (k // kpc, j, k % kpc))] + [tile] * len(extras),
        [tile] * len(out_dtypes), [jax.ShapeDtypeStruct((m, r), d) for d in out_dtypes],
        [pltpu.VMEM((tm, tn), F32)] if nk > 1 else [], (a, w, *extras),
        ("parallel", "parallel", "arbitrary"), name, comm)


def _mm_tn(a, g, p, *, tm, tn, tk, name):
    t, r = a.shape
    c = g.shape[1] // p
    npc = c // tn
    nk = t // tk
    body = _mm_body("tn", nk, 0, 1, None)
    return pl.pallas_call(
        body, name=name,
        grid=(r // tm, p * npc, nk),
        in_specs=[pl.BlockSpec((tk, tm), lambda i, j, k: (k, i)),
                  pl.BlockSpec((tk, tn), lambda i, j, k: (k, j))],
        out_specs=[pl.BlockSpec((None, tm, tn), lambda i, j, k: (j // npc, i, j % npc))],
        out_shape=[jax.ShapeDtypeStruct((p, r, c), F32)],
        scratch_shapes=[pltpu.VMEM((tm, tn), F32)] if nk > 1 else [],
        compiler_params=_params(("parallel", "parallel", "arbitrary")),
    )(a, g)[0]


def _rows(fn, row_ins, const_ins, row_outs, acc_outs, *, tm, name):
    specs, arrays = [], []
    t = None
    for item in row_ins:
        if isinstance(item, tuple):
            arr, width, cb = item
            specs.append(pl.BlockSpec((tm, width), functools.partial(lambda i, cb: (i, cb), cb=cb)))
        else:
            arr = item
            specs.append(pl.BlockSpec((tm, arr.shape[1]), lambda i: (i, 0)))
        arrays.append(arr)
        t = arr.shape[0]
    for arr in const_ins:
        specs.append(pl.BlockSpec(arr.shape, lambda i: (0, 0)))
        arrays.append(arr)
    n_in, n_row, n_acc = len(arrays), len(row_outs), len(acc_outs)

    def body(*refs):
        ins = [r[...] for r in refs[:n_in]]
        outs = refs[n_in:]
        row_res, acc_res = fn(ins[:len(row_ins)], ins[len(row_ins):])
        for o, r in zip(outs[:n_row], row_res):
            o[...] = r.astype(o.dtype)
        if n_acc:
            i = pl.program_id(0)

            @pl.when(i == 0)
            def _():
                for o in outs[n_row:]:
                    o[...] = jnp.zeros_like(o)

            for o, r in zip(outs[n_row:], acc_res):
                o[...] += r

    res = pl.pallas_call(
        body, name=name,
        grid=(t // tm,),
        in_specs=specs,
        out_specs=[pl.BlockSpec((tm, c), lambda i: (i, 0)) for c, _ in row_outs]
                  + [pl.BlockSpec((1, c), lambda i: (0, 0)) for c in acc_outs],
        out_shape=[jax.ShapeDtypeStruct((t, c), d) for c, d in row_outs]
                  + [jax.ShapeDtypeStruct((1, c), F32) for c in acc_outs],
        compiler_params=_params(("arbitrary",)),
    )(*arrays)
    return res[:n_row], res[n_row:]


def _rstd(x):
    return lax.rsqrt(jnp.mean(x * x, axis=-1, keepdims=True) + NORM_EPS)


def _rms_bwd(x, w, dy):
    r = _rstd(x)
    n = x * r
    dn = dy * w
    dx = r * (dn - n * jnp.mean(dn * n, axis=-1, keepdims=True))
    return dx, dy * n


def _colsum(x):
    return jnp.sum(x, axis=0, keepdims=True)


def _heads_map(fn, width, *tiles):
    outs = None
    for h in range(width // HEAD):
        res = fn(*[t[:, h * HEAD:(h + 1) * HEAD] for t in tiles])
        if outs is None:
            outs = [[] for _ in res]
        for lst, r in zip(outs, res):
            lst.append(r)
    return [jnp.concatenate(lst, axis=1) for lst in outs]


def _log_one_minus_beta(z):
    return -(jnp.maximum(z, 0.0) + jnp.log(1.0 + jnp.exp(-jnp.abs(z))))


def _attn_fwd(proj, after_tri, norm_w, n_heads, name, comm=None):
    t = proj.shape[0]
    blk = min(ATTN_BLOCK, t)
    qb = min(ATTN_ROWS, t)
    ns = qb // blk
    scale = HEAD ** -0.5

    def body(q_ref, k_ref, v_ref, tri_ref, w_ref, o_ref, mix_ref, tot_ref):
        i = pl.program_id(1)
        q = (q_ref[...] * scale).astype(BF16)
        tri = tri_ref[...]

        def part(r0, j, acc_l, acc_o, masked):
            sl = pl.ds(pl.multiple_of(j * blk, blk), blk)
            m = qb - r0
            z = _dot_nt(q[r0:, :], k_ref[sl, :].astype(BF16))
            lm = _log_one_minus_beta(z)
            if masked:
                mask = lax.broadcasted_iota(jnp.int32, (m, blk), 1) < lax.broadcasted_iota(jnp.int32, (m, blk), 0)
                lmm = jnp.where(mask, lm, 0.0)
            else:
                lmm = lm
            w = jnp.exp(z + lm + acc_l[r0:, :] + _dot(lmm.astype(BF16), tri))
            if masked:
                w = jnp.where(mask, w, 0.0)
            new_l = acc_l[r0:, :] + jnp.sum(lmm, axis=1, keepdims=True)
            new_o = acc_o[r0:, :] + _dot(w.astype(BF16), v_ref[sl, :].astype(BF16))
            if r0:
                new_l = jnp.concatenate([acc_l[:r0, :], new_l], axis=0)
                new_o = jnp.concatenate([acc_o[:r0, :], new_o], axis=0)
            return new_l, new_o

        acc = (jnp.zeros((qb, 1), F32), jnp.zeros((qb, HEAD), F32))
        for jr in reversed(range(ns)):
            acc = part(jr * blk, ns * i + jr, *acc, True)

        def more(c):
            return (c[0] < ns * i) & (jnp.max(c[1]) > ATTN_DEAD)

        def step(c):
            return (c[0] + 1,) + part(0, ns * i - 1 - c[0], c[1], c[2], False)

        swept, acc_l, acc_o = lax.while_loop(more, step, (jnp.int32(0),) + acc)
        o_ref[...] = acc_o
        mix_ref[...] = (acc_o * _rstd(acc_o) * w_ref[...]).astype(BF16)
        first = (ns * i - swept).astype(F32)
        tot_ref[...] = jnp.where(lax.broadcasted_iota(jnp.int32, (qb, HEAD), 1) == 1, first, acc_l)

    width = n_heads * HEAD
    qblk = pl.BlockSpec((qb, HEAD), lambda h, i: (i, h))
    return _hosted_call(
        body, (n_heads, t // qb),
        [qblk,
         pl.BlockSpec((t, HEAD), lambda h, i: (0, n_heads + h)),
         pl.BlockSpec((t, HEAD), lambda h, i: (0, 2 * n_heads + h)),
         pl.BlockSpec((blk, blk), lambda h, i: (0, 0)),
         pl.BlockSpec((1, HEAD), lambda h, i: (0, 0))],
        [qblk, qblk, qblk],
        [jax.ShapeDtypeStruct((t, width), F32), jax.ShapeDtypeStruct((t, width), BF16),
         jax.ShapeDtypeStruct((t, width), F32)],
        [], (proj, proj, proj, after_tri, norm_w), ("parallel", "arbitrary"), name, comm)


def _attn_bwd(proj, tot, do, after_tri, before_tri, n_heads, name, comm=None):
    t = proj.shape[0]
    blk = min(ATTN_BLOCK, t)
    qb = min(ATTN_ROWS, t)
    ns = qb // blk
    scale = HEAD ** -0.5

    def body(q_ref, k_ref, v_ref, tot_ref, do_ref, after_ref, before_ref, dq_ref, dk_ref, dv_ref):
        i = pl.program_id(1)

        @pl.when(i == 0)
        def _():
            dk_ref[...] = jnp.zeros_like(dk_ref)
            dv_ref[...] = jnp.zeros_like(dv_ref)

        q = (q_ref[...] * scale).astype(BF16)
        dob = do_ref[...].astype(BF16)
        total = tot_ref[:, 0:1]
        after_tri = after_ref[...]
        before = before_ref[...]

        def part(r0, j, seen_l, seen_g, dq, masked):
            sl = pl.ds(pl.multiple_of(j * blk, blk), blk)
            m = qb - r0
            qq, dd = q[r0:, :], dob[r0:, :]
            kb = k_ref[sl, :].astype(BF16)
            z = _dot_nt(qq, kb)
            lm = _log_one_minus_beta(z)
            if masked:
                mask = lax.broadcasted_iota(jnp.int32, (m, blk), 1) < lax.broadcasted_iota(jnp.int32, (m, blk), 0)
                lmm = jnp.where(mask, lm, 0.0)
            else:
                lmm = lm
            row_l = jnp.sum(lmm, axis=1, keepdims=True)
            after = (total[r0:, :] - seen_l[r0:, :] - row_l) + _dot(lmm.astype(BF16), after_tri)
            w = jnp.exp(z + lm + after)
            if masked:
                w = jnp.where(mask, w, 0.0)
            sig = jnp.exp(z + lm)
            g = w * _dot_nt(dd, v_ref[sl, :].astype(BF16))
            g_before = seen_g[r0:, :] + _dot(g.astype(BF16), before)
            dz = g * (1.0 - sig) - g_before * sig
            if masked:
                dz = jnp.where(mask, dz, 0.0)
            dzb = dz.astype(BF16)
            dk_ref[sl, :] += _dot_tn(dzb, qq)
            dv_ref[sl, :] += _dot_tn(w.astype(BF16), dd)
            new = (seen_l[r0:, :] + row_l,
                   seen_g[r0:, :] + jnp.sum(g, axis=1, keepdims=True), dq[r0:, :] + _dot(dzb, kb))
            if r0:
                new = tuple(jnp.concatenate([old[:r0, :], n], axis=0) for old, n in zip((seen_l, seen_g, dq), new))
            return new

        zero = jnp.zeros((qb, 1), F32)
        first = jnp.max(tot_ref[0:8, 1:2]).astype(jnp.int32)
        carry = lax.fori_loop(first, ns * i, lambda j, c: part(0, j, *c, False),
                              (zero, zero, jnp.zeros((qb, HEAD), F32)))
        for jr in range(ns):
            carry = part(jr * blk, ns * i + jr, *carry, True)
        dq_ref[...] = carry[2] * scale

    width = n_heads * HEAD
    qblk = pl.BlockSpec((qb, HEAD), lambda h, i: (i, h))
    full = pl.BlockSpec((t, HEAD), lambda h, i: (0, h))
    tri = pl.BlockSpec((blk, blk), lambda h, i: (0, 0))
    return _hosted_call(
        body, (n_heads, t // qb),
        [qblk,
         pl.BlockSpec((t, HEAD), lambda h, i: (0, n_heads + h)),
         pl.BlockSpec((t, HEAD), lambda h, i: (0, 2 * n_heads + h)),
         qblk, qblk, tri, tri],
        [qblk, full, full], [jax.ShapeDtypeStruct((t, width), F32)] * 3,
        [], (proj, proj, proj, tot, do, after_tri, before_tri), ("parallel", "arbitrary"), name, comm)


def _lower_bound(logits):
    l0, l1 = logits[0:1, :], logits[1:2, :]
    mx = jnp.maximum(l0, l1)
    e0, e1 = jnp.exp(l0 - mx), jnp.exp(l1 - mx)
    return e0 / (e0 + e1)


def _hg_chunk(qc, kc, gc, tri_lo):
    c = qc.shape[0]
    cum = _dot_split(tri_lo, gc)
    mid = cum[c // 2 - 1:c // 2, :]
    last = cum[c - 1:c, :]
    qt = qc * jnp.exp(cum - mid)
    kt = kc * jnp.exp(mid - cum)
    qe = qc * jnp.exp(cum)
    kd = kc * jnp.exp(last - cum)
    return cum, mid, last, qt, kt, qe, kd


def _hgrn_fwd(proj, lb_logits, norm_w, tri_lo, n_heads, heads_per_step, name):
    t = proj.shape[0]
    bt = min(HG_ROWS, t)
    c = HG_CHUNK
    nc = bt // c
    hw = heads_per_step * HEAD
    width = n_heads * HEAD
    col0 = 3 * width // hw

    def body(hq_ref, hf_ref, hi_ref, hgate_ref, lbl_ref, w_ref, tri_ref, o_ref, mix_ref, st_ref,
             state, q_scr, k_scr, g_scr):
        @pl.when(pl.program_id(1) == 0)
        def _():
            state[...] = jnp.zeros_like(state)

        lb = _lower_bound(lbl_ref[...])
        f = hf_ref[...]
        g_scr[...] = jnp.log(lb + (1.0 - lb) * _sigmoid(f))
        k_scr[...] = (1.0 - lb) * _sigmoid(-f)
        hq = hq_ref[...]
        q_scr[...] = hq * _sigmoid(hq)
        tri = tri_ref[...]
        causal = lax.broadcasted_iota(jnp.int32, (c, c), 1) <= lax.broadcasted_iota(jnp.int32, (c, c), 0)

        def chunk(ci, carry):
            r = pl.ds(pl.multiple_of(ci * c, c), c)
            vc = hi_ref[r, :].astype(BF16)
            _, _, last, qt, kt, qe, kd = _hg_chunk(q_scr[r, :], k_scr[r, :], g_scr[r, :], tri)
            qt, kt, qe, kd = qt.astype(BF16), kt.astype(BF16), qe.astype(BF16), kd.astype(BF16)
            e_last = jnp.exp(last)
            old = [state[h] for h in range(heads_per_step)]
            outs, new = [], []
            for h in range(heads_per_step):
                cs = slice(h * HEAD, (h + 1) * HEAD)
                a = jnp.where(causal, _dot_nt(qt[:, cs], kt[:, cs]), 0.0)
                outs.append(_dot(a.astype(BF16), vc[:, cs]) + _dot_nt(qe[:, cs], old[h].astype(BF16)))
                new.append(old[h] * e_last[:, cs] + _dot_tn(vc[:, cs], kd[:, cs]))
            for h in range(heads_per_step):
                st_ref[ci, :, h * HEAD:(h + 1) * HEAD] = old[h]
                state[h] = new[h]
            o_ref[r, :] = jnp.concatenate(outs, axis=1)
            return carry

        lax.fori_loop(0, nc, chunk, 0, unroll=HG_UNROLL)

        def finish(o, gate):
            return ((o * _rstd(o) * w_ref[...]) * (gate * _sigmoid(gate)),)

        mix_ref[...] = _heads_map(finish, hw, o_ref[...], hgate_ref[...])[0].astype(BF16)

    def col(group):
        return pl.BlockSpec((bt, hw), functools.partial(lambda hp, tb, g: (tb, col0 + g * (width // hw) + hp), g=group))

    blk = pl.BlockSpec((bt, hw), lambda hp, tb: (tb, hp))
    return pl.pallas_call(
        body, name=name,
        grid=(n_heads // heads_per_step, t // bt),
        in_specs=[col(0), col(1), col(2), col(3),
                  pl.BlockSpec((2, hw), lambda hp, tb: (0, hp)),
                  pl.BlockSpec((1, HEAD), lambda hp, tb: (0, 0)),
                  pl.BlockSpec((c, c), lambda hp, tb: (0, 0))],
        out_specs=[blk, blk, pl.BlockSpec((nc, HEAD, hw), lambda hp, tb: (tb, 0, hp))],
        out_shape=[jax.ShapeDtypeStruct((t, width), F32), jax.ShapeDtypeStruct((t, width), BF16),
                   jax.ShapeDtypeStruct((t // c, HEAD, width), F32)],
        scratch_shapes=[pltpu.VMEM((heads_per_step, HEAD, HEAD), F32)] + [pltpu.VMEM((bt, hw), F32)] * 3,
        compiler_params=_params(("parallel", "arbitrary")),
    )(proj, proj, proj, proj, lb_logits, norm_w, tri_lo)


def _hgrn_bwd(proj, do, states, lb_logits, tri_lo, tri_up, n_heads, heads_per_step, name):
    t = proj.shape[0]
    bt = min(HG_ROWS, t)
    c = HG_CHUNK
    nc = bt // c
    nb = t // bt
    hw = heads_per_step * HEAD
    width = n_heads * HEAD
    col0 = 3 * width // hw

    def body(hq_ref, hf_ref, hi_ref, do_ref, st_ref, lbl_ref, lo_ref, up_ref, dq_ref, df_ref, di_ref, dlb_ref,
             dstate, q_scr, k_scr, g_scr, dk_scr, dg_scr):
        @pl.when(pl.program_id(1) == 0)
        def _():
            dstate[...] = jnp.zeros_like(dstate)
            dlb_ref[...] = jnp.zeros_like(dlb_ref)

        lb = _lower_bound(lbl_ref[...])
        f = hf_ref[...]
        sg = _sigmoid(f)
        sgn = _sigmoid(-f)
        den = lb + (1.0 - lb) * sg
        g_scr[...] = jnp.log(den)
        k_scr[...] = (1.0 - lb) * sgn
        hq = hq_ref[...]
        sq = _sigmoid(hq)
        q_scr[...] = hq * sq
        tri_lo_v = lo_ref[...]
        tri_up_v = up_ref[...]
        causal = lax.broadcasted_iota(jnp.int32, (c, c), 1) <= lax.broadcasted_iota(jnp.int32, (c, c), 0)
        last_row = lax.broadcasted_iota(jnp.int32, (c, hw), 0) == c - 1

        def chunk(cc, carry):
            ci = nc - 1 - cc
            r = pl.ds(pl.multiple_of(ci * c, c), c)
            qc, kc = q_scr[r, :], k_scr[r, :]
            cum, mid, last, qt, kt, qe, kd = _hg_chunk(qc, kc, g_scr[r, :], tri_lo_v)
            qt, kt, qe, kd, doc, vc = [_hilo(v) for v in (qt, kt, qe, kd, do_ref[r, :], hi_ref[r, :])]
            e_last = jnp.exp(last)
            sts = [st_ref[ci, :, h * HEAD:(h + 1) * HEAD] for h in range(heads_per_step)]
            dsts = [dstate[h] for h in range(heads_per_step)]
            di, dq_inter, dk_inter, dq_intra, dk_intra, st_sums, new = [], [], [], [], [], [], []
            for h in range(heads_per_step):
                cs = slice(h * HEAD, (h + 1) * HEAD)

                def head(pair):
                    return pair[0][:, cs], pair[1][:, cs]

                st, dst = _hilo(sts[h]), _hilo(dsts[h])
                a = _hilo(jnp.where(causal, _dot3(_dot_nt, head(qt), head(kt)), 0.0))
                da = _hilo(jnp.where(causal, _dot3(_dot_nt, head(doc), head(vc)), 0.0))
                di.append(_dot3(_dot_tn, a, head(doc)) + _dot3(_dot_nt, head(kd), dst))
                dq_inter.append(_dot3(_dot, head(doc), st))
                dk_inter.append(_dot3(_dot, head(vc), dst))
                dq_intra.append(_dot3(_dot, da, head(kt)))
                dk_intra.append(_dot3(_dot_tn, da, head(qt)))
                st_sums.append(_colsum(dsts[h] * sts[h]))
                new.append(dsts[h] * e_last[:, cs] + _dot3(_dot_tn, head(doc), head(qe)))

            def wide(parts):
                return jnp.concatenate(parts, axis=1)

            dq_inter = wide(dq_inter) * jnp.exp(cum)
            dk_inter = wide(dk_inter) * jnp.exp(last - cum)
            dq = wide(dq_intra) * jnp.exp(cum - mid) + dq_inter
            dk = wide(dk_intra) * jnp.exp(mid - cum) + dk_inter
            d_last = _colsum(kc * dk_inter) + e_last * wide(st_sums)
            dcum = qc * dq - kc * dk + jnp.where(last_row, d_last, 0.0)
            for h in range(heads_per_step):
                dstate[h] = new[h]
            di_ref[r, :] = wide(di)
            dq_ref[r, :] = dq
            dk_scr[r, :] = dk
            dg_scr[r, :] = _dot_split(tri_up_v, dcum)
            return carry

        lax.fori_loop(0, nc, chunk, 0, unroll=HG_UNROLL)

        e = (dg_scr[...] / den - dk_scr[...]) * sgn
        df_ref[...] = e * (1.0 - lb) * sg
        dlb_ref[...] += _colsum(e)
        dq_ref[...] = dq_ref[...] * (sq * (1.0 + hq * (1.0 - sq)))

    def col(group):
        return pl.BlockSpec((bt, hw), functools.partial(
            lambda hp, tb, g: (nb - 1 - tb, col0 + g * (width // hw) + hp), g=group))

    blk = pl.BlockSpec((bt, hw), lambda hp, tb: (nb - 1 - tb, hp))
    tri = pl.BlockSpec((c, c), lambda hp, tb: (0, 0))
    return pl.pallas_call(
        body, name=name,
        grid=(n_heads // heads_per_step, nb),
        in_specs=[col(0), col(1), col(2), blk,
                  pl.BlockSpec((nc, HEAD, hw), lambda hp, tb: (nb - 1 - tb, 0, hp)),
                  pl.BlockSpec((2, hw), lambda hp, tb: (0, hp)), tri, tri],
        out_specs=[blk, blk, blk, pl.BlockSpec((1, hw), lambda hp, tb: (0, hp))],
        out_shape=[jax.ShapeDtypeStruct((t, width), F32)] * 3 + [jax.ShapeDtypeStruct((1, width), F32)],
        scratch_shapes=[pltpu.VMEM((heads_per_step, HEAD, HEAD), F32)] + [pltpu.VMEM((bt, hw), F32)] * 5,
        compiler_params=_params(("parallel", "arbitrary")),
    )(proj, proj, proj, do, states, lb_logits, tri_lo, tri_up)


def _place():
    x, y, c = lax.axis_index("x"), lax.axis_index("y"), lax.axis_index("c")
    chips = [(1 - x, y), (x, 1 - y), (1 - x, 1 - y)]
    return x, y, c, chips


ANY = pl.BlockSpec(memory_space=pl.ANY)


def _cast_to_slot(shard, ids, name):
    r, c = shard.shape
    tm = _pick(r, (256, 128, 64, 32, 16))

    def body(ids_ref, s_ref, o_ref):
        o_ref[...] = s_ref[...].astype(BF16)

    return pl.pallas_call(
        body, name=name,
        grid_spec=pltpu.PrefetchScalarGridSpec(
            num_scalar_prefetch=1, grid=(r // tm,),
            in_specs=[pl.BlockSpec((tm, c), lambda i, ids: (i, 0))],
            out_specs=pl.BlockSpec((None, tm, c), lambda i, ids: (ids[1], i, 0))),
        out_shape=jax.ShapeDtypeStruct((N_CHIPS, r, c), BF16),
        compiler_params=_params(("parallel",)),
    )(ids, shard)


def _gather_copies(bufs, send, recv):
    x, y, c, chips = _place()
    mine = 2 * x + y

    def half(ref, who, core):
        h = ref.shape[-2] // 2
        return ref.at[who, pl.ds(core * h, h), :]

    def copy(w, k, rows, to):
        return pltpu.make_async_remote_copy(src_ref=rows, dst_ref=rows, send_sem=send.at[6 * w + k],
                                            recv_sem=recv.at[6 * w + k], device_id=to, device_id_type=MESH)

    def to_chips(w):
        return [copy(w, k, half(bufs[w], mine, c), (qx, qy, c)) for k, (qx, qy) in enumerate(chips)]

    def to_sibling(w):
        return [copy(w, 3 + k, half(bufs[w], 2 * qx + qy, c), (x, y, 1 - c)) for k, (qx, qy) in enumerate(chips)]

    def begin():
        for w in range(len(bufs)):
            for cp in to_chips(w):
                cp.start()

    def end():
        for w in range(len(bufs)):
            for k, (qx, qy) in enumerate(chips):
                copy(w, k, half(bufs[w], 2 * qx + qy, c), (x, y, c)).wait_recv()
                to_sibling(w)[k].start()
        for w in range(len(bufs)):
            for k, (qx, qy) in enumerate(chips):
                copy(w, 3 + k, half(bufs[w], 2 * qx + qy, 1 - c), (x, y, c)).wait_recv()
        for w in range(len(bufs)):
            for cp in to_chips(w) + to_sibling(w):
                cp.wait_send()

    return begin, end


def _scatter_copies(ins, outs, send, recv):
    _, _, c, chips = _place()

    def copies():
        return [pltpu.make_async_remote_copy(
            src_ref=ins[w].at[2 * qx + qy], dst_ref=outs[w].at[k], send_sem=send.at[3 * w + k],
            recv_sem=recv.at[3 * w + k], device_id=(qx, qy, c), device_id_type=MESH)
            for w in range(len(ins)) for k, (qx, qy) in enumerate(chips)]

    def begin():
        for cp in copies():
            cp.start()

    def end():
        for cp in copies():
            cp.wait()

    return begin, end


def _gather_weights(slots, name):
    n = len(slots)

    def body(*refs):
        begin, end = _gather_copies(refs[n:2 * n], *refs[2 * n:])
        begin()
        end()

    return pl.pallas_call(
        body, name=name,
        in_specs=[ANY] * n, out_specs=[ANY] * n,
        out_shape=[jax.ShapeDtypeStruct(s.shape, s.dtype) for s in slots],
        input_output_aliases={w: w for w in range(n)},
        scratch_shapes=[pltpu.SemaphoreType.DMA((6 * n,)), pltpu.SemaphoreType.DMA((6 * n,))],
    )(*slots)


def _swap_copies(ins, outs, send, recv):
    x, y, c, _ = _place()

    def copies():
        return [pltpu.make_async_remote_copy(
            src_ref=ins[w].at[:, pl.ds((1 - c) * (ins[w].shape[1] // 2), ins[w].shape[1] // 2), :], dst_ref=outs[w],
            send_sem=send.at[w], recv_sem=recv.at[w], device_id=(x, y, 1 - c), device_id_type=MESH)
            for w in range(len(ins))]

    def begin():
        for cp in copies():
            cp.start()

    def end():
        for cp in copies():
            cp.wait()

    return begin, end


def _sibling_swap(grads, name):
    n = len(grads)

    def body(*refs):
        begin, end = _swap_copies(refs[:n], refs[n:2 * n], *refs[2 * n:])
        begin()
        end()

    return pl.pallas_call(
        body, name=name, in_specs=[ANY] * n, out_specs=[ANY] * n,
        out_shape=[jax.ShapeDtypeStruct((g.shape[0], g.shape[1] // 2, g.shape[2]), g.dtype) for g in grads],
        scratch_shapes=[pltpu.SemaphoreType.DMA((n,)), pltpu.SemaphoreType.DMA((n,))],
    )(*grads)


def _sibling_join(fulls, name):
    n = len(fulls)

    def body(*refs):
        bufs = refs[n:2 * n]
        send, recv = refs[2 * n:]
        x, y, c, _ = _place()
        cps = []
        for w in range(n):
            h = bufs[w].shape[0] // 2
            rows = bufs[w].at[pl.ds(c * h, h), :]
            cps.append(pltpu.make_async_remote_copy(
                src_ref=rows, dst_ref=rows, send_sem=send.at[w], recv_sem=recv.at[w],
                device_id=(x, y, 1 - c), device_id_type=MESH))
            cps[-1].start()
        for w in range(n):
            h = bufs[w].shape[0] // 2
            theirs = bufs[w].at[pl.ds((1 - c) * h, h), :]
            pltpu.make_async_remote_copy(src_ref=theirs, dst_ref=theirs, send_sem=send.at[w], recv_sem=recv.at[w],
                                         device_id=(x, y, c), device_id_type=MESH).wait_recv()
        for cp in cps:
            cp.wait_send()

    return pl.pallas_call(
        body, name=name, in_specs=[ANY] * n, out_specs=[ANY] * n,
        out_shape=[jax.ShapeDtypeStruct(s.shape, s.dtype) for s in fulls],
        input_output_aliases={w: w for w in range(n)},
        scratch_shapes=[pltpu.SemaphoreType.DMA((n,)), pltpu.SemaphoreType.DMA((n,))],
    )(*fulls)


def _all_sum_small(vec, name):
    n = vec.shape[1]

    def body(v_ref, out_ref, buf, send, recv):
        x, y, c, _ = _place()
        me = 4 * x + 2 * y + c
        buf[me] = v_ref[...]
        peers = []
        for mask in range(1, 8):
            px = 1 - x if mask & 4 else x
            py = 1 - y if mask & 2 else y
            pc = 1 - c if mask & 1 else c
            peers.append((px, py, pc))
        cps = []
        for k, peer in enumerate(peers):
            cps.append(pltpu.make_async_remote_copy(src_ref=buf.at[me], dst_ref=buf.at[me], send_sem=send.at[k],
                                                    recv_sem=recv.at[k], device_id=peer, device_id_type=MESH))
            cps[-1].start()
        for k, (px, py, pc) in enumerate(peers):
            slot = buf.at[4 * px + 2 * py + pc]
            pltpu.make_async_remote_copy(src_ref=slot, dst_ref=slot, send_sem=send.at[k], recv_sem=recv.at[k],
                                         device_id=(x, y, c), device_id_type=MESH).wait_recv()
        for cp in cps:
            cp.wait_send()
        total = buf[0]
        for d in range(1, 8):
            total = total + buf[d]
        out_ref[...] = total

    vm = pl.BlockSpec(memory_space=pltpu.VMEM)
    return pl.pallas_call(
        body, name=name, in_specs=[vm], out_specs=vm,
        out_shape=jax.ShapeDtypeStruct(vec.shape, F32),
        scratch_shapes=[pltpu.VMEM((8, 8, n), F32), pltpu.SemaphoreType.DMA((7,)), pltpu.SemaphoreType.DMA((7,))],
    )(vec)


def _pair_sum(g, buf, ids, name):
    p, r, c = g.shape
    h = r // 2
    tr = _pick(h, (256, 128, 64, 32, 16))
    nh = h // tr

    def body(ids_ref, g_ref, b_ref, sums_ref, own_ref):
        s = g_ref[...] + b_ref[...]
        sums_ref[...] = s.astype(BF16)

        @pl.when(pl.program_id(1) == ids_ref[1])
        def _():
            own_ref[...] = s

    return pl.pallas_call(
        body, name=name,
        grid_spec=pltpu.PrefetchScalarGridSpec(
            num_scalar_prefetch=1, grid=(nh, p),
            in_specs=[pl.BlockSpec((None, tr, c), lambda i, q, ids: (q, ids[0] * nh + i, 0)),
                      pl.BlockSpec((None, tr, c), lambda i, q, ids: (q, i, 0))],
            out_specs=[pl.BlockSpec((None, tr, c), lambda i, q, ids: (q, i, 0)),
                       pl.BlockSpec((tr, c), lambda i, q, ids: (i, 0))]),
        out_shape=[jax.ShapeDtypeStruct((p, h, c), BF16), jax.ShapeDtypeStruct((h, c), F32)],
        compiler_params=_params(("parallel", "arbitrary")),
    )(ids, g, buf)


def _final_sum(own, others, ids, name):
    h, c = own.shape
    tr = _pick(h, (256, 128, 64, 32, 16))
    nh = h // tr

    def body(ids_ref, own_ref, oth_ref, out_ref):
        s = own_ref[...]
        for k in range(3):
            s = s + oth_ref[k].astype(F32)
        out_ref[...] = s

    return pl.pallas_call(
        body, name=name,
        grid_spec=pltpu.PrefetchScalarGridSpec(
            num_scalar_prefetch=1, grid=(nh,),
            in_specs=[pl.BlockSpec((tr, c), lambda i, ids: (i, 0)),
                      pl.BlockSpec((3, tr, c), lambda i, ids: (0, i, 0))],
            out_specs=pl.BlockSpec((tr, c), lambda i, ids: (ids[0] * nh + i, 0))),
        out_shape=jax.ShapeDtypeStruct((2 * h, c), F32),
        compiler_params=_params(("parallel",)),
    )(ids, own, others)


def _adamw(w, g, m, v, name):
    r, c = w.shape
    tm = _pick(r, (256, 128, 64, 32, 16, 8)) if r >= 8 else r

    def fn(rows, _):
        w_, g_, m_, v_ = rows
        m2 = ADAM_B1 * m_ + (1.0 - ADAM_B1) * g_
        v2 = ADAM_B2 * v_ + (1.0 - ADAM_B2) * (g_ * g_)
        m_hat = m2 / (1.0 - ADAM_B1 ** ADAM_STEP)
        v_hat = v2 / (1.0 - ADAM_B2 ** ADAM_STEP)
        delta = -ADAM_LR * (m_hat / (jnp.sqrt(v_hat) + ADAM_EPS) + ADAM_WD * w_)
        return [delta, m2, v2], []

    outs, _ = _rows(fn, [w, g, m, v], [], [(c, F32)] * 3, [], tm=tm, name=name)
    return outs


def kernel(x, attn_norm_w, w_in, lb_logits, sb_norm_w, hg_norm_w, w_out, mlp_norm_w, w_up, w_down, final_norm_w, loss_target, m_attn_norm_w, m_w_in, m_lb_logits, m_sb_norm_w, m_hg_norm_w, m_w_out, m_mlp_norm_w, m_w_up, m_w_down, m_final_norm_w, v_attn_norm_w, v_w_in, v_lb_logits, v_sb_norm_w, v_hg_norm_w, v_w_out, v_mlp_norm_w, v_w_up, v_w_down, v_final_norm_w):
    xs, tgt = x[0], loss_target[0]
    t, d = xs.shape
    width = d // 2
    n_heads = width // HEAD
    hps = min(8, n_heads)
    final_w = final_norm_w.reshape(1, d)
    tm_rows = _pick(t, (256, 128))
    tm = _pick(t, (1024, 512, 256))
    blk = min(ATTN_BLOCK, t)
    ones_a = jnp.ones((blk, blk), F32)
    after_tri = jnp.tril(ones_a, -1).astype(BF16)
    before_tri = jnp.triu(ones_a, 1).astype(BF16)
    ones_c = jnp.ones((HG_CHUNK, HG_CHUNK), F32)
    tri_lo, tri_up = jnp.tril(ones_c).astype(BF16), jnp.triu(ones_c).astype(BF16)
    cx, cy, cc = lax.axis_index("x"), lax.axis_index("y"), lax.axis_index("c")
    ids = jnp.stack([cc, 2 * cx + cy]).astype(jnp.int32)

    shards = [w_in[0], w_out[0], w_up[0], w_down[0]]
    cast = [_cast_to_slot(s, ids, f"cast_w{i}") for i, s in enumerate(shards)]
    (g_in,) = _gather_weights(cast[:1], "gather_w_in")
    d_ff = N_CHIPS * w_up.shape[2]
    cs_in, cs_up = g_in.shape[2], w_up.shape[2]
    tn_in = _pick(cs_in, (1792, 896, 512, 256, 128))
    tn_up = _pick(cs_up, (1024, 512, 256))
    tn_d = _pick(d, (1024, 512, 256))
    tk_d = _pick(d, (2048, 1024, 512))

    (u,), _ = _rows(lambda r, c_: ([r[0] * _rstd(r[0]) * c_[0]], []), [xs], [attn_norm_w], [(d, BF16)], [],
                    tm=tm_rows, name="norm_in")
    proj, g_out, g_up = _mm_nn(u, g_in, [F32], tm=_pick(t, (512, 256)), tn=tn_in, tk=tk_d, name="proj_in",
                               comm=("gather", cast[1:3]))
    o_a, mix_a, sb_tot, g_down = _attn_fwd(proj, after_tri, sb_norm_w, n_heads, "sb_fwd", comm=("gather", cast[3:]))
    w_out_all = g_out.reshape(1, d, d)
    w_down_all = g_down.reshape(1, d_ff, d)
    o_b, mix_b, states = _hgrn_fwd(proj, lb_logits, hg_norm_w, tri_lo, n_heads, hps, "hg_fwd")
    mix = jnp.concatenate([mix_a, mix_b], axis=1)
    (h1,) = _mm_nn(mix, w_out_all, [F32], tm=tm, tn=tn_d, tk=tk_d, name="proj_out",
                   epi=lambda acc, res: (acc + res,), extras=(xs,))
    (mn,), _ = _rows(lambda r, c_: ([r[0] * _rstd(r[0]) * c_[0]], []), [h1], [mlp_norm_w], [(d, BF16)], [],
                     tm=tm_rows, name="norm_mlp")
    up_b, act = _mm_nn(mn, g_up, [BF16, BF16], tm=tm, tn=tn_up, tk=tk_d, name="mlp_up",
                       epi=lambda acc: (acc, jnp.square(jnp.maximum(acc, 0.0))))
    (h2,) = _mm_nn(act, w_down_all, [F32], tm=tm, tn=tn_d, tk=_pick(d_ff, (2048, 1024)), name="mlp_down",
                   epi=lambda acc, res: (acc + res,), extras=(h1,))

    def head(rows, consts):
        hh, tg = rows
        w = consts[0]
        n = hh * _rstd(hh)
        err = n * w - tg
        dhh, dw_rows = _rms_bwd(hh, w, err * (1.0 / d))
        return [dhh, dhh], [_colsum(dw_rows), _colsum(err * err)]

    (dh2, dh2_b), (g_final, loss_cols) = _rows(head, [h2, tgt], [final_w], [(d, F32), (d, BF16)], [d, d],
                                                 tm=tm_rows, name="loss_head")

    (dup,) = _mm_nt(dh2_b, w_down_all, [BF16], tm=tm, tn=_pick(d_ff, (1024, 512)), tk=tk_d, name="mlp_down_dx",
                    epi=lambda acc, upv: (acc * (2.0 * jnp.maximum(upv.astype(F32), 0.0)),), extras=(up_b,))
    gw_down = _mm_tn(act, dh2_b, 1, tm=_pick(d_ff, (1024, 512)), tn=tn_d, tk=_pick(t, (1024, 512, 256)),
                     name="mlp_down_dw")
    gw_down = gw_down.reshape(N_CHIPS, d_ff // N_CHIPS, d)
    dmn, their_down = _mm_nt(dup, g_up, [F32], tm=tm, tn=tn_d, tk=_pick(cs_up, (2048, 1024, 512)), name="mlp_up_dx",
                             comm=("swap", [gw_down]))
    gw_up = _mm_tn(mn, dup, N_CHIPS, tm=tn_d, tn=tn_up, tk=_pick(t, (1024, 512, 256)), name="mlp_up_dw")

    def norm_back(rows, consts):
        xx, dy, skip = rows
        dx, dw_rows = _rms_bwd(xx, consts[0], dy)
        tot = dx + skip
        return [tot, tot], [_colsum(dw_rows)]

    (dh1, dh1_b), (g_mlp_norm,) = _rows(norm_back, [h1, dmn, dh2], [mlp_norm_w], [(d, F32), (d, BF16)], [d],
                                         tm=tm_rows, name="norm_mlp_bwd")

    gw_out = _mm_tn(mix, dh1_b, 1, tm=tn_d, tn=tn_d, tk=_pick(t, (1024, 512, 256)), name="proj_out_dw")
    gw_out = gw_out.reshape(N_CHIPS, d // N_CHIPS, d)
    dmix, their_up, their_out = _mm_nt(dh1_b, w_out_all, [F32], tm=tm, tn=tn_d, tk=tk_d, name="proj_out_dx",
                                       comm=("swap", [gw_up, gw_out]))
    pair_mlp = [_pair_sum(g, b, ids, "grads_pair_sum_" + nm)
                for g, b, nm in ((gw_up, their_up, "up"), (gw_down, their_down, "down"))]

    def sb_norm_back(rows, consts):
        dx, dw_rows = _heads_map(lambda o, dy: _rms_bwd(o, consts[0], dy), width, *rows)
        dw = sum(_colsum(dw_rows[:, h * HEAD:(h + 1) * HEAD]) for h in range(n_heads))
        return [dx], [dw]

    (do_a,), (g_sb_norm,) = _rows(sb_norm_back, [o_a, (dmix, width, 0)], [sb_norm_w], [(width, F32)], [HEAD],
                                  tm=tm_rows, name="sb_norm_bwd")
    dq_a, dk_a, dv_a, *landed_mlp = _attn_bwd(proj, sb_tot, do_a, after_tri, before_tri, n_heads, "sb_bwd",
                                              comm=("scatter", [p[0] for p in pair_mlp]))

    def hg_out_back(rows, consts):
        def one(o, gate, dy):
            sg = _sigmoid(gate)
            silu = gate * sg
            n = o * _rstd(o) * consts[0]
            do, dw_rows = _rms_bwd(o, consts[0], dy * silu)
            return do, dy * n * (sg * (1.0 + gate * (1.0 - sg))), dw_rows
        do, dgate, dw_rows = _heads_map(one, width, *rows)
        dw = sum(_colsum(dw_rows[:, h * HEAD:(h + 1) * HEAD]) for h in range(n_heads))
        return [do, dgate], [dw]

    (do_b, dgate), (g_hg_norm,) = _rows(hg_out_back, [o_b, (proj, width, 6), (dmix, width, 1)], [hg_norm_w],
                                         [(width, F32)] * 2, [HEAD], tm=tm_rows, name="hg_out_bwd")
    dhq, dhf, dhi, dlb = _hgrn_bwd(proj, do_b, states, lb_logits, tri_lo, tri_up, n_heads, hps, "hg_bwd")

    (dproj,), _ = _rows(lambda r, _c: ([jnp.concatenate([p.astype(BF16) for p in r], axis=1)], []),
                        [dq_a, dk_a, dv_a, dhq, dhf, dhi, dgate], [], [(7 * width, BF16)], [],
                        tm=tm_rows, name="pack_dproj")
    gw_in = _mm_tn(u, dproj, N_CHIPS, tm=tn_d, tn=tn_in, tk=_pick(t, (1024, 512, 256)), name="proj_in_dw")
    (their_in,) = _sibling_swap([gw_in], "grads_to_sibling_in")
    pair_mix = [_pair_sum(g, b, ids, "grads_pair_sum_" + nm)
                for g, b, nm in ((gw_in, their_in, "in"), (gw_out, their_out, "out"))]
    du, *landed_mix = _mm_nt(dproj, g_in, [F32], tm=tm, tn=tn_d, tk=_pick(cs_in, (1792, 896, 512, 256, 128)),
                             name="proj_in_dx", comm=("scatter", [p[0] for p in pair_mix]))
    (dx,), (g_attn_norm,) = _rows(lambda r, c_: (lambda dxx, dwr: ([dxx + r[2]], [_colsum(dwr)]))(
        *_rms_bwd(r[0], c_[0], r[1])), [xs, du, dh1], [attn_norm_w], [(d, F32)], [d], tm=tm_rows, name="norm_in_bwd")

    halves = [_final_sum(p[1], r, ids, f"grads_final_sum{i}")
              for i, (p, r) in enumerate(zip(pair_mix + pair_mlp, list(landed_mix) + list(landed_mlp)))]
    g_w_in, g_w_out, g_w_up, g_w_down = _sibling_join(halves, "grads_join")

    pieces = [g_attn_norm, g_mlp_norm, g_final, g_sb_norm, g_hg_norm, dlb, loss_cols]
    sizes = [p.shape[1] for p in pieces]
    flat = jnp.concatenate(pieces, axis=1)
    n_small = -(-flat.shape[1] // 1024) * 1024
    flat = jnp.pad(flat, ((0, 0), (0, n_small - flat.shape[1]))).reshape(8, n_small // 8)
    flat = _all_sum_small(flat, "small_all_sum").reshape(1, n_small)
    offs = [sum(sizes[:i]) for i in range(len(sizes))]
    g_attn_norm, g_mlp_norm, g_final, g_sb_norm, g_hg_norm, dlb, loss_cols = [
        flat[:, o:o + s] for o, s in zip(offs, sizes)]

    def small_tail(lbl_ref, dlb_ref, loss_ref, glb_ref, out_ref):
        lb = _lower_bound(lbl_ref[...])
        g0 = dlb_ref[...] * lb * (1.0 - lb)
        glb_ref[0:1, :] = g0
        glb_ref[1:2, :] = -g0
        out_ref[...] = jnp.zeros_like(out_ref) + 0.5 * jnp.sum(loss_ref[...]) * (1.0 / d)

    vm = pl.BlockSpec(memory_space=pltpu.VMEM)
    g_lb, loss11 = pl.pallas_call(
        small_tail, name="small_tail", in_specs=[vm, vm, vm], out_specs=[vm, vm],
        out_shape=[jax.ShapeDtypeStruct(lb_logits.shape, F32), jax.ShapeDtypeStruct((1, 128), F32)],
    )(lb_logits, dlb, loss_cols)
    loss = loss11[0, 0]

    names = ["attn_norm_w", "w_in", "lb_logits", "sb_norm_w", "hg_norm_w", "w_out", "mlp_norm_w", "w_up", "w_down",
             "final_norm_w"]
    ws = [attn_norm_w, w_in[0], lb_logits, sb_norm_w, hg_norm_w, w_out[0], mlp_norm_w, w_up[0], w_down[0], final_w]
    gs = [g_attn_norm, g_w_in, g_lb, g_sb_norm, g_hg_norm, g_w_out, g_mlp_norm, g_w_up, g_w_down, g_final]
    ms = [m_attn_norm_w, m_w_in[0], m_lb_logits, m_sb_norm_w, m_hg_norm_w, m_w_out[0], m_mlp_norm_w, m_w_up[0],
          m_w_down[0], m_final_norm_w.reshape(1, d)]
    vs = [v_attn_norm_w, v_w_in[0], v_lb_logits, v_sb_norm_w, v_hg_norm_w, v_w_out[0], v_mlp_norm_w, v_w_up[0],
          v_w_down[0], v_final_norm_w.reshape(1, d)]
    shapes = [attn_norm_w.shape, w_in.shape, lb_logits.shape, sb_norm_w.shape, hg_norm_w.shape, w_out.shape,
              mlp_norm_w.shape, w_up.shape, w_down.shape, final_norm_w.shape]
    deltas, new_ms, new_vs = [], [], []
    for nm, w_, g_, m_, v_ in zip(names, ws, gs, ms, vs):
        dl, m2, v2 = _adamw(w_, g_, m_, v_, "adamw_" + nm)
        deltas.append(dl)
        new_ms.append(m2)
        new_vs.append(v2)

    def shaped(lst):
        return [a.reshape(s) for a, s in zip(lst, shapes)]

    return (loss, dx[None], *shaped(gs), *shaped(deltas), *shaped(new_ms), *shaped(new_vs))
```

```python
import functools

import jax
import jax.numpy as jnp
from jax import lax
from jax.experimental import pallas as pl
from jax.experimental.pallas import tpu as pltpu

F32 = jnp.float32
BF16 = jnp.bfloat16
MESH = pl.DeviceIdType.MESH

HEAD = 128
NORM_EPS = 1e-5
N_CHIPS = 4
ATTN_BLOCK = 256
ATTN_ROWS = 1024
ATTN_DEAD = -110.0
HG_CHUNK = 32
HG_ROWS = 256
HG_UNROLL = 2
VMEM_LIMIT = 56 * 1024 * 1024

ADAM_LR = 0.001
ADAM_B1 = 0.9
ADAM_B2 = 0.999
ADAM_EPS = 1e-08
ADAM_WD = 0.01
ADAM_STEP = 10


def _pick(n, cands):
    for c in cands:
        if n % c == 0:
            return c
    return n


def _params(sem):
    return pltpu.CompilerParams(dimension_semantics=sem, vmem_limit_bytes=VMEM_LIMIT)


def _dot(a, b):
    return jnp.dot(a, b, preferred_element_type=F32)


def _dot_nt(a, b):
    return lax.dot_general(a, b, (((1,), (1,)), ((), ())), preferred_element_type=F32)


def _dot_tn(a, b):
    return lax.dot_general(a, b, (((0,), (0,)), ((), ())), preferred_element_type=F32)


def _hilo(x):
    hi = x.astype(BF16)
    return hi, (x - hi.astype(F32)).astype(BF16)


def _dot_split(tri, x):
    hi, lo = _hilo(x)
    return _dot(tri, hi) + _dot(tri, lo)


def _dot3(dot, a, b):
    return dot(a[0], b[0]) + (dot(a[0], b[1]) + dot(a[1], b[0]))


def _sigmoid(x):
    return 1.0 / (1.0 + jnp.exp(-x))


def _hosted_call(inner, grid, in_specs, out_specs, out_shape, scratch, args, semantics, name, comm=None):
    kind, arrays = comm if comm else (None, ())
    n_i, n_o, n_s, n_c = len(in_specs), len(out_specs), len(scratch), len(arrays)

    def body(*refs):
        c_in = refs[n_i:n_i + n_c]
        c_out = refs[n_i + n_c + n_o:n_i + 2 * n_c + n_o]
        scr = refs[n_i + 2 * n_c + n_o:]
        if n_c:
            ids = [pl.program_id(ax) for ax in range(len(grid))]
            first, last = ids[0] == 0, ids[0] == grid[0] - 1
            for ax in range(1, len(grid)):
                first, last = first & (ids[ax] == 0), last & (ids[ax] == grid[ax] - 1)
            sems = scr[n_s:]
            copies = {"scatter": _scatter_copies, "swap": _swap_copies}
            begin, end = _gather_copies(c_out, *sems) if kind == "gather" else copies[kind](c_in, c_out, *sems)
            pl.when(first)(begin)
        inner(*refs[:n_i], *refs[n_i + n_c:n_i + n_c + n_o], *scr[:n_s])
        if n_c:
            pl.when(last)(end)

    gather = kind == "gather"
    shape = {"gather": lambda a: a.shape, "scatter": lambda a: (3,) + a.shape[1:],
             "swap": lambda a: (a.shape[0], a.shape[1] // 2, a.shape[2])}
    landed = [jax.ShapeDtypeStruct(shape[kind](a), a.dtype) for a in arrays]
    return pl.pallas_call(
        body, name=name, grid=grid,
        in_specs=list(in_specs) + [ANY] * n_c, out_specs=list(out_specs) + [ANY] * n_c,
        out_shape=list(out_shape) + landed,
        input_output_aliases={n_i + w: n_o + w for w in range(n_c)} if gather else {},
        scratch_shapes=list(scratch) + ([pltpu.SemaphoreType.DMA(
            ({"gather": 6, "scatter": 3, "swap": 1}[kind] * n_c,))] * 2 if n_c else []),
        compiler_params=_params(("arbitrary",) * len(grid) if n_c else semantics),
    )(*args, *arrays)


def _mm_body(kind, nk, n_extra, n_out, epi):
    dot = {"nn": _dot, "nt": _dot_nt, "tn": _dot_tn}[kind]

    def finish(acc, extra_refs, out_refs):
        res = epi(acc, *[e[...] for e in extra_refs]) if epi is not None else (acc,)
        for o, r in zip(out_refs, res):
            o[...] = r.astype(o.dtype)

    def body(a_ref, b_ref, *rest):
        extra_refs = rest[:n_extra]
        out_refs = rest[n_extra:n_extra + n_out]
        if nk == 1:
            finish(dot(a_ref[...], b_ref[...]), extra_refs, out_refs)
            return
        acc_ref = rest[n_extra + n_out]
        k = pl.program_id(2)

        @pl.when(k == 0)
        def _():
            acc_ref[...] = jnp.zeros_like(acc_ref)

        acc_ref[...] += dot(a_ref[...], b_ref[...])

        @pl.when(k == nk - 1)
        def _():
            finish(acc_ref[...], extra_refs, out_refs)

    return body


def _mm_nn(a, w, out_dtypes, *, tm, tn, tk, name, epi=None, extras=(), comm=None):
    m, r = a.shape
    p, _, c = w.shape
    npc = c // tn
    nk = r // tk
    tile = pl.BlockSpec((tm, tn), lambda i, j, k: (i, j))
    return _hosted_call(
        _mm_body("nn", nk, len(extras), len(out_dtypes), epi), (m // tm, p * npc, nk),
        [pl.BlockSpec((tm, tk), lambda i, j, k: (i, k)),
         pl.BlockSpec((None, tk, tn), lambda i, j, k: (j // npc, k, j % npc))] + [tile] * len(extras),
        [tile] * len(out_dtypes), [jax.ShapeDtypeStruct((m, p * c), d) for d in out_dtypes],
        [pltpu.VMEM((tm, tn), F32)] if nk > 1 else [], (a, w, *extras),
        ("parallel", "parallel", "arbitrary"), name, comm)


def _mm_nt(a, w, out_dtypes, *, tm, tn, tk, name, epi=None, extras=(), comm=None):
    m, _ = a.shape
    p, r, c = w.shape
    kpc = c // tk
    nk = p * kpc
    tile = pl.BlockSpec((tm, tn), lambda i, j, k: (i, j))
    return _hosted_call(
        _mm_body("nt", nk, len(extras), len(out_dtypes), epi), (m // tm, r // tn, nk),
        [pl.BlockSpec((tm, tk), lambda i, j, k: (i, k)),
         pl.BlockSpec((None, tn, tk), lambda i, j, k: (k // kpc, j, k % kpc))] + [tile] * len(extras),
        [tile] * len(out_dtypes), [jax.ShapeDtypeStruct((m, r), d) for d in out_dtypes],
        [pltpu.VMEM((tm, tn), F32)] if nk > 1 else [], (a, w, *extras),
        ("parallel", "parallel", "arbitrary"), name, comm)


def _mm_tn(a, g, p, *, tm, tn, tk, name):
    t, r = a.shape
    c = g.shape[1] // p
    npc = c // tn
    nk = t // tk
    body = _mm_body("tn", nk, 0, 1, None)
    return pl.pallas_call(
        body, name=name,
        grid=(r // tm, p * npc, nk),
        in_specs=[pl.BlockSpec((tk, tm), lambda i, j, k: (k, i)),
                  pl.BlockSpec((tk, tn), lambda i, j, k: (k, j))],
        out_specs=[pl.BlockSpec((None, tm, tn), lambda i, j, k: (j // npc, i, j % npc))],
        out_shape=[jax.ShapeDtypeStruct((p, r, c), F32)],
        scratch_shapes=[pltpu.VMEM((tm, tn), F32)] if nk > 1 else [],
        compiler_params=_params(("parallel", "parallel", "arbitrary")),
    )(a, g)[0]


def _rows(fn, row_ins, const_ins, row_outs, acc_outs, *, tm, name):
    specs, arrays = [], []
    t = None
    for item in row_ins:
        if isinstance(item, tuple):
            arr, width, cb = item
            specs.append(pl.BlockSpec((tm, width), functools.partial(lambda i, cb: (i, cb), cb=cb)))
        else:
            arr = item
            specs.append(pl.BlockSpec((tm, arr.shape[1]), lambda i: (i, 0)))
        arrays.append(arr)
        t = arr.shape[0]
    for arr in const_ins:
        specs.append(pl.BlockSpec(arr.shape, lambda i: (0, 0)))
        arrays.append(arr)
    n_in, n_row, n_acc = len(arrays), len(row_outs), len(acc_outs)

    def body(*refs):
        ins = [r[...] for r in refs[:n_in]]
        outs = refs[n_in:]
        row_res, acc_res = fn(ins[:len(row_ins)], ins[len(row_ins):])
        for o, r in zip(outs[:n_row], row_res):
            o[...] = r.astype(o.dtype)
        if n_acc:
            i = pl.program_id(0)

            @pl.when(i == 0)
            def _():
                for o in outs[n_row:]:
                    o[...] = jnp.zeros_like(o)

            for o, r in zip(outs[n_row:], acc_res):
                o[...] += r

    res = pl.pallas_call(
        body, name=name,
        grid=(t // tm,),
        in_specs=specs,
        out_specs=[pl.BlockSpec((tm, c), lambda i: (i, 0)) for c, _ in row_outs]
                  + [pl.BlockSpec((1, c), lambda i: (0, 0)) for c in acc_outs],
        out_shape=[jax.ShapeDtypeStruct((t, c), d) for c, d in row_outs]
                  + [jax.ShapeDtypeStruct((1, c), F32) for c in acc_outs],
        compiler_params=_params(("arbitrary",)),
    )(*arrays)
    return res[:n_row], res[n_row:]


def _rstd(x):
    return lax.rsqrt(jnp.mean(x * x, axis=-1, keepdims=True) + NORM_EPS)


def _rms_bwd(x, w, dy):
    r = _rstd(x)
    n = x * r
    dn = dy * w
    dx = r * (dn - n * jnp.mean(dn * n, axis=-1, keepdims=True))
    return dx, dy * n


def _colsum(x):
    return jnp.sum(x, axis=0, keepdims=True)


def _heads_map(fn, width, *tiles):
    outs = None
    for h in range(width // HEAD):
        res = fn(*[t[:, h * HEAD:(h + 1) * HEAD] for t in tiles])
        if outs is None:
            outs = [[] for _ in res]
        for lst, r in zip(outs, res):
            lst.append(r)
    return [jnp.concatenate(lst, axis=1) for lst in outs]


def _log_one_minus_beta(z):
    return -(jnp.maximum(z, 0.0) + jnp.log(1.0 + jnp.exp(-jnp.abs(z))))


def _attn_fwd(proj, after_tri, norm_w, n_heads, name, comm=None):
    t = proj.shape[0]
    blk = min(ATTN_BLOCK, t)
    qb = min(ATTN_ROWS, t)
    ns = qb // blk
    scale = HEAD ** -0.5

    def body(q_ref, k_ref, v_ref, tri_ref, w_ref, o_ref, mix_ref, tot_ref):
        i = pl.program_id(1)
        q = (q_ref[...] * scale).astype(BF16)
        tri = tri_ref[...]

        def part(r0, j, acc_l, acc_o, masked):
            sl = pl.ds(pl.multiple_of(j * blk, blk), blk)
            m = qb - r0
            z = _dot_nt(q[r0:, :], k_ref[sl, :].astype(BF16))
            lm = _log_one_minus_beta(z)
            if masked:
                mask = lax.broadcasted_iota(jnp.int32, (m, blk), 1) < lax.broadcasted_iota(jnp.int32, (m, blk), 0)
                lmm = jnp.where(mask, lm, 0.0)
            else:
                lmm = lm
            w = jnp.exp(z + lm + acc_l[r0:, :] + _dot(lmm.astype(BF16), tri))
            if masked:
                w = jnp.where(mask, w, 0.0)
            new_l = acc_l[r0:, :] + jnp.sum(lmm, axis=1, keepdims=True)
            new_o = acc_o[r0:, :] + _dot(w.astype(BF16), v_ref[sl, :].astype(BF16))
            if r0:
                new_l = jnp.concatenate([acc_l[:r0, :], new_l], axis=0)
                new_o = jnp.concatenate([acc_o[:r0, :], new_o], axis=0)
            return new_l, new_o

        acc = (jnp.zeros((qb, 1), F32), jnp.zeros((qb, HEAD), F32))
        for jr in reversed(range(ns)):
            acc = part(jr * blk, ns * i + jr, *acc, True)

        def more(c):
            return (c[0] < ns * i) & (jnp.max(c[1]) > ATTN_DEAD)

        def step(c):
            return (c[0] + 1,) + part(0, ns * i - 1 - c[0], c[1], c[2], False)

        swept, acc_l, acc_o = lax.while_loop(more, step, (jnp.int32(0),) + acc)
        o_ref[...] = acc_o
        mix_ref[...] = (acc_o * _rstd(acc_o) * w_ref[...]).astype(BF16)
        first = (ns * i - swept).astype(F32)
        tot_ref[...] = jnp.where(lax.broadcasted_iota(jnp.int32, (qb, HEAD), 1) == 1, first, acc_l)

    width = n_heads * HEAD
    qblk = pl.BlockSpec((qb, HEAD), lambda h, i: (i, h))
    return _hosted_call(
        body, (n_heads, t // qb),
        [qblk,
         pl.BlockSpec((t, HEAD), lambda h, i: (0, n_heads + h)),
         pl.BlockSpec((t, HEAD), lambda h, i: (0, 2 * n_heads + h)),
         pl.BlockSpec((blk, blk), lambda h, i: (0, 0)),
         pl.BlockSpec((1, HEAD), lambda h, i: (0, 0))],
        [qblk, qblk, qblk],
        [jax.ShapeDtypeStruct((t, width), F32), jax.ShapeDtypeStruct((t, width), BF16),
         jax.ShapeDtypeStruct((t, width), F32)],
        [], (proj, proj, proj, after_tri, norm_w), ("parallel", "arbitrary"), name, comm)


def _attn_bwd(proj, tot, do, after_tri, before_tri, n_heads, name, comm=None):
    t = proj.shape[0]
    blk = min(ATTN_BLOCK, t)
    qb = min(ATTN_ROWS, t)
    ns = qb // blk
    scale = HEAD ** -0.5

    def body(q_ref, k_ref, v_ref, tot_ref, do_ref, after_ref, before_ref, dq_ref, dk_ref, dv_ref):
        i = pl.program_id(1)

        @pl.when(i == 0)
        def _():
            dk_ref[...] = jnp.zeros_like(dk_ref)
            dv_ref[...] = jnp.zeros_like(dv_ref)

        q = (q_ref[...] * scale).astype(BF16)
        dob = do_ref[...].astype(BF16)
        total = tot_ref[:, 0:1]
        after_tri = after_ref[...]
        before = before_ref[...]

        def part(r0, j, seen_l, seen_g, dq, masked):
            sl = pl.ds(pl.multiple_of(j * blk, blk), blk)
            m = qb - r0
            qq, dd = q[r0:, :], dob[r0:, :]
            kb = k_ref[sl, :].astype(BF16)
            z = _dot_nt(qq, kb)
            lm = _log_one_minus_beta(z)
            if masked:
                mask = lax.broadcasted_iota(jnp.int32, (m, blk), 1) < lax.broadcasted_iota(jnp.int32, (m, blk), 0)
                lmm = jnp.where(mask, lm, 0.0)
            else:
                lmm = lm
            row_l = jnp.sum(lmm, axis=1, keepdims=True)
            after = (total[r0:, :] - seen_l[r0:, :] - row_l) + _dot(lmm.astype(BF16), after_tri)
            w = jnp.exp(z + lm + after)
            if masked:
                w = jnp.where(mask, w, 0.0)
            sig = jnp.exp(z + lm)
            g = w * _dot_nt(dd, v_ref[sl, :].astype(BF16))
            g_before = seen_g[r0:, :] + _dot(g.astype(BF16), before)
            dz = g * (1.0 - sig) - g_before * sig
            if masked:
                dz = jnp.where(mask, dz, 0.0)
            dzb = dz.astype(BF16)
            dk_ref[sl, :] += _dot_tn(dzb, qq)
            dv_ref[sl, :] += _dot_tn(w.astype(BF16), dd)
            new = (seen_l[r0:, :] + row_l,
                   seen_g[r0:, :] + jnp.sum(g, axis=1, keepdims=True), dq[r0:, :] + _dot(dzb, kb))
            if r0:
                new = tuple(jnp.concatenate([old[:r0, :], n], axis=0) for old, n in zip((seen_l, seen_g, dq), new))
            return new

        zero = jnp.zeros((qb, 1), F32)
        first = jnp.max(tot_ref[0:8, 1:2]).astype(jnp.int32)
        carry = lax.fori_loop(first, ns * i, lambda j, c: part(0, j, *c, False),
                              (zero, zero, jnp.zeros((qb, HEAD), F32)))
        for jr in range(ns):
            carry = part(jr * blk, ns * i + jr, *carry, True)
        dq_ref[...] = carry[2] * scale

    width = n_heads * HEAD
    qblk = pl.BlockSpec((qb, HEAD), lambda h, i: (i, h))
    full = pl.BlockSpec((t, HEAD), lambda h, i: (0, h))
    tri = pl.BlockSpec((blk, blk), lambda h, i: (0, 0))
    return _hosted_call(
        body, (n_heads, t // qb),
        [qblk,
         pl.BlockSpec((t, HEAD), lambda h, i: (0, n_heads + h)),
         pl.BlockSpec((t, HEAD), lambda h, i: (0, 2 * n_heads + h)),
         qblk, qblk, tri, tri],
        [qblk, full, full], [jax.ShapeDtypeStruct((t, width), F32)] * 3,
        [], (proj, proj, proj, tot, do, after_tri, before_tri), ("parallel", "arbitrary"), name, comm)


def _lower_bound(logits):
    l0, l1 = logits[0:1, :], logits[1:2, :]
    mx = jnp.maximum(l0, l1)
    e0, e1 = jnp.exp(l0 - mx), jnp.exp(l1 - mx)
    return e0 / (e0 + e1)


def _hg_chunk(qc, kc, gc, tri_lo):
    c = qc.shape[0]
    cum = _dot_split(tri_lo, gc)
    mid = cum[c // 2 - 1:c // 2, :]
    last = cum[c - 1:c, :]
    qt = qc * jnp.exp(cum - mid)
    kt = kc * jnp.exp(mid - cum)
    qe = qc * jnp.exp(cum)
    kd = kc * jnp.exp(last - cum)
    return cum, mid, last, qt, kt, qe, kd


def _hgrn_fwd(proj, lb_logits, norm_w, tri_lo, n_heads, heads_per_step, name):
    t = proj.shape[0]
    bt = min(HG_ROWS, t)
    c = HG_CHUNK
    nc = bt // c
    hw = heads_per_step * HEAD
    width = n_heads * HEAD
    col0 = 3 * width // hw

    def body(hq_ref, hf_ref, hi_ref, hgate_ref, lbl_ref, w_ref, tri_ref, o_ref, mix_ref, st_ref,
             state, q_scr, k_scr, g_scr):
        @pl.when(pl.program_id(1) == 0)
        def _():
            state[...] = jnp.zeros_like(state)

        lb = _lower_bound(lbl_ref[...])
        f = hf_ref[...]
        g_scr[...] = jnp.log(lb + (1.0 - lb) * _sigmoid(f))
        k_scr[...] = (1.0 - lb) * _sigmoid(-f)
        hq = hq_ref[...]
        q_scr[...] = hq * _sigmoid(hq)
        tri = tri_ref[...]
        causal = lax.broadcasted_iota(jnp.int32, (c, c), 1) <= lax.broadcasted_iota(jnp.int32, (c, c), 0)

        def chunk(ci, carry):
            r = pl.ds(pl.multiple_of(ci * c, c), c)
            vc = hi_ref[r, :].astype(BF16)
            _, _, last, qt, kt, qe, kd = _hg_chunk(q_scr[r, :], k_scr[r, :], g_scr[r, :], tri)
            qt, kt, qe, kd = qt.astype(BF16), kt.astype(BF16), qe.astype(BF16), kd.astype(BF16)
            e_last = jnp.exp(last)
            old = [state[h] for h in range(heads_per_step)]
            outs, new = [], []
            for h in range(heads_per_step):
                cs = slice(h * HEAD, (h + 1) * HEAD)
                a = jnp.where(causal, _dot_nt(qt[:, cs], kt[:, cs]), 0.0)
                outs.append(_dot(a.astype(BF16), vc[:, cs]) + _dot_nt(qe[:, cs], old[h].astype(BF16)))
                new.append(old[h] * e_last[:, cs] + _dot_tn(vc[:, cs], kd[:, cs]))
            for h in range(heads_per_step):
                st_ref[ci, :, h * HEAD:(h + 1) * HEAD] = old[h]
                state[h] = new[h]
            o_ref[r, :] = jnp.concatenate(outs, axis=1)
            return carry

        lax.fori_loop(0, nc, chunk, 0, unroll=HG_UNROLL)

        def finish(o, gate):
            return ((o * _rstd(o) * w_ref[...]) * (gate * _sigmoid(gate)),)

        mix_ref[...] = _heads_map(finish, hw, o_ref[...], hgate_ref[...])[0].astype(BF16)

    def col(group):
        return pl.BlockSpec((bt, hw), functools.partial(lambda hp, tb, g: (tb, col0 + g * (width // hw) + hp), g=group))

    blk = pl.BlockSpec((bt, hw), lambda hp, tb: (tb, hp))
    return pl.pallas_call(
        body, name=name,
        grid=(n_heads // heads_per_step, t // bt),
        in_specs=[col(0), col(1), col(2), col(3),
                  pl.BlockSpec((2, hw), lambda hp, tb: (0, hp)),
                  pl.BlockSpec((1, HEAD), lambda hp, tb: (0, 0)),
                  pl.BlockSpec((c, c), lambda hp, tb: (0, 0))],
        out_specs=[blk, blk, pl.BlockSpec((nc, HEAD, hw), lambda hp, tb: (tb, 0, hp))],
        out_shape=[jax.ShapeDtypeStruct((t, width), F32), jax.ShapeDtypeStruct((t, width), BF16),
                   jax.ShapeDtypeStruct((t // c, HEAD, width), F32)],
        scratch_shapes=[pltpu.VMEM((heads_per_step, HEAD, HEAD), F32)] + [pltpu.VMEM((bt, hw), F32)] * 3,
        compiler_params=_params(("parallel", "arbitrary")),
    )(proj, proj, proj, proj, lb_logits, norm_w, tri_lo)


def _hgrn_bwd(proj, do, states, lb_logits, tri_lo, tri_up, n_heads, heads_per_step, name):
    t = proj.shape[0]
    bt = min(HG_ROWS, t)
    c = HG_CHUNK
    nc = bt // c
    nb = t // bt
    hw = heads_per_step * HEAD
    width = n_heads * HEAD
    col0 = 3 * width // hw

    def body(hq_ref, hf_ref, hi_ref, do_ref, st_ref, lbl_ref, lo_ref, up_ref, dq_ref, df_ref, di_ref, dlb_ref,
             dstate, q_scr, k_scr, g_scr, dk_scr, dg_scr):
        @pl.when(pl.program_id(1) == 0)
        def _():
            dstate[...] = jnp.zeros_like(dstate)
            dlb_ref[...] = jnp.zeros_like(dlb_ref)

        lb = _lower_bound(lbl_ref[...])
        f = hf_ref[...]
        sg = _sigmoid(f)
        sgn = _sigmoid(-f)
        den = lb + (1.0 - lb) * sg
        g_scr[...] = jnp.log(den)
        k_scr[...] = (1.0 - lb) * sgn
        hq = hq_ref[...]
        sq = _sigmoid(hq)
        q_scr[...] = hq * sq
        tri_lo_v = lo_ref[...]
        tri_up_v = up_ref[...]
        causal = lax.broadcasted_iota(jnp.int32, (c, c), 1) <= lax.broadcasted_iota(jnp.int32, (c, c), 0)
        last_row = lax.broadcasted_iota(jnp.int32, (c, hw), 0) == c - 1

        def chunk(cc, carry):
            ci = nc - 1 - cc
            r = pl.ds(pl.multiple_of(ci * c, c), c)
            qc, kc = q_scr[r, :], k_scr[r, :]
            cum, mid, last, qt, kt, qe, kd = _hg_chunk(qc, kc, g_scr[r, :], tri_lo_v)
            qt, kt = _hilo(qt), _hilo(kt)
            qe, kd, doc, vc = [v.astype(BF16) for v in (qe, kd, do_ref[r, :], hi_ref[r, :])]
            e_last = jnp.exp(last)
            sts = [st_ref[ci, :, h * HEAD:(h + 1) * HEAD] for h in range(heads_per_step)]
            dsts = [dstate[h] for h in range(heads_per_step)]
            di, dq_inter, dk_inter, dq_intra, dk_intra, st_sums, new = [], [], [], [], [], [], []
            for h in range(heads_per_step):
                cs = slice(h * HEAD, (h + 1) * HEAD)

                def head(pair):
                    return pair[0][:, cs], pair[1][:, cs]

                st, dst = sts[h].astype(BF16), dsts[h].astype(BF16)
                a = jnp.where(causal, _dot_nt(qt[0][:, cs], kt[0][:, cs]), 0.0).astype(BF16)
                da = _hilo(jnp.where(causal, _dot_nt(doc[:, cs], vc[:, cs]), 0.0))
                di.append(_dot_tn(a, doc[:, cs]) + _dot_nt(kd[:, cs], dst))
                dq_inter.append(_dot(doc[:, cs], st))
                dk_inter.append(_dot(vc[:, cs], dst))
                dq_intra.append(_dot3(_dot, da, head(kt)))
                dk_intra.append(_dot3(_dot_tn, da, head(qt)))
                st_sums.append(_colsum(dsts[h] * sts[h]))
                new.append(dsts[h] * e_last[:, cs] + _dot_tn(doc[:, cs], qe[:, cs]))

            def wide(parts):
                return jnp.concatenate(parts, axis=1)

            dq_inter = wide(dq_inter) * jnp.exp(cum)
            dk_inter = wide(dk_inter) * jnp.exp(last - cum)
            dq = wide(dq_intra) * jnp.exp(cum - mid) + dq_inter
            dk = wide(dk_intra) * jnp.exp(mid - cum) + dk_inter
            d_last = _colsum(kc * dk_inter) + e_last * wide(st_sums)
            dcum = qc * dq - kc * dk + jnp.where(last_row, d_last, 0.0)
            for h in range(heads_per_step):
                dstate[h] = new[h]
            di_ref[r, :] = wide(di)
            dq_ref[r, :] = dq
            dk_scr[r, :] = dk
            dg_scr[r, :] = _dot_split(tri_up_v, dcum)
            return carry

        lax.fori_loop(0, nc, chunk, 0, unroll=HG_UNROLL)

        e = (dg_scr[...] / den - dk_scr[...]) * sgn
        df_ref[...] = e * (1.0 - lb) * sg
        dlb_ref[...] += _colsum(e)
        dq_ref[...] = dq_ref[...] * (sq * (1.0 + hq * (1.0 - sq)))

    def col(group):
        return pl.BlockSpec((bt, hw), functools.partial(
            lambda hp, tb, g: (nb - 1 - tb, col0 + g * (width // hw) + hp), g=group))

    blk = pl.BlockSpec((bt, hw), lambda hp, tb: (nb - 1 - tb, hp))
    tri = pl.BlockSpec((c, c), lambda hp, tb: (0, 0))
    return pl.pallas_call(
        body, name=name,
        grid=(n_heads // heads_per_step, nb),
        in_specs=[col(0), col(1), col(2), blk,
                  pl.BlockSpec((nc, HEAD, hw), lambda hp, tb: (nb - 1 - tb, 0, hp)),
                  pl.BlockSpec((2, hw), lambda hp, tb: (0, hp)), tri, tri],
        out_specs=[blk, blk, blk, pl.BlockSpec((1, hw), lambda hp, tb: (0, hp))],
        out_shape=[jax.ShapeDtypeStruct((t, width), F32)] * 3 + [jax.ShapeDtypeStruct((1, width), F32)],
        scratch_shapes=[pltpu.VMEM((heads_per_step, HEAD, HEAD), F32)] + [pltpu.VMEM((bt, hw), F32)] * 5,
        compiler_params=_params(("parallel", "arbitrary")),
    )(proj, proj, proj, do, states, lb_logits, tri_lo, tri_up)


def _place():
    x, y, c = lax.axis_index("x"), lax.axis_index("y"), lax.axis_index("c")
    chips = [(1 - x, y), (x, 1 - y), (1 - x, 1 - y)]
    return x, y, c, chips


ANY = pl.BlockSpec(memory_space=pl.ANY)


def _cast_to_slot(shard, ids, name):
    r, c = shard.shape
    tm = _pick(r, (256, 128, 64, 32, 16))

    def body(ids_ref, s_ref, o_ref):
        o_ref[...] = s_ref[...].astype(BF16)

    return pl.pallas_call(
        body, name=name,
        grid_spec=pltpu.PrefetchScalarGridSpec(
            num_scalar_prefetch=1, grid=(r // tm,),
            in_specs=[pl.BlockSpec((tm, c), lambda i, ids: (i, 0))],
            out_specs=pl.BlockSpec((None, tm, c), lambda i, ids: (ids[1], i, 0))),
        out_shape=jax.ShapeDtypeStruct((N_CHIPS, r, c), BF16),
        compiler_params=_params(("parallel",)),
    )(ids, shard)


def _gather_copies(bufs, send, recv):
    x, y, c, chips = _place()
    mine = 2 * x + y

    def half(ref, who, core):
        h = ref.shape[-2] // 2
        return ref.at[who, pl.ds(core * h, h), :]

    def copy(w, k, rows, to):
        return pltpu.make_async_remote_copy(src_ref=rows, dst_ref=rows, send_sem=send.at[6 * w + k],
                                            recv_sem=recv.at[6 * w + k], device_id=to, device_id_type=MESH)

    def to_chips(w):
        return [copy(w, k, half(bufs[w], mine, c), (qx, qy, c)) for k, (qx, qy) in enumerate(chips)]

    def to_sibling(w):
        return [copy(w, 3 + k, half(bufs[w], 2 * qx + qy, c), (x, y, 1 - c)) for k, (qx, qy) in enumerate(chips)]

    def begin():
        for w in range(len(bufs)):
            for cp in to_chips(w):
                cp.start()

    def end():
        for w in range(len(bufs)):
            for k, (qx, qy) in enumerate(chips):
                copy(w, k, half(bufs[w], 2 * qx + qy, c), (x, y, c)).wait_recv()
                to_sibling(w)[k].start()
        for w in range(len(bufs)):
            for k, (qx, qy) in enumerate(chips):
                copy(w, 3 + k, half(bufs[w], 2 * qx + qy, 1 - c), (x, y, c)).wait_recv()
        for w in range(len(bufs)):
            for cp in to_chips(w) + to_sibling(w):
                cp.wait_send()

    return begin, end


def _scatter_copies(ins, outs, send, recv):
    _, _, c, chips = _place()

    def copies():
        return [pltpu.make_async_remote_copy(
            src_ref=ins[w].at[2 * qx + qy], dst_ref=outs[w].at[k], send_sem=send.at[3 * w + k],
            recv_sem=recv.at[3 * w + k], device_id=(qx, qy, c), device_id_type=MESH)
            for w in range(len(ins)) for k, (qx, qy) in enumerate(chips)]

    def begin():
        for cp in copies():
            cp.start()

    def end():
        for cp in copies():
            cp.wait()

    return begin, end


def _gather_weights(slots, name):
    n = len(slots)

    def body(*refs):
        begin, end = _gather_copies(refs[n:2 * n], *refs[2 * n:])
        begin()
        end()

    return pl.pallas_call(
        body, name=name,
        in_specs=[ANY] * n, out_specs=[ANY] * n,
        out_shape=[jax.ShapeDtypeStruct(s.shape, s.dtype) for s in slots],
        input_output_aliases={w: w for w in range(n)},
        scratch_shapes=[pltpu.SemaphoreType.DMA((6 * n,)), pltpu.SemaphoreType.DMA((6 * n,))],
    )(*slots)


def _swap_copies(ins, outs, send, recv):
    x, y, c, _ = _place()

    def copies():
        return [pltpu.make_async_remote_copy(
            src_ref=ins[w].at[:, pl.ds((1 - c) * (ins[w].shape[1] // 2), ins[w].shape[1] // 2), :], dst_ref=outs[w],
            send_sem=send.at[w], recv_sem=recv.at[w], device_id=(x, y, 1 - c), device_id_type=MESH)
            for w in range(len(ins))]

    def begin():
        for cp in copies():
            cp.start()

    def end():
        for cp in copies():
            cp.wait()

    return begin, end


def _sibling_swap(grads, name):
    n = len(grads)

    def body(*refs):
        begin, end = _swap_copies(refs[:n], refs[n:2 * n], *refs[2 * n:])
        begin()
        end()

    return pl.pallas_call(
        body, name=name, in_specs=[ANY] * n, out_specs=[ANY] * n,
        out_shape=[jax.ShapeDtypeStruct((g.shape[0], g.shape[1] // 2, g.shape[2]), g.dtype) for g in grads],
        scratch_shapes=[pltpu.SemaphoreType.DMA((n,)), pltpu.SemaphoreType.DMA((n,))],
    )(*grads)


def _sibling_join(fulls, name):
    n = len(fulls)

    def body(*refs):
        bufs = refs[n:2 * n]
        send, recv = refs[2 * n:]
        x, y, c, _ = _place()
        cps = []
        for w in range(n):
            h = bufs[w].shape[0] // 2
            rows = bufs[w].at[pl.ds(c * h, h), :]
            cps.append(pltpu.make_async_remote_copy(
                src_ref=rows, dst_ref=rows, send_sem=send.at[w], recv_sem=recv.at[w],
                device_id=(x, y, 1 - c), device_id_type=MESH))
            cps[-1].start()
        for w in range(n):
            h = bufs[w].shape[0] // 2
            theirs = bufs[w].at[pl.ds((1 - c) * h, h), :]
            pltpu.make_async_remote_copy(src_ref=theirs, dst_ref=theirs, send_sem=send.at[w], recv_sem=recv.at[w],
                                         device_id=(x, y, c), device_id_type=MESH).wait_recv()
        for cp in cps:
            cp.wait_send()

    return pl.pallas_call(
        body, name=name, in_specs=[ANY] * n, out_specs=[ANY] * n,
        out_shape=[jax.ShapeDtypeStruct(s.shape, s.dtype) for s in fulls],
        input_output_aliases={w: w for w in range(n)},
        scratch_shapes=[pltpu.SemaphoreType.DMA((n,)), pltpu.SemaphoreType.DMA((n,))],
    )(*fulls)


def _all_sum_small(vec, name):
    n = vec.shape[1]

    def body(v_ref, out_ref, buf, send, recv):
        x, y, c, _ = _place()
        me = 4 * x + 2 * y + c
        buf[me] = v_ref[...]
        peers = []
        for mask in range(1, 8):
            px = 1 - x if mask & 4 else x
            py = 1 - y if mask & 2 else y
            pc = 1 - c if mask & 1 else c
            peers.append((px, py, pc))
        cps = []
        for k, peer in enumerate(peers):
            cps.append(pltpu.make_async_remote_copy(src_ref=buf.at[me], dst_ref=buf.at[me], send_sem=send.at[k],
                                                    recv_sem=recv.at[k], device_id=peer, device_id_type=MESH))
            cps[-1].start()
        for k, (px, py, pc) in enumerate(peers):
            slot = buf.at[4 * px + 2 * py + pc]
            pltpu.make_async_remote_copy(src_ref=slot, dst_ref=slot, send_sem=send.at[k], recv_sem=recv.at[k],
                                         device_id=(x, y, c), device_id_type=MESH).wait_recv()
        for cp in cps:
            cp.wait_send()
        total = buf[0]
        for d in range(1, 8):
            total = total + buf[d]
        out_ref[...] = total

    vm = pl.BlockSpec(memory_space=pltpu.VMEM)
    return pl.pallas_call(
        body, name=name, in_specs=[vm], out_specs=vm,
        out_shape=jax.ShapeDtypeStruct(vec.shape, F32),
        scratch_shapes=[pltpu.VMEM((8, 8, n), F32), pltpu.SemaphoreType.DMA((7,)), pltpu.SemaphoreType.DMA((7,))],
    )(vec)


def _pair_sum(g, buf, ids, name):
    p, r, c = g.shape
    h = r // 2
    tr = _pick(h, (256, 128, 64, 32, 16))
    nh = h // tr

    def body(ids_ref, g_ref, b_ref, sums_ref, own_ref):
        s = g_ref[...] + b_ref[...]
        sums_ref[...] = s.astype(BF16)

        @pl.when(pl.program_id(1) == ids_ref[1])
        def _():
            own_ref[...] = s

    return pl.pallas_call(
        body, name=name,
        grid_spec=pltpu.PrefetchScalarGridSpec(
            num_scalar_prefetch=1, grid=(nh, p),
            in_specs=[pl.BlockSpec((None, tr, c), lambda i, q, ids: (q, ids[0] * nh + i, 0)),
                      pl.BlockSpec((None, tr, c), lambda i, q, ids: (q, i, 0))],
            out_specs=[pl.BlockSpec((None, tr, c), lambda i, q, ids: (q, i, 0)),
                       pl.BlockSpec((tr, c), lambda i, q, ids: (i, 0))]),
        out_shape=[jax.ShapeDtypeStruct((p, h, c), BF16), jax.ShapeDtypeStruct((h, c), F32)],
        compiler_params=_params(("parallel", "arbitrary")),
    )(ids, g, buf)


def _final_sum(own, others, ids, name):
    h, c = own.shape
    tr = _pick(h, (256, 128, 64, 32, 16))
    nh = h // tr

    def body(ids_ref, own_ref, oth_ref, out_ref):
        s = own_ref[...]
        for k in range(3):
            s = s + oth_ref[k].astype(F32)
        out_ref[...] = s

    return pl.pallas_call(
        body, name=name,
        grid_spec=pltpu.PrefetchScalarGridSpec(
            num_scalar_prefetch=1, grid=(nh,),
            in_specs=[pl.BlockSpec((tr, c), lambda i, ids: (i, 0)),
                      pl.BlockSpec((3, tr, c), lambda i, ids: (0, i, 0))],
            out_specs=pl.BlockSpec((tr, c), lambda i, ids: (ids[0] * nh + i, 0))),
        out_shape=jax.ShapeDtypeStruct((2 * h, c), F32),
        compiler_params=_params(("parallel",)),
    )(ids, own, others)


def _adamw(w, g, m, v, name):
    r, c = w.shape
    tm = _pick(r, (256, 128, 64, 32, 16, 8)) if r >= 8 else r

    def fn(rows, _):
        w_, g_, m_, v_ = rows
        m2 = ADAM_B1 * m_ + (1.0 - ADAM_B1) * g_
        v2 = ADAM_B2 * v_ + (1.0 - ADAM_B2) * (g_ * g_)
        m_hat = m2 / (1.0 - ADAM_B1 ** ADAM_STEP)
        v_hat = v2 / (1.0 - ADAM_B2 ** ADAM_STEP)
        delta = -ADAM_LR * (m_hat / (jnp.sqrt(v_hat) + ADAM_EPS) + ADAM_WD * w_)
        return [delta, m2, v2], []

    outs, _ = _rows(fn, [w, g, m, v], [], [(c, F32)] * 3, [], tm=tm, name=name)
    return outs


def kernel(x, attn_norm_w, w_in, lb_logits, sb_norm_w, hg_norm_w, w_out, mlp_norm_w, w_up, w_down, final_norm_w, loss_target, m_attn_norm_w, m_w_in, m_lb_logits, m_sb_norm_w, m_hg_norm_w, m_w_out, m_mlp_norm_w, m_w_up, m_w_down, m_final_norm_w, v_attn_norm_w, v_w_in, v_lb_logits, v_sb_norm_w, v_hg_norm_w, v_w_out, v_mlp_norm_w, v_w_up, v_w_down, v_final_norm_w):
    xs, tgt = x[0], loss_target[0]
    t, d = xs.shape
    width = d // 2
    n_heads = width // HEAD
    hps = min(8, n_heads)
    final_w = final_norm_w.reshape(1, d)
    tm_rows = _pick(t, (256, 128))
    tm = _pick(t, (1024, 512, 256))
    blk = min(ATTN_BLOCK, t)
    ones_a = jnp.ones((blk, blk), F32)
    after_tri = jnp.tril(ones_a, -1).astype(BF16)
    before_tri = jnp.triu(ones_a, 1).astype(BF16)
    ones_c = jnp.ones((HG_CHUNK, HG_CHUNK), F32)
    tri_lo, tri_up = jnp.tril(ones_c).astype(BF16), jnp.triu(ones_c).astype(BF16)
    cx, cy, cc = lax.axis_index("x"), lax.axis_index("y"), lax.axis_index("c")
    ids = jnp.stack([cc, 2 * cx + cy]).astype(jnp.int32)

    shards = [w_in[0], w_out[0], w_up[0], w_down[0]]
    cast = [_cast_to_slot(s, ids, f"cast_w{i}") for i, s in enumerate(shards)]
    (g_in,) = _gather_weights(cast[:1], "gather_w_in")
    d_ff = N_CHIPS * w_up.shape[2]
    cs_in, cs_up = g_in.shape[2], w_up.shape[2]
    tn_in = _pick(cs_in, (1792, 896, 512, 256, 128))
    tn_up = _pick(cs_up, (1024, 512, 256))
    tn_d = _pick(d, (1024, 512, 256))
    tk_d = _pick(d, (2048, 1024, 512))

    (u,), _ = _rows(lambda r, c_: ([r[0] * _rstd(r[0]) * c_[0]], []), [xs], [attn_norm_w], [(d, BF16)], [],
                    tm=tm_rows, name="norm_in")
    proj, g_out, g_up = _mm_nn(u, g_in, [F32], tm=_pick(t, (512, 256)), tn=tn_in, tk=tk_d, name="proj_in",
                               comm=("gather", cast[1:3]))
    o_a, mix_a, sb_tot, g_down = _attn_fwd(proj, after_tri, sb_norm_w, n_heads, "sb_fwd", comm=("gather", cast[3:]))
    w_out_all = g_out.reshape(1, d, d)
    w_down_all = g_down.reshape(1, d_ff, d)
    o_b, mix_b, states = _hgrn_fwd(proj, lb_logits, hg_norm_w, tri_lo, n_heads, hps, "hg_fwd")
    mix = jnp.concatenate([mix_a, mix_b], axis=1)
    (h1,) = _mm_nn(mix, w_out_all, [F32], tm=tm, tn=tn_d, tk=tk_d, name="proj_out",
                   epi=lambda acc, res: (acc + res,), extras=(xs,))
    (mn,), _ = _rows(lambda r, c_: ([r[0] * _rstd(r[0]) * c_[0]], []), [h1], [mlp_norm_w], [(d, BF16)], [],
                     tm=tm_rows, name="norm_mlp")
    up_b, act = _mm_nn(mn, g_up, [BF16, BF16], tm=tm, tn=tn_up, tk=tk_d, name="mlp_up",
                       epi=lambda acc: (acc, jnp.square(jnp.maximum(acc, 0.0))))
    (h2,) = _mm_nn(act, w_down_all, [F32], tm=tm, tn=tn_d, tk=_pick(d_ff, (2048, 1024)), name="mlp_down",
                   epi=lambda acc, res: (acc + res,), extras=(h1,))

    def head(rows, consts):
        hh, tg = rows
        w = consts[0]
        n = hh * _rstd(hh)
        err = n * w - tg
        dhh, dw_rows = _rms_bwd(hh, w, err * (1.0 / d))
        return [dhh, dhh], [_colsum(dw_rows), _colsum(err * err)]

    (dh2, dh2_b), (g_final, loss_cols) = _rows(head, [h2, tgt], [final_w], [(d, F32), (d, BF16)], [d, d],
                                                 tm=tm_rows, name="loss_head")

    (dup,) = _mm_nt(dh2_b, w_down_all, [BF16], tm=tm, tn=_pick(d_ff, (1024, 512)), tk=tk_d, name="mlp_down_dx",
                    epi=lambda acc, upv: (acc * (2.0 * jnp.maximum(upv.astype(F32), 0.0)),), extras=(up_b,))
    gw_down = _mm_tn(act, dh2_b, 1, tm=_pick(d_ff, (1024, 512)), tn=tn_d, tk=_pick(t, (1024, 512, 256)),
                     name="mlp_down_dw")
    gw_down = gw_down.reshape(N_CHIPS, d_ff // N_CHIPS, d)
    dmn, their_down = _mm_nt(dup, g_up, [F32], tm=tm, tn=tn_d, tk=_pick(cs_up, (2048, 1024, 512)), name="mlp_up_dx",
                             comm=("swap", [gw_down]))
    gw_up = _mm_tn(mn, dup, N_CHIPS, tm=tn_d, tn=tn_up, tk=_pick(t, (1024, 512, 256)), name="mlp_up_dw")

    def norm_back(rows, consts):
        xx, dy, skip = rows
        dx, dw_rows = _rms_bwd(xx, consts[0], dy)
        tot = dx + skip
        return [tot, tot], [_colsum(dw_rows)]

    (dh1, dh1_b), (g_mlp_norm,) = _rows(norm_back, [h1, dmn, dh2], [mlp_norm_w], [(d, F32), (d, BF16)], [d],
                                         tm=tm_rows, name="norm_mlp_bwd")

    gw_out = _mm_tn(mix, dh1_b, 1, tm=tn_d, tn=tn_d, tk=_pick(t, (1024, 512, 256)), name="proj_out_dw")
    gw_out = gw_out.reshape(N_CHIPS, d // N_CHIPS, d)
    dmix, their_up, their_out = _mm_nt(dh1_b, w_out_all, [F32], tm=tm, tn=tn_d, tk=tk_d, name="proj_out_dx",
                                       comm=("swap", [gw_up, gw_out]))
    pair_mlp = [_pair_sum(g, b, ids, "grads_pair_sum_" + nm)
                for g, b, nm in ((gw_up, their_up, "up"), (gw_down, their_down, "down"))]

    def sb_norm_back(rows, consts):
        dx, dw_rows = _heads_map(lambda o, dy: _rms_bwd(o, consts[0], dy), width, *rows)
        dw = sum(_colsum(dw_rows[:, h * HEAD:(h + 1) * HEAD]) for h in range(n_heads))
        return [dx], [dw]

    (do_a,), (g_sb_norm,) = _rows(sb_norm_back, [o_a, (dmix, width, 0)], [sb_norm_w], [(width, F32)], [HEAD],
                                  tm=tm_rows, name="sb_norm_bwd")
    dq_a, dk_a, dv_a, *landed_mlp = _attn_bwd(proj, sb_tot, do_a, after_tri, before_tri, n_heads, "sb_bwd",
                                              comm=("scatter", [p[0] for p in pair_mlp]))

    def hg_out_back(rows, consts):
        def one(o, gate, dy):
            sg = _sigmoid(gate)
            silu = gate * sg
            n = o * _rstd(o) * consts[0]
            do, dw_rows = _rms_bwd(o, consts[0], dy * silu)
            return do, dy * n * (sg * (1.0 + gate * (1.0 - sg))), dw_rows
        do, dgate, dw_rows = _heads_map(one, width, *rows)
        dw = sum(_colsum(dw_rows[:, h * HEAD:(h + 1) * HEAD]) for h in range(n_heads))
        return [do, dgate], [dw]

    (do_b, dgate), (g_hg_norm,) = _rows(hg_out_back, [o_b, (proj, width, 6), (dmix, width, 1)], [hg_norm_w],
                                         [(width, F32)] * 2, [HEAD], tm=tm_rows, name="hg_out_bwd")
    dhq, dhf, dhi, dlb = _hgrn_bwd(proj, do_b, states, lb_logits, tri_lo, tri_up, n_heads, hps, "hg_bwd")

    (dproj,), _ = _rows(lambda r, _c: ([jnp.concatenate([p.astype(BF16) for p in r], axis=1)], []),
                        [dq_a, dk_a, dv_a, dhq, dhf, dhi, dgate], [], [(7 * width, BF16)], [],
                        tm=tm_rows, name="pack_dproj")
    gw_in = _mm_tn(u, dproj, N_CHIPS, tm=tn_d, tn=tn_in, tk=_pick(t, (1024, 512, 256)), name="proj_in_dw")
    (their_in,) = _sibling_swap([gw_in], "grads_to_sibling_in")
    pair_mix = [_pair_sum(g, b, ids, "grads_pair_sum_" + nm)
                for g, b, nm in ((gw_in, their_in, "in"), (gw_out, their_out, "out"))]
    du, *landed_mix = _mm_nt(dproj, g_in, [F32], tm=tm, tn=tn_d, tk=_pick(cs_in, (1792, 896, 512, 256, 128)),
                             name="proj_in_dx", comm=("scatter", [p[0] for p in pair_mix]))
    (dx,), (g_attn_norm,) = _rows(lambda r, c_: (lambda dxx, dwr: ([dxx + r[2]], [_colsum(dwr)]))(
        *_rms_bwd(r[0], c_[0], r[1])), [xs, du, dh1], [attn_norm_w], [(d, F32)], [d], tm=tm_rows, name="norm_in_bwd")

    halves = [_final_sum(p[1], r, ids, f"grads_final_sum{i}")
              for i, (p, r) in enumerate(zip(pair_mix + pair_mlp, list(landed_mix) + list(landed_mlp)))]
    g_w_in, g_w_out, g_w_up, g_w_down = _sibling_join(halves, "grads_join")

    pieces = [g_attn_norm, g_mlp_norm, g_final, g_sb_norm, g_hg_norm, dlb, loss_cols]
    sizes = [p.shape[1] for p in pieces]
    flat = jnp.concatenate(pieces, axis=1)
    n_small = -(-flat.shape[1] // 1024) * 1024
    flat = jnp.pad(flat, ((0, 0), (0, n_small - flat.shape[1]))).reshape(8, n_small // 8)
    flat = _all_sum_small(flat, "small_all_sum").reshape(1, n_small)
    offs = [sum(sizes[:i]) for i in range(len(sizes))]
    g_attn_norm, g_mlp_norm, g_final, g_sb_norm, g_hg_norm, dlb, loss_cols = [
        flat[:, o:o + s] for o, s in zip(offs, sizes)]

    def small_tail(lbl_ref, dlb_ref, loss_ref, glb_ref, out_ref):
        lb = _lower_bound(lbl_ref[...])
        g0 = dlb_ref[...] * lb * (1.0 - lb)
        glb_ref[0:1, :] = g0
        glb_ref[1:2, :] = -g0
        out_ref[...] = jnp.zeros_like(out_ref) + 0.5 * jnp.sum(loss_ref[...]) * (1.0 / d)

    vm = pl.BlockSpec(memory_space=pltpu.VMEM)
    g_lb, loss11 = pl.pallas_call(
        small_tail, name="small_tail", in_specs=[vm, vm, vm], out_specs=[vm, vm],
        out_shape=[jax.ShapeDtypeStruct(lb_logits.shape, F32), jax.ShapeDtypeStruct((1, 128), F32)],
    )(lb_logits, dlb, loss_cols)
    loss = loss11[0, 0]

    names = ["attn_norm_w", "w_in", "lb_logits", "sb_norm_w", "hg_norm_w", "w_out", "mlp_norm_w", "w_up", "w_down",
             "final_norm_w"]
    ws = [attn_norm_w, w_in[0], lb_logits, sb_norm_w, hg_norm_w, w_out[0], mlp_norm_w, w_up[0], w_down[0], final_w]
    gs = [g_attn_norm, g_w_in, g_lb, g_sb_norm, g_hg_norm, g_w_out, g_mlp_norm, g_w_up, g_w_down, g_final]
    ms = [m_attn_norm_w, m_w_in[0], m_lb_logits, m_sb_norm_w, m_hg_norm_w, m_w_out[0], m_mlp_norm_w, m_w_up[0],
          m_w_down[0], m_final_norm_w.reshape(1, d)]
    vs = [v_attn_norm_w, v_w_in[0], v_lb_logits, v_sb_norm_w, v_hg_norm_w, v_w_out[0], v_mlp_norm_w, v_w_up[0],
          v_w_down[0], v_final_norm_w.reshape(1, d)]
    shapes = [attn_norm_w.shape, w_in.shape, lb_logits.shape, sb_norm_w.shape, hg_norm_w.shape, w_out.shape,
              mlp_norm_w.shape, w_up.shape, w_down.shape, final_norm_w.shape]
    deltas, new_ms, new_vs = [], [], []
    for nm, w_, g_, m_, v_ in zip(names, ws, gs, ms, vs):
        dl, m2, v2 = _adamw(w_, g_, m_, v_, "adamw_" + nm)
        deltas.append(dl)
        new_ms.append(m2)
        new_vs.append(v2)

    def shaped(lst):
        return [a.reshape(s) for a, s in zip(lst, shapes)]

    return (loss, dx[None], *shaped(gs), *shaped(deltas), *shaped(new_ms), *shaped(new_vs))
```

```python
import functools

import jax
import jax.numpy as jnp
from jax import lax
from jax.experimental import pallas as pl
from jax.experimental.pallas import tpu as pltpu

F32 = jnp.float32
BF16 = jnp.bfloat16
MESH = pl.DeviceIdType.MESH

HEAD = 128
NORM_EPS = 1e-5
N_CHIPS = 4
ATTN_BLOCK = 256
ATTN_ROWS = 1024
ATTN_DEAD = -110.0
HG_CHUNK = 32
HG_ROWS = 256
HG_UNROLL = 2
VMEM_LIMIT = 56 * 1024 * 1024

ADAM_LR = 0.001
ADAM_B1 = 0.9
ADAM_B2 = 0.999
ADAM_EPS = 1e-08
ADAM_WD = 0.01
ADAM_STEP = 10


def _pick(n, cands):
    for c in cands:
        if n % c == 0:
            return c
    return n


def _params(sem):
    return pltpu.CompilerParams(dimension_semantics=sem, vmem_limit_bytes=VMEM_LIMIT)


def _dot(a, b):
    return jnp.dot(a, b, preferred_element_type=F32)


def _dot_nt(a, b):
    return lax.dot_general(a, b, (((1,), (1,)), ((), ())), preferred_element_type=F32)


def _dot_tn(a, b):
    return lax.dot_general(a, b, (((0,), (0,)), ((), ())), preferred_element_type=F32)


def _hilo(x):
    hi = x.astype(BF16)
    return hi, (x - hi.astype(F32)).astype(BF16)


def _dot_split(tri, x):
    hi, lo = _hilo(x)
    return _dot(tri, hi) + _dot(tri, lo)


def _dot3(dot, a, b):
    return dot(a[0], b[0]) + (dot(a[0], b[1]) + dot(a[1], b[0]))


def _sigmoid(x):
    return 1.0 / (1.0 + jnp.exp(-x))


def _hosted_call(inner, grid, in_specs, out_specs, out_shape, scratch, args, semantics, name, comm=None):
    kind, arrays = comm if comm else (None, ())
    n_i, n_o, n_s, n_c = len(in_specs), len(out_specs), len(scratch), len(arrays)

    def body(*refs):
        c_in = refs[n_i:n_i + n_c]
        c_out = refs[n_i + n_c + n_o:n_i + 2 * n_c + n_o]
        scr = refs[n_i + 2 * n_c + n_o:]
        if n_c:
            ids = [pl.program_id(ax) for ax in range(len(grid))]
            first, last = ids[0] == 0, ids[0] == grid[0] - 1
            for ax in range(1, len(grid)):
                first, last = first & (ids[ax] == 0), last & (ids[ax] == grid[ax] - 1)
            sems = scr[n_s:]
            copies = {"scatter": _scatter_copies, "swap": _swap_copies}
            begin, end = _gather_copies(c_out, *sems) if kind == "gather" else copies[kind](c_in, c_out, *sems)
            pl.when(first)(begin)
        inner(*refs[:n_i], *refs[n_i + n_c:n_i + n_c + n_o], *scr[:n_s])
        if n_c:
            pl.when(last)(end)

    gather = kind == "gather"
    shape = {"gather": lambda a: a.shape, "scatter": lambda a: (3,) + a.shape[1:],
             "swap": lambda a: (a.shape[0], a.shape[1] // 2, a.shape[2])}
    landed = [jax.ShapeDtypeStruct(shape[kind](a), a.dtype) for a in arrays]
    return pl.pallas_call(
        body, name=name, grid=grid,
        in_specs=list(in_specs) + [ANY] * n_c, out_specs=list(out_specs) + [ANY] * n_c,
        out_shape=list(out_shape) + landed,
        input_output_aliases={n_i + w: n_o + w for w in range(n_c)} if gather else {},
        scratch_shapes=list(scratch) + ([pltpu.SemaphoreType.DMA(
            ({"gather": 6, "scatter": 3, "swap": 1}[kind] * n_c,))] * 2 if n_c else []),
        compiler_params=_params(("arbitrary",) * len(grid) if n_c else semantics),
    )(*args, *arrays)


def _mm_body(kind, nk, n_extra, n_out, epi):
    dot = {"nn": _dot, "nt": _dot_nt, "tn": _dot_tn}[kind]

    def finish(acc, extra_refs, out_refs):
        res = epi(acc, *[e[...] for e in extra_refs]) if epi is not None else (acc,)
        for o, r in zip(out_refs, res):
            o[...] = r.astype(o.dtype)

    def body(a_ref, b_ref, *rest):
        extra_refs = rest[:n_extra]
        out_refs = rest[n_extra:n_extra + n_out]
        if nk == 1:
            finish(dot(a_ref[...], b_ref[...]), extra_refs, out_refs)
            return
        acc_ref = rest[n_extra + n_out]
        k = pl.program_id(2)

        @pl.when(k == 0)
        def _():
            acc_ref[...] = jnp.zeros_like(acc_ref)

        acc_ref[...] += dot(a_ref[...], b_ref[...])

        @pl.when(k == nk - 1)
        def _():
            finish(acc_ref[...], extra_refs, out_refs)

    return body


def _mm_nn(a, w, out_dtypes, *, tm, tn, tk, name, epi=None, extras=(), comm=None):
    m, r = a.shape
    p, _, c = w.shape
    npc = c // tn
    nk = r // tk
    tile = pl.BlockSpec((tm, tn), lambda i, j, k: (i, j))
    return _hosted_call(
        _mm_body("nn", nk, len(extras), len(out_dtypes), epi), (m // tm, p * npc, nk),
        [pl.BlockSpec((tm, tk), lambda i, j, k: (i, k)),
         pl.BlockSpec((None, tk, tn), lambda i, j, k: (j // npc, k, j % npc))]
        + [tile if e.shape[0] > 1 else pl.BlockSpec((1, tn), lambda i, j, k: (0, j)) for e in extras],
        [tile] * len(out_dtypes), [jax.ShapeDtypeStruct((m, p * c), d) for d in out_dtypes],
        [pltpu.VMEM((tm, tn), F32)] if nk > 1 else [], (a, w, *extras),
        ("parallel", "parallel", "arbitrary"), name, comm)


def _mm_nt(a, w, out_dtypes, *, tm, tn, tk, name, epi=None, extras=(), comm=None):
    m, _ = a.shape
    p, r, c = w.shape
    kpc = c // tk
    nk = p * kpc
    tile = pl.BlockSpec((tm, tn), lambda i, j, k: (i, j))
    return _hosted_call(
        _mm_body("nt", nk, len(extras), len(out_dtypes), epi), (m // tm, r // tn, nk),
        [pl.BlockSpec((tm, tk), lambda i, j, k: (i, k)),
         pl.BlockSpec((None, tn, tk), lambda i, j, k: (k // kpc, j, k % kpc))] + [tile] * len(extras),
        [tile] * len(out_dtypes), [jax.ShapeDtypeStruct((m, r), d) for d in out_dtypes],
        [pltpu.VMEM((tm, tn), F32)] if nk > 1 else [], (a, w, *extras),
        ("parallel", "parallel", "arbitrary"), name, comm)


def _mm_tn(a, g, p, *, tm, tn, tk, name):
    t, r = a.shape
    c = g.shape[1] // p
    npc = c // tn
    nk = t // tk
    body = _mm_body("tn", nk, 0, 1, None)
    return pl.pallas_call(
        body, name=name,
        grid=(r // tm, p * npc, nk),
        in_specs=[pl.BlockSpec((tk, tm), lambda i, j, k: (k, i)),
                  pl.BlockSpec((tk, tn), lambda i, j, k: (k, j))],
        out_specs=[pl.BlockSpec((None, tm, tn), lambda i, j, k: (j // npc, i, j % npc))],
        out_shape=[jax.ShapeDtypeStruct((p, r, c), F32)],
        scratch_shapes=[pltpu.VMEM((tm, tn), F32)] if nk > 1 else [],
        compiler_params=_params(("parallel", "parallel", "arbitrary")),
    )(a, g)[0]


def _rows(fn, row_ins, const_ins, row_outs, acc_outs, *, tm, name):
    specs, arrays = [], []
    t = None
    for item in row_ins:
        if isinstance(item, tuple):
            arr, width, cb = item
            specs.append(pl.BlockSpec((tm, width), functools.partial(lambda i, cb: (i, cb), cb=cb)))
        else:
            arr = item
            specs.append(pl.BlockSpec((tm, arr.shape[1]), lambda i: (i, 0)))
        arrays.append(arr)
        t = arr.shape[0]
    for arr in const_ins:
        specs.append(pl.BlockSpec(arr.shape, lambda i: (0, 0)))
        arrays.append(arr)
    n_in, n_row, n_acc = len(arrays), len(row_outs), len(acc_outs)

    def body(*refs):
        ins = [r[...] for r in refs[:n_in]]
        outs = refs[n_in:]
        row_res, acc_res = fn(ins[:len(row_ins)], ins[len(row_ins):])
        for o, r in zip(outs[:n_row], row_res):
            o[...] = r.astype(o.dtype)
        if n_acc:
            i = pl.program_id(0)

            @pl.when(i == 0)
            def _():
                for o in outs[n_row:]:
                    o[...] = jnp.zeros_like(o)

            for o, r in zip(outs[n_row:], acc_res):
                o[...] += r

    res = pl.pallas_call(
        body, name=name,
        grid=(t // tm,),
        in_specs=specs,
        out_specs=[pl.BlockSpec((tm, c), lambda i: (i, 0)) for c, _ in row_outs]
                  + [pl.BlockSpec((1, c), lambda i: (0, 0)) for c in acc_outs],
        out_shape=[jax.ShapeDtypeStruct((t, c), d) for c, d in row_outs]
                  + [jax.ShapeDtypeStruct((1, c), F32) for c in acc_outs],
        compiler_params=_params(("arbitrary",)),
    )(*arrays)
    return res[:n_row], res[n_row:]


def _rstd(x):
    return lax.rsqrt(jnp.mean(x * x, axis=-1, keepdims=True) + NORM_EPS)


def _rms_bwd(x, w, dy):
    r = _rstd(x)
    n = x * r
    dn = dy * w
    dx = r * (dn - n * jnp.mean(dn * n, axis=-1, keepdims=True))
    return dx, dy * n


def _colsum(x):
    return jnp.sum(x, axis=0, keepdims=True)


def _heads_map(fn, width, *tiles):
    outs = None
    for h in range(width // HEAD):
        res = fn(*[t[:, h * HEAD:(h + 1) * HEAD] for t in tiles])
        if outs is None:
            outs = [[] for _ in res]
        for lst, r in zip(outs, res):
            lst.append(r)
    return [jnp.concatenate(lst, axis=1) for lst in outs]


def _put_rows(whole, rows, r0):
    r1 = r0 + rows.shape[0]
    parts = ([whole[:r0, :]] if r0 else []) + [rows] + ([whole[r1:, :]] if r1 < whole.shape[0] else [])
    return parts[0] if len(parts) == 1 else jnp.concatenate(parts, axis=0)


def _log_one_minus_beta(z):
    return -(jnp.maximum(z, 0.0) + jnp.log(1.0 + jnp.exp(-jnp.abs(z))))


def _attn_fwd(proj, after_tri, norm_w, n_heads, name, comm=None):
    t = proj.shape[0]
    blk = min(ATTN_BLOCK, t)
    qb = min(ATTN_ROWS, t)
    ns = qb // blk
    scale = HEAD ** -0.5

    def body(q_ref, k_ref, v_ref, tri_ref, w_ref, o_ref, mix_ref, tot_ref):
        i = pl.program_id(1)
        q = (q_ref[...] * scale).astype(BF16)
        tri = tri_ref[...]

        def part(r0, r1, j, acc_l, acc_o, masked):
            sl = pl.ds(pl.multiple_of(j * blk, blk), blk)
            m = r1 - r0
            z = _dot_nt(q[r0:r1, :], k_ref[sl, :].astype(BF16))
            lm = _log_one_minus_beta(z)
            if masked:
                mask = lax.broadcasted_iota(jnp.int32, (m, blk), 1) < lax.broadcasted_iota(jnp.int32, (m, blk), 0)
                lmm = jnp.where(mask, lm, 0.0)
            else:
                lmm = lm
            w = jnp.exp(z + lm + acc_l[r0:r1, :] + _dot(lmm.astype(BF16), tri))
            if masked:
                w = jnp.where(mask, w, 0.0)
            new_l = acc_l[r0:r1, :] + jnp.sum(lmm, axis=1, keepdims=True)
            new_o = acc_o[r0:r1, :] + _dot(w.astype(BF16), v_ref[sl, :].astype(BF16))
            return _put_rows(acc_l, new_l, r0), _put_rows(acc_o, new_o, r0)

        acc = (jnp.zeros((qb, 1), F32), jnp.zeros((qb, HEAD), F32))
        alive = {}
        for jr in reversed(range(ns)):
            r0, r1 = jr * blk, min(qb, (jr + 2) * blk)
            if r1 < qb:
                alive[jr] = jnp.max(acc[0][r1:, :]) > ATTN_DEAD
                acc = lax.cond(alive[jr], functools.partial(lambda a, jr, r1: part(r1, qb, ns * i + jr, *a, False),
                                                            jr=jr, r1=r1), lambda a: a, acc)
            acc = part(r0, r1, ns * i + jr, *acc, True)

        def more(c):
            return (c[0] < ns * i) & (jnp.max(c[1]) > ATTN_DEAD)

        def step(c):
            return (c[0] + 1,) + part(0, qb, ns * i - 1 - c[0], c[1], c[2], False)

        swept, acc_l, acc_o = lax.while_loop(more, step, (jnp.int32(0),) + acc)
        o_ref[...] = acc_o
        mix_ref[...] = (acc_o * _rstd(acc_o) * w_ref[...]).astype(BF16)
        lane = lax.broadcasted_iota(jnp.int32, (qb, HEAD), 1)
        tot = jnp.where(lane == 1, (ns * i - swept).astype(F32), acc_l)
        for jr, flag in alive.items():
            tot = jnp.where(lane == 2 + jr, flag.astype(F32), tot)
        tot_ref[...] = tot

    width = n_heads * HEAD
    qblk = pl.BlockSpec((qb, HEAD), lambda h, i: (i, h))
    return _hosted_call(
        body, (n_heads, t // qb),
        [qblk,
         pl.BlockSpec((t, HEAD), lambda h, i: (0, n_heads + h)),
         pl.BlockSpec((t, HEAD), lambda h, i: (0, 2 * n_heads + h)),
         pl.BlockSpec((blk, blk), lambda h, i: (0, 0)),
         pl.BlockSpec((1, HEAD), lambda h, i: (0, 0))],
        [qblk, qblk, qblk],
        [jax.ShapeDtypeStruct((t, width), F32), jax.ShapeDtypeStruct((t, width), BF16),
         jax.ShapeDtypeStruct((t, width), F32)],
        [], (proj, proj, proj, after_tri, norm_w), ("parallel", "arbitrary"), name, comm)


def _attn_bwd(proj, tot, do, after_tri, before_tri, n_heads, name, comm=None):
    t = proj.shape[0]
    blk = min(ATTN_BLOCK, t)
    qb = min(ATTN_ROWS, t)
    ns = qb // blk
    scale = HEAD ** -0.5

    def body(q_ref, k_ref, v_ref, tot_ref, do_ref, after_ref, before_ref, dq_ref, dk_ref, dv_ref):
        i = pl.program_id(1)

        @pl.when(i == 0)
        def _():
            dk_ref[...] = jnp.zeros_like(dk_ref)
            dv_ref[...] = jnp.zeros_like(dv_ref)

        q = (q_ref[...] * scale).astype(BF16)
        dob = do_ref[...].astype(BF16)
        total = tot_ref[:, 0:1]
        after_tri = after_ref[...]
        before = before_ref[...]

        def part(r0, r1, j, seen_l, seen_g, dq, masked):
            sl = pl.ds(pl.multiple_of(j * blk, blk), blk)
            m = r1 - r0
            qq, dd = q[r0:r1, :], dob[r0:r1, :]
            kb = k_ref[sl, :].astype(BF16)
            z = _dot_nt(qq, kb)
            lm = _log_one_minus_beta(z)
            if masked:
                mask = lax.broadcasted_iota(jnp.int32, (m, blk), 1) < lax.broadcasted_iota(jnp.int32, (m, blk), 0)
                lmm = jnp.where(mask, lm, 0.0)
            else:
                lmm = lm
            row_l = jnp.sum(lmm, axis=1, keepdims=True)
            after = (total[r0:r1, :] - seen_l[r0:r1, :] - row_l) + _dot(lmm.astype(BF16), after_tri)
            w = jnp.exp(z + lm + after)
            if masked:
                w = jnp.where(mask, w, 0.0)
            sig = jnp.exp(z + lm)
            g = w * _dot_nt(dd, v_ref[sl, :].astype(BF16))
            g_before = seen_g[r0:r1, :] + _dot(g.astype(BF16), before)
            dz = g * (1.0 - sig) - g_before * sig
            if masked:
                dz = jnp.where(mask, dz, 0.0)
            dzb = dz.astype(BF16)
            dk_ref[sl, :] += _dot_tn(dzb, qq)
            dv_ref[sl, :] += _dot_tn(w.astype(BF16), dd)
            return (_put_rows(seen_l, seen_l[r0:r1, :] + row_l, r0),
                    _put_rows(seen_g, seen_g[r0:r1, :] + jnp.sum(g, axis=1, keepdims=True), r0),
                    _put_rows(dq, dq[r0:r1, :] + _dot(dzb, kb), r0))

        zero = jnp.zeros((qb, 1), F32)
        first = jnp.max(tot_ref[0:8, 1:2]).astype(jnp.int32)
        carry = lax.fori_loop(first, ns * i, lambda j, c: part(0, qb, j, *c, False),
                              (zero, zero, jnp.zeros((qb, HEAD), F32)))
        for jr in range(ns):
            r0, r1 = jr * blk, min(qb, (jr + 2) * blk)
            carry = part(r0, r1, ns * i + jr, *carry, True)
            if r1 < qb:
                carry = lax.cond(jnp.max(tot_ref[0:8, 2 + jr:3 + jr]) > 0.5,
                                 functools.partial(lambda c, jr, r1: part(r1, qb, ns * i + jr, *c, False), jr=jr, r1=r1),
                                 lambda c: c, carry)
        dq_ref[...] = carry[2] * scale

    width = n_heads * HEAD
    qblk = pl.BlockSpec((qb, HEAD), lambda h, i: (i, h))
    full = pl.BlockSpec((t, HEAD), lambda h, i: (0, h))
    tri = pl.BlockSpec((blk, blk), lambda h, i: (0, 0))
    return _hosted_call(
        body, (n_heads, t // qb),
        [qblk,
         pl.BlockSpec((t, HEAD), lambda h, i: (0, n_heads + h)),
         pl.BlockSpec((t, HEAD), lambda h, i: (0, 2 * n_heads + h)),
         qblk, qblk, tri, tri],
        [qblk, full, full], [jax.ShapeDtypeStruct((t, width), F32)] * 3,
        [], (proj, proj, proj, tot, do, after_tri, before_tri), ("parallel", "arbitrary"), name, comm)


def _lower_bound(logits):
    l0, l1 = logits[0:1, :], logits[1:2, :]
    mx = jnp.maximum(l0, l1)
    e0, e1 = jnp.exp(l0 - mx), jnp.exp(l1 - mx)
    return e0 / (e0 + e1)


def _hg_decay(q, k, g, sums):
    cum, mid, last = [_dot_split(m, g) for m in sums]
    return cum, mid, last, q * jnp.exp(cum - mid), k * jnp.exp(mid - cum), q * jnp.exp(cum), k * jnp.exp(last - cum)


def _hg_sums(rows, chunk):
    t = lax.broadcasted_iota(jnp.int32, (rows, rows), 0)
    j = lax.broadcasted_iota(jnp.int32, (rows, rows), 1)
    same = (t // chunk) == (j // chunk)
    mats = [same & (j <= t), same & (j % chunk < chunk // 2), same, same & (j >= t)]
    return [m.astype(BF16) for m in mats]


def _hgrn_fwd(proj, lb_logits, norm_w, sums, n_heads, heads_per_step, name):
    t = proj.shape[0]
    bt = min(HG_ROWS, t)
    c = HG_CHUNK
    nc = bt // c
    hw = heads_per_step * HEAD
    width = n_heads * HEAD
    col0 = 3 * width // hw

    def body(hq_ref, hf_ref, hi_ref, hgate_ref, lbl_ref, w_ref, s0_ref, s1_ref, s2_ref, o_ref, mix_ref, st_ref,
             state, qt_scr, kt_scr, qe_scr, kd_scr, el_scr):
        @pl.when(pl.program_id(1) == 0)
        def _():
            state[...] = jnp.zeros_like(state)

        lb = _lower_bound(lbl_ref[...])
        f = hf_ref[...]
        hq = hq_ref[...]
        _, _, last, qt, kt, qe, kd = _hg_decay(hq * _sigmoid(hq), (1.0 - lb) * _sigmoid(-f),
                                               jnp.log(lb + (1.0 - lb) * _sigmoid(f)),
                                               (s0_ref[...], s1_ref[...], s2_ref[...]))
        qt_scr[...] = qt.astype(BF16)
        kt_scr[...] = kt.astype(BF16)
        qe_scr[...] = qe.astype(BF16)
        kd_scr[...] = kd.astype(BF16)
        el_scr[...] = jnp.exp(last)
        causal = lax.broadcasted_iota(jnp.int32, (c, c), 1) <= lax.broadcasted_iota(jnp.int32, (c, c), 0)

        def chunk(ci, carry):
            r = pl.ds(pl.multiple_of(ci * c, c), c)
            vc = hi_ref[r, :].astype(BF16)
            qt, kt, qe, kd = qt_scr[r, :], kt_scr[r, :], qe_scr[r, :], kd_scr[r, :]
            e_last = el_scr[r, :][0:1, :]
            old = [state[h] for h in range(heads_per_step)]
            outs, new = [], []
            for h in range(heads_per_step):
                cs = slice(h * HEAD, (h + 1) * HEAD)
                a = jnp.where(causal, _dot_nt(qt[:, cs], kt[:, cs]), 0.0)
                outs.append(_dot(a.astype(BF16), vc[:, cs]) + _dot_nt(qe[:, cs], old[h].astype(BF16)))
                new.append(old[h] * e_last[:, cs] + _dot_tn(vc[:, cs], kd[:, cs]))
            for h in range(heads_per_step):
                st_ref[ci, :, h * HEAD:(h + 1) * HEAD] = old[h]
                state[h] = new[h]
            o_ref[r, :] = jnp.concatenate(outs, axis=1)
            return carry

        lax.fori_loop(0, nc, chunk, 0, unroll=HG_UNROLL)

        def finish(o, gate):
            return ((o * _rstd(o) * w_ref[...]) * (gate * _sigmoid(gate)),)

        mix_ref[...] = _heads_map(finish, hw, o_ref[...], hgate_ref[...])[0].astype(BF16)

    def col(group):
        return pl.BlockSpec((bt, hw), functools.partial(lambda hp, tb, g: (tb, col0 + g * (width // hw) + hp), g=group))

    blk = pl.BlockSpec((bt, hw), lambda hp, tb: (tb, hp))
    mat = pl.BlockSpec((bt, bt), lambda hp, tb: (0, 0))
    return pl.pallas_call(
        body, name=name,
        grid=(n_heads // heads_per_step, t // bt),
        in_specs=[col(0), col(1), col(2), col(3),
                  pl.BlockSpec((2, hw), lambda hp, tb: (0, hp)),
                  pl.BlockSpec((1, HEAD), lambda hp, tb: (0, 0)), mat, mat, mat],
        out_specs=[blk, blk, pl.BlockSpec((nc, HEAD, hw), lambda hp, tb: (tb, 0, hp))],
        out_shape=[jax.ShapeDtypeStruct((t, width), F32), jax.ShapeDtypeStruct((t, width), BF16),
                   jax.ShapeDtypeStruct((t // c, HEAD, width), F32)],
        scratch_shapes=[pltpu.VMEM((heads_per_step, HEAD, HEAD), F32)] + [pltpu.VMEM((bt, hw), BF16)] * 4
                       + [pltpu.VMEM((bt, hw), F32)],
        compiler_params=_params(("parallel", "arbitrary")),
    )(proj, proj, proj, proj, lb_logits, norm_w, *sums[:3])


def _hgrn_bwd(proj, do, states, lb_logits, sums, n_heads, heads_per_step, name):
    t = proj.shape[0]
    bt = min(HG_ROWS, t)
    c = HG_CHUNK
    nc = bt // c
    nb = t // bt
    hw = heads_per_step * HEAD
    width = n_heads * HEAD
    col0 = 3 * width // hw

    def body(hq_ref, hf_ref, hi_ref, do_ref, st_ref, lbl_ref, s0_ref, s1_ref, s2_ref, s3_ref,
             dq_ref, df_ref, di_ref, dlb_ref,
             dstate, qth, qtl, kth, ktl, qe_scr, kd_scr, el_scr, qa_scr, ka_scr, qb_scr, kb_scr, ss_scr):
        @pl.when(pl.program_id(1) == 0)
        def _():
            dstate[...] = jnp.zeros_like(dstate)
            dlb_ref[...] = jnp.zeros_like(dlb_ref)

        lb = _lower_bound(lbl_ref[...])
        f = hf_ref[...]
        sg = _sigmoid(f)
        sgn = _sigmoid(-f)
        den = lb + (1.0 - lb) * sg
        kk = (1.0 - lb) * sgn
        hq = hq_ref[...]
        sq = _sigmoid(hq)
        qq = hq * sq
        cum, mid, last, qt, kt, qe, kd = _hg_decay(qq, kk, jnp.log(den), (s0_ref[...], s1_ref[...], s2_ref[...]))
        qth[...], qtl[...] = _hilo(qt)
        kth[...], ktl[...] = _hilo(kt)
        qe_scr[...] = qe.astype(BF16)
        kd_scr[...] = kd.astype(BF16)
        e_last = jnp.exp(last)
        el_scr[...] = e_last
        causal = lax.broadcasted_iota(jnp.int32, (c, c), 1) <= lax.broadcasted_iota(jnp.int32, (c, c), 0)

        def chunk(cc, carry):
            ci = nc - 1 - cc
            r = pl.ds(pl.multiple_of(ci * c, c), c)
            qt = (qth[r, :], qtl[r, :])
            kt = (kth[r, :], ktl[r, :])
            qe, kd = qe_scr[r, :], kd_scr[r, :]
            doc, vc = do_ref[r, :].astype(BF16), hi_ref[r, :].astype(BF16)
            e_row = el_scr[r, :][0:1, :]
            sts = [st_ref[ci, :, h * HEAD:(h + 1) * HEAD] for h in range(heads_per_step)]
            dsts = [dstate[h] for h in range(heads_per_step)]
            di, dq_inter, dk_inter, dq_intra, dk_intra, st_sums, new = [], [], [], [], [], [], []
            for h in range(heads_per_step):
                cs = slice(h * HEAD, (h + 1) * HEAD)

                def head(pair):
                    return pair[0][:, cs], pair[1][:, cs]

                st, dst = sts[h].astype(BF16), dsts[h].astype(BF16)
                a = jnp.where(causal, _dot_nt(qt[0][:, cs], kt[0][:, cs]), 0.0).astype(BF16)
                da = _hilo(jnp.where(causal, _dot_nt(doc[:, cs], vc[:, cs]), 0.0))
                di.append(_dot_tn(a, doc[:, cs]) + _dot_nt(kd[:, cs], dst))
                dq_inter.append(_dot(doc[:, cs], st))
                dk_inter.append(_dot(vc[:, cs], dst))
                dq_intra.append(_dot3(_dot, da, head(kt)))
                dk_intra.append(_dot3(_dot_tn, da, head(qt)))
                st_sums.append(_colsum(dsts[h] * sts[h]))
                new.append(dsts[h] * e_row[:, cs] + _dot_tn(doc[:, cs], qe[:, cs]))

            def wide(parts):
                return jnp.concatenate(parts, axis=1)

            for h in range(heads_per_step):
                dstate[h] = new[h]
            di_ref[r, :] = wide(di)
            qa_scr[r, :] = wide(dq_intra)
            ka_scr[r, :] = wide(dk_intra)
            qb_scr[r, :] = wide(dq_inter)
            kb_scr[r, :] = wide(dk_inter)
            ss_scr[r, :] = jnp.broadcast_to(wide(st_sums), (c, hw))
            return carry

        lax.fori_loop(0, nc, chunk, 0, unroll=HG_UNROLL)

        dk_inter = kb_scr[...] * jnp.exp(last - cum)
        dq = qa_scr[...] * jnp.exp(cum - mid) + qb_scr[...] * jnp.exp(cum)
        dk = ka_scr[...] * jnp.exp(mid - cum) + dk_inter
        is_last = lax.broadcasted_iota(jnp.int32, (bt, hw), 0) % c == c - 1
        d_last = _dot_split(s2_ref[...], kk * dk_inter) + e_last * ss_scr[...]
        dcum = qq * dq - kk * dk + jnp.where(is_last, d_last, 0.0)
        e = (_dot_split(s3_ref[...], dcum) / den - dk) * sgn
        df_ref[...] = e * (1.0 - lb) * sg
        dlb_ref[...] += _colsum(e)
        dq_ref[...] = dq * (sq * (1.0 + hq * (1.0 - sq)))

    def col(group):
        return pl.BlockSpec((bt, hw), functools.partial(
            lambda hp, tb, g: (nb - 1 - tb, col0 + g * (width // hw) + hp), g=group))

    blk = pl.BlockSpec((bt, hw), lambda hp, tb: (nb - 1 - tb, hp))
    mat = pl.BlockSpec((bt, bt), lambda hp, tb: (0, 0))
    return pl.pallas_call(
        body, name=name,
        grid=(n_heads // heads_per_step, nb),
        in_specs=[col(0), col(1), col(2), blk,
                  pl.BlockSpec((nc, HEAD, hw), lambda hp, tb: (nb - 1 - tb, 0, hp)),
                  pl.BlockSpec((2, hw), lambda hp, tb: (0, hp)), mat, mat, mat, mat],
        out_specs=[blk, blk, blk, pl.BlockSpec((1, hw), lambda hp, tb: (0, hp))],
        out_shape=[jax.ShapeDtypeStruct((t, width), F32)] * 3 + [jax.ShapeDtypeStruct((1, width), F32)],
        scratch_shapes=[pltpu.VMEM((heads_per_step, HEAD, HEAD), F32)] + [pltpu.VMEM((bt, hw), BF16)] * 6
                       + [pltpu.VMEM((bt, hw), F32)] * 6,
        compiler_params=_params(("parallel", "arbitrary")),
    )(proj, proj, proj, do, states, lb_logits, *sums)


def _place():
    x, y, c = lax.axis_index("x"), lax.axis_index("y"), lax.axis_index("c")
    chips = [(1 - x, y), (x, 1 - y), (1 - x, 1 - y)]
    return x, y, c, chips


ANY = pl.BlockSpec(memory_space=pl.ANY)


def _cast_to_slot(shard, ids, name):
    r, c = shard.shape
    tm = _pick(r, (256, 128, 64, 32, 16))

    def body(ids_ref, s_ref, o_ref):
        o_ref[...] = s_ref[...].astype(BF16)

    return pl.pallas_call(
        body, name=name,
        grid_spec=pltpu.PrefetchScalarGridSpec(
            num_scalar_prefetch=1, grid=(r // tm,),
            in_specs=[pl.BlockSpec((tm, c), lambda i, ids: (i, 0))],
            out_specs=pl.BlockSpec((None, tm, c), lambda i, ids: (ids[1], i, 0))),
        out_shape=jax.ShapeDtypeStruct((N_CHIPS, r, c), BF16),
        compiler_params=_params(("parallel",)),
    )(ids, shard)


def _gather_copies(bufs, send, recv):
    x, y, c, chips = _place()
    mine = 2 * x + y

    def half(ref, who, core):
        h = ref.shape[-2] // 2
        return ref.at[who, pl.ds(core * h, h), :]

    def copy(w, k, rows, to):
        return pltpu.make_async_remote_copy(src_ref=rows, dst_ref=rows, send_sem=send.at[6 * w + k],
                                            recv_sem=recv.at[6 * w + k], device_id=to, device_id_type=MESH)

    def to_chips(w):
        return [copy(w, k, half(bufs[w], mine, c), (qx, qy, c)) for k, (qx, qy) in enumerate(chips)]

    def to_sibling(w):
        return [copy(w, 3 + k, half(bufs[w], 2 * qx + qy, c), (x, y, 1 - c)) for k, (qx, qy) in enumerate(chips)]

    def begin():
        for w in range(len(bufs)):
            for cp in to_chips(w):
                cp.start()

    def end():
        for w in range(len(bufs)):
            for k, (qx, qy) in enumerate(chips):
                copy(w, k, half(bufs[w], 2 * qx + qy, c), (x, y, c)).wait_recv()
                to_sibling(w)[k].start()
        for w in range(len(bufs)):
            for k, (qx, qy) in enumerate(chips):
                copy(w, 3 + k, half(bufs[w], 2 * qx + qy, 1 - c), (x, y, c)).wait_recv()
        for w in range(len(bufs)):
            for cp in to_chips(w) + to_sibling(w):
                cp.wait_send()

    return begin, end


def _scatter_copies(ins, outs, send, recv):
    _, _, c, chips = _place()

    def copies():
        return [pltpu.make_async_remote_copy(
            src_ref=ins[w].at[2 * qx + qy], dst_ref=outs[w].at[k], send_sem=send.at[3 * w + k],
            recv_sem=recv.at[3 * w + k], device_id=(qx, qy, c), device_id_type=MESH)
            for w in range(len(ins)) for k, (qx, qy) in enumerate(chips)]

    def begin():
        for cp in copies():
            cp.start()

    def end():
        for cp in copies():
            cp.wait()

    return begin, end


def _gather_weights(slots, name):
    n = len(slots)

    def body(*refs):
        begin, end = _gather_copies(refs[n:2 * n], *refs[2 * n:])
        begin()
        end()

    return pl.pallas_call(
        body, name=name,
        in_specs=[ANY] * n, out_specs=[ANY] * n,
        out_shape=[jax.ShapeDtypeStruct(s.shape, s.dtype) for s in slots],
        input_output_aliases={w: w for w in range(n)},
        scratch_shapes=[pltpu.SemaphoreType.DMA((6 * n,)), pltpu.SemaphoreType.DMA((6 * n,))],
    )(*slots)


def _swap_copies(ins, outs, send, recv):
    x, y, c, _ = _place()

    def copies():
        return [pltpu.make_async_remote_copy(
            src_ref=ins[w].at[:, pl.ds((1 - c) * (ins[w].shape[1] // 2), ins[w].shape[1] // 2), :], dst_ref=outs[w],
            send_sem=send.at[w], recv_sem=recv.at[w], device_id=(x, y, 1 - c), device_id_type=MESH)
            for w in range(len(ins))]

    def begin():
        for cp in copies():
            cp.start()

    def end():
        for cp in copies():
            cp.wait()

    return begin, end


def _sibling_swap(grads, name):
    n = len(grads)

    def body(*refs):
        begin, end = _swap_copies(refs[:n], refs[n:2 * n], *refs[2 * n:])
        begin()
        end()

    return pl.pallas_call(
        body, name=name, in_specs=[ANY] * n, out_specs=[ANY] * n,
        out_shape=[jax.ShapeDtypeStruct((g.shape[0], g.shape[1] // 2, g.shape[2]), g.dtype) for g in grads],
        scratch_shapes=[pltpu.SemaphoreType.DMA((n,)), pltpu.SemaphoreType.DMA((n,))],
    )(*grads)


def _sibling_join(fulls, name):
    n = len(fulls)

    def body(*refs):
        bufs = refs[n:2 * n]
        send, recv = refs[2 * n:]
        x, y, c, _ = _place()
        cps = []
        for w in range(n):
            h = bufs[w].shape[0] // 2
            rows = bufs[w].at[pl.ds(c * h, h), :]
            cps.append(pltpu.make_async_remote_copy(
                src_ref=rows, dst_ref=rows, send_sem=send.at[w], recv_sem=recv.at[w],
                device_id=(x, y, 1 - c), device_id_type=MESH))
            cps[-1].start()
        for w in range(n):
            h = bufs[w].shape[0] // 2
            theirs = bufs[w].at[pl.ds((1 - c) * h, h), :]
            pltpu.make_async_remote_copy(src_ref=theirs, dst_ref=theirs, send_sem=send.at[w], recv_sem=recv.at[w],
                                         device_id=(x, y, c), device_id_type=MESH).wait_recv()
        for cp in cps:
            cp.wait_send()

    return pl.pallas_call(
        body, name=name, in_specs=[ANY] * n, out_specs=[ANY] * n,
        out_shape=[jax.ShapeDtypeStruct(s.shape, s.dtype) for s in fulls],
        input_output_aliases={w: w for w in range(n)},
        scratch_shapes=[pltpu.SemaphoreType.DMA((n,)), pltpu.SemaphoreType.DMA((n,))],
    )(*fulls)


def _all_sum_small(vec, name):
    n = vec.shape[1]

    def body(v_ref, out_ref, buf, send, recv):
        x, y, c, _ = _place()
        me = 4 * x + 2 * y + c
        buf[me] = v_ref[...]
        peers = []
        for mask in range(1, 8):
            px = 1 - x if mask & 4 else x
            py = 1 - y if mask & 2 else y
            pc = 1 - c if mask & 1 else c
            peers.append((px, py, pc))
        cps = []
        for k, peer in enumerate(peers):
            cps.append(pltpu.make_async_remote_copy(src_ref=buf.at[me], dst_ref=buf.at[me], send_sem=send.at[k],
                                                    recv_sem=recv.at[k], device_id=peer, device_id_type=MESH))
            cps[-1].start()
        for k, (px, py, pc) in enumerate(peers):
            slot = buf.at[4 * px + 2 * py + pc]
            pltpu.make_async_remote_copy(src_ref=slot, dst_ref=slot, send_sem=send.at[k], recv_sem=recv.at[k],
                                         device_id=(x, y, c), device_id_type=MESH).wait_recv()
        for cp in cps:
            cp.wait_send()
        total = buf[0]
        for d in range(1, 8):
            total = total + buf[d]
        out_ref[...] = total

    vm = pl.BlockSpec(memory_space=pltpu.VMEM)
    return pl.pallas_call(
        body, name=name, in_specs=[vm], out_specs=vm,
        out_shape=jax.ShapeDtypeStruct(vec.shape, F32),
        scratch_shapes=[pltpu.VMEM((8, 8, n), F32), pltpu.SemaphoreType.DMA((7,)), pltpu.SemaphoreType.DMA((7,))],
    )(vec)


def _pair_sum(g, buf, ids, name):
    p, r, c = g.shape
    h = r // 2
    tr = _pick(h, (256, 128, 64, 32, 16))
    nh = h // tr

    def body(ids_ref, g_ref, b_ref, sums_ref, own_ref):
        s = g_ref[...] + b_ref[...]
        sums_ref[...] = s.astype(BF16)

        @pl.when(pl.program_id(1) == ids_ref[1])
        def _():
            own_ref[...] = s

    return pl.pallas_call(
        body, name=name,
        grid_spec=pltpu.PrefetchScalarGridSpec(
            num_scalar_prefetch=1, grid=(nh, p),
            in_specs=[pl.BlockSpec((None, tr, c), lambda i, q, ids: (q, ids[0] * nh + i, 0)),
                      pl.BlockSpec((None, tr, c), lambda i, q, ids: (q, i, 0))],
            out_specs=[pl.BlockSpec((None, tr, c), lambda i, q, ids: (q, i, 0)),
                       pl.BlockSpec((tr, c), lambda i, q, ids: (i, 0))]),
        out_shape=[jax.ShapeDtypeStruct((p, h, c), BF16), jax.ShapeDtypeStruct((h, c), F32)],
        compiler_params=_params(("parallel", "arbitrary")),
    )(ids, g, buf)


def _final_sum(own, others, ids, name):
    h, c = own.shape
    tr = _pick(h, (256, 128, 64, 32, 16))
    nh = h // tr

    def body(ids_ref, own_ref, oth_ref, out_ref):
        s = own_ref[...]
        for k in range(3):
            s = s + oth_ref[k].astype(F32)
        out_ref[...] = s

    return pl.pallas_call(
        body, name=name,
        grid_spec=pltpu.PrefetchScalarGridSpec(
            num_scalar_prefetch=1, grid=(nh,),
            in_specs=[pl.BlockSpec((tr, c), lambda i, ids: (i, 0)),
                      pl.BlockSpec((3, tr, c), lambda i, ids: (0, i, 0))],
            out_specs=pl.BlockSpec((tr, c), lambda i, ids: (ids[0] * nh + i, 0))),
        out_shape=jax.ShapeDtypeStruct((2 * h, c), F32),
        compiler_params=_params(("parallel",)),
    )(ids, own, others)


def _adamw(w, g, m, v, name):
    r, c = w.shape
    tm = _pick(r, (256, 128, 64, 32, 16, 8)) if r >= 8 else r

    def fn(rows, _):
        w_, g_, m_, v_ = rows
        m2 = ADAM_B1 * m_ + (1.0 - ADAM_B1) * g_
        v2 = ADAM_B2 * v_ + (1.0 - ADAM_B2) * (g_ * g_)
        m_hat = m2 / (1.0 - ADAM_B1 ** ADAM_STEP)
        v_hat = v2 / (1.0 - ADAM_B2 ** ADAM_STEP)
        delta = -ADAM_LR * (m_hat / (jnp.sqrt(v_hat) + ADAM_EPS) + ADAM_WD * w_)
        return [delta, m2, v2], []

    outs, _ = _rows(fn, [w, g, m, v], [], [(c, F32)] * 3, [], tm=tm, name=name)
    return outs


def kernel(x, attn_norm_w, w_in, lb_logits, sb_norm_w, hg_norm_w, w_out, mlp_norm_w, w_up, w_down, final_norm_w, loss_target, m_attn_norm_w, m_w_in, m_lb_logits, m_sb_norm_w, m_hg_norm_w, m_w_out, m_mlp_norm_w, m_w_up, m_w_down, m_final_norm_w, v_attn_norm_w, v_w_in, v_lb_logits, v_sb_norm_w, v_hg_norm_w, v_w_out, v_mlp_norm_w, v_w_up, v_w_down, v_final_norm_w):
    xs, tgt = x[0], loss_target[0]
    t, d = xs.shape
    width = d // 2
    n_heads = width // HEAD
    hps = min(8, n_heads)
    final_w = final_norm_w.reshape(1, d)
    tm_rows = _pick(t, (256, 128))
    tm = _pick(t, (1024, 512, 256))
    blk = min(ATTN_BLOCK, t)
    ones_a = jnp.ones((blk, blk), F32)
    after_tri = jnp.tril(ones_a, -1).astype(BF16)
    before_tri = jnp.triu(ones_a, 1).astype(BF16)
    hg_sums = _hg_sums(min(HG_ROWS, t), HG_CHUNK)
    cx, cy, cc = lax.axis_index("x"), lax.axis_index("y"), lax.axis_index("c")
    ids = jnp.stack([cc, 2 * cx + cy]).astype(jnp.int32)

    shards = [w_in[0], w_out[0], w_up[0], w_down[0]]
    cast = [_cast_to_slot(s, ids, f"cast_w{i}") for i, s in enumerate(shards)]
    (g_in,) = _gather_weights(cast[:1], "gather_w_in")
    d_ff = N_CHIPS * w_up.shape[2]
    cs_in, cs_up = g_in.shape[2], w_up.shape[2]
    tn_in = _pick(cs_in, (1792, 896, 512, 256, 128))
    tn_up = _pick(cs_up, (1024, 512, 256))
    tn_d = _pick(d, (1024, 512, 256))
    tk_d = _pick(d, (2048, 1024, 512))

    (u,), _ = _rows(lambda r, c_: ([r[0] * _rstd(r[0]) * c_[0]], []), [xs], [attn_norm_w], [(d, BF16)], [],
                    tm=tm_rows, name="norm_in")
    proj, g_out, g_up = _mm_nn(u, g_in, [F32], tm=_pick(t, (512, 256)), tn=tn_in, tk=tk_d, name="proj_in",
                               comm=("gather", cast[1:3]))
    o_a, mix_a, sb_tot, g_down = _attn_fwd(proj, after_tri, sb_norm_w, n_heads, "sb_fwd", comm=("gather", cast[3:]))
    w_out_all = g_out.reshape(1, d, d)
    w_down_all = g_down.reshape(1, d_ff, d)
    o_b, mix_b, states = _hgrn_fwd(proj, lb_logits, hg_norm_w, hg_sums, n_heads, hps, "hg_fwd")
    mix = jnp.concatenate([mix_a, mix_b], axis=1)
    def out_and_norm(acc, res, w):
        hh = acc + res
        return hh, hh * _rstd(hh) * w

    h1, mn = _mm_nn(mix, w_out_all, [F32, BF16], tm=_pick(t, (512, 256)), tn=d, tk=tk_d, name="proj_out",
                    epi=out_and_norm, extras=(xs, mlp_norm_w))
    up_b, act = _mm_nn(mn, g_up, [BF16, BF16], tm=tm, tn=tn_up, tk=tk_d, name="mlp_up",
                       epi=lambda acc: (acc, jnp.square(jnp.maximum(acc, 0.0))))
    (h2,) = _mm_nn(act, w_down_all, [F32], tm=tm, tn=_pick(d, (512, 256)), tk=_pick(d_ff, (4096, 2048, 1024)), name="mlp_down",
                   epi=lambda acc, res: (acc + res,), extras=(h1,))

    def head(rows, consts):
        hh, tg = rows
        w = consts[0]
        n = hh * _rstd(hh)
        err = n * w - tg
        dhh, dw_rows = _rms_bwd(hh, w, err * (1.0 / d))
        return [dhh, dhh], [_colsum(dw_rows), _colsum(err * err)]

    (dh2, dh2_b), (g_final, loss_cols) = _rows(head, [h2, tgt], [final_w], [(d, F32), (d, BF16)], [d, d],
                                                 tm=tm_rows, name="loss_head")

    (dup,) = _mm_nt(dh2_b, w_down_all, [BF16], tm=tm, tn=_pick(d_ff, (1024, 512)), tk=tk_d, name="mlp_down_dx",
                    epi=lambda acc, upv: (acc * (2.0 * jnp.maximum(upv.astype(F32), 0.0)),), extras=(up_b,))
    gw_down = _mm_tn(act, dh2_b, 1, tm=_pick(d_ff, (1024, 512)), tn=tn_d, tk=_pick(t, (2048, 1024, 512, 256)),
                     name="mlp_down_dw")
    gw_down = gw_down.reshape(N_CHIPS, d_ff // N_CHIPS, d)
    dmn, their_down = _mm_nt(dup, g_up, [F32], tm=tm, tn=tn_d, tk=_pick(cs_up, (2048, 1024, 512)), name="mlp_up_dx",
                             comm=("swap", [gw_down]))
    gw_up = _mm_tn(mn, dup, N_CHIPS, tm=tn_d, tn=tn_up, tk=_pick(t, (2048, 1024, 512, 256)), name="mlp_up_dw")

    def norm_back(rows, consts):
        xx, dy, skip = rows
        dx, dw_rows = _rms_bwd(xx, consts[0], dy)
        tot = dx + skip
        return [tot, tot], [_colsum(dw_rows)]

    (dh1, dh1_b), (g_mlp_norm,) = _rows(norm_back, [h1, dmn, dh2], [mlp_norm_w], [(d, F32), (d, BF16)], [d],
                                         tm=tm_rows, name="norm_mlp_bwd")

    gw_out = _mm_tn(mix, dh1_b, 1, tm=tn_d, tn=tn_d, tk=_pick(t, (2048, 1024, 512, 256)), name="proj_out_dw")
    gw_out = gw_out.reshape(N_CHIPS, d // N_CHIPS, d)
    dmix, their_up, their_out = _mm_nt(dh1_b, w_out_all, [F32], tm=tm, tn=tn_d, tk=tk_d, name="proj_out_dx",
                                       comm=("swap", [gw_up, gw_out]))
    pair_mlp = [_pair_sum(g, b, ids, "grads_pair_sum_" + nm)
                for g, b, nm in ((gw_up, their_up, "up"), (gw_down, their_down, "down"))]

    def sb_norm_back(rows, consts):
        dx, dw_rows = _heads_map(lambda o, dy: _rms_bwd(o, consts[0], dy), width, *rows)
        dw = sum(_colsum(dw_rows[:, h * HEAD:(h + 1) * HEAD]) for h in range(n_heads))
        return [dx], [dw]

    (do_a,), (g_sb_norm,) = _rows(sb_norm_back, [o_a, (dmix, width, 0)], [sb_norm_w], [(width, F32)], [HEAD],
                                  tm=tm_rows, name="sb_norm_bwd")
    dq_a, dk_a, dv_a, *landed_mlp = _attn_bwd(proj, sb_tot, do_a, after_tri, before_tri, n_heads, "sb_bwd",
                                              comm=("scatter", [p[0] for p in pair_mlp]))

    def hg_out_back(rows, consts):
        def one(o, gate, dy):
            sg = _sigmoid(gate)
            silu = gate * sg
            n = o * _rstd(o) * consts[0]
            do, dw_rows = _rms_bwd(o, consts[0], dy * silu)
            return do, dy * n * (sg * (1.0 + gate * (1.0 - sg))), dw_rows
        do, dgate, dw_rows = _heads_map(one, width, *rows)
        dw = sum(_colsum(dw_rows[:, h * HEAD:(h + 1) * HEAD]) for h in range(n_heads))
        return [do, dgate], [dw]

    (do_b, dgate), (g_hg_norm,) = _rows(hg_out_back, [o_b, (proj, width, 6), (dmix, width, 1)], [hg_norm_w],
                                         [(width, F32)] * 2, [HEAD], tm=tm_rows, name="hg_out_bwd")
    dhq, dhf, dhi, dlb = _hgrn_bwd(proj, do_b, states, lb_logits, hg_sums, n_heads, hps, "hg_bwd")

    (dproj,), _ = _rows(lambda r, _c: ([jnp.concatenate([p.astype(BF16) for p in r], axis=1)], []),
                        [dq_a, dk_a, dv_a, dhq, dhf, dhi, dgate], [], [(7 * width, BF16)], [],
                        tm=tm_rows, name="pack_dproj")
    gw_in = _mm_tn(u, dproj, N_CHIPS, tm=tn_d, tn=tn_in, tk=_pick(t, (2048, 1024, 512, 256)), name="proj_in_dw")
    (their_in,) = _sibling_swap([gw_in], "grads_to_sibling_in")
    pair_mix = [_pair_sum(g, b, ids, "grads_pair_sum_" + nm)
                for g, b, nm in ((gw_in, their_in, "in"), (gw_out, their_out, "out"))]
    du, *landed_mix = _mm_nt(dproj, g_in, [F32], tm=tm, tn=tn_d, tk=_pick(cs_in, (1792, 896, 512, 256, 128)),
                             name="proj_in_dx", comm=("scatter", [p[0] for p in pair_mix]))
    (dx,), (g_attn_norm,) = _rows(lambda r, c_: (lambda dxx, dwr: ([dxx + r[2]], [_colsum(dwr)]))(
        *_rms_bwd(r[0], c_[0], r[1])), [xs, du, dh1], [attn_norm_w], [(d, F32)], [d], tm=tm_rows, name="norm_in_bwd")

    halves = [_final_sum(p[1], r, ids, f"grads_final_sum{i}")
              for i, (p, r) in enumerate(zip(pair_mix + pair_mlp, list(landed_mix) + list(landed_mlp)))]
    g_w_in, g_w_out, g_w_up, g_w_down = _sibling_join(halves, "grads_join")

    pieces = [g_attn_norm, g_mlp_norm, g_final, g_sb_norm, g_hg_norm, dlb, loss_cols]
    sizes = [p.shape[1] for p in pieces]
    flat = jnp.concatenate(pieces, axis=1)
    n_small = -(-flat.shape[1] // 1024) * 1024
    flat = jnp.pad(flat, ((0, 0), (0, n_small - flat.shape[1]))).reshape(8, n_small // 8)
    flat = _all_sum_small(flat, "small_all_sum").reshape(1, n_small)
    offs = [sum(sizes[:i]) for i in range(len(sizes))]
    g_attn_norm, g_mlp_norm, g_final, g_sb_norm, g_hg_norm, dlb, loss_cols = [
        flat[:, o:o + s] for o, s in zip(offs, sizes)]

    def small_tail(lbl_ref, dlb_ref, loss_ref, glb_ref, out_ref):
        lb = _lower_bound(lbl_ref[...])
        g0 = dlb_ref[...] * lb * (1.0 - lb)
        glb_ref[0:1, :] = g0
        glb_ref[1:2, :] = -g0
        out_ref[...] = jnp.zeros_like(out_ref) + 0.5 * jnp.sum(loss_ref[...]) * (1.0 / d)

    vm = pl.BlockSpec(memory_space=pltpu.VMEM)
    g_lb, loss11 = pl.pallas_call(
        small_tail, name="small_tail", in_specs=[vm, vm, vm], out_specs=[vm, vm],
        out_shape=[jax.ShapeDtypeStruct(lb_logits.shape, F32), jax.ShapeDtypeStruct((1, 128), F32)],
    )(lb_logits, dlb, loss_cols)
    loss = loss11[0, 0]

    names = ["attn_norm_w", "w_in", "lb_logits", "sb_norm_w", "hg_norm_w", "w_out", "mlp_norm_w", "w_up", "w_down",
             "final_norm_w"]
    ws = [attn_norm_w, w_in[0], lb_logits, sb_norm_w, hg_norm_w, w_out[0], mlp_norm_w, w_up[0], w_down[0], final_w]
    gs = [g_attn_norm, g_w_in, g_lb, g_sb_norm, g_hg_norm, g_w_out, g_mlp_norm, g_w_up, g_w_down, g_final]
    ms = [m_attn_norm_w, m_w_in[0], m_lb_logits, m_sb_norm_w, m_hg_norm_w, m_w_out[0], m_mlp_norm_w, m_w_up[0],
          m_w_down[0], m_final_norm_w.reshape(1, d)]
    vs = [v_attn_norm_w, v_w_in[0], v_lb_logits, v_sb_norm_w, v_hg_norm_w, v_w_out[0], v_mlp_norm_w, v_w_up[0],
          v_w_down[0], v_final_norm_w.reshape(1, d)]
    shapes = [attn_norm_w.shape, w_in.shape, lb_logits.shape, sb_norm_w.shape, hg_norm_w.shape, w_out.shape,
              mlp_norm_w.shape, w_up.shape, w_down.shape, final_norm_w.shape]
    deltas, new_ms, new_vs = [], [], []
    for nm, w_, g_, m_, v_ in zip(names, ws, gs, ms, vs):
        dl, m2, v2 = _adamw(w_, g_, m_, v_, "adamw_" + nm)
        deltas.append(dl)
        new_ms.append(m2)
        new_vs.append(v2)

    def shaped(lst):
        return [a.reshape(s) for a, s in zip(lst, shapes)]

    return (loss, dx[None], *shaped(gs), *shaped(deltas), *shaped(new_ms), *shaped(new_vs))
```

```python
import functools

import jax
import jax.numpy as jnp
from jax import lax
from jax.experimental import pallas as pl
from jax.experimental.pallas import tpu as pltpu

F32 = jnp.float32
BF16 = jnp.bfloat16
MESH = pl.DeviceIdType.MESH

HEAD = 128
NORM_EPS = 1e-5
N_CHIPS = 4
ATTN_BLOCK = 256
ATTN_ROWS = 1024
ATTN_DEAD = -110.0
HG_CHUNK = 32
HG_ROWS = 256
HG_UNROLL = 2
VMEM_LIMIT = 56 * 1024 * 1024

ADAM_LR = 0.001
ADAM_B1 = 0.9
ADAM_B2 = 0.999
ADAM_EPS = 1e-08
ADAM_WD = 0.01
ADAM_STEP = 10


def _pick(n, cands):
    for c in cands:
        if n % c == 0:
            return c
    return n


def _params(sem):
    return pltpu.CompilerParams(dimension_semantics=sem, vmem_limit_bytes=VMEM_LIMIT)


def _dot(a, b):
    return jnp.dot(a, b, preferred_element_type=F32)


def _dot_nt(a, b):
    return lax.dot_general(a, b, (((1,), (1,)), ((), ())), preferred_element_type=F32)


def _dot_tn(a, b):
    return lax.dot_general(a, b, (((0,), (0,)), ((), ())), preferred_element_type=F32)


def _hilo(x):
    hi = x.astype(BF16)
    return hi, (x - hi.astype(F32)).astype(BF16)


def _dot_split(tri, x):
    hi, lo = _hilo(x)
    return _dot(tri, hi) + _dot(tri, lo)


def _dot3(dot, a, b):
    return dot(a[0], b[0]) + (dot(a[0], b[1]) + dot(a[1], b[0]))


def _sigmoid(x):
    return 1.0 / (1.0 + jnp.exp(-x))


def _hosted_call(inner, grid, in_specs, out_specs, out_shape, scratch, args, semantics, name, comm=None):
    kind, arrays = comm if comm else (None, ())
    n_i, n_o, n_s, n_c = len(in_specs), len(out_specs), len(scratch), len(arrays)

    def body(*refs):
        c_in = refs[n_i:n_i + n_c]
        c_out = refs[n_i + n_c + n_o:n_i + 2 * n_c + n_o]
        scr = refs[n_i + 2 * n_c + n_o:]
        if n_c:
            ids = [pl.program_id(ax) for ax in range(len(grid))]
            first, last = ids[0] == 0, ids[0] == grid[0] - 1
            for ax in range(1, len(grid)):
                first, last = first & (ids[ax] == 0), last & (ids[ax] == grid[ax] - 1)
            sems = scr[n_s:]
            copies = {"scatter": _scatter_copies, "swap": _swap_copies}
            begin, end = _gather_copies(c_out, *sems) if kind == "gather" else copies[kind](c_in, c_out, *sems)
            pl.when(first)(begin)
        inner(*refs[:n_i], *refs[n_i + n_c:n_i + n_c + n_o], *scr[:n_s])
        if n_c:
            pl.when(last)(end)

    gather = kind == "gather"
    shape = {"gather": lambda a: a.shape, "scatter": lambda a: (3,) + a.shape[1:],
             "swap": lambda a: (a.shape[0], a.shape[1] // 2, a.shape[2])}
    landed = [jax.ShapeDtypeStruct(shape[kind](a), a.dtype) for a in arrays]
    return pl.pallas_call(
        body, name=name, grid=grid,
        in_specs=list(in_specs) + [ANY] * n_c, out_specs=list(out_specs) + [ANY] * n_c,
        out_shape=list(out_shape) + landed,
        input_output_aliases={n_i + w: n_o + w for w in range(n_c)} if gather else {},
        scratch_shapes=list(scratch) + ([pltpu.SemaphoreType.DMA(
            ({"gather": 6, "scatter": 3, "swap": 1}[kind] * n_c,))] * 2 if n_c else []),
        compiler_params=_params(("arbitrary",) * len(grid) if n_c else semantics),
    )(*args, *arrays)


def _mm_body(kind, nk, n_extra, n_out, epi):
    dot = {"nn": _dot, "nt": _dot_nt, "tn": _dot_tn}[kind]

    def finish(acc, extra_refs, out_refs):
        res = epi(acc, *[e[...] for e in extra_refs]) if epi is not None else (acc,)
        for o, r in zip(out_refs, res):
            o[...] = r.astype(o.dtype)

    def body(a_ref, b_ref, *rest):
        extra_refs = rest[:n_extra]
        out_refs = rest[n_extra:n_extra + n_out]
        if nk == 1:
            finish(dot(a_ref[...], b_ref[...]), extra_refs, out_refs)
            return
        acc_ref = rest[n_extra + n_out]
        k = pl.program_id(2)

        @pl.when(k == 0)
        def _():
            acc_ref[...] = jnp.zeros_like(acc_ref)

        acc_ref[...] += dot(a_ref[...], b_ref[...])

        @pl.when(k == nk - 1)
        def _():
            finish(acc_ref[...], extra_refs, out_refs)

    return body


def _mm_nn(a, w, out_dtypes, *, tm, tn, tk, name, epi=None, extras=(), comm=None):
    m, r = a.shape
    p, _, c = w.shape
    npc = c // tn
    nk = r // tk
    tile = pl.BlockSpec((tm, tn), lambda i, j, k: (i, j))
    return _hosted_call(
        _mm_body("nn", nk, len(extras), len(out_dtypes), epi), (m // tm, p * npc, nk),
        [pl.BlockSpec((tm, tk), lambda i, j, k: (i, k)),
         pl.BlockSpec((None, tk, tn), lambda i, j, k: (j // npc, k, j % npc))]
        + [tile if e.shape[0] > 1 else pl.BlockSpec((1, tn), lambda i, j, k: (0, j)) for e in extras],
        [tile] * len(out_dtypes), [jax.ShapeDtypeStruct((m, p * c), d) for d in out_dtypes],
        [pltpu.VMEM((tm, tn), F32)] if nk > 1 else [], (a, w, *extras),
        ("parallel", "parallel", "arbitrary"), name, comm)


def _mm_nt(a, w, out_dtypes, *, tm, tn, tk, name, epi=None, extras=(), comm=None):
    m, _ = a.shape
    p, r, c = w.shape
    kpc = c // tk
    nk = p * kpc
    tile = pl.BlockSpec((tm, tn), lambda i, j, k: (i, j))
    return _hosted_call(
        _mm_body("nt", nk, len(extras), len(out_dtypes), epi), (m // tm, r // tn, nk),
        [pl.BlockSpec((tm, tk), lambda i, j, k: (i, k)),
         pl.BlockSpec((None, tn, tk), lambda i, j, k: (k // kpc, j, k % kpc))] + [tile] * len(extras),
        [tile] * len(out_dtypes), [jax.ShapeDtypeStruct((m, r), d) for d in out_dtypes],
        [pltpu.VMEM((tm, tn), F32)] if nk > 1 else [], (a, w, *extras),
        ("parallel", "parallel", "arbitrary"), name, comm)


def _mm_tn(a, g, p, *, tm, tn, tk, name):
    t, r = a.shape
    c = g.shape[1] // p
    npc = c // tn
    nk = t // tk
    body = _mm_body("tn", nk, 0, 1, None)
    return pl.pallas_call(
        body, name=name,
        grid=(r // tm, p * npc, nk),
        in_specs=[pl.BlockSpec((tk, tm), lambda i, j, k: (k, i)),
                  pl.BlockSpec((tk, tn), lambda i, j, k: (k, j))],
        out_specs=[pl.BlockSpec((None, tm, tn), lambda i, j, k: (j // npc, i, j % npc))],
        out_shape=[jax.ShapeDtypeStruct((p, r, c), F32)],
        scratch_shapes=[pltpu.VMEM((tm, tn), F32)] if nk > 1 else [],
        compiler_params=_params(("parallel", "parallel", "arbitrary")),
    )(a, g)[0]


def _rows(fn, row_ins, const_ins, row_outs, acc_outs, *, tm, name):
    specs, arrays = [], []
    t = None
    for item in row_ins:
        if isinstance(item, tuple):
            arr, width, cb = item
            specs.append(pl.BlockSpec((tm, width), functools.partial(lambda i, cb: (i, cb), cb=cb)))
        else:
            arr = item
            specs.append(pl.BlockSpec((tm, arr.shape[1]), lambda i: (i, 0)))
        arrays.append(arr)
        t = arr.shape[0]
    for arr in const_ins:
        specs.append(pl.BlockSpec(arr.shape, lambda i: (0, 0)))
        arrays.append(arr)
    n_in, n_row, n_acc = len(arrays), len(row_outs), len(acc_outs)

    def body(*refs):
        ins = [r[...] for r in refs[:n_in]]
        outs = refs[n_in:]
        row_res, acc_res = fn(ins[:len(row_ins)], ins[len(row_ins):])
        for o, r in zip(outs[:n_row], row_res):
            o[...] = r.astype(o.dtype)
        if n_acc:
            i = pl.program_id(0)

            @pl.when(i == 0)
            def _():
                for o in outs[n_row:]:
                    o[...] = jnp.zeros_like(o)

            for o, r in zip(outs[n_row:], acc_res):
                o[...] += r

    res = pl.pallas_call(
        body, name=name,
        grid=(t // tm,),
        in_specs=specs,
        out_specs=[pl.BlockSpec((tm, c), lambda i: (i, 0)) for c, _ in row_outs]
                  + [pl.BlockSpec((1, c), lambda i: (0, 0)) for c in acc_outs],
        out_shape=[jax.ShapeDtypeStruct((t, c), d) for c, d in row_outs]
                  + [jax.ShapeDtypeStruct((1, c), F32) for c in acc_outs],
        compiler_params=_params(("arbitrary",)),
    )(*arrays)
    return res[:n_row], res[n_row:]


def _rstd(x):
    return lax.rsqrt(jnp.mean(x * x, axis=-1, keepdims=True) + NORM_EPS)


def _rms_bwd(x, w, dy):
    r = _rstd(x)
    n = x * r
    dn = dy * w
    dx = r * (dn - n * jnp.mean(dn * n, axis=-1, keepdims=True))
    return dx, dy * n


def _colsum(x):
    return jnp.sum(x, axis=0, keepdims=True)


def _heads_map(fn, width, *tiles):
    outs = None
    for h in range(width // HEAD):
        res = fn(*[t[:, h * HEAD:(h + 1) * HEAD] for t in tiles])
        if outs is None:
            outs = [[] for _ in res]
        for lst, r in zip(outs, res):
            lst.append(r)
    return [jnp.concatenate(lst, axis=1) for lst in outs]


def _log_one_minus_beta(z):
    return -(jnp.maximum(z, 0.0) + jnp.log(1.0 + jnp.exp(-jnp.abs(z))))


def _attn_fwd(proj, after_tri, norm_w, n_heads, name, comm=None):
    t = proj.shape[0]
    blk = min(ATTN_BLOCK, t)
    qb = min(ATTN_ROWS, t)
    ns = qb // blk
    scale = HEAD ** -0.5

    def body(q_ref, k_ref, v_ref, tri_ref, w_ref, o_ref, mix_ref, tot_ref):
        i = pl.program_id(1)
        q = (q_ref[...] * scale).astype(BF16)
        tri = tri_ref[...]

        def part(r0, j, acc_l, acc_o, masked):
            sl = pl.ds(pl.multiple_of(j * blk, blk), blk)
            m = qb - r0
            z = _dot_nt(q[r0:, :], k_ref[sl, :].astype(BF16))
            lm = _log_one_minus_beta(z)
            if masked:
                mask = lax.broadcasted_iota(jnp.int32, (m, blk), 1) < lax.broadcasted_iota(jnp.int32, (m, blk), 0)
                lmm = jnp.where(mask, lm, 0.0)
            else:
                lmm = lm
            w = jnp.exp(z + lm + acc_l[r0:, :] + _dot(lmm.astype(BF16), tri))
            if masked:
                w = jnp.where(mask, w, 0.0)
            new_l = acc_l[r0:, :] + jnp.sum(lmm, axis=1, keepdims=True)
            new_o = acc_o[r0:, :] + _dot(w.astype(BF16), v_ref[sl, :].astype(BF16))
            if r0:
                new_l = jnp.concatenate([acc_l[:r0, :], new_l], axis=0)
                new_o = jnp.concatenate([acc_o[:r0, :], new_o], axis=0)
            return new_l, new_o

        acc = (jnp.zeros((qb, 1), F32), jnp.zeros((qb, HEAD), F32))
        for jr in reversed(range(ns)):
            acc = part(jr * blk, ns * i + jr, *acc, True)

        def more(c):
            return (c[0] < ns * i) & (jnp.max(c[1]) > ATTN_DEAD)

        def step(c):
            return (c[0] + 1,) + part(0, ns * i - 1 - c[0], c[1], c[2], False)

        swept, acc_l, acc_o = lax.while_loop(more, step, (jnp.int32(0),) + acc)
        o_ref[...] = acc_o
        mix_ref[...] = (acc_o * _rstd(acc_o) * w_ref[...]).astype(BF16)
        first = (ns * i - swept).astype(F32)
        tot_ref[...] = jnp.where(lax.broadcasted_iota(jnp.int32, (qb, HEAD), 1) == 1, first, acc_l)

    width = n_heads * HEAD
    qblk = pl.BlockSpec((qb, HEAD), lambda h, i: (i, h))
    return _hosted_call(
        body, (n_heads, t // qb),
        [qblk,
         pl.BlockSpec((t, HEAD), lambda h, i: (0, n_heads + h)),
         pl.BlockSpec((t, HEAD), lambda h, i: (0, 2 * n_heads + h)),
         pl.BlockSpec((blk, blk), lambda h, i: (0, 0)),
         pl.BlockSpec((1, HEAD), lambda h, i: (0, 0))],
        [qblk, qblk, qblk],
        [jax.ShapeDtypeStruct((t, width), F32), jax.ShapeDtypeStruct((t, width), BF16),
         jax.ShapeDtypeStruct((t, width), F32)],
        [], (proj, proj, proj, after_tri, norm_w), ("parallel", "arbitrary"), name, comm)


def _attn_bwd(proj, tot, do, after_tri, before_tri, n_heads, name, comm=None):
    t = proj.shape[0]
    blk = min(ATTN_BLOCK, t)
    qb = min(ATTN_ROWS, t)
    ns = qb // blk
    scale = HEAD ** -0.5

    def body(q_ref, k_ref, v_ref, tot_ref, do_ref, after_ref, before_ref, dq_ref, dk_ref, dv_ref):
        i = pl.program_id(1)

        @pl.when(i == 0)
        def _():
            dk_ref[...] = jnp.zeros_like(dk_ref)
            dv_ref[...] = jnp.zeros_like(dv_ref)

        q = (q_ref[...] * scale).astype(BF16)
        dob = do_ref[...].astype(BF16)
        total = tot_ref[:, 0:1]
        after_tri = after_ref[...]
        before = before_ref[...]

        def part(r0, j, seen_l, seen_g, dq, masked):
            sl = pl.ds(pl.multiple_of(j * blk, blk), blk)
            m = qb - r0
            qq, dd = q[r0:, :], dob[r0:, :]
            kb = k_ref[sl, :].astype(BF16)
            z = _dot_nt(qq, kb)
            lm = _log_one_minus_beta(z)
            if masked:
                mask = lax.broadcasted_iota(jnp.int32, (m, blk), 1) < lax.broadcasted_iota(jnp.int32, (m, blk), 0)
                lmm = jnp.where(mask, lm, 0.0)
            else:
                lmm = lm
            row_l = jnp.sum(lmm, axis=1, keepdims=True)
            after = (total[r0:, :] - seen_l[r0:, :] - row_l) + _dot(lmm.astype(BF16), after_tri)
            w = jnp.exp(z + lm + after)
            if masked:
                w = jnp.where(mask, w, 0.0)
            sig = jnp.exp(z + lm)
            g = w * _dot_nt(dd, v_ref[sl, :].astype(BF16))
            g_before = seen_g[r0:, :] + _dot(g.astype(BF16), before)
            dz = g * (1.0 - sig) - g_before * sig
            if masked:
                dz = jnp.where(mask, dz, 0.0)
            dzb = dz.astype(BF16)
            dk_ref[sl, :] += _dot_tn(dzb, qq)
            dv_ref[sl, :] += _dot_tn(w.astype(BF16), dd)
            new = (seen_l[r0:, :] + row_l,
                   seen_g[r0:, :] + jnp.sum(g, axis=1, keepdims=True), dq[r0:, :] + _dot(dzb, kb))
            if r0:
                new = tuple(jnp.concatenate([old[:r0, :], n], axis=0) for old, n in zip((seen_l, seen_g, dq), new))
            return new

        zero = jnp.zeros((qb, 1), F32)
        first = jnp.max(tot_ref[0:8, 1:2]).astype(jnp.int32)
        carry = lax.fori_loop(first, ns * i, lambda j, c: part(0, j, *c, False),
                              (zero, zero, jnp.zeros((qb, HEAD), F32)))
        for jr in range(ns):
            carry = part(jr * blk, ns * i + jr, *carry, True)
        dq_ref[...] = (carry[2] * scale).astype(dq_ref.dtype)

    width = n_heads * HEAD
    qblk = pl.BlockSpec((qb, HEAD), lambda h, i: (i, h))
    full = pl.BlockSpec((t, HEAD), lambda h, i: (0, h))
    tri = pl.BlockSpec((blk, blk), lambda h, i: (0, 0))
    return _hosted_call(
        body, (n_heads, t // qb),
        [qblk,
         pl.BlockSpec((t, HEAD), lambda h, i: (0, n_heads + h)),
         pl.BlockSpec((t, HEAD), lambda h, i: (0, 2 * n_heads + h)),
         qblk, qblk, tri, tri],
        [qblk, full, full], [jax.ShapeDtypeStruct((t, width), BF16)] + [jax.ShapeDtypeStruct((t, width), F32)] * 2,
        [], (proj, proj, proj, tot, do, after_tri, before_tri), ("parallel", "arbitrary"), name, comm)


def _lower_bound(logits):
    l0, l1 = logits[0:1, :], logits[1:2, :]
    mx = jnp.maximum(l0, l1)
    e0, e1 = jnp.exp(l0 - mx), jnp.exp(l1 - mx)
    return e0 / (e0 + e1)


def _hg_decay(q, k, g, sums):
    cum, mid, last = [_dot_split(m, g) for m in sums]
    return cum, mid, last, q * jnp.exp(cum - mid), k * jnp.exp(mid - cum), q * jnp.exp(cum), k * jnp.exp(last - cum)


def _hg_sums(rows, chunk):
    t = lax.broadcasted_iota(jnp.int32, (rows, rows), 0)
    j = lax.broadcasted_iota(jnp.int32, (rows, rows), 1)
    same = (t // chunk) == (j // chunk)
    mats = [same & (j <= t), same & (j % chunk < chunk // 2), same, same & (j >= t)]
    return [m.astype(BF16) for m in mats]


def _hgrn_fwd(proj, lb_logits, norm_w, sums, n_heads, heads_per_step, name):
    t = proj.shape[0]
    bt = min(HG_ROWS, t)
    c = HG_CHUNK
    nc = bt // c
    hw = heads_per_step * HEAD
    width = n_heads * HEAD
    col0 = 3 * width // hw

    def body(hq_ref, hf_ref, hi_ref, hgate_ref, lbl_ref, w_ref, s0_ref, s1_ref, s2_ref, o_ref, mix_ref, st_ref,
             state, qt_scr, kt_scr, qe_scr, kd_scr, el_scr):
        @pl.when(pl.program_id(1) == 0)
        def _():
            state[...] = jnp.zeros_like(state)

        lb = _lower_bound(lbl_ref[...])
        f = hf_ref[...]
        hq = hq_ref[...]
        _, _, last, qt, kt, qe, kd = _hg_decay(hq * _sigmoid(hq), (1.0 - lb) * _sigmoid(-f),
                                               jnp.log(lb + (1.0 - lb) * _sigmoid(f)),
                                               (s0_ref[...], s1_ref[...], s2_ref[...]))
        qt_scr[...] = qt.astype(BF16)
        kt_scr[...] = kt.astype(BF16)
        qe_scr[...] = qe.astype(BF16)
        kd_scr[...] = kd.astype(BF16)
        el_scr[...] = jnp.exp(last)
        causal = lax.broadcasted_iota(jnp.int32, (c, c), 1) <= lax.broadcasted_iota(jnp.int32, (c, c), 0)

        def chunk(ci, carry):
            r = pl.ds(pl.multiple_of(ci * c, c), c)
            vc = hi_ref[r, :].astype(BF16)
            qt, kt, qe, kd = qt_scr[r, :], kt_scr[r, :], qe_scr[r, :], kd_scr[r, :]
            e_last = el_scr[r, :][0:1, :]
            old = [state[h] for h in range(heads_per_step)]
            outs, new = [], []
            for h in range(heads_per_step):
                cs = slice(h * HEAD, (h + 1) * HEAD)
                a = jnp.where(causal, _dot_nt(qt[:, cs], kt[:, cs]), 0.0)
                outs.append(_dot(a.astype(BF16), vc[:, cs]) + _dot_nt(qe[:, cs], old[h].astype(BF16)))
                new.append(old[h] * e_last[:, cs] + _dot_tn(vc[:, cs], kd[:, cs]))
            for h in range(heads_per_step):
                st_ref[ci, :, h * HEAD:(h + 1) * HEAD] = old[h]
                state[h] = new[h]
            o_ref[r, :] = jnp.concatenate(outs, axis=1)
            return carry

        lax.fori_loop(0, nc, chunk, 0, unroll=HG_UNROLL)

        def finish(o, gate):
            return ((o * _rstd(o) * w_ref[...]) * (gate * _sigmoid(gate)),)

        mix_ref[...] = _heads_map(finish, hw, o_ref[...], hgate_ref[...])[0].astype(BF16)

    def col(group):
        return pl.BlockSpec((bt, hw), functools.partial(lambda hp, tb, g: (tb, col0 + g * (width // hw) + hp), g=group))

    blk = pl.BlockSpec((bt, hw), lambda hp, tb: (tb, hp))
    mat = pl.BlockSpec((bt, bt), lambda hp, tb: (0, 0))
    return pl.pallas_call(
        body, name=name,
        grid=(n_heads // heads_per_step, t // bt),
        in_specs=[col(0), col(1), col(2), col(3),
                  pl.BlockSpec((2, hw), lambda hp, tb: (0, hp)),
                  pl.BlockSpec((1, HEAD), lambda hp, tb: (0, 0)), mat, mat, mat],
        out_specs=[blk, blk, pl.BlockSpec((nc, HEAD, hw), lambda hp, tb: (tb, 0, hp))],
        out_shape=[jax.ShapeDtypeStruct((t, width), F32), jax.ShapeDtypeStruct((t, width), BF16),
                   jax.ShapeDtypeStruct((t // c, HEAD, width), F32)],
        scratch_shapes=[pltpu.VMEM((heads_per_step, HEAD, HEAD), F32)] + [pltpu.VMEM((bt, hw), BF16)] * 4
                       + [pltpu.VMEM((bt, hw), F32)],
        compiler_params=_params(("parallel", "arbitrary")),
    )(proj, proj, proj, proj, lb_logits, norm_w, *sums[:3])


def _hgrn_bwd(proj, do, states, lb_logits, sums, n_heads, heads_per_step, name):
    t = proj.shape[0]
    bt = min(HG_ROWS, t)
    c = HG_CHUNK
    nc = bt // c
    nb = t // bt
    hw = heads_per_step * HEAD
    width = n_heads * HEAD
    col0 = 3 * width // hw

    def body(hq_ref, hf_ref, hi_ref, do_ref, st_ref, lbl_ref, s0_ref, s1_ref, s2_ref, s3_ref,
             dq_ref, df_ref, di_ref, dlb_ref,
             dstate, qth, qtl, kth, ktl, qe_scr, kd_scr, el_scr, qa_scr, ka_scr, qb_scr, kb_scr, ss_scr):
        @pl.when(pl.program_id(1) == 0)
        def _():
            dstate[...] = jnp.zeros_like(dstate)
            dlb_ref[...] = jnp.zeros_like(dlb_ref)

        lb = _lower_bound(lbl_ref[...])
        f = hf_ref[...]
        sg = _sigmoid(f)
        sgn = _sigmoid(-f)
        den = lb + (1.0 - lb) * sg
        kk = (1.0 - lb) * sgn
        hq = hq_ref[...]
        sq = _sigmoid(hq)
        qq = hq * sq
        cum, mid, last, qt, kt, qe, kd = _hg_decay(qq, kk, jnp.log(den), (s0_ref[...], s1_ref[...], s2_ref[...]))
        qth[...], qtl[...] = _hilo(qt)
        kth[...], ktl[...] = _hilo(kt)
        qe_scr[...] = qe.astype(BF16)
        kd_scr[...] = kd.astype(BF16)
        e_last = jnp.exp(last)
        el_scr[...] = e_last
        causal = lax.broadcasted_iota(jnp.int32, (c, c), 1) <= lax.broadcasted_iota(jnp.int32, (c, c), 0)

        def chunk(cc, carry):
            ci = nc - 1 - cc
            r = pl.ds(pl.multiple_of(ci * c, c), c)
            qt = (qth[r, :], qtl[r, :])
            kt = (kth[r, :], ktl[r, :])
            qe, kd = qe_scr[r, :], kd_scr[r, :]
            doc, vc = do_ref[r, :].astype(BF16), hi_ref[r, :].astype(BF16)
            e_row = el_scr[r, :][0:1, :]
            sts = [st_ref[ci, :, h * HEAD:(h + 1) * HEAD] for h in range(heads_per_step)]
            dsts = [dstate[h] for h in range(heads_per_step)]
            di, dq_inter, dk_inter, dq_intra, dk_intra, st_sums, new = [], [], [], [], [], [], []
            for h in range(heads_per_step):
                cs = slice(h * HEAD, (h + 1) * HEAD)

                def head(pair):
                    return pair[0][:, cs], pair[1][:, cs]

                st, dst = sts[h].astype(BF16), dsts[h].astype(BF16)
                a = jnp.where(causal, _dot_nt(qt[0][:, cs], kt[0][:, cs]), 0.0).astype(BF16)
                da = _hilo(jnp.where(causal, _dot_nt(doc[:, cs], vc[:, cs]), 0.0))
                di.append(_dot_tn(a, doc[:, cs]) + _dot_nt(kd[:, cs], dst))
                dq_inter.append(_dot(doc[:, cs], st))
                dk_inter.append(_dot(vc[:, cs], dst))
                dq_intra.append(_dot3(_dot, da, head(kt)))
                dk_intra.append(_dot3(_dot_tn, da, head(qt)))
                st_sums.append(_colsum(dsts[h] * sts[h]))
                new.append(dsts[h] * e_row[:, cs] + _dot_tn(doc[:, cs], qe[:, cs]))

            def wide(parts):
                return jnp.concatenate(parts, axis=1)

            for h in range(heads_per_step):
                dstate[h] = new[h]
            di_ref[r, :] = wide(di).astype(BF16)
            qa_scr[r, :] = wide(dq_intra)
            ka_scr[r, :] = wide(dk_intra)
            qb_scr[r, :] = wide(dq_inter)
            kb_scr[r, :] = wide(dk_inter)
            ss_scr[r, :] = jnp.broadcast_to(wide(st_sums), (c, hw))
            return carry

        lax.fori_loop(0, nc, chunk, 0, unroll=HG_UNROLL)

        dk_inter = kb_scr[...] * jnp.exp(last - cum)
        dq = qa_scr[...] * jnp.exp(cum - mid) + qb_scr[...] * jnp.exp(cum)
        dk = ka_scr[...] * jnp.exp(mid - cum) + dk_inter
        is_last = lax.broadcasted_iota(jnp.int32, (bt, hw), 0) % c == c - 1
        d_last = _dot_split(s2_ref[...], kk * dk_inter) + e_last * ss_scr[...]
        dcum = qq * dq - kk * dk + jnp.where(is_last, d_last, 0.0)
        e = (_dot_split(s3_ref[...], dcum) / den - dk) * sgn
        df_ref[...] = (e * (1.0 - lb) * sg).astype(BF16)
        dlb_ref[...] += _colsum(e)
        dq_ref[...] = (dq * (sq * (1.0 + hq * (1.0 - sq)))).astype(BF16)

    def col(group):
        return pl.BlockSpec((bt, hw), functools.partial(
            lambda hp, tb, g: (nb - 1 - tb, col0 + g * (width // hw) + hp), g=group))

    blk = pl.BlockSpec((bt, hw), lambda hp, tb: (nb - 1 - tb, hp))
    mat = pl.BlockSpec((bt, bt), lambda hp, tb: (0, 0))
    return pl.pallas_call(
        body, name=name,
        grid=(n_heads // heads_per_step, nb),
        in_specs=[col(0), col(1), col(2), blk,
                  pl.BlockSpec((nc, HEAD, hw), lambda hp, tb: (nb - 1 - tb, 0, hp)),
                  pl.BlockSpec((2, hw), lambda hp, tb: (0, hp)), mat, mat, mat, mat],
        out_specs=[blk, blk, blk, pl.BlockSpec((1, hw), lambda hp, tb: (0, hp))],
        out_shape=[jax.ShapeDtypeStruct((t, width), BF16)] * 3 + [jax.ShapeDtypeStruct((1, width), F32)],
        scratch_shapes=[pltpu.VMEM((heads_per_step, HEAD, HEAD), F32)] + [pltpu.VMEM((bt, hw), BF16)] * 6
                       + [pltpu.VMEM((bt, hw), F32)] * 6,
        compiler_params=_params(("parallel", "arbitrary")),
    )(proj, proj, proj, do, states, lb_logits, *sums)


def _place():
    x, y, c = lax.axis_index("x"), lax.axis_index("y"), lax.axis_index("c")
    chips = [(1 - x, y), (x, 1 - y), (1 - x, 1 - y)]
    return x, y, c, chips


ANY = pl.BlockSpec(memory_space=pl.ANY)


def _cast_to_slot(shard, ids, name):
    r, c = shard.shape
    tm = _pick(r, (256, 128, 64, 32, 16))

    def body(ids_ref, s_ref, o_ref):
        o_ref[...] = s_ref[...].astype(BF16)

    return pl.pallas_call(
        body, name=name,
        grid_spec=pltpu.PrefetchScalarGridSpec(
            num_scalar_prefetch=1, grid=(r // tm,),
            in_specs=[pl.BlockSpec((tm, c), lambda i, ids: (i, 0))],
            out_specs=pl.BlockSpec((None, tm, c), lambda i, ids: (ids[1], i, 0))),
        out_shape=jax.ShapeDtypeStruct((N_CHIPS, r, c), BF16),
        compiler_params=_params(("parallel",)),
    )(ids, shard)


def _gather_copies(bufs, send, recv):
    x, y, c, chips = _place()
    mine = 2 * x + y

    def half(ref, who, core):
        h = ref.shape[-2] // 2
        return ref.at[who, pl.ds(core * h, h), :]

    def copy(w, k, rows, to):
        return pltpu.make_async_remote_copy(src_ref=rows, dst_ref=rows, send_sem=send.at[6 * w + k],
                                            recv_sem=recv.at[6 * w + k], device_id=to, device_id_type=MESH)

    def to_chips(w):
        return [copy(w, k, half(bufs[w], mine, c), (qx, qy, c)) for k, (qx, qy) in enumerate(chips)]

    def to_sibling(w):
        return [copy(w, 3 + k, half(bufs[w], 2 * qx + qy, c), (x, y, 1 - c)) for k, (qx, qy) in enumerate(chips)]

    def begin():
        for w in range(len(bufs)):
            for cp in to_chips(w):
                cp.start()

    def end():
        for w in range(len(bufs)):
            for k, (qx, qy) in enumerate(chips):
                copy(w, k, half(bufs[w], 2 * qx + qy, c), (x, y, c)).wait_recv()
                to_sibling(w)[k].start()
        for w in range(len(bufs)):
            for k, (qx, qy) in enumerate(chips):
                copy(w, 3 + k, half(bufs[w], 2 * qx + qy, 1 - c), (x, y, c)).wait_recv()
        for w in range(len(bufs)):
            for cp in to_chips(w) + to_sibling(w):
                cp.wait_send()

    return begin, end


def _scatter_copies(ins, outs, send, recv):
    _, _, c, chips = _place()

    def copies():
        return [pltpu.make_async_remote_copy(
            src_ref=ins[w].at[2 * qx + qy], dst_ref=outs[w].at[k], send_sem=send.at[3 * w + k],
            recv_sem=recv.at[3 * w + k], device_id=(qx, qy, c), device_id_type=MESH)
            for w in range(len(ins)) for k, (qx, qy) in enumerate(chips)]

    def begin():
        for cp in copies():
            cp.start()

    def end():
        for cp in copies():
            cp.wait()

    return begin, end


def _gather_weights(slots, name):
    n = len(slots)

    def body(*refs):
        begin, end = _gather_copies(refs[n:2 * n], *refs[2 * n:])
        begin()
        end()

    return pl.pallas_call(
        body, name=name,
        in_specs=[ANY] * n, out_specs=[ANY] * n,
        out_shape=[jax.ShapeDtypeStruct(s.shape, s.dtype) for s in slots],
        input_output_aliases={w: w for w in range(n)},
        scratch_shapes=[pltpu.SemaphoreType.DMA((6 * n,)), pltpu.SemaphoreType.DMA((6 * n,))],
    )(*slots)


def _swap_copies(ins, outs, send, recv):
    x, y, c, _ = _place()

    def copies():
        return [pltpu.make_async_remote_copy(
            src_ref=ins[w].at[:, pl.ds((1 - c) * (ins[w].shape[1] // 2), ins[w].shape[1] // 2), :], dst_ref=outs[w],
            send_sem=send.at[w], recv_sem=recv.at[w], device_id=(x, y, 1 - c), device_id_type=MESH)
            for w in range(len(ins))]

    def begin():
        for cp in copies():
            cp.start()

    def end():
        for cp in copies():
            cp.wait()

    return begin, end


def _sibling_swap(grads, name):
    n = len(grads)

    def body(*refs):
        begin, end = _swap_copies(refs[:n], refs[n:2 * n], *refs[2 * n:])
        begin()
        end()

    return pl.pallas_call(
        body, name=name, in_specs=[ANY] * n, out_specs=[ANY] * n,
        out_shape=[jax.ShapeDtypeStruct((g.shape[0], g.shape[1] // 2, g.shape[2]), g.dtype) for g in grads],
        scratch_shapes=[pltpu.SemaphoreType.DMA((n,)), pltpu.SemaphoreType.DMA((n,))],
    )(*grads)


def _sibling_join(fulls, name):
    n = len(fulls)

    def body(*refs):
        bufs = refs[n:2 * n]
        send, recv = refs[2 * n:]
        x, y, c, _ = _place()
        cps = []
        for w in range(n):
            h = bufs[w].shape[0] // 2
            rows = bufs[w].at[pl.ds(c * h, h), :]
            cps.append(pltpu.make_async_remote_copy(
                src_ref=rows, dst_ref=rows, send_sem=send.at[w], recv_sem=recv.at[w],
                device_id=(x, y, 1 - c), device_id_type=MESH))
            cps[-1].start()
        for w in range(n):
            h = bufs[w].shape[0] // 2
            theirs = bufs[w].at[pl.ds((1 - c) * h, h), :]
            pltpu.make_async_remote_copy(src_ref=theirs, dst_ref=theirs, send_sem=send.at[w], recv_sem=recv.at[w],
                                         device_id=(x, y, c), device_id_type=MESH).wait_recv()
        for cp in cps:
            cp.wait_send()

    return pl.pallas_call(
        body, name=name, in_specs=[ANY] * n, out_specs=[ANY] * n,
        out_shape=[jax.ShapeDtypeStruct(s.shape, s.dtype) for s in fulls],
        input_output_aliases={w: w for w in range(n)},
        scratch_shapes=[pltpu.SemaphoreType.DMA((n,)), pltpu.SemaphoreType.DMA((n,))],
    )(*fulls)


def _all_sum_small(vec, name):
    n = vec.shape[1]

    def body(v_ref, out_ref, buf, send, recv):
        x, y, c, _ = _place()
        me = 4 * x + 2 * y + c
        buf[me] = v_ref[...]
        peers = []
        for mask in range(1, 8):
            px = 1 - x if mask & 4 else x
            py = 1 - y if mask & 2 else y
            pc = 1 - c if mask & 1 else c
            peers.append((px, py, pc))
        cps = []
        for k, peer in enumerate(peers):
            cps.append(pltpu.make_async_remote_copy(src_ref=buf.at[me], dst_ref=buf.at[me], send_sem=send.at[k],
                                                    recv_sem=recv.at[k], device_id=peer, device_id_type=MESH))
            cps[-1].start()
        for k, (px, py, pc) in enumerate(peers):
            slot = buf.at[4 * px + 2 * py + pc]
            pltpu.make_async_remote_copy(src_ref=slot, dst_ref=slot, send_sem=send.at[k], recv_sem=recv.at[k],
                                         device_id=(x, y, c), device_id_type=MESH).wait_recv()
        for cp in cps:
            cp.wait_send()
        total = buf[0]
        for d in range(1, 8):
            total = total + buf[d]
        out_ref[...] = total

    vm = pl.BlockSpec(memory_space=pltpu.VMEM)
    return pl.pallas_call(
        body, name=name, in_specs=[vm], out_specs=vm,
        out_shape=jax.ShapeDtypeStruct(vec.shape, F32),
        scratch_shapes=[pltpu.VMEM((8, 8, n), F32), pltpu.SemaphoreType.DMA((7,)), pltpu.SemaphoreType.DMA((7,))],
    )(vec)


def _pair_sum(g, buf, ids, name):
    p, r, c = g.shape
    h = r // 2
    tr = _pick(h, (256, 128, 64, 32, 16))
    nh = h // tr

    def body(ids_ref, g_ref, b_ref, sums_ref, own_ref):
        s = g_ref[...] + b_ref[...]
        sums_ref[...] = s.astype(BF16)

        @pl.when(pl.program_id(1) == ids_ref[1])
        def _():
            own_ref[...] = s

    return pl.pallas_call(
        body, name=name,
        grid_spec=pltpu.PrefetchScalarGridSpec(
            num_scalar_prefetch=1, grid=(nh, p),
            in_specs=[pl.BlockSpec((None, tr, c), lambda i, q, ids: (q, ids[0] * nh + i, 0)),
                      pl.BlockSpec((None, tr, c), lambda i, q, ids: (q, i, 0))],
            out_specs=[pl.BlockSpec((None, tr, c), lambda i, q, ids: (q, i, 0)),
                       pl.BlockSpec((tr, c), lambda i, q, ids: (i, 0))]),
        out_shape=[jax.ShapeDtypeStruct((p, h, c), BF16), jax.ShapeDtypeStruct((h, c), F32)],
        compiler_params=_params(("parallel", "arbitrary")),
    )(ids, g, buf)


def _final_sum(own, others, ids, name):
    h, c = own.shape
    tr = _pick(h, (256, 128, 64, 32, 16))
    nh = h // tr

    def body(ids_ref, own_ref, oth_ref, out_ref):
        s = own_ref[...]
        for k in range(3):
            s = s + oth_ref[k].astype(F32)
        out_ref[...] = s

    return pl.pallas_call(
        body, name=name,
        grid_spec=pltpu.PrefetchScalarGridSpec(
            num_scalar_prefetch=1, grid=(nh,),
            in_specs=[pl.BlockSpec((tr, c), lambda i, ids: (i, 0)),
                      pl.BlockSpec((3, tr, c), lambda i, ids: (0, i, 0))],
            out_specs=pl.BlockSpec((tr, c), lambda i, ids: (ids[0] * nh + i, 0))),
        out_shape=jax.ShapeDtypeStruct((2 * h, c), F32),
        compiler_params=_params(("parallel",)),
    )(ids, own, others)


def _adamw(w, g, m, v, name):
    r, c = w.shape
    tm = _pick(r, (256, 128, 64, 32, 16, 8)) if r >= 8 else r

    def fn(rows, _):
        w_, g_, m_, v_ = rows
        m2 = ADAM_B1 * m_ + (1.0 - ADAM_B1) * g_
        v2 = ADAM_B2 * v_ + (1.0 - ADAM_B2) * (g_ * g_)
        m_hat = m2 / (1.0 - ADAM_B1 ** ADAM_STEP)
        v_hat = v2 / (1.0 - ADAM_B2 ** ADAM_STEP)
        delta = -ADAM_LR * (m_hat / (jnp.sqrt(v_hat) + ADAM_EPS) + ADAM_WD * w_)
        return [delta, m2, v2], []

    outs, _ = _rows(fn, [w, g, m, v], [], [(c, F32)] * 3, [], tm=tm, name=name)
    return outs


def kernel(x, attn_norm_w, w_in, lb_logits, sb_norm_w, hg_norm_w, w_out, mlp_norm_w, w_up, w_down, final_norm_w, loss_target, m_attn_norm_w, m_w_in, m_lb_logits, m_sb_norm_w, m_hg_norm_w, m_w_out, m_mlp_norm_w, m_w_up, m_w_down, m_final_norm_w, v_attn_norm_w, v_w_in, v_lb_logits, v_sb_norm_w, v_hg_norm_w, v_w_out, v_mlp_norm_w, v_w_up, v_w_down, v_final_norm_w):
    xs, tgt = x[0], loss_target[0]
    t, d = xs.shape
    width = d // 2
    n_heads = width // HEAD
    hps = min(8, n_heads)
    final_w = final_norm_w.reshape(1, d)
    tm_rows = _pick(t, (256, 128))
    tm = _pick(t, (1024, 512, 256))
    blk = min(ATTN_BLOCK, t)
    ones_a = jnp.ones((blk, blk), F32)
    after_tri = jnp.tril(ones_a, -1).astype(BF16)
    before_tri = jnp.triu(ones_a, 1).astype(BF16)
    hg_sums = _hg_sums(min(HG_ROWS, t), HG_CHUNK)
    cx, cy, cc = lax.axis_index("x"), lax.axis_index("y"), lax.axis_index("c")
    ids = jnp.stack([cc, 2 * cx + cy]).astype(jnp.int32)

    shards = [w_in[0], w_out[0], w_up[0], w_down[0]]
    cast = [_cast_to_slot(s, ids, f"cast_w{i}") for i, s in enumerate(shards)]
    (g_in,) = _gather_weights(cast[:1], "gather_w_in")
    d_ff = N_CHIPS * w_up.shape[2]
    cs_in, cs_up = g_in.shape[2], w_up.shape[2]
    tn_in = _pick(cs_in, (1792, 896, 512, 256, 128))
    tn_up = _pick(cs_up, (1024, 512, 256))
    tn_d = _pick(d, (1024, 512, 256))
    tk_d = _pick(d, (2048, 1024, 512))

    (u,), _ = _rows(lambda r, c_: ([r[0] * _rstd(r[0]) * c_[0]], []), [xs], [attn_norm_w], [(d, BF16)], [],
                    tm=tm_rows, name="norm_in")
    proj, g_out, g_up = _mm_nn(u, g_in, [F32], tm=_pick(t, (512, 256)), tn=tn_in, tk=tk_d, name="proj_in",
                               comm=("gather", cast[1:3]))
    o_a, mix_a, sb_tot, g_down = _attn_fwd(proj, after_tri, sb_norm_w, n_heads, "sb_fwd", comm=("gather", cast[3:]))
    w_out_all = g_out.reshape(1, d, d)
    w_down_all = g_down.reshape(1, d_ff, d)
    o_b, mix_b, states = _hgrn_fwd(proj, lb_logits, hg_norm_w, hg_sums, n_heads, hps, "hg_fwd")
    mix = jnp.concatenate([mix_a, mix_b], axis=1)
    def out_and_norm(acc, res, w):
        hh = acc + res
        return hh, hh * _rstd(hh) * w

    h1, mn = _mm_nn(mix, w_out_all, [F32, BF16], tm=_pick(t, (512, 256)), tn=d, tk=tk_d, name="proj_out",
                    epi=out_and_norm, extras=(xs, mlp_norm_w))
    up_b, act = _mm_nn(mn, g_up, [BF16, BF16], tm=tm, tn=tn_up, tk=tk_d, name="mlp_up",
                       epi=lambda acc: (acc, jnp.square(jnp.maximum(acc, 0.0))))
    (h2,) = _mm_nn(act, w_down_all, [F32], tm=tm, tn=tn_d, tk=_pick(d_ff, (2048, 1024)), name="mlp_down",
                   epi=lambda acc, res: (acc + res,), extras=(h1,))

    def head(rows, consts):
        hh, tg = rows
        w = consts[0]
        n = hh * _rstd(hh)
        err = n * w - tg
        dhh, dw_rows = _rms_bwd(hh, w, err * (1.0 / d))
        return [dhh, dhh], [_colsum(dw_rows), _colsum(err * err)]

    (dh2, dh2_b), (g_final, loss_cols) = _rows(head, [h2, tgt], [final_w], [(d, F32), (d, BF16)], [d, d],
                                                 tm=tm_rows, name="loss_head")

    (dup,) = _mm_nt(dh2_b, w_down_all, [BF16], tm=tm, tn=_pick(d_ff, (1024, 512)), tk=tk_d, name="mlp_down_dx",
                    epi=lambda acc, upv: (acc * (2.0 * jnp.maximum(upv.astype(F32), 0.0)),), extras=(up_b,))
    gw_down = _mm_tn(act, dh2_b, 1, tm=_pick(d_ff, (1024, 512)), tn=tn_d, tk=_pick(t, (2048, 1024, 512, 256)),
                     name="mlp_down_dw")
    gw_down = gw_down.reshape(N_CHIPS, d_ff // N_CHIPS, d)
    dmn, their_down = _mm_nt(dup, g_up, [F32], tm=tm, tn=tn_d, tk=_pick(cs_up, (2048, 1024, 512)), name="mlp_up_dx",
                             comm=("swap", [gw_down]))
    gw_up = _mm_tn(mn, dup, N_CHIPS, tm=tn_d, tn=tn_up, tk=_pick(t, (2048, 1024, 512, 256)), name="mlp_up_dw")

    def norm_back(rows, consts):
        xx, dy, skip = rows
        dx, dw_rows = _rms_bwd(xx, consts[0], dy)
        tot = dx + skip
        return [tot, tot], [_colsum(dw_rows)]

    (dh1, dh1_b), (g_mlp_norm,) = _rows(norm_back, [h1, dmn, dh2], [mlp_norm_w], [(d, F32), (d, BF16)], [d],
                                         tm=tm_rows, name="norm_mlp_bwd")

    gw_out = _mm_tn(mix, dh1_b, 1, tm=tn_d, tn=tn_d, tk=_pick(t, (2048, 1024, 512, 256)), name="proj_out_dw")
    gw_out = gw_out.reshape(N_CHIPS, d // N_CHIPS, d)
    dmix, their_up, their_out = _mm_nt(dh1_b, w_out_all, [F32], tm=tm, tn=tn_d, tk=tk_d, name="proj_out_dx",
                                       comm=("swap", [gw_up, gw_out]))
    pair_mlp = [_pair_sum(g, b, ids, "grads_pair_sum_" + nm)
                for g, b, nm in ((gw_up, their_up, "up"), (gw_down, their_down, "down"))]

    def sb_norm_back(rows, consts):
        dx, dw_rows = _heads_map(lambda o, dy: _rms_bwd(o, consts[0], dy), width, *rows)
        dw = sum(_colsum(dw_rows[:, h * HEAD:(h + 1) * HEAD]) for h in range(n_heads))
        return [dx], [dw]

    (do_a,), (g_sb_norm,) = _rows(sb_norm_back, [o_a, (dmix, width, 0)], [sb_norm_w], [(width, BF16)], [HEAD],
                                  tm=tm_rows, name="sb_norm_bwd")
    dq_a, dk_a, dv_a, *landed_mlp = _attn_bwd(proj, sb_tot, do_a, after_tri, before_tri, n_heads, "sb_bwd",
                                              comm=("scatter", [p[0] for p in pair_mlp]))

    def hg_out_back(rows, consts):
        def one(o, gate, dy):
            sg = _sigmoid(gate)
            silu = gate * sg
            n = o * _rstd(o) * consts[0]
            do, dw_rows = _rms_bwd(o, consts[0], dy * silu)
            return do, dy * n * (sg * (1.0 + gate * (1.0 - sg))), dw_rows
        do, dgate, dw_rows = _heads_map(one, width, *rows)
        dw = sum(_colsum(dw_rows[:, h * HEAD:(h + 1) * HEAD]) for h in range(n_heads))
        return [do, dgate], [dw]

    (do_b, dgate), (g_hg_norm,) = _rows(hg_out_back, [o_b, (proj, width, 6), (dmix, width, 1)], [hg_norm_w],
                                         [(width, BF16)] * 2, [HEAD], tm=tm_rows, name="hg_out_bwd")
    dhq, dhf, dhi, dlb = _hgrn_bwd(proj, do_b, states, lb_logits, hg_sums, n_heads, hps, "hg_bwd")

    (dproj,), _ = _rows(lambda r, _c: ([jnp.concatenate([p.astype(BF16) for p in r], axis=1)], []),
                        [dq_a, dk_a, dv_a, dhq, dhf, dhi, dgate], [], [(7 * width, BF16)], [],
                        tm=tm_rows, name="pack_dproj")
    gw_in = _mm_tn(u, dproj, N_CHIPS, tm=tn_d, tn=tn_in, tk=_pick(t, (2048, 1024, 512, 256)), name="proj_in_dw")
    (their_in,) = _sibling_swap([gw_in], "grads_to_sibling_in")
    pair_mix = [_pair_sum(g, b, ids, "grads_pair_sum_" + nm)
                for g, b, nm in ((gw_in, their_in, "in"), (gw_out, their_out, "out"))]
    du, *landed_mix = _mm_nt(dproj, g_in, [F32], tm=tm, tn=tn_d, tk=_pick(cs_in, (1792, 896, 512, 256, 128)),
                             name="proj_in_dx", comm=("scatter", [p[0] for p in pair_mix]))
    (dx,), (g_attn_norm,) = _rows(lambda r, c_: (lambda dxx, dwr: ([dxx + r[2]], [_colsum(dwr)]))(
        *_rms_bwd(r[0], c_[0], r[1])), [xs, du, dh1], [attn_norm_w], [(d, F32)], [d], tm=tm_rows, name="norm_in_bwd")

    halves = [_final_sum(p[1], r, ids, f"grads_final_sum{i}")
              for i, (p, r) in enumerate(zip(pair_mix + pair_mlp, list(landed_mix) + list(landed_mlp)))]
    g_w_in, g_w_out, g_w_up, g_w_down = _sibling_join(halves, "grads_join")

    pieces = [g_attn_norm, g_mlp_norm, g_final, g_sb_norm, g_hg_norm, dlb, loss_cols]
    sizes = [p.shape[1] for p in pieces]
    flat = jnp.concatenate(pieces, axis=1)
    n_small = -(-flat.shape[1] // 1024) * 1024
    flat = jnp.pad(flat, ((0, 0), (0, n_small - flat.shape[1]))).reshape(8, n_small // 8)
    flat = _all_sum_small(flat, "small_all_sum").reshape(1, n_small)
    offs = [sum(sizes[:i]) for i in range(len(sizes))]
    g_attn_norm, g_mlp_norm, g_final, g_sb_norm, g_hg_norm, dlb, loss_cols = [
        flat[:, o:o + s] for o, s in zip(offs, sizes)]

    def small_tail(lbl_ref, dlb_ref, loss_ref, glb_ref, out_ref):
        lb = _lower_bound(lbl_ref[...])
        g0 = dlb_ref[...] * lb * (1.0 - lb)
        glb_ref[0:1, :] = g0
        glb_ref[1:2, :] = -g0
        out_ref[...] = jnp.zeros_like(out_ref) + 0.5 * jnp.sum(loss_ref[...]) * (1.0 / d)

    vm = pl.BlockSpec(memory_space=pltpu.VMEM)
    g_lb, loss11 = pl.pallas_call(
        small_tail, name="small_tail", in_specs=[vm, vm, vm], out_specs=[vm, vm],
        out_shape=[jax.ShapeDtypeStruct(lb_logits.shape, F32), jax.ShapeDtypeStruct((1, 128), F32)],
    )(lb_logits, dlb, loss_cols)
    loss = loss11[0, 0]

    names = ["attn_norm_w", "w_in", "lb_logits", "sb_norm_w", "hg_norm_w", "w_out", "mlp_norm_w", "w_up", "w_down",
             "final_norm_w"]
    ws = [attn_norm_w, w_in[0], lb_logits, sb_norm_w, hg_norm_w, w_out[0], mlp_norm_w, w_up[0], w_down[0], final_w]
    gs = [g_attn_norm, g_w_in, g_lb, g_sb_norm, g_hg_norm, g_w_out, g_mlp_norm, g_w_up, g_w_down, g_final]
    ms = [m_attn_norm_w, m_w_in[0], m_lb_logits, m_sb_norm_w, m_hg_norm_w, m_w_out[0], m_mlp_norm_w, m_w_up[0],
          m_w_down[0], m_final_norm_w.reshape(1, d)]
    vs = [v_attn_norm_w, v_w_in[0], v_lb_logits, v_sb_norm_w, v_hg_norm_w, v_w_out[0], v_mlp_norm_w, v_w_up[0],
          v_w_down[0], v_final_norm_w.reshape(1, d)]
    shapes = [attn_norm_w.shape, w_in.shape, lb_logits.shape, sb_norm_w.shape, hg_norm_w.shape, w_out.shape,
              mlp_norm_w.shape, w_up.shape, w_down.shape, final_norm_w.shape]
    deltas, new_ms, new_vs = [], [], []
    for nm, w_, g_, m_, v_ in zip(names, ws, gs, ms, vs):
        dl, m2, v2 = _adamw(w_, g_, m_, v_, "adamw_" + nm)
        deltas.append(dl)
        new_ms.append(m2)
        new_vs.append(v2)

    def shaped(lst):
        return [a.reshape(s) for a, s in zip(lst, shapes)]

    return (loss, dx[None], *shaped(gs), *shaped(deltas), *shaped(new_ms), *shaped(new_vs))
```

```python
import functools

import jax
import jax.numpy as jnp
from jax import lax
from jax.experimental import pallas as pl
from jax.experimental.pallas import tpu as pltpu

F32 = jnp.float32
BF16 = jnp.bfloat16
MESH = pl.DeviceIdType.MESH

HEAD = 128
NORM_EPS = 1e-5
N_CHIPS = 4
ATTN_BLOCK = 256
ATTN_ROWS = 1024
ATTN_DEAD = -110.0
HG_CHUNK = 32
HG_ROWS = 256
HG_UNROLL = 2
VMEM_LIMIT = 56 * 1024 * 1024

ADAM_LR = 0.001
ADAM_B1 = 0.9
ADAM_B2 = 0.999
ADAM_EPS = 1e-08
ADAM_WD = 0.01
ADAM_STEP = 10


def _pick(n, cands):
    for c in cands:
        if n % c == 0:
            return c
    return n


def _params(sem):
    return pltpu.CompilerParams(dimension_semantics=sem, vmem_limit_bytes=VMEM_LIMIT)


def _dot(a, b):
    return jnp.dot(a, b, preferred_element_type=F32)


def _dot_nt(a, b):
    return lax.dot_general(a, b, (((1,), (1,)), ((), ())), preferred_element_type=F32)


def _dot_tn(a, b):
    return lax.dot_general(a, b, (((0,), (0,)), ((), ())), preferred_element_type=F32)


def _hilo(x):
    hi = x.astype(BF16)
    return hi, (x - hi.astype(F32)).astype(BF16)


def _dot_split(tri, x):
    hi, lo = _hilo(x)
    return _dot(tri, hi) + _dot(tri, lo)


def _dot3(dot, a, b):
    return dot(a[0], b[0]) + (dot(a[0], b[1]) + dot(a[1], b[0]))


def _sigmoid(x):
    return 1.0 / (1.0 + jnp.exp(-x))


def _hosted_call(inner, grid, in_specs, out_specs, out_shape, scratch, args, semantics, name, comm=None):
    kind, arrays = comm if comm else (None, ())
    n_i, n_o, n_s, n_c = len(in_specs), len(out_specs), len(scratch), len(arrays)

    def body(*refs):
        c_in = refs[n_i:n_i + n_c]
        c_out = refs[n_i + n_c + n_o:n_i + 2 * n_c + n_o]
        scr = refs[n_i + 2 * n_c + n_o:]
        if n_c:
            ids = [pl.program_id(ax) for ax in range(len(grid))]
            first, last = ids[0] == 0, ids[0] == grid[0] - 1
            for ax in range(1, len(grid)):
                first, last = first & (ids[ax] == 0), last & (ids[ax] == grid[ax] - 1)
            sems = scr[n_s:]
            copies = {"scatter": _scatter_copies, "swap": _swap_copies}
            begin, end = _gather_copies(c_out, *sems) if kind == "gather" else copies[kind](c_in, c_out, *sems)
            pl.when(first)(begin)
        inner(*refs[:n_i], *refs[n_i + n_c:n_i + n_c + n_o], *scr[:n_s])
        if n_c:
            pl.when(last)(end)

    gather = kind == "gather"
    shape = {"gather": lambda a: a.shape, "scatter": lambda a: (3,) + a.shape[1:],
             "swap": lambda a: (a.shape[0], a.shape[1] // 2, a.shape[2])}
    landed = [jax.ShapeDtypeStruct(shape[kind](a), a.dtype) for a in arrays]
    return pl.pallas_call(
        body, name=name, grid=grid,
        in_specs=list(in_specs) + [ANY] * n_c, out_specs=list(out_specs) + [ANY] * n_c,
        out_shape=list(out_shape) + landed,
        input_output_aliases={n_i + w: n_o + w for w in range(n_c)} if gather else {},
        scratch_shapes=list(scratch) + ([pltpu.SemaphoreType.DMA(
            ({"gather": 6, "scatter": 3, "swap": 1}[kind] * n_c,))] * 2 if n_c else []),
        compiler_params=_params(("arbitrary",) * len(grid) if n_c else semantics),
    )(*args, *arrays)


def _mm_body(kind, nk, n_extra, n_out, epi):
    dot = {"nn": _dot, "nt": _dot_nt, "tn": _dot_tn}[kind]

    def finish(acc, extra_refs, out_refs):
        res = epi(acc, *[e[...] for e in extra_refs]) if epi is not None else (acc,)
        for o, r in zip(out_refs, res):
            o[...] = r.astype(o.dtype)

    def body(a_ref, b_ref, *rest):
        extra_refs = rest[:n_extra]
        out_refs = rest[n_extra:n_extra + n_out]
        if nk == 1:
            finish(dot(a_ref[...], b_ref[...]), extra_refs, out_refs)
            return
        acc_ref = rest[n_extra + n_out]
        k = pl.program_id(2)

        @pl.when(k == 0)
        def _():
            acc_ref[...] = jnp.zeros_like(acc_ref)

        acc_ref[...] += dot(a_ref[...], b_ref[...])

        @pl.when(k == nk - 1)
        def _():
            finish(acc_ref[...], extra_refs, out_refs)

    return body


def _mm_nn(a, w, out_dtypes, *, tm, tn, tk, name, epi=None, extras=(), comm=None):
    m, r = a.shape
    p, _, c = w.shape
    npc = c // tn
    nk = r // tk
    tile = pl.BlockSpec((tm, tn), lambda i, j, k: (i, j))
    return _hosted_call(
        _mm_body("nn", nk, len(extras), len(out_dtypes), epi), (m // tm, p * npc, nk),
        [pl.BlockSpec((tm, tk), lambda i, j, k: (i, k)),
         pl.BlockSpec((None, tk, tn), lambda i, j, k: (j // npc, k, j % npc))]
        + [tile if e.shape[0] > 1 else pl.BlockSpec((1, tn), lambda i, j, k: (0, j)) for e in extras],
        [tile] * len(out_dtypes), [jax.ShapeDtypeStruct((m, p * c), d) for d in out_dtypes],
        [pltpu.VMEM((tm, tn), F32)] if nk > 1 else [], (a, w, *extras),
        ("parallel", "parallel", "arbitrary"), name, comm)


def _mm_nt(a, w, out_dtypes, *, tm, tn, tk, name, epi=None, extras=(), comm=None):
    m, _ = a.shape
    p, r, c = w.shape
    kpc = c // tk
    nk = p * kpc
    tile = pl.BlockSpec((tm, tn), lambda i, j, k: (i, j))
    return _hosted_call(
        _mm_body("nt", nk, len(extras), len(out_dtypes), epi), (m // tm, r // tn, nk),
        [pl.BlockSpec((tm, tk), lambda i, j, k: (i, k)),
         pl.BlockSpec((None, tn, tk), lambda i, j, k: (k // kpc, j, k % kpc))] + [tile] * len(extras),
        [tile] * len(out_dtypes), [jax.ShapeDtypeStruct((m, r), d) for d in out_dtypes],
        [pltpu.VMEM((tm, tn), F32)] if nk > 1 else [], (a, w, *extras),
        ("parallel", "parallel", "arbitrary"), name, comm)


def _mm_tn(a, g, p, *, tm, tn, tk, name):
    t, r = a.shape
    c = g.shape[1] // p
    npc = c // tn
    nk = t // tk
    body = _mm_body("tn", nk, 0, 1, None)
    return pl.pallas_call(
        body, name=name,
        grid=(r // tm, p * npc, nk),
        in_specs=[pl.BlockSpec((tk, tm), lambda i, j, k: (k, i)),
                  pl.BlockSpec((tk, tn), lambda i, j, k: (k, j))],
        out_specs=[pl.BlockSpec((None, tm, tn), lambda i, j, k: (j // npc, i, j % npc))],
        out_shape=[jax.ShapeDtypeStruct((p, r, c), F32)],
        scratch_shapes=[pltpu.VMEM((tm, tn), F32)] if nk > 1 else [],
        compiler_params=_params(("parallel", "parallel", "arbitrary")),
    )(a, g)[0]


def _rows(fn, row_ins, const_ins, row_outs, acc_outs, *, tm, name, comm=None):
    specs, arrays = [], []
    t = None
    for item in row_ins:
        if isinstance(item, tuple):
            arr, width, cb = item
            specs.append(pl.BlockSpec((tm, width), functools.partial(lambda i, cb: (i, cb), cb=cb)))
        else:
            arr = item
            specs.append(pl.BlockSpec((tm, arr.shape[1]), lambda i: (i, 0)))
        arrays.append(arr)
        t = arr.shape[0]
    for arr in const_ins:
        specs.append(pl.BlockSpec(arr.shape, lambda i: (0, 0)))
        arrays.append(arr)
    n_in, n_row, n_acc = len(arrays), len(row_outs), len(acc_outs)

    def body(*refs):
        ins = [r[...] for r in refs[:n_in]]
        outs = refs[n_in:]
        row_res, acc_res = fn(ins[:len(row_ins)], ins[len(row_ins):])
        for o, r in zip(outs[:n_row], row_res):
            o[...] = r.astype(o.dtype)
        if n_acc:
            i = pl.program_id(0)

            @pl.when(i == 0)
            def _():
                for o in outs[n_row:]:
                    o[...] = jnp.zeros_like(o)

            for o, r in zip(outs[n_row:], acc_res):
                o[...] += r

    res = _hosted_call(
        body, (t // tm,), specs,
        [pl.BlockSpec((tm, c), lambda i: (i, 0)) for c, _ in row_outs]
        + [pl.BlockSpec((1, c), lambda i: (0, 0)) for c in acc_outs],
        [jax.ShapeDtypeStruct((t, c), d) for c, d in row_outs] + [jax.ShapeDtypeStruct((1, c), F32) for c in acc_outs],
        [], arrays, ("arbitrary",), name, comm)
    if comm:
        return res[:n_row], res[n_row:n_row + n_acc], res[n_row + n_acc:]
    return res[:n_row], res[n_row:]


def _rstd(x):
    return lax.rsqrt(jnp.mean(x * x, axis=-1, keepdims=True) + NORM_EPS)


def _rms_bwd(x, w, dy):
    r = _rstd(x)
    n = x * r
    dn = dy * w
    dx = r * (dn - n * jnp.mean(dn * n, axis=-1, keepdims=True))
    return dx, dy * n


def _colsum(x):
    return jnp.sum(x, axis=0, keepdims=True)


def _heads_map(fn, width, *tiles):
    outs = None
    for h in range(width // HEAD):
        res = fn(*[t[:, h * HEAD:(h + 1) * HEAD] for t in tiles])
        if outs is None:
            outs = [[] for _ in res]
        for lst, r in zip(outs, res):
            lst.append(r)
    return [jnp.concatenate(lst, axis=1) for lst in outs]


def _log_one_minus_beta(z):
    return -(jnp.maximum(z, 0.0) + jnp.log(1.0 + jnp.exp(-jnp.abs(z))))


def _attn_fwd(proj, after_tri, norm_w, n_heads, name, comm=None):
    t = proj.shape[0]
    blk = min(ATTN_BLOCK, t)
    qb = min(ATTN_ROWS, t)
    ns = qb // blk
    scale = HEAD ** -0.5

    def body(q_ref, k_ref, v_ref, tri_ref, w_ref, o_ref, mix_ref, tot_ref):
        i = pl.program_id(1)
        q = (q_ref[...] * scale).astype(BF16)
        tri = tri_ref[...]

        def part(r0, j, acc_l, acc_o, masked):
            sl = pl.ds(pl.multiple_of(j * blk, blk), blk)
            m = qb - r0
            z = _dot_nt(q[r0:, :], k_ref[sl, :].astype(BF16))
            lm = _log_one_minus_beta(z)
            if masked:
                mask = lax.broadcasted_iota(jnp.int32, (m, blk), 1) < lax.broadcasted_iota(jnp.int32, (m, blk), 0)
                lmm = jnp.where(mask, lm, 0.0)
            else:
                lmm = lm
            w = jnp.exp(z + lm + acc_l[r0:, :] + _dot(lmm.astype(BF16), tri))
            if masked:
                w = jnp.where(mask, w, 0.0)
            new_l = acc_l[r0:, :] + jnp.sum(lmm, axis=1, keepdims=True)
            new_o = acc_o[r0:, :] + _dot(w.astype(BF16), v_ref[sl, :].astype(BF16))
            if r0:
                new_l = jnp.concatenate([acc_l[:r0, :], new_l], axis=0)
                new_o = jnp.concatenate([acc_o[:r0, :], new_o], axis=0)
            return new_l, new_o

        acc = (jnp.zeros((qb, 1), F32), jnp.zeros((qb, HEAD), F32))
        for jr in reversed(range(ns)):
            acc = part(jr * blk, ns * i + jr, *acc, True)

        def more(c):
            return (c[0] < ns * i) & (jnp.max(c[1]) > ATTN_DEAD)

        def step(c):
            return (c[0] + 1,) + part(0, ns * i - 1 - c[0], c[1], c[2], False)

        swept, acc_l, acc_o = lax.while_loop(more, step, (jnp.int32(0),) + acc)
        o_ref[...] = acc_o
        mix_ref[...] = (acc_o * _rstd(acc_o) * w_ref[...]).astype(BF16)
        first = (ns * i - swept).astype(F32)
        tot_ref[...] = jnp.where(lax.broadcasted_iota(jnp.int32, (qb, HEAD), 1) == 1, first, acc_l)

    width = n_heads * HEAD
    qblk = pl.BlockSpec((qb, HEAD), lambda h, i: (i, h))
    return _hosted_call(
        body, (n_heads, t // qb),
        [qblk,
         pl.BlockSpec((t, HEAD), lambda h, i: (0, n_heads + h)),
         pl.BlockSpec((t, HEAD), lambda h, i: (0, 2 * n_heads + h)),
         pl.BlockSpec((blk, blk), lambda h, i: (0, 0)),
         pl.BlockSpec((1, HEAD), lambda h, i: (0, 0))],
        [qblk, qblk, qblk],
        [jax.ShapeDtypeStruct((t, width), F32), jax.ShapeDtypeStruct((t, width), BF16),
         jax.ShapeDtypeStruct((t, width), F32)],
        [], (proj, proj, proj, after_tri, norm_w), ("parallel", "arbitrary"), name, comm)


def _attn_bwd(proj, tot, do, after_tri, before_tri, n_heads, name, comm=None):
    t = proj.shape[0]
    blk = min(ATTN_BLOCK, t)
    qb = min(ATTN_ROWS, t)
    ns = qb // blk
    scale = HEAD ** -0.5

    def body(q_ref, k_ref, v_ref, tot_ref, do_ref, after_ref, before_ref, dq_ref, dk_ref, dv_ref):
        i = pl.program_id(1)

        @pl.when(i == 0)
        def _():
            dk_ref[...] = jnp.zeros_like(dk_ref)
            dv_ref[...] = jnp.zeros_like(dv_ref)

        q = (q_ref[...] * scale).astype(BF16)
        dob = do_ref[...].astype(BF16)
        total = tot_ref[:, 0:1]
        after_tri = after_ref[...]
        before = before_ref[...]

        def part(r0, j, seen_l, seen_g, dq, masked):
            sl = pl.ds(pl.multiple_of(j * blk, blk), blk)
            m = qb - r0
            qq, dd = q[r0:, :], dob[r0:, :]
            kb = k_ref[sl, :].astype(BF16)
            z = _dot_nt(qq, kb)
            lm = _log_one_minus_beta(z)
            if masked:
                mask = lax.broadcasted_iota(jnp.int32, (m, blk), 1) < lax.broadcasted_iota(jnp.int32, (m, blk), 0)
                lmm = jnp.where(mask, lm, 0.0)
            else:
                lmm = lm
            row_l = jnp.sum(lmm, axis=1, keepdims=True)
            after = (total[r0:, :] - seen_l[r0:, :] - row_l) + _dot(lmm.astype(BF16), after_tri)
            w = jnp.exp(z + lm + after)
            if masked:
                w = jnp.where(mask, w, 0.0)
            sig = jnp.exp(z + lm)
            g = w * _dot_nt(dd, v_ref[sl, :].astype(BF16))
            g_before = seen_g[r0:, :] + _dot(g.astype(BF16), before)
            dz = g * (1.0 - sig) - g_before * sig
            if masked:
                dz = jnp.where(mask, dz, 0.0)
            dzb = dz.astype(BF16)
            dk_ref[sl, :] += _dot_tn(dzb, qq)
            dv_ref[sl, :] += _dot_tn(w.astype(BF16), dd)
            new = (seen_l[r0:, :] + row_l,
                   seen_g[r0:, :] + jnp.sum(g, axis=1, keepdims=True), dq[r0:, :] + _dot(dzb, kb))
            if r0:
                new = tuple(jnp.concatenate([old[:r0, :], n], axis=0) for old, n in zip((seen_l, seen_g, dq), new))
            return new

        zero = jnp.zeros((qb, 1), F32)
        first = jnp.max(tot_ref[0:8, 1:2]).astype(jnp.int32)
        carry = lax.fori_loop(first, ns * i, lambda j, c: part(0, j, *c, False),
                              (zero, zero, jnp.zeros((qb, HEAD), F32)))
        for jr in range(ns):
            carry = part(jr * blk, ns * i + jr, *carry, True)
        dq_ref[...] = (carry[2] * scale).astype(dq_ref.dtype)

    width = n_heads * HEAD
    qblk = pl.BlockSpec((qb, HEAD), lambda h, i: (i, h))
    full = pl.BlockSpec((t, HEAD), lambda h, i: (0, h))
    tri = pl.BlockSpec((blk, blk), lambda h, i: (0, 0))
    return _hosted_call(
        body, (n_heads, t // qb),
        [qblk,
         pl.BlockSpec((t, HEAD), lambda h, i: (0, n_heads + h)),
         pl.BlockSpec((t, HEAD), lambda h, i: (0, 2 * n_heads + h)),
         qblk, qblk, tri, tri],
        [qblk, full, full], [jax.ShapeDtypeStruct((t, width), BF16)] + [jax.ShapeDtypeStruct((t, width), F32)] * 2,
        [], (proj, proj, proj, tot, do, after_tri, before_tri), ("parallel", "arbitrary"), name, comm)


def _lower_bound(logits):
    l0, l1 = logits[0:1, :], logits[1:2, :]
    mx = jnp.maximum(l0, l1)
    e0, e1 = jnp.exp(l0 - mx), jnp.exp(l1 - mx)
    return e0 / (e0 + e1)


def _hg_decay(q, k, g, sums):
    cum, mid, last = [_dot_split(m, g) for m in sums]
    return cum, mid, last, q * jnp.exp(cum - mid), k * jnp.exp(mid - cum), q * jnp.exp(cum), k * jnp.exp(last - cum)


def _hg_sums(rows, chunk):
    t = lax.broadcasted_iota(jnp.int32, (rows, rows), 0)
    j = lax.broadcasted_iota(jnp.int32, (rows, rows), 1)
    same = (t // chunk) == (j // chunk)
    mats = [same & (j <= t), same & (j % chunk < chunk // 2), same, same & (j >= t)]
    return [m.astype(BF16) for m in mats]


def _hgrn_fwd(proj, lb_logits, norm_w, sums, n_heads, heads_per_step, name, comm=None):
    t = proj.shape[0]
    bt = min(HG_ROWS, t)
    c = HG_CHUNK
    nc = bt // c
    hw = heads_per_step * HEAD
    width = n_heads * HEAD
    col0 = 3 * width // hw

    def body(hq_ref, hf_ref, hi_ref, hgate_ref, lbl_ref, w_ref, s0_ref, s1_ref, s2_ref, o_ref, mix_ref, st_ref,
             state, qt_scr, kt_scr, qe_scr, kd_scr, el_scr):
        @pl.when(pl.program_id(1) == 0)
        def _():
            state[...] = jnp.zeros_like(state)

        lb = _lower_bound(lbl_ref[...])
        f = hf_ref[...]
        hq = hq_ref[...]
        _, _, last, qt, kt, qe, kd = _hg_decay(hq * _sigmoid(hq), (1.0 - lb) * _sigmoid(-f),
                                               jnp.log(lb + (1.0 - lb) * _sigmoid(f)),
                                               (s0_ref[...], s1_ref[...], s2_ref[...]))
        qt_scr[...] = qt.astype(BF16)
        kt_scr[...] = kt.astype(BF16)
        qe_scr[...] = qe.astype(BF16)
        kd_scr[...] = kd.astype(BF16)
        el_scr[...] = jnp.exp(last)
        causal = lax.broadcasted_iota(jnp.int32, (c, c), 1) <= lax.broadcasted_iota(jnp.int32, (c, c), 0)

        def chunk(ci, carry):
            r = pl.ds(pl.multiple_of(ci * c, c), c)
            vc = hi_ref[r, :].astype(BF16)
            qt, kt, qe, kd = qt_scr[r, :], kt_scr[r, :], qe_scr[r, :], kd_scr[r, :]
            e_last = el_scr[r, :][0:1, :]
            old = [state[h] for h in range(heads_per_step)]
            outs, new = [], []
            for h in range(heads_per_step):
                cs = slice(h * HEAD, (h + 1) * HEAD)
                a = jnp.where(causal, _dot_nt(qt[:, cs], kt[:, cs]), 0.0)
                outs.append(_dot(a.astype(BF16), vc[:, cs]) + _dot_nt(qe[:, cs], old[h].astype(BF16)))
                new.append(old[h] * e_last[:, cs] + _dot_tn(vc[:, cs], kd[:, cs]))
            for h in range(heads_per_step):
                st_ref[ci, :, h * HEAD:(h + 1) * HEAD] = old[h]
                state[h] = new[h]
            o_ref[r, :] = jnp.concatenate(outs, axis=1)
            return carry

        lax.fori_loop(0, nc, chunk, 0, unroll=HG_UNROLL)

        def finish(o, gate):
            return ((o * _rstd(o) * w_ref[...]) * (gate * _sigmoid(gate)),)

        mix_ref[...] = _heads_map(finish, hw, o_ref[...], hgate_ref[...])[0].astype(BF16)

    def col(group):
        return pl.BlockSpec((bt, hw), functools.partial(lambda hp, tb, g: (tb, col0 + g * (width // hw) + hp), g=group))

    blk = pl.BlockSpec((bt, hw), lambda hp, tb: (tb, hp))
    mat = pl.BlockSpec((bt, bt), lambda hp, tb: (0, 0))
    return _hosted_call(
        body, (n_heads // heads_per_step, t // bt),
        [col(0), col(1), col(2), col(3),
         pl.BlockSpec((2, hw), lambda hp, tb: (0, hp)),
         pl.BlockSpec((1, HEAD), lambda hp, tb: (0, 0)), mat, mat, mat],
        [blk, blk, pl.BlockSpec((nc, HEAD, hw), lambda hp, tb: (tb, 0, hp))],
        [jax.ShapeDtypeStruct((t, width), F32), jax.ShapeDtypeStruct((t, width), BF16),
         jax.ShapeDtypeStruct((t // c, HEAD, width), F32)],
        [pltpu.VMEM((heads_per_step, HEAD, HEAD), F32)] + [pltpu.VMEM((bt, hw), BF16)] * 4
        + [pltpu.VMEM((bt, hw), F32)],
        (proj, proj, proj, proj, lb_logits, norm_w, *sums[:3]), ("parallel", "arbitrary"), name, comm)


def _hgrn_bwd(proj, do, states, lb_logits, sums, n_heads, heads_per_step, name):
    t = proj.shape[0]
    bt = min(HG_ROWS, t)
    c = HG_CHUNK
    nc = bt // c
    nb = t // bt
    hw = heads_per_step * HEAD
    width = n_heads * HEAD
    col0 = 3 * width // hw

    def body(hq_ref, hf_ref, hi_ref, do_ref, st_ref, lbl_ref, s0_ref, s1_ref, s2_ref, s3_ref,
             dq_ref, df_ref, di_ref, dlb_ref,
             dstate, qth, qtl, kth, ktl, qe_scr, kd_scr, el_scr, qa_scr, ka_scr, qb_scr, kb_scr, ss_scr):
        @pl.when(pl.program_id(1) == 0)
        def _():
            dstate[...] = jnp.zeros_like(dstate)
            dlb_ref[...] = jnp.zeros_like(dlb_ref)

        lb = _lower_bound(lbl_ref[...])
        f = hf_ref[...]
        sg = _sigmoid(f)
        sgn = _sigmoid(-f)
        den = lb + (1.0 - lb) * sg
        kk = (1.0 - lb) * sgn
        hq = hq_ref[...]
        sq = _sigmoid(hq)
        qq = hq * sq
        cum, mid, last, qt, kt, qe, kd = _hg_decay(qq, kk, jnp.log(den), (s0_ref[...], s1_ref[...], s2_ref[...]))
        qth[...], qtl[...] = _hilo(qt)
        kth[...], ktl[...] = _hilo(kt)
        qe_scr[...] = qe.astype(BF16)
        kd_scr[...] = kd.astype(BF16)
        e_last = jnp.exp(last)
        el_scr[...] = e_last
        causal = lax.broadcasted_iota(jnp.int32, (c, c), 1) <= lax.broadcasted_iota(jnp.int32, (c, c), 0)

        def chunk(cc, carry):
            ci = nc - 1 - cc
            r = pl.ds(pl.multiple_of(ci * c, c), c)
            qt = (qth[r, :], qtl[r, :])
            kt = (kth[r, :], ktl[r, :])
            qe, kd = qe_scr[r, :], kd_scr[r, :]
            doc, vc = do_ref[r, :].astype(BF16), hi_ref[r, :].astype(BF16)
            e_row = el_scr[r, :][0:1, :]
            sts = [st_ref[ci, :, h * HEAD:(h + 1) * HEAD] for h in range(heads_per_step)]
            dsts = [dstate[h] for h in range(heads_per_step)]
            di, dq_inter, dk_inter, dq_intra, dk_intra, st_sums, new = [], [], [], [], [], [], []
            for h in range(heads_per_step):
                cs = slice(h * HEAD, (h + 1) * HEAD)

                def head(pair):
                    return pair[0][:, cs], pair[1][:, cs]

                st, dst = sts[h].astype(BF16), dsts[h].astype(BF16)
                a = jnp.where(causal, _dot_nt(qt[0][:, cs], kt[0][:, cs]), 0.0).astype(BF16)
                da = _hilo(jnp.where(causal, _dot_nt(doc[:, cs], vc[:, cs]), 0.0))
                di.append(_dot_tn(a, doc[:, cs]) + _dot_nt(kd[:, cs], dst))
                dq_inter.append(_dot(doc[:, cs], st))
                dk_inter.append(_dot(vc[:, cs], dst))
                dq_intra.append(_dot3(_dot, da, head(kt)))
                dk_intra.append(_dot3(_dot_tn, da, head(qt)))
                st_sums.append(_colsum(dsts[h] * sts[h]))
                new.append(dsts[h] * e_row[:, cs] + _dot_tn(doc[:, cs], qe[:, cs]))

            def wide(parts):
                return jnp.concatenate(parts, axis=1)

            for h in range(heads_per_step):
                dstate[h] = new[h]
            di_ref[r, :] = wide(di).astype(BF16)
            qa_scr[r, :] = wide(dq_intra)
            ka_scr[r, :] = wide(dk_intra)
            qb_scr[r, :] = wide(dq_inter)
            kb_scr[r, :] = wide(dk_inter)
            ss_scr[r, :] = jnp.broadcast_to(wide(st_sums), (c, hw))
            return carry

        lax.fori_loop(0, nc, chunk, 0, unroll=HG_UNROLL)

        dk_inter = kb_scr[...] * jnp.exp(last - cum)
        dq = qa_scr[...] * jnp.exp(cum - mid) + qb_scr[...] * jnp.exp(cum)
        dk = ka_scr[...] * jnp.exp(mid - cum) + dk_inter
        is_last = lax.broadcasted_iota(jnp.int32, (bt, hw), 0) % c == c - 1
        d_last = _dot_split(s2_ref[...], kk * dk_inter) + e_last * ss_scr[...]
        dcum = qq * dq - kk * dk + jnp.where(is_last, d_last, 0.0)
        e = (_dot_split(s3_ref[...], dcum) / den - dk) * sgn
        df_ref[...] = (e * (1.0 - lb) * sg).astype(BF16)
        dlb_ref[...] += _colsum(e)
        dq_ref[...] = (dq * (sq * (1.0 + hq * (1.0 - sq)))).astype(BF16)

    def col(group):
        return pl.BlockSpec((bt, hw), functools.partial(
            lambda hp, tb, g: (nb - 1 - tb, col0 + g * (width // hw) + hp), g=group))

    blk = pl.BlockSpec((bt, hw), lambda hp, tb: (nb - 1 - tb, hp))
    mat = pl.BlockSpec((bt, bt), lambda hp, tb: (0, 0))
    return pl.pallas_call(
        body, name=name,
        grid=(n_heads // heads_per_step, nb),
        in_specs=[col(0), col(1), col(2), blk,
                  pl.BlockSpec((nc, HEAD, hw), lambda hp, tb: (nb - 1 - tb, 0, hp)),
                  pl.BlockSpec((2, hw), lambda hp, tb: (0, hp)), mat, mat, mat, mat],
        out_specs=[blk, blk, blk, pl.BlockSpec((1, hw), lambda hp, tb: (0, hp))],
        out_shape=[jax.ShapeDtypeStruct((t, width), BF16)] * 3 + [jax.ShapeDtypeStruct((1, width), F32)],
        scratch_shapes=[pltpu.VMEM((heads_per_step, HEAD, HEAD), F32)] + [pltpu.VMEM((bt, hw), BF16)] * 6
                       + [pltpu.VMEM((bt, hw), F32)] * 6,
        compiler_params=_params(("parallel", "arbitrary")),
    )(proj, proj, proj, do, states, lb_logits, *sums)


def _place():
    x, y, c = lax.axis_index("x"), lax.axis_index("y"), lax.axis_index("c")
    chips = [(1 - x, y), (x, 1 - y), (1 - x, 1 - y)]
    return x, y, c, chips


ANY = pl.BlockSpec(memory_space=pl.ANY)


def _cast_to_slot(shard, ids, name):
    r, c = shard.shape
    tm = _pick(r, (256, 128, 64, 32, 16))

    def body(ids_ref, s_ref, o_ref):
        o_ref[...] = s_ref[...].astype(BF16)

    return pl.pallas_call(
        body, name=name,
        grid_spec=pltpu.PrefetchScalarGridSpec(
            num_scalar_prefetch=1, grid=(r // tm,),
            in_specs=[pl.BlockSpec((tm, c), lambda i, ids: (i, 0))],
            out_specs=pl.BlockSpec((None, tm, c), lambda i, ids: (ids[1], i, 0))),
        out_shape=jax.ShapeDtypeStruct((N_CHIPS, r, c), BF16),
        compiler_params=_params(("parallel",)),
    )(ids, shard)


def _gather_copies(bufs, send, recv):
    x, y, c, chips = _place()
    mine = 2 * x + y

    def half(ref, who, core):
        h = ref.shape[-2] // 2
        return ref.at[who, pl.ds(core * h, h), :]

    def copy(w, k, rows, to):
        return pltpu.make_async_remote_copy(src_ref=rows, dst_ref=rows, send_sem=send.at[6 * w + k],
                                            recv_sem=recv.at[6 * w + k], device_id=to, device_id_type=MESH)

    def to_chips(w):
        return [copy(w, k, half(bufs[w], mine, c), (qx, qy, c)) for k, (qx, qy) in enumerate(chips)]

    def to_sibling(w):
        return [copy(w, 3 + k, half(bufs[w], 2 * qx + qy, c), (x, y, 1 - c)) for k, (qx, qy) in enumerate(chips)]

    def begin():
        for w in range(len(bufs)):
            for cp in to_chips(w):
                cp.start()

    def end():
        for w in range(len(bufs)):
            for k, (qx, qy) in enumerate(chips):
                copy(w, k, half(bufs[w], 2 * qx + qy, c), (x, y, c)).wait_recv()
                to_sibling(w)[k].start()
        for w in range(len(bufs)):
            for k, (qx, qy) in enumerate(chips):
                copy(w, 3 + k, half(bufs[w], 2 * qx + qy, 1 - c), (x, y, c)).wait_recv()
        for w in range(len(bufs)):
            for cp in to_chips(w) + to_sibling(w):
                cp.wait_send()

    return begin, end


def _scatter_copies(ins, outs, send, recv):
    _, _, c, chips = _place()

    def copies():
        return [pltpu.make_async_remote_copy(
            src_ref=ins[w].at[2 * qx + qy], dst_ref=outs[w].at[k], send_sem=send.at[3 * w + k],
            recv_sem=recv.at[3 * w + k], device_id=(qx, qy, c), device_id_type=MESH)
            for w in range(len(ins)) for k, (qx, qy) in enumerate(chips)]

    def begin():
        for cp in copies():
            cp.start()

    def end():
        for cp in copies():
            cp.wait()

    return begin, end


def _swap_copies(ins, outs, send, recv):
    x, y, c, _ = _place()

    def copies():
        return [pltpu.make_async_remote_copy(
            src_ref=ins[w].at[:, pl.ds((1 - c) * (ins[w].shape[1] // 2), ins[w].shape[1] // 2), :], dst_ref=outs[w],
            send_sem=send.at[w], recv_sem=recv.at[w], device_id=(x, y, 1 - c), device_id_type=MESH)
            for w in range(len(ins))]

    def begin():
        for cp in copies():
            cp.start()

    def end():
        for cp in copies():
            cp.wait()

    return begin, end


def _sibling_swap(grads, name):
    n = len(grads)

    def body(*refs):
        begin, end = _swap_copies(refs[:n], refs[n:2 * n], *refs[2 * n:])
        begin()
        end()

    return pl.pallas_call(
        body, name=name, in_specs=[ANY] * n, out_specs=[ANY] * n,
        out_shape=[jax.ShapeDtypeStruct((g.shape[0], g.shape[1] // 2, g.shape[2]), g.dtype) for g in grads],
        scratch_shapes=[pltpu.SemaphoreType.DMA((n,)), pltpu.SemaphoreType.DMA((n,))],
    )(*grads)


def _sibling_join(fulls, name):
    n = len(fulls)

    def body(*refs):
        bufs = refs[n:2 * n]
        send, recv = refs[2 * n:]
        x, y, c, _ = _place()
        cps = []
        for w in range(n):
            h = bufs[w].shape[0] // 2
            rows = bufs[w].at[pl.ds(c * h, h), :]
            cps.append(pltpu.make_async_remote_copy(
                src_ref=rows, dst_ref=rows, send_sem=send.at[w], recv_sem=recv.at[w],
                device_id=(x, y, 1 - c), device_id_type=MESH))
            cps[-1].start()
        for w in range(n):
            h = bufs[w].shape[0] // 2
            theirs = bufs[w].at[pl.ds((1 - c) * h, h), :]
            pltpu.make_async_remote_copy(src_ref=theirs, dst_ref=theirs, send_sem=send.at[w], recv_sem=recv.at[w],
                                         device_id=(x, y, c), device_id_type=MESH).wait_recv()
        for cp in cps:
            cp.wait_send()

    return pl.pallas_call(
        body, name=name, in_specs=[ANY] * n, out_specs=[ANY] * n,
        out_shape=[jax.ShapeDtypeStruct(s.shape, s.dtype) for s in fulls],
        input_output_aliases={w: w for w in range(n)},
        scratch_shapes=[pltpu.SemaphoreType.DMA((n,)), pltpu.SemaphoreType.DMA((n,))],
    )(*fulls)


def _all_sum_small(vec, name):
    n = vec.shape[1]

    def body(v_ref, out_ref, buf, send, recv):
        x, y, c, _ = _place()
        me = 4 * x + 2 * y + c
        buf[me] = v_ref[...]
        peers = []
        for mask in range(1, 8):
            px = 1 - x if mask & 4 else x
            py = 1 - y if mask & 2 else y
            pc = 1 - c if mask & 1 else c
            peers.append((px, py, pc))
        cps = []
        for k, peer in enumerate(peers):
            cps.append(pltpu.make_async_remote_copy(src_ref=buf.at[me], dst_ref=buf.at[me], send_sem=send.at[k],
                                                    recv_sem=recv.at[k], device_id=peer, device_id_type=MESH))
            cps[-1].start()
        for k, (px, py, pc) in enumerate(peers):
            slot = buf.at[4 * px + 2 * py + pc]
            pltpu.make_async_remote_copy(src_ref=slot, dst_ref=slot, send_sem=send.at[k], recv_sem=recv.at[k],
                                         device_id=(x, y, c), device_id_type=MESH).wait_recv()
        for cp in cps:
            cp.wait_send()
        total = buf[0]
        for d in range(1, 8):
            total = total + buf[d]
        out_ref[...] = total

    vm = pl.BlockSpec(memory_space=pltpu.VMEM)
    return pl.pallas_call(
        body, name=name, in_specs=[vm], out_specs=vm,
        out_shape=jax.ShapeDtypeStruct(vec.shape, F32),
        scratch_shapes=[pltpu.VMEM((8, 8, n), F32), pltpu.SemaphoreType.DMA((7,)), pltpu.SemaphoreType.DMA((7,))],
    )(vec)


def _pair_sum(g, buf, ids, name):
    p, r, c = g.shape
    h = r // 2
    tr = _pick(h, (256, 128, 64, 32, 16))
    nh = h // tr

    def body(ids_ref, g_ref, b_ref, sums_ref, own_ref):
        s = g_ref[...] + b_ref[...]
        sums_ref[...] = s.astype(BF16)

        @pl.when(pl.program_id(1) == ids_ref[1])
        def _():
            own_ref[...] = s

    return pl.pallas_call(
        body, name=name,
        grid_spec=pltpu.PrefetchScalarGridSpec(
            num_scalar_prefetch=1, grid=(nh, p),
            in_specs=[pl.BlockSpec((None, tr, c), lambda i, q, ids: (q, ids[0] * nh + i, 0)),
                      pl.BlockSpec((None, tr, c), lambda i, q, ids: (q, i, 0))],
            out_specs=[pl.BlockSpec((None, tr, c), lambda i, q, ids: (q, i, 0)),
                       pl.BlockSpec((tr, c), lambda i, q, ids: (i, 0))]),
        out_shape=[jax.ShapeDtypeStruct((p, h, c), BF16), jax.ShapeDtypeStruct((h, c), F32)],
        compiler_params=_params(("parallel", "arbitrary")),
    )(ids, g, buf)


def _final_sum(own, others, ids, name):
    h, c = own.shape
    tr = _pick(h, (256, 128, 64, 32, 16))
    nh = h // tr

    def body(ids_ref, own_ref, oth_ref, out_ref):
        s = own_ref[...]
        for k in range(3):
            s = s + oth_ref[k].astype(F32)
        out_ref[...] = s

    return pl.pallas_call(
        body, name=name,
        grid_spec=pltpu.PrefetchScalarGridSpec(
            num_scalar_prefetch=1, grid=(nh,),
            in_specs=[pl.BlockSpec((tr, c), lambda i, ids: (i, 0)),
                      pl.BlockSpec((3, tr, c), lambda i, ids: (0, i, 0))],
            out_specs=pl.BlockSpec((tr, c), lambda i, ids: (ids[0] * nh + i, 0))),
        out_shape=jax.ShapeDtypeStruct((2 * h, c), F32),
        compiler_params=_params(("parallel",)),
    )(ids, own, others)


def _adamw(w, g, m, v, name):
    r, c = w.shape
    tm = _pick(r, (256, 128, 64, 32, 16, 8)) if r >= 8 else r

    def fn(rows, _):
        w_, g_, m_, v_ = rows
        m2 = ADAM_B1 * m_ + (1.0 - ADAM_B1) * g_
        v2 = ADAM_B2 * v_ + (1.0 - ADAM_B2) * (g_ * g_)
        m_hat = m2 / (1.0 - ADAM_B1 ** ADAM_STEP)
        v_hat = v2 / (1.0 - ADAM_B2 ** ADAM_STEP)
        delta = -ADAM_LR * (m_hat / (jnp.sqrt(v_hat) + ADAM_EPS) + ADAM_WD * w_)
        return [delta, m2, v2], []

    outs, _ = _rows(fn, [w, g, m, v], [], [(c, F32)] * 3, [], tm=tm, name=name)
    return outs


def kernel(x, attn_norm_w, w_in, lb_logits, sb_norm_w, hg_norm_w, w_out, mlp_norm_w, w_up, w_down, final_norm_w, loss_target, m_attn_norm_w, m_w_in, m_lb_logits, m_sb_norm_w, m_hg_norm_w, m_w_out, m_mlp_norm_w, m_w_up, m_w_down, m_final_norm_w, v_attn_norm_w, v_w_in, v_lb_logits, v_sb_norm_w, v_hg_norm_w, v_w_out, v_mlp_norm_w, v_w_up, v_w_down, v_final_norm_w):
    xs, tgt = x[0], loss_target[0]
    t, d = xs.shape
    width = d // 2
    n_heads = width // HEAD
    hps = min(8, n_heads)
    final_w = final_norm_w.reshape(1, d)
    tm_rows = _pick(t, (256, 128))
    tm = _pick(t, (1024, 512, 256))
    blk = min(ATTN_BLOCK, t)
    ones_a = jnp.ones((blk, blk), F32)
    after_tri = jnp.tril(ones_a, -1).astype(BF16)
    before_tri = jnp.triu(ones_a, 1).astype(BF16)
    hg_sums = _hg_sums(min(HG_ROWS, t), HG_CHUNK)
    cx, cy, cc = lax.axis_index("x"), lax.axis_index("y"), lax.axis_index("c")
    ids = jnp.stack([cc, 2 * cx + cy]).astype(jnp.int32)

    shards = [w_in[0], w_out[0], w_up[0], w_down[0]]
    cast = [_cast_to_slot(s, ids, f"cast_w{i}") for i, s in enumerate(shards)]
    d_ff = N_CHIPS * w_up.shape[2]
    cs_in, cs_up = w_in.shape[2], w_up.shape[2]
    tn_in = _pick(cs_in, (1792, 896, 512, 256, 128))
    tn_up = _pick(cs_up, (1024, 512, 256))
    tn_d = _pick(d, (1024, 512, 256))
    tk_d = _pick(d, (2048, 1024, 512))

    (u,), _, (g_in,) = _rows(lambda r, c_: ([r[0] * _rstd(r[0]) * c_[0]], []), [xs], [attn_norm_w], [(d, BF16)], [],
                             tm=tm_rows, name="norm_in", comm=("gather", cast[:1]))
    proj, g_out = _mm_nn(u, g_in, [F32], tm=_pick(t, (512, 256)), tn=tn_in, tk=tk_d, name="proj_in",
                         comm=("gather", cast[1:2]))
    o_a, mix_a, sb_tot, g_down = _attn_fwd(proj, after_tri, sb_norm_w, n_heads, "sb_fwd", comm=("gather", cast[3:]))
    w_out_all = g_out.reshape(1, d, d)
    w_down_all = g_down.reshape(1, d_ff, d)
    o_b, mix_b, states, g_up = _hgrn_fwd(proj, lb_logits, hg_norm_w, hg_sums, n_heads, hps, "hg_fwd",
                                         comm=("gather", cast[2:3]))
    mix = jnp.concatenate([mix_a, mix_b], axis=1)
    def out_and_norm(acc, res, w):
        hh = acc + res
        return hh, hh * _rstd(hh) * w

    h1, mn = _mm_nn(mix, w_out_all, [F32, BF16], tm=_pick(t, (512, 256)), tn=d, tk=tk_d, name="proj_out",
                    epi=out_and_norm, extras=(xs, mlp_norm_w))
    up_b, act = _mm_nn(mn, g_up, [BF16, BF16], tm=tm, tn=tn_up, tk=tk_d, name="mlp_up",
                       epi=lambda acc: (acc, jnp.square(jnp.maximum(acc, 0.0))))
    (h2,) = _mm_nn(act, w_down_all, [F32], tm=tm, tn=tn_d, tk=_pick(d_ff, (2048, 1024)), name="mlp_down",
                   epi=lambda acc, res: (acc + res,), extras=(h1,))

    def head(rows, consts):
        hh, tg = rows
        w = consts[0]
        n = hh * _rstd(hh)
        err = n * w - tg
        dhh, dw_rows = _rms_bwd(hh, w, err * (1.0 / d))
        return [dhh, dhh], [_colsum(dw_rows), _colsum(err * err)]

    (dh2, dh2_b), (g_final, loss_cols) = _rows(head, [h2, tgt], [final_w], [(d, F32), (d, BF16)], [d, d],
                                                 tm=tm_rows, name="loss_head")

    (dup,) = _mm_nt(dh2_b, w_down_all, [BF16], tm=tm, tn=_pick(d_ff, (1024, 512)), tk=tk_d, name="mlp_down_dx",
                    epi=lambda acc, upv: (acc * (2.0 * jnp.maximum(upv.astype(F32), 0.0)),), extras=(up_b,))
    gw_down = _mm_tn(act, dh2_b, 1, tm=_pick(d_ff, (1024, 512)), tn=tn_d, tk=_pick(t, (2048, 1024, 512, 256)),
                     name="mlp_down_dw")
    gw_down = gw_down.reshape(N_CHIPS, d_ff // N_CHIPS, d)
    dmn, their_down = _mm_nt(dup, g_up, [F32], tm=tm, tn=tn_d, tk=_pick(cs_up, (2048, 1024, 512)), name="mlp_up_dx",
                             comm=("swap", [gw_down]))
    gw_up = _mm_tn(mn, dup, N_CHIPS, tm=tn_d, tn=tn_up, tk=_pick(t, (2048, 1024, 512, 256)), name="mlp_up_dw")

    def norm_back(rows, consts):
        xx, dy, skip = rows
        dx, dw_rows = _rms_bwd(xx, consts[0], dy)
        tot = dx + skip
        return [tot, tot], [_colsum(dw_rows)]

    (dh1, dh1_b), (g_mlp_norm,) = _rows(norm_back, [h1, dmn, dh2], [mlp_norm_w], [(d, F32), (d, BF16)], [d],
                                         tm=tm_rows, name="norm_mlp_bwd")

    gw_out = _mm_tn(mix, dh1_b, 1, tm=tn_d, tn=tn_d, tk=_pick(t, (2048, 1024, 512, 256)), name="proj_out_dw")
    gw_out = gw_out.reshape(N_CHIPS, d // N_CHIPS, d)
    dmix, their_up, their_out = _mm_nt(dh1_b, w_out_all, [F32], tm=tm, tn=tn_d, tk=tk_d, name="proj_out_dx",
                                       comm=("swap", [gw_up, gw_out]))
    pair_mlp = [_pair_sum(g, b, ids, "grads_pair_sum_" + nm)
                for g, b, nm in ((gw_up, their_up, "up"), (gw_down, their_down, "down"))]

    def sb_norm_back(rows, consts):
        dx, dw_rows = _heads_map(lambda o, dy: _rms_bwd(o, consts[0], dy), width, *rows)
        dw = sum(_colsum(dw_rows[:, h * HEAD:(h + 1) * HEAD]) for h in range(n_heads))
        return [dx], [dw]

    (do_a,), (g_sb_norm,) = _rows(sb_norm_back, [o_a, (dmix, width, 0)], [sb_norm_w], [(width, BF16)], [HEAD],
                                  tm=tm_rows, name="sb_norm_bwd")
    dq_a, dk_a, dv_a, *landed_mlp = _attn_bwd(proj, sb_tot, do_a, after_tri, before_tri, n_heads, "sb_bwd",
                                              comm=("scatter", [p[0] for p in pair_mlp]))

    def hg_out_back(rows, consts):
        def one(o, gate, dy):
            sg = _sigmoid(gate)
            silu = gate * sg
            n = o * _rstd(o) * consts[0]
            do, dw_rows = _rms_bwd(o, consts[0], dy * silu)
            return do, dy * n * (sg * (1.0 + gate * (1.0 - sg))), dw_rows
        do, dgate, dw_rows = _heads_map(one, width, *rows)
        dw = sum(_colsum(dw_rows[:, h * HEAD:(h + 1) * HEAD]) for h in range(n_heads))
        return [do, dgate], [dw]

    (do_b, dgate), (g_hg_norm,) = _rows(hg_out_back, [o_b, (proj, width, 6), (dmix, width, 1)], [hg_norm_w],
                                         [(width, BF16)] * 2, [HEAD], tm=tm_rows, name="hg_out_bwd")
    dhq, dhf, dhi, dlb = _hgrn_bwd(proj, do_b, states, lb_logits, hg_sums, n_heads, hps, "hg_bwd")

    (dproj,), _ = _rows(lambda r, _c: ([jnp.concatenate([p.astype(BF16) for p in r], axis=1)], []),
                        [dq_a, dk_a, dv_a, dhq, dhf, dhi, dgate], [], [(7 * width, BF16)], [],
                        tm=tm_rows, name="pack_dproj")
    gw_in = _mm_tn(u, dproj, N_CHIPS, tm=tn_d, tn=tn_in, tk=_pick(t, (2048, 1024, 512, 256)), name="proj_in_dw")
    (their_in,) = _sibling_swap([gw_in], "grads_to_sibling_in")
    pair_mix = [_pair_sum(g, b, ids, "grads_pair_sum_" + nm)
                for g, b, nm in ((gw_in, their_in, "in"), (gw_out, their_out, "out"))]
    du, *landed_mix = _mm_nt(dproj, g_in, [F32], tm=tm, tn=tn_d, tk=_pick(cs_in, (1792, 896, 512, 256, 128)),
                             name="proj_in_dx", comm=("scatter", [p[0] for p in pair_mix]))
    (dx,), (g_attn_norm,) = _rows(lambda r, c_: (lambda dxx, dwr: ([dxx + r[2]], [_colsum(dwr)]))(
        *_rms_bwd(r[0], c_[0], r[1])), [xs, du, dh1], [attn_norm_w], [(d, F32)], [d], tm=tm_rows, name="norm_in_bwd")

    halves = [_final_sum(p[1], r, ids, f"grads_final_sum{i}")
              for i, (p, r) in enumerate(zip(pair_mix + pair_mlp, list(landed_mix) + list(landed_mlp)))]
    g_w_in, g_w_out, g_w_up, g_w_down = _sibling_join(halves, "grads_join")

    pieces = [g_attn_norm, g_mlp_norm, g_final, g_sb_norm, g_hg_norm, dlb, loss_cols]
    sizes = [p.shape[1] for p in pieces]
    flat = jnp.concatenate(pieces, axis=1)
    n_small = -(-flat.shape[1] // 1024) * 1024
    flat = jnp.pad(flat, ((0, 0), (0, n_small - flat.shape[1]))).reshape(8, n_small // 8)
    flat = _all_sum_small(flat, "small_all_sum").reshape(1, n_small)
    offs = [sum(sizes[:i]) for i in range(len(sizes))]
    g_attn_norm, g_mlp_norm, g_final, g_sb_norm, g_hg_norm, dlb, loss_cols = [
        flat[:, o:o + s] for o, s in zip(offs, sizes)]

    def small_tail(lbl_ref, dlb_ref, loss_ref, glb_ref, out_ref):
        lb = _lower_bound(lbl_ref[...])
        g0 = dlb_ref[...] * lb * (1.0 - lb)
        glb_ref[0:1, :] = g0
        glb_ref[1:2, :] = -g0
        out_ref[...] = jnp.zeros_like(out_ref) + 0.5 * jnp.sum(loss_ref[...]) * (1.0 / d)

    vm = pl.BlockSpec(memory_space=pltpu.VMEM)
    g_lb, loss11 = pl.pallas_call(
        small_tail, name="small_tail", in_specs=[vm, vm, vm], out_specs=[vm, vm],
        out_shape=[jax.ShapeDtypeStruct(lb_logits.shape, F32), jax.ShapeDtypeStruct((1, 128), F32)],
    )(lb_logits, dlb, loss_cols)
    loss = loss11[0, 0]

    names = ["attn_norm_w", "w_in", "lb_logits", "sb_norm_w", "hg_norm_w", "w_out", "mlp_norm_w", "w_up", "w_down",
             "final_norm_w"]
    ws = [attn_norm_w, w_in[0], lb_logits, sb_norm_w, hg_norm_w, w_out[0], mlp_norm_w, w_up[0], w_down[0], final_w]
    gs = [g_attn_norm, g_w_in, g_lb, g_sb_norm, g_hg_norm, g_w_out, g_mlp_norm, g_w_up, g_w_down, g_final]
    ms = [m_attn_norm_w, m_w_in[0], m_lb_logits, m_sb_norm_w, m_hg_norm_w, m_w_out[0], m_mlp_norm_w, m_w_up[0],
          m_w_down[0], m_final_norm_w.reshape(1, d)]
    vs = [v_attn_norm_w, v_w_in[0], v_lb_logits, v_sb_norm_w, v_hg_norm_w, v_w_out[0], v_mlp_norm_w, v_w_up[0],
          v_w_down[0], v_final_norm_w.reshape(1, d)]
    shapes = [attn_norm_w.shape, w_in.shape, lb_logits.shape, sb_norm_w.shape, hg_norm_w.shape, w_out.shape,
              mlp_norm_w.shape, w_up.shape, w_down.shape, final_norm_w.shape]
    deltas, new_ms, new_vs = [], [], []
    for nm, w_, g_, m_, v_ in zip(names, ws, gs, ms, vs):
        dl, m2, v2 = _adamw(w_, g_, m_, v_, "adamw_" + nm)
        deltas.append(dl)
        new_ms.append(m2)
        new_vs.append(v2)

    def shaped(lst):
        return [a.reshape(s) for a, s in zip(lst, shapes)]

    return (loss, dx[None], *shaped(gs), *shaped(deltas), *shaped(new_ms), *shaped(new_vs))
```

```python
import functools

import jax
import jax.numpy as jnp
from jax import lax
from jax.experimental import pallas as pl
from jax.experimental.pallas import tpu as pltpu

F32 = jnp.float32
BF16 = jnp.bfloat16
MESH = pl.DeviceIdType.MESH

HEAD = 128
NORM_EPS = 1e-5
N_CHIPS = 4
ATTN_BLOCK = 256
ATTN_ROWS = 1024
ATTN_DEAD = -110.0
HG_CHUNK = 32
HG_ROWS = 256
HG_UNROLL = 2
VMEM_LIMIT = 56 * 1024 * 1024

ADAM_LR = 0.001
ADAM_B1 = 0.9
ADAM_B2 = 0.999
ADAM_EPS = 1e-08
ADAM_WD = 0.01
ADAM_STEP = 10


def _pick(n, cands):
    for c in cands:
        if n % c == 0:
            return c
    return n


def _params(sem):
    return pltpu.CompilerParams(dimension_semantics=sem, vmem_limit_bytes=VMEM_LIMIT)


def _dot(a, b):
    return jnp.dot(a, b, preferred_element_type=F32)


def _dot_nt(a, b):
    return lax.dot_general(a, b, (((1,), (1,)), ((), ())), preferred_element_type=F32)


def _dot_tn(a, b):
    return lax.dot_general(a, b, (((0,), (0,)), ((), ())), preferred_element_type=F32)


def _hilo(x):
    hi = x.astype(BF16)
    return hi, (x - hi.astype(F32)).astype(BF16)


def _dot_split(tri, x):
    hi, lo = _hilo(x)
    return _dot(tri, hi) + _dot(tri, lo)


def _dot3(dot, a, b):
    return dot(a[0], b[0]) + (dot(a[0], b[1]) + dot(a[1], b[0]))


def _sigmoid(x):
    return 1.0 / (1.0 + jnp.exp(-x))


def _hosted_call(inner, grid, in_specs, out_specs, out_shape, scratch, args, semantics, name, comm=None):
    kind, arrays = comm if comm else (None, ())
    n_i, n_o, n_s, n_c = len(in_specs), len(out_specs), len(scratch), len(arrays)

    def body(*refs):
        c_in = refs[n_i:n_i + n_c]
        c_out = refs[n_i + n_c + n_o:n_i + 2 * n_c + n_o]
        scr = refs[n_i + 2 * n_c + n_o:]
        if n_c:
            ids = [pl.program_id(ax) for ax in range(len(grid))]
            first, last = ids[0] == 0, ids[0] == grid[0] - 1
            for ax in range(1, len(grid)):
                first, last = first & (ids[ax] == 0), last & (ids[ax] == grid[ax] - 1)
            sems = scr[n_s:]
            copies = {"scatter": _scatter_copies, "swap": _swap_copies}
            begin, end = _gather_copies(c_out, *sems) if kind == "gather" else copies[kind](c_in, c_out, *sems)
            pl.when(first)(begin)
        inner(*refs[:n_i], *refs[n_i + n_c:n_i + n_c + n_o], *scr[:n_s])
        if n_c:
            pl.when(last)(end)

    gather = kind == "gather"
    shape = {"gather": lambda a: a.shape, "scatter": lambda a: (3,) + a.shape[1:],
             "swap": lambda a: (a.shape[0], a.shape[1] // 2, a.shape[2])}
    landed = [jax.ShapeDtypeStruct(shape[kind](a), a.dtype) for a in arrays]
    return pl.pallas_call(
        body, name=name, grid=grid,
        in_specs=list(in_specs) + [ANY] * n_c, out_specs=list(out_specs) + [ANY] * n_c,
        out_shape=list(out_shape) + landed,
        input_output_aliases={n_i + w: n_o + w for w in range(n_c)} if gather else {},
        scratch_shapes=list(scratch) + ([pltpu.SemaphoreType.DMA(
            ({"gather": 6, "scatter": 3, "swap": 1}[kind] * n_c,))] * 2 if n_c else []),
        compiler_params=_params(("arbitrary",) * len(grid) if n_c else semantics),
    )(*args, *arrays)


def _mm_body(kind, nk, n_extra, n_out, epi):
    dot = {"nn": _dot, "nt": _dot_nt, "tn": _dot_tn}[kind]

    def finish(acc, extra_refs, out_refs):
        res = epi(acc, *[e[...] for e in extra_refs]) if epi is not None else (acc,)
        for o, r in zip(out_refs, res):
            o[...] = r.astype(o.dtype)

    def body(a_ref, b_ref, *rest):
        extra_refs = rest[:n_extra]
        out_refs = rest[n_extra:n_extra + n_out]
        if nk == 1:
            finish(dot(a_ref[...], b_ref[...]), extra_refs, out_refs)
            return
        acc_ref = rest[n_extra + n_out]
        k = pl.program_id(2)

        @pl.when(k == 0)
        def _():
            acc_ref[...] = jnp.zeros_like(acc_ref)

        acc_ref[...] += dot(a_ref[...], b_ref[...])

        @pl.when(k == nk - 1)
        def _():
            finish(acc_ref[...], extra_refs, out_refs)

    return body


def _mm_nn(a, w, out_dtypes, *, tm, tn, tk, name, epi=None, extras=(), comm=None):
    m, r = a.shape
    p, _, c = w.shape
    npc = c // tn
    nk = r // tk
    tile = pl.BlockSpec((tm, tn), lambda i, j, k: (i, j))
    return _hosted_call(
        _mm_body("nn", nk, len(extras), len(out_dtypes), epi), (m // tm, p * npc, nk),
        [pl.BlockSpec((tm, tk), lambda i, j, k: (i, k)),
         pl.BlockSpec((None, tk, tn), lambda i, j, k: (j // npc, k, j % npc))]
        + [tile if e.shape[0] > 1 else pl.BlockSpec((1, tn), lambda i, j, k: (0, j)) for e in extras],
        [tile] * len(out_dtypes), [jax.ShapeDtypeStruct((m, p * c), d) for d in out_dtypes],
        [pltpu.VMEM((tm, tn), F32)] if nk > 1 else [], (a, w, *extras),
        ("parallel", "parallel", "arbitrary"), name, comm)


def _mm_nt(a, w, out_dtypes, *, tm, tn, tk, name, epi=None, extras=(), comm=None):
    m, _ = a.shape
    p, r, c = w.shape
    kpc = c // tk
    nk = p * kpc
    tile = pl.BlockSpec((tm, tn), lambda i, j, k: (i, j))
    return _hosted_call(
        _mm_body("nt", nk, len(extras), len(out_dtypes), epi), (m // tm, r // tn, nk),
        [pl.BlockSpec((tm, tk), lambda i, j, k: (i, k)),
         pl.BlockSpec((None, tn, tk), lambda i, j, k: (k // kpc, j, k % kpc))] + [tile] * len(extras),
        [tile] * len(out_dtypes), [jax.ShapeDtypeStruct((m, r), d) for d in out_dtypes],
        [pltpu.VMEM((tm, tn), F32)] if nk > 1 else [], (a, w, *extras),
        ("parallel", "parallel", "arbitrary"), name, comm)


def _mm_tn(a, g, p, *, tm, tn, tk, name):
    t, r = a.shape
    c = g.shape[1] // p
    npc = c // tn
    nk = t // tk
    body = _mm_body("tn", nk, 0, 1, None)
    return pl.pallas_call(
        body, name=name,
        grid=(r // tm, p * npc, nk),
        in_specs=[pl.BlockSpec((tk, tm), lambda i, j, k: (k, i)),
                  pl.BlockSpec((tk, tn), lambda i, j, k: (k, j))],
        out_specs=[pl.BlockSpec((None, tm, tn), lambda i, j, k: (j // npc, i, j % npc))],
        out_shape=[jax.ShapeDtypeStruct((p, r, c), F32)],
        scratch_shapes=[pltpu.VMEM((tm, tn), F32)] if nk > 1 else [],
        compiler_params=_params(("parallel", "parallel", "arbitrary")),
    )(a, g)[0]


def _rows(fn, row_ins, const_ins, row_outs, acc_outs, *, tm, name, comm=None):
    specs, arrays = [], []
    t = None
    for item in row_ins:
        if isinstance(item, tuple):
            arr, width, cb = item
            specs.append(pl.BlockSpec((tm, width), functools.partial(lambda i, cb: (i, cb), cb=cb)))
        else:
            arr = item
            specs.append(pl.BlockSpec((tm, arr.shape[1]), lambda i: (i, 0)))
        arrays.append(arr)
        t = arr.shape[0]
    for arr in const_ins:
        specs.append(pl.BlockSpec(arr.shape, lambda i: (0, 0)))
        arrays.append(arr)
    n_in, n_row, n_acc = len(arrays), len(row_outs), len(acc_outs)

    def body(*refs):
        ins = [r[...] for r in refs[:n_in]]
        outs = refs[n_in:]
        row_res, acc_res = fn(ins[:len(row_ins)], ins[len(row_ins):])
        for o, r in zip(outs[:n_row], row_res):
            o[...] = r.astype(o.dtype)
        if n_acc:
            i = pl.program_id(0)

            @pl.when(i == 0)
            def _():
                for o in outs[n_row:]:
                    o[...] = jnp.zeros_like(o)

            for o, r in zip(outs[n_row:], acc_res):
                o[...] += r

    res = _hosted_call(
        body, (t // tm,), specs,
        [pl.BlockSpec((tm, c), lambda i: (i, 0)) for c, _ in row_outs]
        + [pl.BlockSpec((1, c), lambda i: (0, 0)) for c in acc_outs],
        [jax.ShapeDtypeStruct((t, c), d) for c, d in row_outs] + [jax.ShapeDtypeStruct((1, c), F32) for c in acc_outs],
        [], arrays, ("arbitrary",), name, comm)
    if comm:
        return res[:n_row], res[n_row:n_row + n_acc], res[n_row + n_acc:]
    return res[:n_row], res[n_row:]


def _rstd(x):
    return lax.rsqrt(jnp.mean(x * x, axis=-1, keepdims=True) + NORM_EPS)


def _rms_bwd(x, w, dy):
    r = _rstd(x)
    n = x * r
    dn = dy * w
    dx = r * (dn - n * jnp.mean(dn * n, axis=-1, keepdims=True))
    return dx, dy * n


def _colsum(x):
    return jnp.sum(x, axis=0, keepdims=True)


def _heads_map(fn, width, *tiles):
    outs = None
    for h in range(width // HEAD):
        res = fn(*[t[:, h * HEAD:(h + 1) * HEAD] for t in tiles])
        if outs is None:
            outs = [[] for _ in res]
        for lst, r in zip(outs, res):
            lst.append(r)
    return [jnp.concatenate(lst, axis=1) for lst in outs]


def _log_one_minus_beta(z):
    return -(jnp.maximum(z, 0.0) + jnp.log(1.0 + jnp.exp(-jnp.abs(z))))


def _attn_fwd(proj, after_tri, norm_w, n_heads, name, comm=None):
    t = proj.shape[0]
    blk = min(ATTN_BLOCK, t)
    qb = min(ATTN_ROWS, t)
    ns = qb // blk
    scale = HEAD ** -0.5

    def body(q_ref, k_ref, v_ref, tri_ref, w_ref, o_ref, mix_ref, tot_ref):
        i = pl.program_id(1)
        q = (q_ref[...] * scale).astype(BF16)
        tri = tri_ref[...]

        def part(r0, j, acc_l, acc_o, masked):
            sl = pl.ds(pl.multiple_of(j * blk, blk), blk)
            m = qb - r0
            z = _dot_nt(q[r0:, :], k_ref[sl, :].astype(BF16))
            lm = _log_one_minus_beta(z)
            if masked:
                mask = lax.broadcasted_iota(jnp.int32, (m, blk), 1) < lax.broadcasted_iota(jnp.int32, (m, blk), 0)
                lmm = jnp.where(mask, lm, 0.0)
            else:
                lmm = lm
            w = jnp.exp(z + lm + acc_l[r0:, :] + _dot(lmm.astype(BF16), tri))
            if masked:
                w = jnp.where(mask, w, 0.0)
            new_l = acc_l[r0:, :] + jnp.sum(lmm, axis=1, keepdims=True)
            new_o = acc_o[r0:, :] + _dot(w.astype(BF16), v_ref[sl, :].astype(BF16))
            if r0:
                new_l = jnp.concatenate([acc_l[:r0, :], new_l], axis=0)
                new_o = jnp.concatenate([acc_o[:r0, :], new_o], axis=0)
            return new_l, new_o

        acc = (jnp.zeros((qb, 1), F32), jnp.zeros((qb, HEAD), F32))
        for jr in reversed(range(ns)):
            acc = part(jr * blk, ns * i + jr, *acc, True)

        def more(c):
            return (c[0] < ns * i) & (jnp.max(c[1]) > ATTN_DEAD)

        def step(c):
            return (c[0] + 1,) + part(0, ns * i - 1 - c[0], c[1], c[2], False)

        swept, acc_l, acc_o = lax.while_loop(more, step, (jnp.int32(0),) + acc)
        o_ref[...] = acc_o
        mix_ref[...] = (acc_o * _rstd(acc_o) * w_ref[...]).astype(BF16)
        first = (ns * i - swept).astype(F32)
        tot_ref[...] = jnp.where(lax.broadcasted_iota(jnp.int32, (qb, HEAD), 1) == 1, first, acc_l)

    width = n_heads * HEAD
    qblk = pl.BlockSpec((qb, HEAD), lambda h, i: (i, h))
    return _hosted_call(
        body, (n_heads, t // qb),
        [qblk,
         pl.BlockSpec((t, HEAD), lambda h, i: (0, n_heads + h)),
         pl.BlockSpec((t, HEAD), lambda h, i: (0, 2 * n_heads + h)),
         pl.BlockSpec((blk, blk), lambda h, i: (0, 0)),
         pl.BlockSpec((1, HEAD), lambda h, i: (0, 0))],
        [qblk, qblk, qblk],
        [jax.ShapeDtypeStruct((t, width), F32), jax.ShapeDtypeStruct((t, width), BF16),
         jax.ShapeDtypeStruct((t, width), F32)],
        [], (proj, proj, proj, after_tri, norm_w), ("parallel", "arbitrary"), name, comm)


def _attn_bwd(proj, tot, do, after_tri, before_tri, n_heads, name, comm=None):
    t = proj.shape[0]
    blk = min(ATTN_BLOCK, t)
    qb = min(ATTN_ROWS, t)
    ns = qb // blk
    scale = HEAD ** -0.5

    def body(q_ref, k_ref, v_ref, tot_ref, do_ref, after_ref, before_ref, dq_ref, dk_ref, dv_ref):
        i = pl.program_id(1)

        @pl.when(i == 0)
        def _():
            dk_ref[...] = jnp.zeros_like(dk_ref)
            dv_ref[...] = jnp.zeros_like(dv_ref)

        q = (q_ref[...] * scale).astype(BF16)
        dob = do_ref[...].astype(BF16)
        total = tot_ref[:, 0:1]
        after_tri = after_ref[...]
        before = before_ref[...]

        def part(r0, j, seen_l, seen_g, dq, masked):
            sl = pl.ds(pl.multiple_of(j * blk, blk), blk)
            m = qb - r0
            qq, dd = q[r0:, :], dob[r0:, :]
            kb = k_ref[sl, :].astype(BF16)
            z = _dot_nt(qq, kb)
            lm = _log_one_minus_beta(z)
            if masked:
                mask = lax.broadcasted_iota(jnp.int32, (m, blk), 1) < lax.broadcasted_iota(jnp.int32, (m, blk), 0)
                lmm = jnp.where(mask, lm, 0.0)
            else:
                lmm = lm
            row_l = jnp.sum(lmm, axis=1, keepdims=True)
            after = (total[r0:, :] - seen_l[r0:, :] - row_l) + _dot(lmm.astype(BF16), after_tri)
            w = jnp.exp(z + lm + after)
            if masked:
                w = jnp.where(mask, w, 0.0)
            sig = jnp.exp(z + lm)
            g = w * _dot_nt(dd, v_ref[sl, :].astype(BF16))
            g_before = seen_g[r0:, :] + _dot(g.astype(BF16), before)
            dz = g * (1.0 - sig) - g_before * sig
            if masked:
                dz = jnp.where(mask, dz, 0.0)
            dzb = dz.astype(BF16)
            dk_ref[sl, :] += _dot_tn(dzb, qq)
            dv_ref[sl, :] += _dot_tn(w.astype(BF16), dd)
            new = (seen_l[r0:, :] + row_l,
                   seen_g[r0:, :] + jnp.sum(g, axis=1, keepdims=True), dq[r0:, :] + _dot(dzb, kb))
            if r0:
                new = tuple(jnp.concatenate([old[:r0, :], n], axis=0) for old, n in zip((seen_l, seen_g, dq), new))
            return new

        zero = jnp.zeros((qb, 1), F32)
        first = jnp.max(tot_ref[0:8, 1:2]).astype(jnp.int32)
        carry = lax.fori_loop(first, ns * i, lambda j, c: part(0, j, *c, False),
                              (zero, zero, jnp.zeros((qb, HEAD), F32)))
        for jr in range(ns):
            carry = part(jr * blk, ns * i + jr, *carry, True)
        dq_ref[...] = (carry[2] * scale).astype(dq_ref.dtype)

    width = n_heads * HEAD
    qblk = pl.BlockSpec((qb, HEAD), lambda h, i: (i, h))
    full = pl.BlockSpec((t, HEAD), lambda h, i: (0, h))
    tri = pl.BlockSpec((blk, blk), lambda h, i: (0, 0))
    return _hosted_call(
        body, (n_heads, t // qb),
        [qblk,
         pl.BlockSpec((t, HEAD), lambda h, i: (0, n_heads + h)),
         pl.BlockSpec((t, HEAD), lambda h, i: (0, 2 * n_heads + h)),
         qblk, qblk, tri, tri],
        [qblk, full, full], [jax.ShapeDtypeStruct((t, width), BF16)] + [jax.ShapeDtypeStruct((t, width), F32)] * 2,
        [], (proj, proj, proj, tot, do, after_tri, before_tri), ("parallel", "arbitrary"), name, comm)


def _lower_bound(logits):
    l0, l1 = logits[0:1, :], logits[1:2, :]
    mx = jnp.maximum(l0, l1)
    e0, e1 = jnp.exp(l0 - mx), jnp.exp(l1 - mx)
    return e0 / (e0 + e1)


def _hg_decay(q, k, g, sums):
    cum, mid, last = [_dot_split(m, g) for m in sums]
    return cum, mid, last, q * jnp.exp(cum - mid), k * jnp.exp(mid - cum), q * jnp.exp(cum), k * jnp.exp(last - cum)


def _hg_sums(rows, chunk):
    t = lax.broadcasted_iota(jnp.int32, (rows, rows), 0)
    j = lax.broadcasted_iota(jnp.int32, (rows, rows), 1)
    same = (t // chunk) == (j // chunk)
    mats = [same & (j <= t), same & (j % chunk < chunk // 2), same, same & (j >= t)]
    return [m.astype(BF16) for m in mats]


def _hgrn_fwd(proj, lb_logits, norm_w, sums, n_heads, heads_per_step, name, comm=None):
    t = proj.shape[0]
    bt = min(HG_ROWS, t)
    c = HG_CHUNK
    nc = bt // c
    hw = heads_per_step * HEAD
    width = n_heads * HEAD
    col0 = 3 * width // hw

    def body(hq_ref, hf_ref, hi_ref, hgate_ref, lbl_ref, w_ref, s0_ref, s1_ref, s2_ref, o_ref, mix_ref, st_ref,
             state, qt_scr, kt_scr, qe_scr, kd_scr, el_scr):
        @pl.when(pl.program_id(1) == 0)
        def _():
            state[...] = jnp.zeros_like(state)

        lb = _lower_bound(lbl_ref[...])
        f = hf_ref[...]
        hq = hq_ref[...]
        _, _, last, qt, kt, qe, kd = _hg_decay(hq * _sigmoid(hq), (1.0 - lb) * _sigmoid(-f),
                                               jnp.log(lb + (1.0 - lb) * _sigmoid(f)),
                                               (s0_ref[...], s1_ref[...], s2_ref[...]))
        qt_scr[...] = qt.astype(BF16)
        kt_scr[...] = kt.astype(BF16)
        qe_scr[...] = qe.astype(BF16)
        kd_scr[...] = kd.astype(BF16)
        el_scr[...] = jnp.exp(last)
        causal = lax.broadcasted_iota(jnp.int32, (c, c), 1) <= lax.broadcasted_iota(jnp.int32, (c, c), 0)

        def chunk(ci, carry):
            r = pl.ds(pl.multiple_of(ci * c, c), c)
            vc = hi_ref[r, :].astype(BF16)
            qt, kt, qe, kd = qt_scr[r, :], kt_scr[r, :], qe_scr[r, :], kd_scr[r, :]
            e_last = el_scr[r, :][0:1, :]
            old = [state[h] for h in range(heads_per_step)]
            scores, inter, outs, new = [], [], [], []
            for h in range(heads_per_step):
                cs = slice(h * HEAD, (h + 1) * HEAD)
                scores.append(jnp.where(causal, _dot_nt(qt[:, cs], kt[:, cs]), 0.0).astype(BF16))
                inter.append(_dot_nt(qe[:, cs], old[h].astype(BF16)))
                new.append(old[h] * e_last[:, cs] + _dot_tn(vc[:, cs], kd[:, cs]))
            for h in range(heads_per_step):
                outs.append(_dot(scores[h], vc[:, h * HEAD:(h + 1) * HEAD]) + inter[h])
            for h in range(heads_per_step):
                st_ref[ci, :, h * HEAD:(h + 1) * HEAD] = old[h]
                state[h] = new[h]
            o_ref[r, :] = jnp.concatenate(outs, axis=1)
            return carry

        lax.fori_loop(0, nc, chunk, 0, unroll=HG_UNROLL)

        def finish(o, gate):
            return ((o * _rstd(o) * w_ref[...]) * (gate * _sigmoid(gate)),)

        mix_ref[...] = _heads_map(finish, hw, o_ref[...], hgate_ref[...])[0].astype(BF16)

    def col(group):
        return pl.BlockSpec((bt, hw), functools.partial(lambda hp, tb, g: (tb, col0 + g * (width // hw) + hp), g=group))

    blk = pl.BlockSpec((bt, hw), lambda hp, tb: (tb, hp))
    mat = pl.BlockSpec((bt, bt), lambda hp, tb: (0, 0))
    return _hosted_call(
        body, (n_heads // heads_per_step, t // bt),
        [col(0), col(1), col(2), col(3),
         pl.BlockSpec((2, hw), lambda hp, tb: (0, hp)),
         pl.BlockSpec((1, HEAD), lambda hp, tb: (0, 0)), mat, mat, mat],
        [blk, blk, pl.BlockSpec((nc, HEAD, hw), lambda hp, tb: (tb, 0, hp))],
        [jax.ShapeDtypeStruct((t, width), F32), jax.ShapeDtypeStruct((t, width), BF16),
         jax.ShapeDtypeStruct((t // c, HEAD, width), F32)],
        [pltpu.VMEM((heads_per_step, HEAD, HEAD), F32)] + [pltpu.VMEM((bt, hw), BF16)] * 4
        + [pltpu.VMEM((bt, hw), F32)],
        (proj, proj, proj, proj, lb_logits, norm_w, *sums[:3]), ("parallel", "arbitrary"), name, comm)


def _hgrn_bwd(proj, do, states, lb_logits, sums, n_heads, heads_per_step, name):
    t = proj.shape[0]
    bt = min(HG_ROWS, t)
    c = HG_CHUNK
    nc = bt // c
    nb = t // bt
    hw = heads_per_step * HEAD
    width = n_heads * HEAD
    col0 = 3 * width // hw

    def body(hq_ref, hf_ref, hi_ref, do_ref, st_ref, lbl_ref, s0_ref, s1_ref, s2_ref, s3_ref,
             dq_ref, df_ref, di_ref, dlb_ref,
             dstate, qth, qtl, kth, ktl, qe_scr, kd_scr, el_scr, qa_scr, ka_scr, qb_scr, kb_scr, ss_scr):
        @pl.when(pl.program_id(1) == 0)
        def _():
            dstate[...] = jnp.zeros_like(dstate)
            dlb_ref[...] = jnp.zeros_like(dlb_ref)

        lb = _lower_bound(lbl_ref[...])
        f = hf_ref[...]
        sg = _sigmoid(f)
        sgn = _sigmoid(-f)
        den = lb + (1.0 - lb) * sg
        kk = (1.0 - lb) * sgn
        hq = hq_ref[...]
        sq = _sigmoid(hq)
        qq = hq * sq
        cum, mid, last, qt, kt, qe, kd = _hg_decay(qq, kk, jnp.log(den), (s0_ref[...], s1_ref[...], s2_ref[...]))
        qth[...], qtl[...] = _hilo(qt)
        kth[...], ktl[...] = _hilo(kt)
        qe_scr[...] = qe.astype(BF16)
        kd_scr[...] = kd.astype(BF16)
        e_last = jnp.exp(last)
        el_scr[...] = e_last
        causal = lax.broadcasted_iota(jnp.int32, (c, c), 1) <= lax.broadcasted_iota(jnp.int32, (c, c), 0)

        def chunk(cc, carry):
            ci = nc - 1 - cc
            r = pl.ds(pl.multiple_of(ci * c, c), c)
            qt = (qth[r, :], qtl[r, :])
            kt = (kth[r, :], ktl[r, :])
            qe, kd = qe_scr[r, :], kd_scr[r, :]
            doc, vc = do_ref[r, :].astype(BF16), hi_ref[r, :].astype(BF16)
            e_row = el_scr[r, :][0:1, :]
            sts = [st_ref[ci, :, h * HEAD:(h + 1) * HEAD] for h in range(heads_per_step)]
            dsts = [dstate[h] for h in range(heads_per_step)]
            a, da, di, dq_inter, dk_inter, dq_intra, dk_intra, st_sums, new = [], [], [], [], [], [], [], [], []
            for h in range(heads_per_step):
                cs = slice(h * HEAD, (h + 1) * HEAD)
                st, dst = sts[h].astype(BF16), dsts[h].astype(BF16)
                a.append(jnp.where(causal, _dot_nt(qt[0][:, cs], kt[0][:, cs]), 0.0).astype(BF16))
                da.append(jnp.where(causal, _dot_nt(doc[:, cs], vc[:, cs]), 0.0))
                di.append(_dot_nt(kd[:, cs], dst))
                dq_inter.append(_dot(doc[:, cs], st))
                dk_inter.append(_dot(vc[:, cs], dst))
                st_sums.append(_colsum(dsts[h] * sts[h]))
                new.append(dsts[h] * e_row[:, cs] + _dot_tn(doc[:, cs], qe[:, cs]))
            for h in range(heads_per_step):
                cs = slice(h * HEAD, (h + 1) * HEAD)
                da_h = _hilo(da[h])
                di[h] = di[h] + _dot_tn(a[h], doc[:, cs])
                dq_intra.append(_dot3(_dot, da_h, (kt[0][:, cs], kt[1][:, cs])))
                dk_intra.append(_dot3(_dot_tn, da_h, (qt[0][:, cs], qt[1][:, cs])))

            def wide(parts):
                return jnp.concatenate(parts, axis=1)

            for h in range(heads_per_step):
                dstate[h] = new[h]
            di_ref[r, :] = wide(di).astype(BF16)
            qa_scr[r, :] = wide(dq_intra)
            ka_scr[r, :] = wide(dk_intra)
            qb_scr[r, :] = wide(dq_inter)
            kb_scr[r, :] = wide(dk_inter)
            ss_scr[r, :] = jnp.broadcast_to(wide(st_sums), (c, hw))
            return carry

        lax.fori_loop(0, nc, chunk, 0, unroll=HG_UNROLL)

        dk_inter = kb_scr[...] * jnp.exp(last - cum)
        dq = qa_scr[...] * jnp.exp(cum - mid) + qb_scr[...] * jnp.exp(cum)
        dk = ka_scr[...] * jnp.exp(mid - cum) + dk_inter
        is_last = lax.broadcasted_iota(jnp.int32, (bt, hw), 0) % c == c - 1
        d_last = _dot_split(s2_ref[...], kk * dk_inter) + e_last * ss_scr[...]
        dcum = qq * dq - kk * dk + jnp.where(is_last, d_last, 0.0)
        e = (_dot_split(s3_ref[...], dcum) / den - dk) * sgn
        df_ref[...] = (e * (1.0 - lb) * sg).astype(BF16)
        dlb_ref[...] += _colsum(e)
        dq_ref[...] = (dq * (sq * (1.0 + hq * (1.0 - sq)))).astype(BF16)

    def col(group):
        return pl.BlockSpec((bt, hw), functools.partial(
            lambda hp, tb, g: (nb - 1 - tb, col0 + g * (width // hw) + hp), g=group))

    blk = pl.BlockSpec((bt, hw), lambda hp, tb: (nb - 1 - tb, hp))
    mat = pl.BlockSpec((bt, bt), lambda hp, tb: (0, 0))
    return pl.pallas_call(
        body, name=name,
        grid=(n_heads // heads_per_step, nb),
        in_specs=[col(0), col(1), col(2), blk,
                  pl.BlockSpec((nc, HEAD, hw), lambda hp, tb: (nb - 1 - tb, 0, hp)),
                  pl.BlockSpec((2, hw), lambda hp, tb: (0, hp)), mat, mat, mat, mat],
        out_specs=[blk, blk, blk, pl.BlockSpec((1, hw), lambda hp, tb: (0, hp))],
        out_shape=[jax.ShapeDtypeStruct((t, width), BF16)] * 3 + [jax.ShapeDtypeStruct((1, width), F32)],
        scratch_shapes=[pltpu.VMEM((heads_per_step, HEAD, HEAD), F32)] + [pltpu.VMEM((bt, hw), BF16)] * 6
                       + [pltpu.VMEM((bt, hw), F32)] * 6,
        compiler_params=_params(("parallel", "arbitrary")),
    )(proj, proj, proj, do, states, lb_logits, *sums)


def _place():
    x, y, c = lax.axis_index("x"), lax.axis_index("y"), lax.axis_index("c")
    chips = [(1 - x, y), (x, 1 - y), (1 - x, 1 - y)]
    return x, y, c, chips


ANY = pl.BlockSpec(memory_space=pl.ANY)


def _cast_to_slot(shard, ids, name):
    r, c = shard.shape
    tm = _pick(r, (256, 128, 64, 32, 16))

    def body(ids_ref, s_ref, o_ref):
        o_ref[...] = s_ref[...].astype(BF16)

    return pl.pallas_call(
        body, name=name,
        grid_spec=pltpu.PrefetchScalarGridSpec(
            num_scalar_prefetch=1, grid=(r // tm,),
            in_specs=[pl.BlockSpec((tm, c), lambda i, ids: (i, 0))],
            out_specs=pl.BlockSpec((None, tm, c), lambda i, ids: (ids[1], i, 0))),
        out_shape=jax.ShapeDtypeStruct((N_CHIPS, r, c), BF16),
        compiler_params=_params(("parallel",)),
    )(ids, shard)


def _gather_copies(bufs, send, recv):
    x, y, c, chips = _place()
    mine = 2 * x + y

    def half(ref, who, core):
        h = ref.shape[-2] // 2
        return ref.at[who, pl.ds(core * h, h), :]

    def copy(w, k, rows, to):
        return pltpu.make_async_remote_copy(src_ref=rows, dst_ref=rows, send_sem=send.at[6 * w + k],
                                            recv_sem=recv.at[6 * w + k], device_id=to, device_id_type=MESH)

    def to_chips(w):
        return [copy(w, k, half(bufs[w], mine, c), (qx, qy, c)) for k, (qx, qy) in enumerate(chips)]

    def to_sibling(w):
        return [copy(w, 3 + k, half(bufs[w], 2 * qx + qy, c), (x, y, 1 - c)) for k, (qx, qy) in enumerate(chips)]

    def begin():
        for w in range(len(bufs)):
            for cp in to_chips(w):
                cp.start()

    def end():
        for w in range(len(bufs)):
            for k, (qx, qy) in enumerate(chips):
                copy(w, k, half(bufs[w], 2 * qx + qy, c), (x, y, c)).wait_recv()
                to_sibling(w)[k].start()
        for w in range(len(bufs)):
            for k, (qx, qy) in enumerate(chips):
                copy(w, 3 + k, half(bufs[w], 2 * qx + qy, 1 - c), (x, y, c)).wait_recv()
        for w in range(len(bufs)):
            for cp in to_chips(w) + to_sibling(w):
                cp.wait_send()

    return begin, end


def _scatter_copies(ins, outs, send, recv):
    _, _, c, chips = _place()

    def copies():
        return [pltpu.make_async_remote_copy(
            src_ref=ins[w].at[2 * qx + qy], dst_ref=outs[w].at[k], send_sem=send.at[3 * w + k],
            recv_sem=recv.at[3 * w + k], device_id=(qx, qy, c), device_id_type=MESH)
            for w in range(len(ins)) for k, (qx, qy) in enumerate(chips)]

    def begin():
        for cp in copies():
            cp.start()

    def end():
        for cp in copies():
            cp.wait()

    return begin, end


def _swap_copies(ins, outs, send, recv):
    x, y, c, _ = _place()

    def copies():
        return [pltpu.make_async_remote_copy(
            src_ref=ins[w].at[:, pl.ds((1 - c) * (ins[w].shape[1] // 2), ins[w].shape[1] // 2), :], dst_ref=outs[w],
            send_sem=send.at[w], recv_sem=recv.at[w], device_id=(x, y, 1 - c), device_id_type=MESH)
            for w in range(len(ins))]

    def begin():
        for cp in copies():
            cp.start()

    def end():
        for cp in copies():
            cp.wait()

    return begin, end


def _sibling_swap(grads, name):
    n = len(grads)

    def body(*refs):
        begin, end = _swap_copies(refs[:n], refs[n:2 * n], *refs[2 * n:])
        begin()
        end()

    return pl.pallas_call(
        body, name=name, in_specs=[ANY] * n, out_specs=[ANY] * n,
        out_shape=[jax.ShapeDtypeStruct((g.shape[0], g.shape[1] // 2, g.shape[2]), g.dtype) for g in grads],
        scratch_shapes=[pltpu.SemaphoreType.DMA((n,)), pltpu.SemaphoreType.DMA((n,))],
    )(*grads)


def _sibling_join(fulls, name):
    n = len(fulls)

    def body(*refs):
        bufs = refs[n:2 * n]
        send, recv = refs[2 * n:]
        x, y, c, _ = _place()
        cps = []
        for w in range(n):
            h = bufs[w].shape[0] // 2
            rows = bufs[w].at[pl.ds(c * h, h), :]
            cps.append(pltpu.make_async_remote_copy(
                src_ref=rows, dst_ref=rows, send_sem=send.at[w], recv_sem=recv.at[w],
                device_id=(x, y, 1 - c), device_id_type=MESH))
            cps[-1].start()
        for w in range(n):
            h = bufs[w].shape[0] // 2
            theirs = bufs[w].at[pl.ds((1 - c) * h, h), :]
            pltpu.make_async_remote_copy(src_ref=theirs, dst_ref=theirs, send_sem=send.at[w], recv_sem=recv.at[w],
                                         device_id=(x, y, c), device_id_type=MESH).wait_recv()
        for cp in cps:
            cp.wait_send()

    return pl.pallas_call(
        body, name=name, in_specs=[ANY] * n, out_specs=[ANY] * n,
        out_shape=[jax.ShapeDtypeStruct(s.shape, s.dtype) for s in fulls],
        input_output_aliases={w: w for w in range(n)},
        scratch_shapes=[pltpu.SemaphoreType.DMA((n,)), pltpu.SemaphoreType.DMA((n,))],
    )(*fulls)


def _all_sum_small(vec, name):
    n = vec.shape[1]

    def body(v_ref, out_ref, buf, send, recv):
        x, y, c, _ = _place()
        me = 4 * x + 2 * y + c
        buf[me] = v_ref[...]
        peers = []
        for mask in range(1, 8):
            px = 1 - x if mask & 4 else x
            py = 1 - y if mask & 2 else y
            pc = 1 - c if mask & 1 else c
            peers.append((px, py, pc))
        cps = []
        for k, peer in enumerate(peers):
            cps.append(pltpu.make_async_remote_copy(src_ref=buf.at[me], dst_ref=buf.at[me], send_sem=send.at[k],
                                                    recv_sem=recv.at[k], device_id=peer, device_id_type=MESH))
            cps[-1].start()
        for k, (px, py, pc) in enumerate(peers):
            slot = buf.at[4 * px + 2 * py + pc]
            pltpu.make_async_remote_copy(src_ref=slot, dst_ref=slot, send_sem=send.at[k], recv_sem=recv.at[k],
                                         device_id=(x, y, c), device_id_type=MESH).wait_recv()
        for cp in cps:
            cp.wait_send()
        total = buf[0]
        for d in range(1, 8):
            total = total + buf[d]
        out_ref[...] = total

    vm = pl.BlockSpec(memory_space=pltpu.VMEM)
    return pl.pallas_call(
        body, name=name, in_specs=[vm], out_specs=vm,
        out_shape=jax.ShapeDtypeStruct(vec.shape, F32),
        scratch_shapes=[pltpu.VMEM((8, 8, n), F32), pltpu.SemaphoreType.DMA((7,)), pltpu.SemaphoreType.DMA((7,))],
    )(vec)


def _pair_sum(g, buf, ids, name):
    p, r, c = g.shape
    h = r // 2
    tr = _pick(h, (256, 128, 64, 32, 16))
    nh = h // tr

    def body(ids_ref, g_ref, b_ref, sums_ref, own_ref):
        s = g_ref[...] + b_ref[...]
        sums_ref[...] = s.astype(BF16)

        @pl.when(pl.program_id(1) == ids_ref[1])
        def _():
            own_ref[...] = s

    return pl.pallas_call(
        body, name=name,
        grid_spec=pltpu.PrefetchScalarGridSpec(
            num_scalar_prefetch=1, grid=(nh, p),
            in_specs=[pl.BlockSpec((None, tr, c), lambda i, q, ids: (q, ids[0] * nh + i, 0)),
                      pl.BlockSpec((None, tr, c), lambda i, q, ids: (q, i, 0))],
            out_specs=[pl.BlockSpec((None, tr, c), lambda i, q, ids: (q, i, 0)),
                       pl.BlockSpec((tr, c), lambda i, q, ids: (i, 0))]),
        out_shape=[jax.ShapeDtypeStruct((p, h, c), BF16), jax.ShapeDtypeStruct((h, c), F32)],
        compiler_params=_params(("parallel", "arbitrary")),
    )(ids, g, buf)


def _final_sum(own, others, ids, name):
    h, c = own.shape
    tr = _pick(h, (256, 128, 64, 32, 16))
    nh = h // tr

    def body(ids_ref, own_ref, oth_ref, out_ref):
        s = own_ref[...]
        for k in range(3):
            s = s + oth_ref[k].astype(F32)
        out_ref[...] = s

    return pl.pallas_call(
        body, name=name,
        grid_spec=pltpu.PrefetchScalarGridSpec(
            num_scalar_prefetch=1, grid=(nh,),
            in_specs=[pl.BlockSpec((tr, c), lambda i, ids: (i, 0)),
                      pl.BlockSpec((3, tr, c), lambda i, ids: (0, i, 0))],
            out_specs=pl.BlockSpec((tr, c), lambda i, ids: (ids[0] * nh + i, 0))),
        out_shape=jax.ShapeDtypeStruct((2 * h, c), F32),
        compiler_params=_params(("parallel",)),
    )(ids, own, others)


def _adamw(w, g, m, v, name):
    r, c = w.shape
    tm = _pick(r, (256, 128, 64, 32, 16, 8)) if r >= 8 else r

    def fn(rows, _):
        w_, g_, m_, v_ = rows
        m2 = ADAM_B1 * m_ + (1.0 - ADAM_B1) * g_
        v2 = ADAM_B2 * v_ + (1.0 - ADAM_B2) * (g_ * g_)
        m_hat = m2 / (1.0 - ADAM_B1 ** ADAM_STEP)
        v_hat = v2 / (1.0 - ADAM_B2 ** ADAM_STEP)
        delta = -ADAM_LR * (m_hat / (jnp.sqrt(v_hat) + ADAM_EPS) + ADAM_WD * w_)
        return [delta, m2, v2], []

    outs, _ = _rows(fn, [w, g, m, v], [], [(c, F32)] * 3, [], tm=tm, name=name)
    return outs


def kernel(x, attn_norm_w, w_in, lb_logits, sb_norm_w, hg_norm_w, w_out, mlp_norm_w, w_up, w_down, final_norm_w, loss_target, m_attn_norm_w, m_w_in, m_lb_logits, m_sb_norm_w, m_hg_norm_w, m_w_out, m_mlp_norm_w, m_w_up, m_w_down, m_final_norm_w, v_attn_norm_w, v_w_in, v_lb_logits, v_sb_norm_w, v_hg_norm_w, v_w_out, v_mlp_norm_w, v_w_up, v_w_down, v_final_norm_w):
    xs, tgt = x[0], loss_target[0]
    t, d = xs.shape
    width = d // 2
    n_heads = width // HEAD
    hps = min(8, n_heads)
    final_w = final_norm_w.reshape(1, d)
    tm_rows = _pick(t, (256, 128))
    tm = _pick(t, (1024, 512, 256))
    blk = min(ATTN_BLOCK, t)
    ones_a = jnp.ones((blk, blk), F32)
    after_tri = jnp.tril(ones_a, -1).astype(BF16)
    before_tri = jnp.triu(ones_a, 1).astype(BF16)
    hg_sums = _hg_sums(min(HG_ROWS, t), HG_CHUNK)
    cx, cy, cc = lax.axis_index("x"), lax.axis_index("y"), lax.axis_index("c")
    ids = jnp.stack([cc, 2 * cx + cy]).astype(jnp.int32)

    shards = [w_in[0], w_out[0], w_up[0], w_down[0]]
    cast = [_cast_to_slot(s, ids, f"cast_w{i}") for i, s in enumerate(shards)]
    d_ff = N_CHIPS * w_up.shape[2]
    cs_in, cs_up = w_in.shape[2], w_up.shape[2]
    tn_in = _pick(cs_in, (1792, 896, 512, 256, 128))
    tn_up = _pick(cs_up, (1024, 512, 256))
    tn_d = _pick(d, (1024, 512, 256))
    tk_d = _pick(d, (2048, 1024, 512))

    (u,), _, (g_in,) = _rows(lambda r, c_: ([r[0] * _rstd(r[0]) * c_[0]], []), [xs], [attn_norm_w], [(d, BF16)], [],
                             tm=tm_rows, name="norm_in", comm=("gather", cast[:1]))
    proj, g_out = _mm_nn(u, g_in, [F32], tm=_pick(t, (512, 256)), tn=tn_in, tk=tk_d, name="proj_in",
                         comm=("gather", cast[1:2]))
    o_a, mix_a, sb_tot, g_down = _attn_fwd(proj, after_tri, sb_norm_w, n_heads, "sb_fwd", comm=("gather", cast[3:]))
    w_out_all = g_out.reshape(1, d, d)
    w_down_all = g_down.reshape(1, d_ff, d)
    o_b, mix_b, states, g_up = _hgrn_fwd(proj, lb_logits, hg_norm_w, hg_sums, n_heads, hps, "hg_fwd",
                                         comm=("gather", cast[2:3]))
    mix = jnp.concatenate([mix_a, mix_b], axis=1)
    def out_and_norm(acc, res, w):
        hh = acc + res
        return hh, hh * _rstd(hh) * w

    h1, mn = _mm_nn(mix, w_out_all, [F32, BF16], tm=_pick(t, (512, 256)), tn=d, tk=tk_d, name="proj_out",
                    epi=out_and_norm, extras=(xs, mlp_norm_w))
    up_b, act = _mm_nn(mn, g_up, [BF16, BF16], tm=tm, tn=tn_up, tk=tk_d, name="mlp_up",
                       epi=lambda acc: (acc, jnp.square(jnp.maximum(acc, 0.0))))
    (h2,) = _mm_nn(act, w_down_all, [F32], tm=tm, tn=tn_d, tk=_pick(d_ff, (2048, 1024)), name="mlp_down",
                   epi=lambda acc, res: (acc + res,), extras=(h1,))

    def head(rows, consts):
        hh, tg = rows
        w = consts[0]
        n = hh * _rstd(hh)
        err = n * w - tg
        dhh, dw_rows = _rms_bwd(hh, w, err * (1.0 / d))
        return [dhh, dhh], [_colsum(dw_rows), _colsum(err * err)]

    (dh2, dh2_b), (g_final, loss_cols) = _rows(head, [h2, tgt], [final_w], [(d, F32), (d, BF16)], [d, d],
                                                 tm=tm_rows, name="loss_head")

    (dup,) = _mm_nt(dh2_b, w_down_all, [BF16], tm=tm, tn=_pick(d_ff, (1024, 512)), tk=tk_d, name="mlp_down_dx",
                    epi=lambda acc, upv: (acc * (2.0 * jnp.maximum(upv.astype(F32), 0.0)),), extras=(up_b,))
    gw_down = _mm_tn(act, dh2_b, 1, tm=_pick(d_ff, (1024, 512)), tn=tn_d, tk=_pick(t, (2048, 1024, 512, 256)),
                     name="mlp_down_dw")
    gw_down = gw_down.reshape(N_CHIPS, d_ff // N_CHIPS, d)
    dmn, their_down = _mm_nt(dup, g_up, [F32], tm=tm, tn=tn_d, tk=_pick(cs_up, (2048, 1024, 512)), name="mlp_up_dx",
                             comm=("swap", [gw_down]))
    gw_up = _mm_tn(mn, dup, N_CHIPS, tm=tn_d, tn=tn_up, tk=_pick(t, (2048, 1024, 512, 256)), name="mlp_up_dw")

    def norm_back(rows, consts):
        xx, dy, skip = rows
        dx, dw_rows = _rms_bwd(xx, consts[0], dy)
        tot = dx + skip
        return [tot, tot], [_colsum(dw_rows)]

    (dh1, dh1_b), (g_mlp_norm,) = _rows(norm_back, [h1, dmn, dh2], [mlp_norm_w], [(d, F32), (d, BF16)], [d],
                                         tm=tm_rows, name="norm_mlp_bwd")

    gw_out = _mm_tn(mix, dh1_b, 1, tm=tn_d, tn=tn_d, tk=_pick(t, (2048, 1024, 512, 256)), name="proj_out_dw")
    gw_out = gw_out.reshape(N_CHIPS, d // N_CHIPS, d)
    dmix, their_up, their_out = _mm_nt(dh1_b, w_out_all, [F32], tm=tm, tn=tn_d, tk=tk_d, name="proj_out_dx",
                                       comm=("swap", [gw_up, gw_out]))
    pair_mlp = [_pair_sum(g, b, ids, "grads_pair_sum_" + nm)
                for g, b, nm in ((gw_up, their_up, "up"), (gw_down, their_down, "down"))]

    def sb_norm_back(rows, consts):
        dx, dw_rows = _heads_map(lambda o, dy: _rms_bwd(o, consts[0], dy), width, *rows)
        dw = sum(_colsum(dw_rows[:, h * HEAD:(h + 1) * HEAD]) for h in range(n_heads))
        return [dx], [dw]

    (do_a,), (g_sb_norm,) = _rows(sb_norm_back, [o_a, (dmix, width, 0)], [sb_norm_w], [(width, BF16)], [HEAD],
                                  tm=tm_rows, name="sb_norm_bwd")
    dq_a, dk_a, dv_a, *landed_mlp = _attn_bwd(proj, sb_tot, do_a, after_tri, before_tri, n_heads, "sb_bwd",
                                              comm=("scatter", [p[0] for p in pair_mlp]))

    def hg_out_back(rows, consts):
        def one(o, gate, dy):
            sg = _sigmoid(gate)
            silu = gate * sg
            n = o * _rstd(o) * consts[0]
            do, dw_rows = _rms_bwd(o, consts[0], dy * silu)
            return do, dy * n * (sg * (1.0 + gate * (1.0 - sg))), dw_rows
        do, dgate, dw_rows = _heads_map(one, width, *rows)
        dw = sum(_colsum(dw_rows[:, h * HEAD:(h + 1) * HEAD]) for h in range(n_heads))
        return [do, dgate], [dw]

    (do_b, dgate), (g_hg_norm,) = _rows(hg_out_back, [o_b, (proj, width, 6), (dmix, width, 1)], [hg_norm_w],
                                         [(width, BF16)] * 2, [HEAD], tm=tm_rows, name="hg_out_bwd")
    dhq, dhf, dhi, dlb = _hgrn_bwd(proj, do_b, states, lb_logits, hg_sums, n_heads, hps, "hg_bwd")

    (dproj,), _ = _rows(lambda r, _c: ([jnp.concatenate([p.astype(BF16) for p in r], axis=1)], []),
                        [dq_a, dk_a, dv_a, dhq, dhf, dhi, dgate], [], [(7 * width, BF16)], [],
                        tm=tm_rows, name="pack_dproj")
    gw_in = _mm_tn(u, dproj, N_CHIPS, tm=tn_d, tn=tn_in, tk=_pick(t, (2048, 1024, 512, 256)), name="proj_in_dw")
    (their_in,) = _sibling_swap([gw_in], "grads_to_sibling_in")
    pair_mix = [_pair_sum(g, b, ids, "grads_pair_sum_" + nm)
                for g, b, nm in ((gw_in, their_in, "in"), (gw_out, their_out, "out"))]
    du, *landed_mix = _mm_nt(dproj, g_in, [F32], tm=tm, tn=tn_d, tk=_pick(cs_in, (1792, 896, 512, 256, 128)),
                             name="proj_in_dx", comm=("scatter", [p[0] for p in pair_mix]))
    (dx,), (g_attn_norm,) = _rows(lambda r, c_: (lambda dxx, dwr: ([dxx + r[2]], [_colsum(dwr)]))(
        *_rms_bwd(r[0], c_[0], r[1])), [xs, du, dh1], [attn_norm_w], [(d, F32)], [d], tm=tm_rows, name="norm_in_bwd")

    halves = [_final_sum(p[1], r, ids, f"grads_final_sum{i}")
              for i, (p, r) in enumerate(zip(pair_mix + pair_mlp, list(landed_mix) + list(landed_mlp)))]
    g_w_in, g_w_out, g_w_up, g_w_down = _sibling_join(halves, "grads_join")

    pieces = [g_attn_norm, g_mlp_norm, g_final, g_sb_norm, g_hg_norm, dlb, loss_cols]
    sizes = [p.shape[1] for p in pieces]
    flat = jnp.concatenate(pieces, axis=1)
    n_small = -(-flat.shape[1] // 1024) * 1024
    flat = jnp.pad(flat, ((0, 0), (0, n_small - flat.shape[1]))).reshape(8, n_small // 8)
    flat = _all_sum_small(flat, "small_all_sum").reshape(1, n_small)
    offs = [sum(sizes[:i]) for i in range(len(sizes))]
    g_attn_norm, g_mlp_norm, g_final, g_sb_norm, g_hg_norm, dlb, loss_cols = [
        flat[:, o:o + s] for o, s in zip(offs, sizes)]

    def small_tail(lbl_ref, dlb_ref, loss_ref, glb_ref, out_ref):
        lb = _lower_bound(lbl_ref[...])
        g0 = dlb_ref[...] * lb * (1.0 - lb)
        glb_ref[0:1, :] = g0
        glb_ref[1:2, :] = -g0
        out_ref[...] = jnp.zeros_like(out_ref) + 0.5 * jnp.sum(loss_ref[...]) * (1.0 / d)

    vm = pl.BlockSpec(memory_space=pltpu.VMEM)
    g_lb, loss11 = pl.pallas_call(
        small_tail, name="small_tail", in_specs=[vm, vm, vm], out_specs=[vm, vm],
        out_shape=[jax.ShapeDtypeStruct(lb_logits.shape, F32), jax.ShapeDtypeStruct((1, 128), F32)],
    )(lb_logits, dlb, loss_cols)
    loss = loss11[0, 0]

    names = ["attn_norm_w", "w_in", "lb_logits", "sb_norm_w", "hg_norm_w", "w_out", "mlp_norm_w", "w_up", "w_down",
             "final_norm_w"]
    ws = [attn_norm_w, w_in[0], lb_logits, sb_norm_w, hg_norm_w, w_out[0], mlp_norm_w, w_up[0], w_down[0], final_w]
    gs = [g_attn_norm, g_w_in, g_lb, g_sb_norm, g_hg_norm, g_w_out, g_mlp_norm, g_w_up, g_w_down, g_final]
    ms = [m_attn_norm_w, m_w_in[0], m_lb_logits, m_sb_norm_w, m_hg_norm_w, m_w_out[0], m_mlp_norm_w, m_w_up[0],
          m_w_down[0], m_final_norm_w.reshape(1, d)]
    vs = [v_attn_norm_w, v_w_in[0], v_lb_logits, v_sb_norm_w, v_hg_norm_w, v_w_out[0], v_mlp_norm_w, v_w_up[0],
          v_w_down[0], v_final_norm_w.reshape(1, d)]
    shapes = [attn_norm_w.shape, w_in.shape, lb_logits.shape, sb_norm_w.shape, hg_norm_w.shape, w_out.shape,
              mlp_norm_w.shape, w_up.shape, w_down.shape, final_norm_w.shape]
    deltas, new_ms, new_vs = [], [], []
    for nm, w_, g_, m_, v_ in zip(names, ws, gs, ms, vs):
        dl, m2, v2 = _adamw(w_, g_, m_, v_, "adamw_" + nm)
        deltas.append(dl)
        new_ms.append(m2)
        new_vs.append(v2)

    def shaped(lst):
        return [a.reshape(s) for a, s in zip(lst, shapes)]

    return (loss, dx[None], *shaped(gs), *shaped(deltas), *shaped(new_ms), *shaped(new_vs))
```

```python
import functools

import jax
import jax.numpy as jnp
from jax import lax
from jax.experimental import pallas as pl
from jax.experimental.pallas import tpu as pltpu

F32 = jnp.float32
BF16 = jnp.bfloat16
MESH = pl.DeviceIdType.MESH

HEAD = 128
NORM_EPS = 1e-5
N_CHIPS = 4
ATTN_BLOCK = 256
ATTN_ROWS = 1024
ATTN_DEAD = -110.0
HG_CHUNK = 32
HG_ROWS = 256
HG_UNROLL = 4
VMEM_LIMIT = 56 * 1024 * 1024

ADAM_LR = 0.001
ADAM_B1 = 0.9
ADAM_B2 = 0.999
ADAM_EPS = 1e-08
ADAM_WD = 0.01
ADAM_STEP = 10


def _pick(n, cands):
    for c in cands:
        if n % c == 0:
            return c
    return n


def _params(sem):
    return pltpu.CompilerParams(dimension_semantics=sem, vmem_limit_bytes=VMEM_LIMIT)


def _dot(a, b):
    return jnp.dot(a, b, preferred_element_type=F32)


def _dot_nt(a, b):
    return lax.dot_general(a, b, (((1,), (1,)), ((), ())), preferred_element_type=F32)


def _dot_tn(a, b):
    return lax.dot_general(a, b, (((0,), (0,)), ((), ())), preferred_element_type=F32)


def _hilo(x):
    hi = x.astype(BF16)
    return hi, (x - hi.astype(F32)).astype(BF16)


def _dot_split(tri, x):
    hi, lo = _hilo(x)
    return _dot(tri, hi) + _dot(tri, lo)


def _dot3(dot, a, b):
    return dot(a[0], b[0]) + (dot(a[0], b[1]) + dot(a[1], b[0]))


def _sigmoid(x):
    return 1.0 / (1.0 + jnp.exp(-x))


def _hosted_call(inner, grid, in_specs, out_specs, out_shape, scratch, args, semantics, name, comm=None):
    kind, arrays = comm if comm else (None, ())
    n_i, n_o, n_s, n_c = len(in_specs), len(out_specs), len(scratch), len(arrays)

    def body(*refs):
        c_in = refs[n_i:n_i + n_c]
        c_out = refs[n_i + n_c + n_o:n_i + 2 * n_c + n_o]
        scr = refs[n_i + 2 * n_c + n_o:]
        if n_c:
            ids = [pl.program_id(ax) for ax in range(len(grid))]
            first, last = ids[0] == 0, ids[0] == grid[0] - 1
            for ax in range(1, len(grid)):
                first, last = first & (ids[ax] == 0), last & (ids[ax] == grid[ax] - 1)
            sems = scr[n_s:]
            copies = {"scatter": _scatter_copies, "swap": _swap_copies}
            begin, end = _gather_copies(c_out, *sems) if kind == "gather" else copies[kind](c_in, c_out, *sems)
            pl.when(first)(begin)
        inner(*refs[:n_i], *refs[n_i + n_c:n_i + n_c + n_o], *scr[:n_s])
        if n_c:
            pl.when(last)(end)

    gather = kind == "gather"
    shape = {"gather": lambda a: a.shape, "scatter": lambda a: (3,) + a.shape[1:],
             "swap": lambda a: (a.shape[0], a.shape[1] // 2, a.shape[2])}
    landed = [jax.ShapeDtypeStruct(shape[kind](a), a.dtype) for a in arrays]
    return pl.pallas_call(
        body, name=name, grid=grid,
        in_specs=list(in_specs) + [ANY] * n_c, out_specs=list(out_specs) + [ANY] * n_c,
        out_shape=list(out_shape) + landed,
        input_output_aliases={n_i + w: n_o + w for w in range(n_c)} if gather else {},
        scratch_shapes=list(scratch) + ([pltpu.SemaphoreType.DMA(
            ({"gather": 6, "scatter": 3, "swap": 1}[kind] * n_c,))] * 2 if n_c else []),
        compiler_params=_params(("arbitrary",) * len(grid) if n_c else semantics),
    )(*args, *arrays)


def _mm_body(kind, nk, n_extra, n_out, epi):
    dot = {"nn": _dot, "nt": _dot_nt, "tn": _dot_tn}[kind]

    def finish(acc, extra_refs, out_refs):
        res = epi(acc, *[e[...] for e in extra_refs]) if epi is not None else (acc,)
        for o, r in zip(out_refs, res):
            o[...] = r.astype(o.dtype)

    def body(a_ref, b_ref, *rest):
        extra_refs = rest[:n_extra]
        out_refs = rest[n_extra:n_extra + n_out]
        if nk == 1:
            finish(dot(a_ref[...], b_ref[...]), extra_refs, out_refs)
            return
        acc_ref = rest[n_extra + n_out]
        k = pl.program_id(2)

        @pl.when(k == 0)
        def _():
            acc_ref[...] = jnp.zeros_like(acc_ref)

        acc_ref[...] += dot(a_ref[...], b_ref[...])

        @pl.when(k == nk - 1)
        def _():
            finish(acc_ref[...], extra_refs, out_refs)

    return body


def _mm_nn(a, w, out_dtypes, *, tm, tn, tk, name, epi=None, extras=(), comm=None):
    m, r = a.shape
    p, _, c = w.shape
    npc = c // tn
    nk = r // tk
    tile = pl.BlockSpec((tm, tn), lambda i, j, k: (i, j))
    return _hosted_call(
        _mm_body("nn", nk, len(extras), len(out_dtypes), epi), (m // tm, p * npc, nk),
        [pl.BlockSpec((tm, tk), lambda i, j, k: (i, k)),
         pl.BlockSpec((None, tk, tn), lambda i, j, k: (j // npc, k, j % npc))]
        + [tile if e.shape[0] > 1 else pl.BlockSpec((1, tn), lambda i, j, k: (0, j)) for e in extras],
        [tile] * len(out_dtypes), [jax.ShapeDtypeStruct((m, p * c), d) for d in out_dtypes],
        [pltpu.VMEM((tm, tn), F32)] if nk > 1 else [], (a, w, *extras),
        ("parallel", "parallel", "arbitrary"), name, comm)


def _mm_nt(a, w, out_dtypes, *, tm, tn, tk, name, epi=None, extras=(), comm=None):
    m, _ = a.shape
    p, r, c = w.shape
    kpc = c // tk
    nk = p * kpc
    tile = pl.BlockSpec((tm, tn), lambda i, j, k: (i, j))
    return _hosted_call(
        _mm_body("nt", nk, len(extras), len(out_dtypes), epi), (m // tm, r // tn, nk),
        [pl.BlockSpec((tm, tk), lambda i, j, k: (i, k)),
         pl.BlockSpec((None, tn, tk), lambda i, j, k: (k // kpc, j, k % kpc))] + [tile] * len(extras),
        [tile] * len(out_dtypes), [jax.ShapeDtypeStruct((m, r), d) for d in out_dtypes],
        [pltpu.VMEM((tm, tn), F32)] if nk > 1 else [], (a, w, *extras),
        ("parallel", "parallel", "arbitrary"), name, comm)


def _mm_tn(a, g, p, *, tm, tn, tk, name):
    t, r = a.shape
    c = g.shape[1] // p
    npc = c // tn
    nk = t // tk
    body = _mm_body("tn", nk, 0, 1, None)
    return pl.pallas_call(
        body, name=name,
        grid=(r // tm, p * npc, nk),
        in_specs=[pl.BlockSpec((tk, tm), lambda i, j, k: (k, i)),
                  pl.BlockSpec((tk, tn), lambda i, j, k: (k, j))],
        out_specs=[pl.BlockSpec((None, tm, tn), lambda i, j, k: (j // npc, i, j % npc))],
        out_shape=[jax.ShapeDtypeStruct((p, r, c), F32)],
        scratch_shapes=[pltpu.VMEM((tm, tn), F32)] if nk > 1 else [],
        compiler_params=_params(("parallel", "parallel", "arbitrary")),
    )(a, g)[0]


def _rows(fn, row_ins, const_ins, row_outs, acc_outs, *, tm, name, comm=None):
    specs, arrays = [], []
    t = None
    for item in row_ins:
        if isinstance(item, tuple):
            arr, width, cb = item
            specs.append(pl.BlockSpec((tm, width), functools.partial(lambda i, cb: (i, cb), cb=cb)))
        else:
            arr = item
            specs.append(pl.BlockSpec((tm, arr.shape[1]), lambda i: (i, 0)))
        arrays.append(arr)
        t = arr.shape[0]
    for arr in const_ins:
        specs.append(pl.BlockSpec(arr.shape, lambda i: (0, 0)))
        arrays.append(arr)
    n_in, n_row, n_acc = len(arrays), len(row_outs), len(acc_outs)

    def body(*refs):
        ins = [r[...] for r in refs[:n_in]]
        outs = refs[n_in:]
        row_res, acc_res = fn(ins[:len(row_ins)], ins[len(row_ins):])
        for o, r in zip(outs[:n_row], row_res):
            o[...] = r.astype(o.dtype)
        if n_acc:
            i = pl.program_id(0)

            @pl.when(i == 0)
            def _():
                for o in outs[n_row:]:
                    o[...] = jnp.zeros_like(o)

            for o, r in zip(outs[n_row:], acc_res):
                o[...] += r

    res = _hosted_call(
        body, (t // tm,), specs,
        [pl.BlockSpec((tm, c), lambda i: (i, 0)) for c, _ in row_outs]
        + [pl.BlockSpec((1, c), lambda i: (0, 0)) for c in acc_outs],
        [jax.ShapeDtypeStruct((t, c), d) for c, d in row_outs] + [jax.ShapeDtypeStruct((1, c), F32) for c in acc_outs],
        [], arrays, ("arbitrary",), name, comm)
    if comm:
        return res[:n_row], res[n_row:n_row + n_acc], res[n_row + n_acc:]
    return res[:n_row], res[n_row:]


def _rstd(x):
    return lax.rsqrt(jnp.mean(x * x, axis=-1, keepdims=True) + NORM_EPS)


def _rms_bwd(x, w, dy):
    r = _rstd(x)
    n = x * r
    dn = dy * w
    dx = r * (dn - n * jnp.mean(dn * n, axis=-1, keepdims=True))
    return dx, dy * n


def _colsum(x):
    return jnp.sum(x, axis=0, keepdims=True)


def _heads_map(fn, width, *tiles):
    outs = None
    for h in range(width // HEAD):
        res = fn(*[t[:, h * HEAD:(h + 1) * HEAD] for t in tiles])
        if outs is None:
            outs = [[] for _ in res]
        for lst, r in zip(outs, res):
            lst.append(r)
    return [jnp.concatenate(lst, axis=1) for lst in outs]


def _log_one_minus_beta(z):
    return -(jnp.maximum(z, 0.0) + jnp.log(1.0 + jnp.exp(-jnp.abs(z))))


def _attn_fwd(proj, after_tri, norm_w, n_heads, name, comm=None):
    t = proj.shape[0]
    blk = min(ATTN_BLOCK, t)
    qb = min(ATTN_ROWS, t)
    ns = qb // blk
    scale = HEAD ** -0.5

    def body(q_ref, k_ref, v_ref, tri_ref, w_ref, o_ref, mix_ref, tot_ref):
        i = pl.program_id(1)
        q = (q_ref[...] * scale).astype(BF16)
        tri = tri_ref[...]

        def part(r0, j, acc_l, acc_o, masked):
            sl = pl.ds(pl.multiple_of(j * blk, blk), blk)
            m = qb - r0
            z = _dot_nt(q[r0:, :], k_ref[sl, :].astype(BF16))
            lm = _log_one_minus_beta(z)
            if masked:
                mask = lax.broadcasted_iota(jnp.int32, (m, blk), 1) < lax.broadcasted_iota(jnp.int32, (m, blk), 0)
                lmm = jnp.where(mask, lm, 0.0)
            else:
                lmm = lm
            w = jnp.exp(z + lm + acc_l[r0:, :] + _dot(lmm.astype(BF16), tri))
            if masked:
                w = jnp.where(mask, w, 0.0)
            new_l = acc_l[r0:, :] + jnp.sum(lmm, axis=1, keepdims=True)
            new_o = acc_o[r0:, :] + _dot(w.astype(BF16), v_ref[sl, :].astype(BF16))
            if r0:
                new_l = jnp.concatenate([acc_l[:r0, :], new_l], axis=0)
                new_o = jnp.concatenate([acc_o[:r0, :], new_o], axis=0)
            return new_l, new_o

        acc = (jnp.zeros((qb, 1), F32), jnp.zeros((qb, HEAD), F32))
        for jr in reversed(range(ns)):
            acc = part(jr * blk, ns * i + jr, *acc, True)

        def more(c):
            return (c[0] < ns * i) & (jnp.max(c[1]) > ATTN_DEAD)

        def step(c):
            return (c[0] + 1,) + part(0, ns * i - 1 - c[0], c[1], c[2], False)

        swept, acc_l, acc_o = lax.while_loop(more, step, (jnp.int32(0),) + acc)
        o_ref[...] = acc_o
        mix_ref[...] = (acc_o * _rstd(acc_o) * w_ref[...]).astype(BF16)
        first = (ns * i - swept).astype(F32)
        tot_ref[...] = jnp.where(lax.broadcasted_iota(jnp.int32, (qb, HEAD), 1) == 1, first, acc_l)

    width = n_heads * HEAD
    qblk = pl.BlockSpec((qb, HEAD), lambda h, i: (i, h))
    return _hosted_call(
        body, (n_heads, t // qb),
        [qblk,
         pl.BlockSpec((t, HEAD), lambda h, i: (0, n_heads + h)),
         pl.BlockSpec((t, HEAD), lambda h, i: (0, 2 * n_heads + h)),
         pl.BlockSpec((blk, blk), lambda h, i: (0, 0)),
         pl.BlockSpec((1, HEAD), lambda h, i: (0, 0))],
        [qblk, qblk, qblk],
        [jax.ShapeDtypeStruct((t, width), F32), jax.ShapeDtypeStruct((t, width), BF16),
         jax.ShapeDtypeStruct((t, width), F32)],
        [], (proj, proj, proj, after_tri, norm_w), ("parallel", "arbitrary"), name, comm)


def _attn_bwd(proj, tot, do, after_tri, before_tri, n_heads, name, comm=None):
    t = proj.shape[0]
    blk = min(ATTN_BLOCK, t)
    qb = min(ATTN_ROWS, t)
    ns = qb // blk
    scale = HEAD ** -0.5

    def body(q_ref, k_ref, v_ref, tot_ref, do_ref, after_ref, before_ref, dq_ref, dk_ref, dv_ref):
        i = pl.program_id(1)

        @pl.when(i == 0)
        def _():
            dk_ref[...] = jnp.zeros_like(dk_ref)
            dv_ref[...] = jnp.zeros_like(dv_ref)

        q = (q_ref[...] * scale).astype(BF16)
        dob = do_ref[...].astype(BF16)
        total = tot_ref[:, 0:1]
        after_tri = after_ref[...]
        before = before_ref[...]

        def part(r0, j, seen_l, seen_g, dq, masked):
            sl = pl.ds(pl.multiple_of(j * blk, blk), blk)
            m = qb - r0
            qq, dd = q[r0:, :], dob[r0:, :]
            kb = k_ref[sl, :].astype(BF16)
            z = _dot_nt(qq, kb)
            dw = _dot_nt(dd, v_ref[sl, :].astype(BF16))
            lm = _log_one_minus_beta(z)
            if masked:
                mask = lax.broadcasted_iota(jnp.int32, (m, blk), 1) < lax.broadcasted_iota(jnp.int32, (m, blk), 0)
                lmm = jnp.where(mask, lm, 0.0)
            else:
                lmm = lm
            row_l = jnp.sum(lmm, axis=1, keepdims=True)
            after = (total[r0:, :] - seen_l[r0:, :] - row_l) + _dot(lmm.astype(BF16), after_tri)
            w = jnp.exp(z + lm + after)
            if masked:
                w = jnp.where(mask, w, 0.0)
            sig = jnp.exp(z + lm)
            g = w * dw
            g_before = seen_g[r0:, :] + _dot(g.astype(BF16), before)
            dz = g * (1.0 - sig) - g_before * sig
            if masked:
                dz = jnp.where(mask, dz, 0.0)
            dzb = dz.astype(BF16)
            dk_ref[sl, :] += _dot_tn(dzb, qq)
            dv_ref[sl, :] += _dot_tn(w.astype(BF16), dd)
            new = (seen_l[r0:, :] + row_l,
                   seen_g[r0:, :] + jnp.sum(g, axis=1, keepdims=True), dq[r0:, :] + _dot(dzb, kb))
            if r0:
                new = tuple(jnp.concatenate([old[:r0, :], n], axis=0) for old, n in zip((seen_l, seen_g, dq), new))
            return new

        zero = jnp.zeros((qb, 1), F32)
        first = jnp.max(tot_ref[0:8, 1:2]).astype(jnp.int32)
        carry = lax.fori_loop(first, ns * i, lambda j, c: part(0, j, *c, False),
                              (zero, zero, jnp.zeros((qb, HEAD), F32)))
        for jr in range(ns):
            carry = part(jr * blk, ns * i + jr, *carry, True)
        dq_ref[...] = (carry[2] * scale).astype(dq_ref.dtype)

    width = n_heads * HEAD
    qblk = pl.BlockSpec((qb, HEAD), lambda h, i: (i, h))
    full = pl.BlockSpec((t, HEAD), lambda h, i: (0, h))
    tri = pl.BlockSpec((blk, blk), lambda h, i: (0, 0))
    return _hosted_call(
        body, (n_heads, t // qb),
        [qblk,
         pl.BlockSpec((t, HEAD), lambda h, i: (0, n_heads + h)),
         pl.BlockSpec((t, HEAD), lambda h, i: (0, 2 * n_heads + h)),
         qblk, qblk, tri, tri],
        [qblk, full, full], [jax.ShapeDtypeStruct((t, width), BF16)] + [jax.ShapeDtypeStruct((t, width), F32)] * 2,
        [], (proj, proj, proj, tot, do, after_tri, before_tri), ("parallel", "arbitrary"), name, comm)


def _lower_bound(logits):
    l0, l1 = logits[0:1, :], logits[1:2, :]
    mx = jnp.maximum(l0, l1)
    e0, e1 = jnp.exp(l0 - mx), jnp.exp(l1 - mx)
    return e0 / (e0 + e1)


def _hg_decay(q, k, g, sums):
    cum, mid, last = [_dot_split(m, g) for m in sums]
    return cum, mid, last, q * jnp.exp(cum - mid), k * jnp.exp(mid - cum), q * jnp.exp(cum), k * jnp.exp(last - cum)


def _hg_sums(rows, chunk):
    t = lax.broadcasted_iota(jnp.int32, (rows, rows), 0)
    j = lax.broadcasted_iota(jnp.int32, (rows, rows), 1)
    same = (t // chunk) == (j // chunk)
    mats = [same & (j <= t), same & (j % chunk < chunk // 2), same, same & (j >= t)]
    return [m.astype(BF16) for m in mats]


def _hgrn_fwd(proj, lb_logits, norm_w, sums, n_heads, heads_per_step, name, comm=None):
    t = proj.shape[0]
    bt = min(HG_ROWS, t)
    c = HG_CHUNK
    nc = bt // c
    hw = heads_per_step * HEAD
    width = n_heads * HEAD
    col0 = 3 * width // hw

    def body(hq_ref, hf_ref, hi_ref, hgate_ref, lbl_ref, w_ref, s0_ref, s1_ref, s2_ref, o_ref, mix_ref, st_ref,
             state, qt_scr, kt_scr, qe_scr, kd_scr, el_scr):
        @pl.when(pl.program_id(1) == 0)
        def _():
            state[...] = jnp.zeros_like(state)

        lb = _lower_bound(lbl_ref[...])
        f = hf_ref[...]
        hq = hq_ref[...]
        _, _, last, qt, kt, qe, kd = _hg_decay(hq * _sigmoid(hq), (1.0 - lb) * _sigmoid(-f),
                                               jnp.log(lb + (1.0 - lb) * _sigmoid(f)),
                                               (s0_ref[...], s1_ref[...], s2_ref[...]))
        qt_scr[...] = qt.astype(BF16)
        kt_scr[...] = kt.astype(BF16)
        qe_scr[...] = qe.astype(BF16)
        kd_scr[...] = kd.astype(BF16)
        el_scr[...] = jnp.exp(last)
        causal = lax.broadcasted_iota(jnp.int32, (c, c), 1) <= lax.broadcasted_iota(jnp.int32, (c, c), 0)

        def chunk(ci, carry):
            r = pl.ds(pl.multiple_of(ci * c, c), c)
            vc = hi_ref[r, :].astype(BF16)
            qt, kt, qe, kd = qt_scr[r, :], kt_scr[r, :], qe_scr[r, :], kd_scr[r, :]
            e_last = el_scr[r, :][0:1, :]
            old = [state[h] for h in range(heads_per_step)]
            scores, inter, outs, new = [], [], [], []
            for h in range(heads_per_step):
                cs = slice(h * HEAD, (h + 1) * HEAD)
                scores.append(jnp.where(causal, _dot_nt(qt[:, cs], kt[:, cs]), 0.0).astype(BF16))
                inter.append(_dot_nt(qe[:, cs], old[h].astype(BF16)))
                new.append(old[h] * e_last[:, cs] + _dot_tn(vc[:, cs], kd[:, cs]))
            for h in range(heads_per_step):
                outs.append(_dot(scores[h], vc[:, h * HEAD:(h + 1) * HEAD]) + inter[h])
            for h in range(heads_per_step):
                st_ref[ci, :, h * HEAD:(h + 1) * HEAD] = old[h]
                state[h] = new[h]
            o_ref[r, :] = jnp.concatenate(outs, axis=1)
            return carry

        lax.fori_loop(0, nc, chunk, 0, unroll=HG_UNROLL)

        def finish(o, gate):
            return ((o * _rstd(o) * w_ref[...]) * (gate * _sigmoid(gate)),)

        mix_ref[...] = _heads_map(finish, hw, o_ref[...], hgate_ref[...])[0].astype(BF16)

    def col(group):
        return pl.BlockSpec((bt, hw), functools.partial(lambda hp, tb, g: (tb, col0 + g * (width // hw) + hp), g=group))

    blk = pl.BlockSpec((bt, hw), lambda hp, tb: (tb, hp))
    mat = pl.BlockSpec((bt, bt), lambda hp, tb: (0, 0))
    return _hosted_call(
        body, (n_heads // heads_per_step, t // bt),
        [col(0), col(1), col(2), col(3),
         pl.BlockSpec((2, hw), lambda hp, tb: (0, hp)),
         pl.BlockSpec((1, HEAD), lambda hp, tb: (0, 0)), mat, mat, mat],
        [blk, blk, pl.BlockSpec((nc, HEAD, hw), lambda hp, tb: (tb, 0, hp))],
        [jax.ShapeDtypeStruct((t, width), F32), jax.ShapeDtypeStruct((t, width), BF16),
         jax.ShapeDtypeStruct((t // c, HEAD, width), F32)],
        [pltpu.VMEM((heads_per_step, HEAD, HEAD), F32)] + [pltpu.VMEM((bt, hw), BF16)] * 4
        + [pltpu.VMEM((bt, hw), F32)],
        (proj, proj, proj, proj, lb_logits, norm_w, *sums[:3]), ("parallel", "arbitrary"), name, comm)


def _hgrn_bwd(proj, do, states, lb_logits, sums, n_heads, heads_per_step, name):
    t = proj.shape[0]
    bt = min(HG_ROWS, t)
    c = HG_CHUNK
    nc = bt // c
    nb = t // bt
    hw = heads_per_step * HEAD
    width = n_heads * HEAD
    col0 = 3 * width // hw

    def body(hq_ref, hf_ref, hi_ref, do_ref, st_ref, lbl_ref, s0_ref, s1_ref, s2_ref, s3_ref,
             dq_ref, df_ref, di_ref, dlb_ref,
             dstate, qth, qtl, kth, ktl, qe_scr, kd_scr, el_scr, qa_scr, ka_scr, qb_scr, kb_scr, ss_scr):
        @pl.when(pl.program_id(1) == 0)
        def _():
            dstate[...] = jnp.zeros_like(dstate)
            dlb_ref[...] = jnp.zeros_like(dlb_ref)

        lb = _lower_bound(lbl_ref[...])
        f = hf_ref[...]
        sg = _sigmoid(f)
        sgn = _sigmoid(-f)
        den = lb + (1.0 - lb) * sg
        kk = (1.0 - lb) * sgn
        hq = hq_ref[...]
        sq = _sigmoid(hq)
        qq = hq * sq
        cum, mid, last, qt, kt, qe, kd = _hg_decay(qq, kk, jnp.log(den), (s0_ref[...], s1_ref[...], s2_ref[...]))
        qth[...], qtl[...] = _hilo(qt)
        kth[...], ktl[...] = _hilo(kt)
        qe_scr[...] = qe.astype(BF16)
        kd_scr[...] = kd.astype(BF16)
        e_last = jnp.exp(last)
        el_scr[...] = e_last
        causal = lax.broadcasted_iota(jnp.int32, (c, c), 1) <= lax.broadcasted_iota(jnp.int32, (c, c), 0)

        def chunk(cc, carry):
            ci = nc - 1 - cc
            r = pl.ds(pl.multiple_of(ci * c, c), c)
            qt = (qth[r, :], qtl[r, :])
            kt = (kth[r, :], ktl[r, :])
            qe, kd = qe_scr[r, :], kd_scr[r, :]
            doc, vc = do_ref[r, :].astype(BF16), hi_ref[r, :].astype(BF16)
            e_row = el_scr[r, :][0:1, :]
            sts = [st_ref[ci, :, h * HEAD:(h + 1) * HEAD] for h in range(heads_per_step)]
            dsts = [dstate[h] for h in range(heads_per_step)]
            a, da, di, dq_inter, dk_inter, dq_intra, dk_intra, st_sums, new = [], [], [], [], [], [], [], [], []
            for h in range(heads_per_step):
                cs = slice(h * HEAD, (h + 1) * HEAD)
                st, dst = sts[h].astype(BF16), dsts[h].astype(BF16)
                a.append(jnp.where(causal, _dot_nt(qt[0][:, cs], kt[0][:, cs]), 0.0).astype(BF16))
                da.append(jnp.where(causal, _dot_nt(doc[:, cs], vc[:, cs]), 0.0))
                di.append(_dot_nt(kd[:, cs], dst))
                dq_inter.append(_dot(doc[:, cs], st))
                dk_inter.append(_dot(vc[:, cs], dst))
                st_sums.append(_colsum(dsts[h] * sts[h]))
                new.append(dsts[h] * e_row[:, cs] + _dot_tn(doc[:, cs], qe[:, cs]))
            for h in range(heads_per_step):
                cs = slice(h * HEAD, (h + 1) * HEAD)
                da_h = _hilo(da[h])
                di[h] = di[h] + _dot_tn(a[h], doc[:, cs])
                dq_intra.append(_dot3(_dot, da_h, (kt[0][:, cs], kt[1][:, cs])))
                dk_intra.append(_dot3(_dot_tn, da_h, (qt[0][:, cs], qt[1][:, cs])))

            def wide(parts):
                return jnp.concatenate(parts, axis=1)

            for h in range(heads_per_step):
                dstate[h] = new[h]
            di_ref[r, :] = wide(di).astype(BF16)
            qa_scr[r, :] = wide(dq_intra)
            ka_scr[r, :] = wide(dk_intra)
            qb_scr[r, :] = wide(dq_inter)
            kb_scr[r, :] = wide(dk_inter)
            ss_scr[r, :] = jnp.broadcast_to(wide(st_sums), (c, hw))
            return carry

        lax.fori_loop(0, nc, chunk, 0, unroll=HG_UNROLL)

        dk_inter = kb_scr[...] * jnp.exp(last - cum)
        dq = qa_scr[...] * jnp.exp(cum - mid) + qb_scr[...] * jnp.exp(cum)
        dk = ka_scr[...] * jnp.exp(mid - cum) + dk_inter
        is_last = lax.broadcasted_iota(jnp.int32, (bt, hw), 0) % c == c - 1
        d_last = _dot_split(s2_ref[...], kk * dk_inter) + e_last * ss_scr[...]
        dcum = qq * dq - kk * dk + jnp.where(is_last, d_last, 0.0)
        e = (_dot_split(s3_ref[...], dcum) / den - dk) * sgn
        df_ref[...] = (e * (1.0 - lb) * sg).astype(BF16)
        dlb_ref[...] += _colsum(e)
        dq_ref[...] = (dq * (sq * (1.0 + hq * (1.0 - sq)))).astype(BF16)

    def col(group):
        return pl.BlockSpec((bt, hw), functools.partial(
            lambda hp, tb, g: (nb - 1 - tb, col0 + g * (width // hw) + hp), g=group))

    blk = pl.BlockSpec((bt, hw), lambda hp, tb: (nb - 1 - tb, hp))
    mat = pl.BlockSpec((bt, bt), lambda hp, tb: (0, 0))
    return pl.pallas_call(
        body, name=name,
        grid=(n_heads // heads_per_step, nb),
        in_specs=[col(0), col(1), col(2), blk,
                  pl.BlockSpec((nc, HEAD, hw), lambda hp, tb: (nb - 1 - tb, 0, hp)),
                  pl.BlockSpec((2, hw), lambda hp, tb: (0, hp)), mat, mat, mat, mat],
        out_specs=[blk, blk, blk, pl.BlockSpec((1, hw), lambda hp, tb: (0, hp))],
        out_shape=[jax.ShapeDtypeStruct((t, width), BF16)] * 3 + [jax.ShapeDtypeStruct((1, width), F32)],
        scratch_shapes=[pltpu.VMEM((heads_per_step, HEAD, HEAD), F32)] + [pltpu.VMEM((bt, hw), BF16)] * 6
                       + [pltpu.VMEM((bt, hw), F32)] * 6,
        compiler_params=_params(("parallel", "arbitrary")),
    )(proj, proj, proj, do, states, lb_logits, *sums)


def _place():
    x, y, c = lax.axis_index("x"), lax.axis_index("y"), lax.axis_index("c")
    chips = [(1 - x, y), (x, 1 - y), (1 - x, 1 - y)]
    return x, y, c, chips


ANY = pl.BlockSpec(memory_space=pl.ANY)


def _cast_to_slot(shard, ids, name):
    r, c = shard.shape
    tm = _pick(r, (256, 128, 64, 32, 16))

    def body(ids_ref, s_ref, o_ref):
        o_ref[...] = s_ref[...].astype(BF16)

    return pl.pallas_call(
        body, name=name,
        grid_spec=pltpu.PrefetchScalarGridSpec(
            num_scalar_prefetch=1, grid=(r // tm,),
            in_specs=[pl.BlockSpec((tm, c), lambda i, ids: (i, 0))],
            out_specs=pl.BlockSpec((None, tm, c), lambda i, ids: (ids[1], i, 0))),
        out_shape=jax.ShapeDtypeStruct((N_CHIPS, r, c), BF16),
        compiler_params=_params(("parallel",)),
    )(ids, shard)


def _gather_copies(bufs, send, recv):
    x, y, c, chips = _place()
    mine = 2 * x + y

    def half(ref, who, core):
        h = ref.shape[-2] // 2
        return ref.at[who, pl.ds(core * h, h), :]

    def copy(w, k, rows, to):
        return pltpu.make_async_remote_copy(src_ref=rows, dst_ref=rows, send_sem=send.at[6 * w + k],
                                            recv_sem=recv.at[6 * w + k], device_id=to, device_id_type=MESH)

    def to_chips(w):
        return [copy(w, k, half(bufs[w], mine, c), (qx, qy, c)) for k, (qx, qy) in enumerate(chips)]

    def to_sibling(w):
        return [copy(w, 3 + k, half(bufs[w], 2 * qx + qy, c), (x, y, 1 - c)) for k, (qx, qy) in enumerate(chips)]

    def begin():
        for w in range(len(bufs)):
            for cp in to_chips(w):
                cp.start()

    def end():
        for w in range(len(bufs)):
            for k, (qx, qy) in enumerate(chips):
                copy(w, k, half(bufs[w], 2 * qx + qy, c), (x, y, c)).wait_recv()
                to_sibling(w)[k].start()
        for w in range(len(bufs)):
            for k, (qx, qy) in enumerate(chips):
                copy(w, 3 + k, half(bufs[w], 2 * qx + qy, 1 - c), (x, y, c)).wait_recv()
        for w in range(len(bufs)):
            for cp in to_chips(w) + to_sibling(w):
                cp.wait_send()

    return begin, end


def _scatter_copies(ins, outs, send, recv):
    _, _, c, chips = _place()

    def copies():
        return [pltpu.make_async_remote_copy(
            src_ref=ins[w].at[2 * qx + qy], dst_ref=outs[w].at[k], send_sem=send.at[3 * w + k],
            recv_sem=recv.at[3 * w + k], device_id=(qx, qy, c), device_id_type=MESH)
            for w in range(len(ins)) for k, (qx, qy) in enumerate(chips)]

    def begin():
        for cp in copies():
            cp.start()

    def end():
        for cp in copies():
            cp.wait()

    return begin, end


def _swap_copies(ins, outs, send, recv):
    x, y, c, _ = _place()

    def copies():
        return [pltpu.make_async_remote_copy(
            src_ref=ins[w].at[:, pl.ds((1 - c) * (ins[w].shape[1] // 2), ins[w].shape[1] // 2), :], dst_ref=outs[w],
            send_sem=send.at[w], recv_sem=recv.at[w], device_id=(x, y, 1 - c), device_id_type=MESH)
            for w in range(len(ins))]

    def begin():
        for cp in copies():
            cp.start()

    def end():
        for cp in copies():
            cp.wait()

    return begin, end


def _sibling_swap(grads, name):
    n = len(grads)

    def body(*refs):
        begin, end = _swap_copies(refs[:n], refs[n:2 * n], *refs[2 * n:])
        begin()
        end()

    return pl.pallas_call(
        body, name=name, in_specs=[ANY] * n, out_specs=[ANY] * n,
        out_shape=[jax.ShapeDtypeStruct((g.shape[0], g.shape[1] // 2, g.shape[2]), g.dtype) for g in grads],
        scratch_shapes=[pltpu.SemaphoreType.DMA((n,)), pltpu.SemaphoreType.DMA((n,))],
    )(*grads)


def _sibling_join(fulls, name):
    n = len(fulls)

    def body(*refs):
        bufs = refs[n:2 * n]
        send, recv = refs[2 * n:]
        x, y, c, _ = _place()
        cps = []
        for w in range(n):
            h = bufs[w].shape[0] // 2
            rows = bufs[w].at[pl.ds(c * h, h), :]
            cps.append(pltpu.make_async_remote_copy(
                src_ref=rows, dst_ref=rows, send_sem=send.at[w], recv_sem=recv.at[w],
                device_id=(x, y, 1 - c), device_id_type=MESH))
            cps[-1].start()
        for w in range(n):
            h = bufs[w].shape[0] // 2
            theirs = bufs[w].at[pl.ds((1 - c) * h, h), :]
            pltpu.make_async_remote_copy(src_ref=theirs, dst_ref=theirs, send_sem=send.at[w], recv_sem=recv.at[w],
                                         device_id=(x, y, c), device_id_type=MESH).wait_recv()
        for cp in cps:
            cp.wait_send()

    return pl.pallas_call(
        body, name=name, in_specs=[ANY] * n, out_specs=[ANY] * n,
        out_shape=[jax.ShapeDtypeStruct(s.shape, s.dtype) for s in fulls],
        input_output_aliases={w: w for w in range(n)},
        scratch_shapes=[pltpu.SemaphoreType.DMA((n,)), pltpu.SemaphoreType.DMA((n,))],
    )(*fulls)


def _all_sum_small(vec, name):
    n = vec.shape[1]

    def body(v_ref, out_ref, buf, send, recv):
        x, y, c, _ = _place()
        me = 4 * x + 2 * y + c
        buf[me] = v_ref[...]
        peers = []
        for mask in range(1, 8):
            px = 1 - x if mask & 4 else x
            py = 1 - y if mask & 2 else y
            pc = 1 - c if mask & 1 else c
            peers.append((px, py, pc))
        cps = []
        for k, peer in enumerate(peers):
            cps.append(pltpu.make_async_remote_copy(src_ref=buf.at[me], dst_ref=buf.at[me], send_sem=send.at[k],
                                                    recv_sem=recv.at[k], device_id=peer, device_id_type=MESH))
            cps[-1].start()
        for k, (px, py, pc) in enumerate(peers):
            slot = buf.at[4 * px + 2 * py + pc]
            pltpu.make_async_remote_copy(src_ref=slot, dst_ref=slot, send_sem=send.at[k], recv_sem=recv.at[k],
                                         device_id=(x, y, c), device_id_type=MESH).wait_recv()
        for cp in cps:
            cp.wait_send()
        total = buf[0]
        for d in range(1, 8):
            total = total + buf[d]
        out_ref[...] = total

    vm = pl.BlockSpec(memory_space=pltpu.VMEM)
    return pl.pallas_call(
        body, name=name, in_specs=[vm], out_specs=vm,
        out_shape=jax.ShapeDtypeStruct(vec.shape, F32),
        scratch_shapes=[pltpu.VMEM((8, 8, n), F32), pltpu.SemaphoreType.DMA((7,)), pltpu.SemaphoreType.DMA((7,))],
    )(vec)


def _pair_sum(g, buf, ids, name):
    p, r, c = g.shape
    h = r // 2
    tr = _pick(h, (256, 128, 64, 32, 16))
    nh = h // tr

    def body(ids_ref, g_ref, b_ref, sums_ref, own_ref):
        s = g_ref[...] + b_ref[...]
        sums_ref[...] = s.astype(BF16)

        @pl.when(pl.program_id(1) == ids_ref[1])
        def _():
            own_ref[...] = s

    return pl.pallas_call(
        body, name=name,
        grid_spec=pltpu.PrefetchScalarGridSpec(
            num_scalar_prefetch=1, grid=(nh, p),
            in_specs=[pl.BlockSpec((None, tr, c), lambda i, q, ids: (q, ids[0] * nh + i, 0)),
                      pl.BlockSpec((None, tr, c), lambda i, q, ids: (q, i, 0))],
            out_specs=[pl.BlockSpec((None, tr, c), lambda i, q, ids: (q, i, 0)),
                       pl.BlockSpec((tr, c), lambda i, q, ids: (i, 0))]),
        out_shape=[jax.ShapeDtypeStruct((p, h, c), BF16), jax.ShapeDtypeStruct((h, c), F32)],
        compiler_params=_params(("parallel", "arbitrary")),
    )(ids, g, buf)


def _final_sum(own, others, ids, name):
    h, c = own.shape
    tr = _pick(h, (256, 128, 64, 32, 16))
    nh = h // tr

    def body(ids_ref, own_ref, oth_ref, out_ref):
        s = own_ref[...]
        for k in range(3):
            s = s + oth_ref[k].astype(F32)
        out_ref[...] = s

    return pl.pallas_call(
        body, name=name,
        grid_spec=pltpu.PrefetchScalarGridSpec(
            num_scalar_prefetch=1, grid=(nh,),
            in_specs=[pl.BlockSpec((tr, c), lambda i, ids: (i, 0)),
                      pl.BlockSpec((3, tr, c), lambda i, ids: (0, i, 0))],
            out_specs=pl.BlockSpec((tr, c), lambda i, ids: (ids[0] * nh + i, 0))),
        out_shape=jax.ShapeDtypeStruct((2 * h, c), F32),
        compiler_params=_params(("parallel",)),
    )(ids, own, others)


def _adamw(w, g, m, v, name):
    r, c = w.shape
    tm = _pick(r, (256, 128, 64, 32, 16, 8)) if r >= 8 else r

    def fn(rows, _):
        w_, g_, m_, v_ = rows
        m2 = ADAM_B1 * m_ + (1.0 - ADAM_B1) * g_
        v2 = ADAM_B2 * v_ + (1.0 - ADAM_B2) * (g_ * g_)
        m_hat = m2 / (1.0 - ADAM_B1 ** ADAM_STEP)
        v_hat = v2 / (1.0 - ADAM_B2 ** ADAM_STEP)
        delta = -ADAM_LR * (m_hat / (jnp.sqrt(v_hat) + ADAM_EPS) + ADAM_WD * w_)
        return [delta, m2, v2], []

    outs, _ = _rows(fn, [w, g, m, v], [], [(c, F32)] * 3, [], tm=tm, name=name)
    return outs


def kernel(x, attn_norm_w, w_in, lb_logits, sb_norm_w, hg_norm_w, w_out, mlp_norm_w, w_up, w_down, final_norm_w, loss_target, m_attn_norm_w, m_w_in, m_lb_logits, m_sb_norm_w, m_hg_norm_w, m_w_out, m_mlp_norm_w, m_w_up, m_w_down, m_final_norm_w, v_attn_norm_w, v_w_in, v_lb_logits, v_sb_norm_w, v_hg_norm_w, v_w_out, v_mlp_norm_w, v_w_up, v_w_down, v_final_norm_w):
    xs, tgt = x[0], loss_target[0]
    t, d = xs.shape
    width = d // 2
    n_heads = width // HEAD
    hps = min(8, n_heads)
    final_w = final_norm_w.reshape(1, d)
    tm_rows = _pick(t, (256, 128))
    tm = _pick(t, (1024, 512, 256))
    blk = min(ATTN_BLOCK, t)
    ones_a = jnp.ones((blk, blk), F32)
    after_tri = jnp.tril(ones_a, -1).astype(BF16)
    before_tri = jnp.triu(ones_a, 1).astype(BF16)
    hg_sums = _hg_sums(min(HG_ROWS, t), HG_CHUNK)
    cx, cy, cc = lax.axis_index("x"), lax.axis_index("y"), lax.axis_index("c")
    ids = jnp.stack([cc, 2 * cx + cy]).astype(jnp.int32)

    shards = [w_in[0], w_out[0], w_up[0], w_down[0]]
    cast = [_cast_to_slot(s, ids, f"cast_w{i}") for i, s in enumerate(shards)]
    d_ff = N_CHIPS * w_up.shape[2]
    cs_in, cs_up = w_in.shape[2], w_up.shape[2]
    tn_in = _pick(cs_in, (1792, 896, 512, 256, 128))
    tn_up = _pick(cs_up, (1024, 512, 256))
    tn_d = _pick(d, (1024, 512, 256))
    tk_d = _pick(d, (2048, 1024, 512))

    (u,), _, (g_in,) = _rows(lambda r, c_: ([r[0] * _rstd(r[0]) * c_[0]], []), [xs], [attn_norm_w], [(d, BF16)], [],
                             tm=tm_rows, name="norm_in", comm=("gather", cast[:1]))
    proj, g_out = _mm_nn(u, g_in, [F32], tm=_pick(t, (512, 256)), tn=tn_in, tk=tk_d, name="proj_in",
                         comm=("gather", cast[1:2]))
    o_a, mix_a, sb_tot, g_down = _attn_fwd(proj, after_tri, sb_norm_w, n_heads, "sb_fwd", comm=("gather", cast[3:]))
    w_out_all = g_out.reshape(1, d, d)
    w_down_all = g_down.reshape(1, d_ff, d)
    o_b, mix_b, states, g_up = _hgrn_fwd(proj, lb_logits, hg_norm_w, hg_sums, n_heads, hps, "hg_fwd",
                                         comm=("gather", cast[2:3]))
    mix = jnp.concatenate([mix_a, mix_b], axis=1)
    def out_and_norm(acc, res, w):
        hh = acc + res
        return hh, hh * _rstd(hh) * w

    h1, mn = _mm_nn(mix, w_out_all, [F32, BF16], tm=_pick(t, (512, 256)), tn=d, tk=tk_d, name="proj_out",
                    epi=out_and_norm, extras=(xs, mlp_norm_w))
    up_b, act = _mm_nn(mn, g_up, [BF16, BF16], tm=tm, tn=tn_up, tk=tk_d, name="mlp_up",
                       epi=lambda acc: (acc, jnp.square(jnp.maximum(acc, 0.0))))
    (h2,) = _mm_nn(act, w_down_all, [F32], tm=tm, tn=tn_d, tk=_pick(d_ff, (2048, 1024)), name="mlp_down",
                   epi=lambda acc, res: (acc + res,), extras=(h1,))

    def head(rows, consts):
        hh, tg = rows
        w = consts[0]
        n = hh * _rstd(hh)
        err = n * w - tg
        dhh, dw_rows = _rms_bwd(hh, w, err * (1.0 / d))
        return [dhh, dhh], [_colsum(dw_rows), _colsum(err * err)]

    (dh2, dh2_b), (g_final, loss_cols) = _rows(head, [h2, tgt], [final_w], [(d, F32), (d, BF16)], [d, d],
                                                 tm=tm_rows, name="loss_head")

    (dup,) = _mm_nt(dh2_b, w_down_all, [BF16], tm=tm, tn=_pick(d_ff, (1024, 512)), tk=tk_d, name="mlp_down_dx",
                    epi=lambda acc, upv: (acc * (2.0 * jnp.maximum(upv.astype(F32), 0.0)),), extras=(up_b,))
    gw_down = _mm_tn(act, dh2_b, 1, tm=_pick(d_ff, (1024, 512)), tn=tn_d, tk=_pick(t, (2048, 1024, 512, 256)),
                     name="mlp_down_dw")
    gw_down = gw_down.reshape(N_CHIPS, d_ff // N_CHIPS, d)
    dmn, their_down = _mm_nt(dup, g_up, [F32], tm=tm, tn=tn_d, tk=_pick(cs_up, (2048, 1024, 512)), name="mlp_up_dx",
                             comm=("swap", [gw_down]))
    gw_up = _mm_tn(mn, dup, N_CHIPS, tm=tn_d, tn=tn_up, tk=_pick(t, (2048, 1024, 512, 256)), name="mlp_up_dw")

    def norm_back(rows, consts):
        xx, dy, skip = rows
        dx, dw_rows = _rms_bwd(xx, consts[0], dy)
        tot = dx + skip
        return [tot, tot], [_colsum(dw_rows)]

    (dh1, dh1_b), (g_mlp_norm,), (their_up,) = _rows(norm_back, [h1, dmn, dh2], [mlp_norm_w], [(d, F32), (d, BF16)],
                                                     [d], tm=tm_rows, name="norm_mlp_bwd", comm=("swap", [gw_up]))

    gw_out = _mm_tn(mix, dh1_b, 1, tm=tn_d, tn=tn_d, tk=_pick(t, (2048, 1024, 512, 256)), name="proj_out_dw")
    gw_out = gw_out.reshape(N_CHIPS, d // N_CHIPS, d)
    dmix, their_out = _mm_nt(dh1_b, w_out_all, [F32], tm=tm, tn=tn_d, tk=tk_d, name="proj_out_dx",
                             comm=("swap", [gw_out]))
    pair_mlp = [_pair_sum(g, b, ids, "grads_pair_sum_" + nm)
                for g, b, nm in ((gw_up, their_up, "up"), (gw_down, their_down, "down"))]

    def sb_norm_back(rows, consts):
        dx, dw_rows = _heads_map(lambda o, dy: _rms_bwd(o, consts[0], dy), width, *rows)
        dw = sum(_colsum(dw_rows[:, h * HEAD:(h + 1) * HEAD]) for h in range(n_heads))
        return [dx], [dw]

    (do_a,), (g_sb_norm,) = _rows(sb_norm_back, [o_a, (dmix, width, 0)], [sb_norm_w], [(width, BF16)], [HEAD],
                                  tm=tm_rows, name="sb_norm_bwd")
    dq_a, dk_a, dv_a, *landed_mlp = _attn_bwd(proj, sb_tot, do_a, after_tri, before_tri, n_heads, "sb_bwd",
                                              comm=("scatter", [p[0] for p in pair_mlp]))

    def hg_out_back(rows, consts):
        def one(o, gate, dy):
            sg = _sigmoid(gate)
            silu = gate * sg
            n = o * _rstd(o) * consts[0]
            do, dw_rows = _rms_bwd(o, consts[0], dy * silu)
            return do, dy * n * (sg * (1.0 + gate * (1.0 - sg))), dw_rows
        do, dgate, dw_rows = _heads_map(one, width, *rows)
        dw = sum(_colsum(dw_rows[:, h * HEAD:(h + 1) * HEAD]) for h in range(n_heads))
        return [do, dgate], [dw]

    (do_b, dgate), (g_hg_norm,) = _rows(hg_out_back, [o_b, (proj, width, 6), (dmix, width, 1)], [hg_norm_w],
                                         [(width, BF16)] * 2, [HEAD], tm=tm_rows, name="hg_out_bwd")
    dhq, dhf, dhi, dlb = _hgrn_bwd(proj, do_b, states, lb_logits, hg_sums, n_heads, hps, "hg_bwd")

    (dproj,), _ = _rows(lambda r, _c: ([jnp.concatenate([p.astype(BF16) for p in r], axis=1)], []),
                        [dq_a, dk_a, dv_a, dhq, dhf, dhi, dgate], [], [(7 * width, BF16)], [],
                        tm=tm_rows, name="pack_dproj")
    gw_in = _mm_tn(u, dproj, N_CHIPS, tm=tn_d, tn=tn_in, tk=_pick(t, (2048, 1024, 512, 256)), name="proj_in_dw")
    (their_in,) = _sibling_swap([gw_in], "grads_to_sibling_in")
    pair_mix = [_pair_sum(g, b, ids, "grads_pair_sum_" + nm)
                for g, b, nm in ((gw_in, their_in, "in"), (gw_out, their_out, "out"))]
    du, *landed_mix = _mm_nt(dproj, g_in, [F32], tm=tm, tn=tn_d, tk=_pick(cs_in, (1792, 896, 512, 256, 128)),
                             name="proj_in_dx", comm=("scatter", [p[0] for p in pair_mix]))
    (dx,), (g_attn_norm,) = _rows(lambda r, c_: (lambda dxx, dwr: ([dxx + r[2]], [_colsum(dwr)]))(
        *_rms_bwd(r[0], c_[0], r[1])), [xs, du, dh1], [attn_norm_w], [(d, F32)], [d], tm=tm_rows, name="norm_in_bwd")

    halves = [_final_sum(p[1], r, ids, f"grads_final_sum{i}")
              for i, (p, r) in enumerate(zip(pair_mix + pair_mlp, list(landed_mix) + list(landed_mlp)))]
    g_w_in, g_w_out, g_w_up, g_w_down = _sibling_join(halves, "grads_join")

    pieces = [g_attn_norm, g_mlp_norm, g_final, g_sb_norm, g_hg_norm, dlb, loss_cols]
    sizes = [p.shape[1] for p in pieces]
    flat = jnp.concatenate(pieces, axis=1)
    n_small = -(-flat.shape[1] // 1024) * 1024
    flat = jnp.pad(flat, ((0, 0), (0, n_small - flat.shape[1]))).reshape(8, n_small // 8)
    flat = _all_sum_small(flat, "small_all_sum").reshape(1, n_small)
    offs = [sum(sizes[:i]) for i in range(len(sizes))]
    g_attn_norm, g_mlp_norm, g_final, g_sb_norm, g_hg_norm, dlb, loss_cols = [
        flat[:, o:o + s] for o, s in zip(offs, sizes)]

    def small_tail(lbl_ref, dlb_ref, loss_ref, glb_ref, out_ref):
        lb = _lower_bound(lbl_ref[...])
        g0 = dlb_ref[...] * lb * (1.0 - lb)
        glb_ref[0:1, :] = g0
        glb_ref[1:2, :] = -g0
        out_ref[...] = jnp.zeros_like(out_ref) + 0.5 * jnp.sum(loss_ref[...]) * (1.0 / d)

    vm = pl.BlockSpec(memory_space=pltpu.VMEM)
    g_lb, loss11 = pl.pallas_call(
        small_tail, name="small_tail", in_specs=[vm, vm, vm], out_specs=[vm, vm],
        out_shape=[jax.ShapeDtypeStruct(lb_logits.shape, F32), jax.ShapeDtypeStruct((1, 128), F32)],
    )(lb_logits, dlb, loss_cols)
    loss = loss11[0, 0]

    names = ["attn_norm_w", "w_in", "lb_logits", "sb_norm_w", "hg_norm_w", "w_out", "mlp_norm_w", "w_up", "w_down",
             "final_norm_w"]
    ws = [attn_norm_w, w_in[0], lb_logits, sb_norm_w, hg_norm_w, w_out[0], mlp_norm_w, w_up[0], w_down[0], final_w]
    gs = [g_attn_norm, g_w_in, g_lb, g_sb_norm, g_hg_norm, g_w_out, g_mlp_norm, g_w_up, g_w_down, g_final]
    ms = [m_attn_norm_w, m_w_in[0], m_lb_logits, m_sb_norm_w, m_hg_norm_w, m_w_out[0], m_mlp_norm_w, m_w_up[0],
          m_w_down[0], m_final_norm_w.reshape(1, d)]
    vs = [v_attn_norm_w, v_w_in[0], v_lb_logits, v_sb_norm_w, v_hg_norm_w, v_w_out[0], v_mlp_norm_w, v_w_up[0],
          v_w_down[0], v_final_norm_w.reshape(1, d)]
    shapes = [attn_norm_w.shape, w_in.shape, lb_logits.shape, sb_norm_w.shape, hg_norm_w.shape, w_out.shape,
              mlp_norm_w.shape, w_up.shape, w_down.shape, final_norm_w.shape]
    deltas, new_ms, new_vs = [], [], []
    for nm, w_, g_, m_, v_ in zip(names, ws, gs, ms, vs):
        dl, m2, v2 = _adamw(w_, g_, m_, v_, "adamw_" + nm)
        deltas.append(dl)
        new_ms.append(m2)
        new_vs.append(v2)

    def shaped(lst):
        return [a.reshape(s) for a, s in zip(lst, shapes)]

    return (loss, dx[None], *shaped(gs), *shaped(deltas), *shaped(new_ms), *shaped(new_vs))
```

```python
import functools

import jax
import jax.numpy as jnp
from jax import lax
from jax.experimental import pallas as pl
from jax.experimental.pallas import tpu as pltpu

F32 = jnp.float32
BF16 = jnp.bfloat16
MESH = pl.DeviceIdType.MESH

HEAD = 128
NORM_EPS = 1e-5
N_CHIPS = 4
ATTN_BLOCK = 256
ATTN_ROWS = 1024
ATTN_DEAD = -110.0
HG_CHUNK = 32
HG_ROWS = 256
HG_UNROLL = 4
VMEM_LIMIT = 56 * 1024 * 1024

ADAM_LR = 0.001
ADAM_B1 = 0.9
ADAM_B2 = 0.999
ADAM_EPS = 1e-08
ADAM_WD = 0.01
ADAM_STEP = 10


def _pick(n, cands):
    for c in cands:
        if n % c == 0:
            return c
    return n


def _params(sem):
    return pltpu.CompilerParams(dimension_semantics=sem, vmem_limit_bytes=VMEM_LIMIT)


def _dot(a, b):
    return jnp.dot(a, b, preferred_element_type=F32)


def _dot_nt(a, b):
    return lax.dot_general(a, b, (((1,), (1,)), ((), ())), preferred_element_type=F32)


def _dot_tn(a, b):
    return lax.dot_general(a, b, (((0,), (0,)), ((), ())), preferred_element_type=F32)


def _hilo(x):
    hi = x.astype(BF16)
    return hi, (x - hi.astype(F32)).astype(BF16)


def _dot_split(tri, x):
    hi, lo = _hilo(x)
    return _dot(tri, hi) + _dot(tri, lo)


def _dot3(dot, a, b):
    return dot(a[0], b[0]) + (dot(a[0], b[1]) + dot(a[1], b[0]))


def _sigmoid(x):
    return 1.0 / (1.0 + jnp.exp(-x))


def _hosted_call(inner, grid, in_specs, out_specs, out_shape, scratch, args, semantics, name, comm=None):
    kind, arrays = comm if comm else (None, ())
    n_i, n_o, n_s, n_c = len(in_specs), len(out_specs), len(scratch), len(arrays)

    def body(*refs):
        c_in = refs[n_i:n_i + n_c]
        c_out = refs[n_i + n_c + n_o:n_i + 2 * n_c + n_o]
        scr = refs[n_i + 2 * n_c + n_o:]
        if n_c:
            ids = [pl.program_id(ax) for ax in range(len(grid))]
            first, last = ids[0] == 0, ids[0] == grid[0] - 1
            for ax in range(1, len(grid)):
                first, last = first & (ids[ax] == 0), last & (ids[ax] == grid[ax] - 1)
            sems = scr[n_s:]
            copies = {"scatter": _scatter_copies, "swap": _swap_copies}
            begin, end = _gather_copies(c_out, *sems) if kind == "gather" else copies[kind](c_in, c_out, *sems)
            pl.when(first)(begin)
        inner(*refs[:n_i], *refs[n_i + n_c:n_i + n_c + n_o], *scr[:n_s])
        if n_c:
            pl.when(last)(end)

    gather = kind == "gather"
    shape = {"gather": lambda a: a.shape, "scatter": lambda a: (3,) + a.shape[1:],
             "swap": lambda a: (a.shape[0], a.shape[1] // 2, a.shape[2])}
    landed = [jax.ShapeDtypeStruct(shape[kind](a), a.dtype) for a in arrays]
    return pl.pallas_call(
        body, name=name, grid=grid,
        in_specs=list(in_specs) + [ANY] * n_c, out_specs=list(out_specs) + [ANY] * n_c,
        out_shape=list(out_shape) + landed,
        input_output_aliases={n_i + w: n_o + w for w in range(n_c)} if gather else {},
        scratch_shapes=list(scratch) + ([pltpu.SemaphoreType.DMA(
            ({"gather": 6, "scatter": 3, "swap": 1}[kind] * n_c,))] * 2 if n_c else []),
        compiler_params=_params(("arbitrary",) * len(grid) if n_c else semantics),
    )(*args, *arrays)


def _mm_body(kind, nk, n_extra, n_out, epi):
    dot = {"nn": _dot, "nt": _dot_nt, "tn": _dot_tn}[kind]

    def finish(acc, extra_refs, out_refs):
        res = epi(acc, *[e[...] for e in extra_refs]) if epi is not None else (acc,)
        for o, r in zip(out_refs, res):
            o[...] = r.astype(o.dtype)

    def body(a_ref, b_ref, *rest):
        extra_refs = rest[:n_extra]
        out_refs = rest[n_extra:n_extra + n_out]
        if nk == 1:
            finish(dot(a_ref[...], b_ref[...]), extra_refs, out_refs)
            return
        acc_ref = rest[n_extra + n_out]
        k = pl.program_id(2)

        @pl.when(k == 0)
        def _():
            acc_ref[...] = jnp.zeros_like(acc_ref)

        acc_ref[...] += dot(a_ref[...], b_ref[...])

        @pl.when(k == nk - 1)
        def _():
            finish(acc_ref[...], extra_refs, out_refs)

    return body


def _mm_nn(a, w, out_dtypes, *, tm, tn, tk, name, epi=None, extras=(), comm=None):
    m, r = a.shape
    p, _, c = w.shape
    npc = c // tn
    nk = r // tk
    tile = pl.BlockSpec((tm, tn), lambda i, j, k: (i, j))
    return _hosted_call(
        _mm_body("nn", nk, len(extras), len(out_dtypes), epi), (m // tm, p * npc, nk),
        [pl.BlockSpec((tm, tk), lambda i, j, k: (i, k)),
         pl.BlockSpec((None, tk, tn), lambda i, j, k: (j // npc, k, j % npc))]
        + [tile if e.shape[0] > 1 else pl.BlockSpec((1, tn), lambda i, j, k: (0, j)) for e in extras],
        [tile] * len(out_dtypes), [jax.ShapeDtypeStruct((m, p * c), d) for d in out_dtypes],
        [pltpu.VMEM((tm, tn), F32)] if nk > 1 else [], (a, w, *extras),
        ("parallel", "parallel", "arbitrary"), name, comm)


def _mm_nt(a, w, out_dtypes, *, tm, tn, tk, name, epi=None, extras=(), comm=None):
    m, _ = a.shape
    p, r, c = w.shape
    kpc = c // tk
    nk = p * kpc
    tile = pl.BlockSpec((tm, tn), lambda i, j, k: (i, j))
    return _hosted_call(
        _mm_body("nt", nk, len(extras), len(out_dtypes), epi), (m // tm, r // tn, nk),
        [pl.BlockSpec((tm, tk), lambda i, j, k: (i, k)),
         pl.BlockSpec((None, tn, tk), lambda i, j, k: (k // kpc, j, k % kpc))] + [tile] * len(extras),
        [tile] * len(out_dtypes), [jax.ShapeDtypeStruct((m, r), d) for d in out_dtypes],
        [pltpu.VMEM((tm, tn), F32)] if nk > 1 else [], (a, w, *extras),
        ("parallel", "parallel", "arbitrary"), name, comm)


def _mm_tn(a, g, p, *, tm, tn, tk, name):
    t, r = a.shape
    c = g.shape[1] // p
    npc = c // tn
    nk = t // tk
    body = _mm_body("tn", nk, 0, 1, None)
    return pl.pallas_call(
        body, name=name,
        grid=(r // tm, p * npc, nk),
        in_specs=[pl.BlockSpec((tk, tm), lambda i, j, k: (k, i)),
                  pl.BlockSpec((tk, tn), lambda i, j, k: (k, j))],
        out_specs=[pl.BlockSpec((None, tm, tn), lambda i, j, k: (j // npc, i, j % npc))],
        out_shape=[jax.ShapeDtypeStruct((p, r, c), F32)],
        scratch_shapes=[pltpu.VMEM((tm, tn), F32)] if nk > 1 else [],
        compiler_params=_params(("parallel", "parallel", "arbitrary")),
    )(a, g)[0]


def _rows(fn, row_ins, const_ins, row_outs, acc_outs, *, tm, name, comm=None):
    specs, arrays = [], []
    t = None
    for item in row_ins:
        if isinstance(item, tuple):
            arr, width, cb = item
            specs.append(pl.BlockSpec((tm, width), functools.partial(lambda i, cb: (i, cb), cb=cb)))
        else:
            arr = item
            specs.append(pl.BlockSpec((tm, arr.shape[1]), lambda i: (i, 0)))
        arrays.append(arr)
        t = arr.shape[0]
    for arr in const_ins:
        specs.append(pl.BlockSpec(arr.shape, lambda i: (0, 0)))
        arrays.append(arr)
    n_in, n_row, n_acc = len(arrays), len(row_outs), len(acc_outs)

    def body(*refs):
        ins = [r[...] for r in refs[:n_in]]
        outs = refs[n_in:]
        row_res, acc_res = fn(ins[:len(row_ins)], ins[len(row_ins):])
        for o, r in zip(outs[:n_row], row_res):
            o[...] = r.astype(o.dtype)
        if n_acc:
            i = pl.program_id(0)

            @pl.when(i == 0)
            def _():
                for o in outs[n_row:]:
                    o[...] = jnp.zeros_like(o)

            for o, r in zip(outs[n_row:], acc_res):
                o[...] += r

    res = _hosted_call(
        body, (t // tm,), specs,
        [pl.BlockSpec((tm, c), lambda i: (i, 0)) for c, _ in row_outs]
        + [pl.BlockSpec((1, c), lambda i: (0, 0)) for c in acc_outs],
        [jax.ShapeDtypeStruct((t, c), d) for c, d in row_outs] + [jax.ShapeDtypeStruct((1, c), F32) for c in acc_outs],
        [], arrays, ("arbitrary",), name, comm)
    if comm:
        return res[:n_row], res[n_row:n_row + n_acc], res[n_row + n_acc:]
    return res[:n_row], res[n_row:]


def _rstd(x):
    return lax.rsqrt(jnp.mean(x * x, axis=-1, keepdims=True) + NORM_EPS)


def _rms_bwd(x, w, dy):
    r = _rstd(x)
    n = x * r
    dn = dy * w
    dx = r * (dn - n * jnp.mean(dn * n, axis=-1, keepdims=True))
    return dx, dy * n


def _colsum(x):
    return jnp.sum(x, axis=0, keepdims=True)


def _heads_map(fn, width, *tiles):
    outs = None
    for h in range(width // HEAD):
        res = fn(*[t[:, h * HEAD:(h + 1) * HEAD] for t in tiles])
        if outs is None:
            outs = [[] for _ in res]
        for lst, r in zip(outs, res):
            lst.append(r)
    return [jnp.concatenate(lst, axis=1) for lst in outs]


def _log_one_minus_beta(z):
    return -(jnp.maximum(z, 0.0) + jnp.log(1.0 + jnp.exp(-jnp.abs(z))))


def _attn_fwd(proj, after_tri, norm_w, n_heads, name, comm=None):
    t = proj.shape[0]
    blk = min(ATTN_BLOCK, t)
    qb = min(ATTN_ROWS, t)
    ns = qb // blk
    scale = HEAD ** -0.5

    def body(q_ref, k_ref, v_ref, tri_ref, w_ref, o_ref, mix_ref, tot_ref):
        i = pl.program_id(1)
        q = (q_ref[...] * scale).astype(BF16)
        tri = tri_ref[...]

        def part(r0, j, acc_l, acc_o, masked):
            sl = pl.ds(pl.multiple_of(j * blk, blk), blk)
            m = qb - r0
            z = _dot_nt(q[r0:, :], k_ref[sl, :].astype(BF16))
            lm = _log_one_minus_beta(z)
            if masked:
                mask = lax.broadcasted_iota(jnp.int32, (m, blk), 1) < lax.broadcasted_iota(jnp.int32, (m, blk), 0)
                lmm = jnp.where(mask, lm, 0.0)
            else:
                lmm = lm
            w = jnp.exp(z + lm + acc_l[r0:, :] + _dot(lmm.astype(BF16), tri))
            if masked:
                w = jnp.where(mask, w, 0.0)
            new_l = acc_l[r0:, :] + jnp.sum(lmm, axis=1, keepdims=True)
            new_o = acc_o[r0:, :] + _dot(w.astype(BF16), v_ref[sl, :].astype(BF16))
            if r0:
                new_l = jnp.concatenate([acc_l[:r0, :], new_l], axis=0)
                new_o = jnp.concatenate([acc_o[:r0, :], new_o], axis=0)
            return new_l, new_o

        acc = (jnp.zeros((qb, 1), F32), jnp.zeros((qb, HEAD), F32))
        for jr in reversed(range(ns)):
            acc = part(jr * blk, ns * i + jr, *acc, True)

        def more(c):
            return (c[0] < ns * i) & (jnp.max(c[1]) > ATTN_DEAD)

        def step(c):
            return (c[0] + 1,) + part(0, ns * i - 1 - c[0], c[1], c[2], False)

        swept, acc_l, acc_o = lax.while_loop(more, step, (jnp.int32(0),) + acc)
        o_ref[...] = acc_o
        mix_ref[...] = (acc_o * _rstd(acc_o) * w_ref[...]).astype(BF16)
        first = (ns * i - swept).astype(F32)
        tot_ref[...] = jnp.where(lax.broadcasted_iota(jnp.int32, (qb, HEAD), 1) == 1, first, acc_l)

    width = n_heads * HEAD
    qblk = pl.BlockSpec((qb, HEAD), lambda h, i: (i, h))
    return _hosted_call(
        body, (n_heads, t // qb),
        [qblk,
         pl.BlockSpec((t, HEAD), lambda h, i: (0, n_heads + h)),
         pl.BlockSpec((t, HEAD), lambda h, i: (0, 2 * n_heads + h)),
         pl.BlockSpec((blk, blk), lambda h, i: (0, 0)),
         pl.BlockSpec((1, HEAD), lambda h, i: (0, 0))],
        [qblk, qblk, qblk],
        [jax.ShapeDtypeStruct((t, width), F32), jax.ShapeDtypeStruct((t, width), BF16),
         jax.ShapeDtypeStruct((t, width), F32)],
        [], (proj, proj, proj, after_tri, norm_w), ("parallel", "arbitrary"), name, comm)


def _attn_bwd(proj, tot, do, after_tri, before_tri, n_heads, name, comm=None):
    t = proj.shape[0]
    blk = min(ATTN_BLOCK, t)
    qb = min(ATTN_ROWS, t)
    ns = qb // blk
    scale = HEAD ** -0.5

    def body(q_ref, k_ref, v_ref, tot_ref, do_ref, after_ref, before_ref, dq_ref, dk_ref, dv_ref):
        i = pl.program_id(1)

        @pl.when(i == 0)
        def _():
            dk_ref[...] = jnp.zeros_like(dk_ref)
            dv_ref[...] = jnp.zeros_like(dv_ref)

        q = (q_ref[...] * scale).astype(BF16)
        dob = do_ref[...].astype(BF16)
        total = tot_ref[:, 0:1]
        after_tri = after_ref[...]
        before = before_ref[...]

        def part(r0, j, seen_l, seen_g, dq, masked):
            sl = pl.ds(pl.multiple_of(j * blk, blk), blk)
            m = qb - r0
            qq, dd = q[r0:, :], dob[r0:, :]
            kb = k_ref[sl, :].astype(BF16)
            z = _dot_nt(qq, kb)
            dw = _dot_nt(dd, v_ref[sl, :].astype(BF16))
            lm = _log_one_minus_beta(z)
            if masked:
                mask = lax.broadcasted_iota(jnp.int32, (m, blk), 1) < lax.broadcasted_iota(jnp.int32, (m, blk), 0)
                lmm = jnp.where(mask, lm, 0.0)
            else:
                lmm = lm
            row_l = jnp.sum(lmm, axis=1, keepdims=True)
            after = (total[r0:, :] - seen_l[r0:, :] - row_l) + _dot(lmm.astype(BF16), after_tri)
            w = jnp.exp(z + lm + after)
            if masked:
                w = jnp.where(mask, w, 0.0)
            sig = jnp.exp(z + lm)
            g = w * dw
            g_before = seen_g[r0:, :] + _dot(g.astype(BF16), before)
            dz = g * (1.0 - sig) - g_before * sig
            if masked:
                dz = jnp.where(mask, dz, 0.0)
            dzb = dz.astype(BF16)
            dk_ref[sl, :] += _dot_tn(dzb, qq)
            dv_ref[sl, :] += _dot_tn(w.astype(BF16), dd)
            new = (seen_l[r0:, :] + row_l,
                   seen_g[r0:, :] + jnp.sum(g, axis=1, keepdims=True), dq[r0:, :] + _dot(dzb, kb))
            if r0:
                new = tuple(jnp.concatenate([old[:r0, :], n], axis=0) for old, n in zip((seen_l, seen_g, dq), new))
            return new

        zero = jnp.zeros((qb, 1), F32)
        first = jnp.max(tot_ref[0:8, 1:2]).astype(jnp.int32)
        carry = lax.fori_loop(first, ns * i, lambda j, c: part(0, j, *c, False),
                              (zero, zero, jnp.zeros((qb, HEAD), F32)))
        for jr in range(ns):
            carry = part(jr * blk, ns * i + jr, *carry, True)
        dq_ref[...] = (carry[2] * scale).astype(dq_ref.dtype)

    width = n_heads * HEAD
    qblk = pl.BlockSpec((qb, HEAD), lambda h, i: (i, h))
    full = pl.BlockSpec((t, HEAD), lambda h, i: (0, h))
    tri = pl.BlockSpec((blk, blk), lambda h, i: (0, 0))
    return _hosted_call(
        body, (n_heads, t // qb),
        [qblk,
         pl.BlockSpec((t, HEAD), lambda h, i: (0, n_heads + h)),
         pl.BlockSpec((t, HEAD), lambda h, i: (0, 2 * n_heads + h)),
         qblk, qblk, tri, tri],
        [qblk, full, full], [jax.ShapeDtypeStruct((t, width), BF16)] + [jax.ShapeDtypeStruct((t, width), F32)] * 2,
        [], (proj, proj, proj, tot, do, after_tri, before_tri), ("parallel", "arbitrary"), name, comm)


def _lower_bound(logits):
    l0, l1 = logits[0:1, :], logits[1:2, :]
    mx = jnp.maximum(l0, l1)
    e0, e1 = jnp.exp(l0 - mx), jnp.exp(l1 - mx)
    return e0 / (e0 + e1)


def _hg_decay(q, k, g, sums):
    cum, mid, last = [_dot_split(m, g) for m in sums]
    return cum, mid, last, q * jnp.exp(cum - mid), k * jnp.exp(mid - cum), q * jnp.exp(cum), k * jnp.exp(last - cum)


def _hg_sums(rows, chunk):
    t = lax.broadcasted_iota(jnp.int32, (rows, rows), 0)
    j = lax.broadcasted_iota(jnp.int32, (rows, rows), 1)
    same = (t // chunk) == (j // chunk)
    mats = [same & (j <= t), same & (j % chunk < chunk // 2), same, same & (j >= t)]
    return [m.astype(BF16) for m in mats]


def _hgrn_fwd(proj, lb_logits, norm_w, sums, n_heads, heads_per_step, name, comm=None):
    t = proj.shape[0]
    bt = min(HG_ROWS, t)
    c = HG_CHUNK
    nc = bt // c
    hw = heads_per_step * HEAD
    width = n_heads * HEAD
    col0 = 3 * width // hw

    def body(hq_ref, hf_ref, hi_ref, hgate_ref, lbl_ref, w_ref, s0_ref, s1_ref, s2_ref, o_ref, mix_ref, st_ref,
             state, qt_scr, kt_scr, qe_scr, kd_scr, el_scr):
        @pl.when(pl.program_id(1) == 0)
        def _():
            state[...] = jnp.zeros_like(state)

        lb = _lower_bound(lbl_ref[...])
        f = hf_ref[...]
        hq = hq_ref[...]
        _, _, last, qt, kt, qe, kd = _hg_decay(hq * _sigmoid(hq), (1.0 - lb) * _sigmoid(-f),
                                               jnp.log(lb + (1.0 - lb) * _sigmoid(f)),
                                               (s0_ref[...], s1_ref[...], s2_ref[...]))
        qt_scr[...] = qt.astype(BF16)
        kt_scr[...] = kt.astype(BF16)
        qe_scr[...] = qe.astype(BF16)
        kd_scr[...] = kd.astype(BF16)
        el_scr[...] = jnp.exp(last)
        causal = lax.broadcasted_iota(jnp.int32, (c, c), 1) <= lax.broadcasted_iota(jnp.int32, (c, c), 0)

        def chunk(ci, carry):
            r = pl.ds(pl.multiple_of(ci * c, c), c)
            vc = hi_ref[r, :].astype(BF16)
            qt, kt, qe, kd = qt_scr[r, :], kt_scr[r, :], qe_scr[r, :], kd_scr[r, :]
            e_last = el_scr[r, :][0:1, :]
            old = [state[h] for h in range(heads_per_step)]
            scores, inter, outs, new = [], [], [], []
            for h in range(heads_per_step):
                cs = slice(h * HEAD, (h + 1) * HEAD)
                scores.append(jnp.where(causal, _dot_nt(qt[:, cs], kt[:, cs]), 0.0).astype(BF16))
                inter.append(_dot_nt(qe[:, cs], old[h].astype(BF16)))
                new.append(old[h] * e_last[:, cs] + _dot_tn(vc[:, cs], kd[:, cs]))
            for h in range(heads_per_step):
                outs.append(_dot(scores[h], vc[:, h * HEAD:(h + 1) * HEAD]) + inter[h])
            for h in range(heads_per_step):
                st_ref[ci, :, h * HEAD:(h + 1) * HEAD] = old[h]
                state[h] = new[h]
            o_ref[r, :] = jnp.concatenate(outs, axis=1)
            return carry

        lax.fori_loop(0, nc, chunk, 0, unroll=HG_UNROLL)

        def finish(o, gate):
            return ((o * _rstd(o) * w_ref[...]) * (gate * _sigmoid(gate)),)

        mix_ref[...] = _heads_map(finish, hw, o_ref[...], hgate_ref[...])[0].astype(BF16)

    def col(group):
        return pl.BlockSpec((bt, hw), functools.partial(lambda hp, tb, g: (tb, col0 + g * (width // hw) + hp), g=group))

    blk = pl.BlockSpec((bt, hw), lambda hp, tb: (tb, hp))
    mat = pl.BlockSpec((bt, bt), lambda hp, tb: (0, 0))
    return _hosted_call(
        body, (n_heads // heads_per_step, t // bt),
        [col(0), col(1), col(2), col(3),
         pl.BlockSpec((2, hw), lambda hp, tb: (0, hp)),
         pl.BlockSpec((1, HEAD), lambda hp, tb: (0, 0)), mat, mat, mat],
        [blk, blk, pl.BlockSpec((nc, HEAD, hw), lambda hp, tb: (tb, 0, hp))],
        [jax.ShapeDtypeStruct((t, width), F32), jax.ShapeDtypeStruct((t, width), BF16),
         jax.ShapeDtypeStruct((t // c, HEAD, width), F32)],
        [pltpu.VMEM((heads_per_step, HEAD, HEAD), F32)] + [pltpu.VMEM((bt, hw), BF16)] * 4
        + [pltpu.VMEM((bt, hw), F32)],
        (proj, proj, proj, proj, lb_logits, norm_w, *sums[:3]), ("parallel", "arbitrary"), name, comm)


def _hgrn_bwd(proj, do, states, lb_logits, sums, n_heads, heads_per_step, name):
    t = proj.shape[0]
    bt = min(HG_ROWS, t)
    c = HG_CHUNK
    nc = bt // c
    nb = t // bt
    hw = heads_per_step * HEAD
    width = n_heads * HEAD
    col0 = 3 * width // hw

    def body(hq_ref, hf_ref, hi_ref, do_ref, st_ref, lbl_ref, s0_ref, s1_ref, s2_ref, s3_ref,
             dq_ref, df_ref, di_ref, dlb_ref,
             dstate, qth, qtl, kth, ktl, qe_scr, kd_scr, el_scr, qa_scr, ka_scr, qb_scr, kb_scr, ss_scr):
        @pl.when(pl.program_id(1) == 0)
        def _():
            dstate[...] = jnp.zeros_like(dstate)
            dlb_ref[...] = jnp.zeros_like(dlb_ref)

        lb = _lower_bound(lbl_ref[...])
        f = hf_ref[...]
        sg = _sigmoid(f)
        sgn = _sigmoid(-f)
        den = lb + (1.0 - lb) * sg
        kk = (1.0 - lb) * sgn
        hq = hq_ref[...]
        sq = _sigmoid(hq)
        qq = hq * sq
        cum, mid, last, qt, kt, qe, kd = _hg_decay(qq, kk, jnp.log(den), (s0_ref[...], s1_ref[...], s2_ref[...]))
        qth[...], qtl[...] = _hilo(qt)
        kth[...], ktl[...] = _hilo(kt)
        qe_scr[...] = qe.astype(BF16)
        kd_scr[...] = kd.astype(BF16)
        e_last = jnp.exp(last)
        el_scr[...] = e_last
        causal = lax.broadcasted_iota(jnp.int32, (c, c), 1) <= lax.broadcasted_iota(jnp.int32, (c, c), 0)

        def chunk(cc, carry):
            ci = nc - 1 - cc
            r = pl.ds(pl.multiple_of(ci * c, c), c)
            qt = (qth[r, :], qtl[r, :])
            kt = (kth[r, :], ktl[r, :])
            qe, kd = qe_scr[r, :], kd_scr[r, :]
            doc, vc = do_ref[r, :].astype(BF16), hi_ref[r, :].astype(BF16)
            e_row = el_scr[r, :][0:1, :]
            sts = [st_ref[ci, :, h * HEAD:(h + 1) * HEAD] for h in range(heads_per_step)]
            dsts = [dstate[h] for h in range(heads_per_step)]
            a, da, di, dq_inter, dk_inter, dq_intra, dk_intra, st_sums, new = [], [], [], [], [], [], [], [], []
            for h in range(heads_per_step):
                cs = slice(h * HEAD, (h + 1) * HEAD)
                st, dst = sts[h].astype(BF16), dsts[h].astype(BF16)
                a.append(jnp.where(causal, _dot_nt(qt[0][:, cs], kt[0][:, cs]), 0.0).astype(BF16))
                da.append(jnp.where(causal, _dot_nt(doc[:, cs], vc[:, cs]), 0.0))
                di.append(_dot_nt(kd[:, cs], dst))
                dq_inter.append(_dot(doc[:, cs], st))
                dk_inter.append(_dot(vc[:, cs], dst))
                st_sums.append(_colsum(dsts[h] * sts[h]))
                new.append(dsts[h] * e_row[:, cs] + _dot_tn(doc[:, cs], qe[:, cs]))
            for h in range(heads_per_step):
                cs = slice(h * HEAD, (h + 1) * HEAD)
                da_h = _hilo(da[h])
                di[h] = di[h] + _dot_tn(a[h], doc[:, cs])
                dq_intra.append(_dot3(_dot, da_h, (kt[0][:, cs], kt[1][:, cs])))
                dk_intra.append(_dot3(_dot_tn, da_h, (qt[0][:, cs], qt[1][:, cs])))

            def wide(parts):
                return jnp.concatenate(parts, axis=1)

            for h in range(heads_per_step):
                dstate[h] = new[h]
            di_ref[r, :] = wide(di).astype(BF16)
            qa_scr[r, :] = wide(dq_intra)
            ka_scr[r, :] = wide(dk_intra)
            qb_scr[r, :] = wide(dq_inter)
            kb_scr[r, :] = wide(dk_inter)
            ss_scr[r, :] = jnp.broadcast_to(wide(st_sums), (c, hw))
            return carry

        lax.fori_loop(0, nc, chunk, 0, unroll=HG_UNROLL)

        dk_inter = kb_scr[...] * jnp.exp(last - cum)
        dq = qa_scr[...] * jnp.exp(cum - mid) + qb_scr[...] * jnp.exp(cum)
        dk = ka_scr[...] * jnp.exp(mid - cum) + dk_inter
        is_last = lax.broadcasted_iota(jnp.int32, (bt, hw), 0) % c == c - 1
        d_last = _dot_split(s2_ref[...], kk * dk_inter) + e_last * ss_scr[...]
        dcum = qq * dq - kk * dk + jnp.where(is_last, d_last, 0.0)
        e = (_dot_split(s3_ref[...], dcum) / den - dk) * sgn
        df_ref[...] = (e * (1.0 - lb) * sg).astype(BF16)
        dlb_ref[...] += _colsum(e)
        dq_ref[...] = (dq * (sq * (1.0 + hq * (1.0 - sq)))).astype(BF16)

    def col(group):
        return pl.BlockSpec((bt, hw), functools.partial(
            lambda hp, tb, g: (nb - 1 - tb, col0 + g * (width // hw) + hp), g=group))

    blk = pl.BlockSpec((bt, hw), lambda hp, tb: (nb - 1 - tb, hp))
    mat = pl.BlockSpec((bt, bt), lambda hp, tb: (0, 0))
    return pl.pallas_call(
        body, name=name,
        grid=(n_heads // heads_per_step, nb),
        in_specs=[col(0), col(1), col(2), blk,
                  pl.BlockSpec((nc, HEAD, hw), lambda hp, tb: (nb - 1 - tb, 0, hp)),
                  pl.BlockSpec((2, hw), lambda hp, tb: (0, hp)), mat, mat, mat, mat],
        out_specs=[blk, blk, blk, pl.BlockSpec((1, hw), lambda hp, tb: (0, hp))],
        out_shape=[jax.ShapeDtypeStruct((t, width), BF16)] * 3 + [jax.ShapeDtypeStruct((1, width), F32)],
        scratch_shapes=[pltpu.VMEM((heads_per_step, HEAD, HEAD), F32)] + [pltpu.VMEM((bt, hw), BF16)] * 6
                       + [pltpu.VMEM((bt, hw), F32)] * 6,
        compiler_params=_params(("parallel", "arbitrary")),
    )(proj, proj, proj, do, states, lb_logits, *sums)


def _place():
    x, y, c = lax.axis_index("x"), lax.axis_index("y"), lax.axis_index("c")
    chips = [(1 - x, y), (x, 1 - y), (1 - x, 1 - y)]
    return x, y, c, chips


ANY = pl.BlockSpec(memory_space=pl.ANY)


def _cast_to_slot(shard, ids, name):
    r, c = shard.shape
    tm = _pick(r, (256, 128, 64, 32, 16))

    def body(ids_ref, s_ref, o_ref):
        o_ref[...] = s_ref[...].astype(BF16)

    return pl.pallas_call(
        body, name=name,
        grid_spec=pltpu.PrefetchScalarGridSpec(
            num_scalar_prefetch=1, grid=(r // tm,),
            in_specs=[pl.BlockSpec((tm, c), lambda i, ids: (i, 0))],
            out_specs=pl.BlockSpec((None, tm, c), lambda i, ids: (ids[1], i, 0))),
        out_shape=jax.ShapeDtypeStruct((N_CHIPS, r, c), BF16),
        compiler_params=_params(("parallel",)),
    )(ids, shard)


def _gather_copies(bufs, send, recv):
    x, y, c, chips = _place()
    mine = 2 * x + y

    def half(ref, who, core):
        h = ref.shape[-2] // 2
        return ref.at[who, pl.ds(core * h, h), :]

    def copy(w, k, rows, to):
        return pltpu.make_async_remote_copy(src_ref=rows, dst_ref=rows, send_sem=send.at[6 * w + k],
                                            recv_sem=recv.at[6 * w + k], device_id=to, device_id_type=MESH)

    def to_chips(w):
        return [copy(w, k, half(bufs[w], mine, c), (qx, qy, c)) for k, (qx, qy) in enumerate(chips)]

    def to_sibling(w):
        return [copy(w, 3 + k, half(bufs[w], 2 * qx + qy, c), (x, y, 1 - c)) for k, (qx, qy) in enumerate(chips)]

    def begin():
        for w in range(len(bufs)):
            for cp in to_chips(w):
                cp.start()

    def end():
        for w in range(len(bufs)):
            for k, (qx, qy) in enumerate(chips):
                copy(w, k, half(bufs[w], 2 * qx + qy, c), (x, y, c)).wait_recv()
                to_sibling(w)[k].start()
        for w in range(len(bufs)):
            for k, (qx, qy) in enumerate(chips):
                copy(w, 3 + k, half(bufs[w], 2 * qx + qy, 1 - c), (x, y, c)).wait_recv()
        for w in range(len(bufs)):
            for cp in to_chips(w) + to_sibling(w):
                cp.wait_send()

    return begin, end


def _scatter_copies(ins, outs, send, recv):
    _, _, c, chips = _place()

    def copies():
        return [pltpu.make_async_remote_copy(
            src_ref=ins[w].at[2 * qx + qy], dst_ref=outs[w].at[k], send_sem=send.at[3 * w + k],
            recv_sem=recv.at[3 * w + k], device_id=(qx, qy, c), device_id_type=MESH)
            for w in range(len(ins)) for k, (qx, qy) in enumerate(chips)]

    def begin():
        for cp in copies():
            cp.start()

    def end():
        for cp in copies():
            cp.wait()

    return begin, end


def _swap_copies(ins, outs, send, recv):
    x, y, c, _ = _place()

    def copies():
        return [pltpu.make_async_remote_copy(
            src_ref=ins[w].at[:, pl.ds((1 - c) * (ins[w].shape[1] // 2), ins[w].shape[1] // 2), :], dst_ref=outs[w],
            send_sem=send.at[w], recv_sem=recv.at[w], device_id=(x, y, 1 - c), device_id_type=MESH)
            for w in range(len(ins))]

    def begin():
        for cp in copies():
            cp.start()

    def end():
        for cp in copies():
            cp.wait()

    return begin, end


def _sibling_swap(grads, name):
    n = len(grads)

    def body(*refs):
        begin, end = _swap_copies(refs[:n], refs[n:2 * n], *refs[2 * n:])
        begin()
        end()

    return pl.pallas_call(
        body, name=name, in_specs=[ANY] * n, out_specs=[ANY] * n,
        out_shape=[jax.ShapeDtypeStruct((g.shape[0], g.shape[1] // 2, g.shape[2]), g.dtype) for g in grads],
        scratch_shapes=[pltpu.SemaphoreType.DMA((n,)), pltpu.SemaphoreType.DMA((n,))],
    )(*grads)


def _sibling_join(fulls, name):
    n = len(fulls)

    def body(*refs):
        bufs = refs[n:2 * n]
        send, recv = refs[2 * n:]
        x, y, c, _ = _place()
        cps = []
        for w in range(n):
            h = bufs[w].shape[0] // 2
            rows = bufs[w].at[pl.ds(c * h, h), :]
            cps.append(pltpu.make_async_remote_copy(
                src_ref=rows, dst_ref=rows, send_sem=send.at[w], recv_sem=recv.at[w],
                device_id=(x, y, 1 - c), device_id_type=MESH))
            cps[-1].start()
        for w in range(n):
            h = bufs[w].shape[0] // 2
            theirs = bufs[w].at[pl.ds((1 - c) * h, h), :]
            pltpu.make_async_remote_copy(src_ref=theirs, dst_ref=theirs, send_sem=send.at[w], recv_sem=recv.at[w],
                                         device_id=(x, y, c), device_id_type=MESH).wait_recv()
        for cp in cps:
            cp.wait_send()

    return pl.pallas_call(
        body, name=name, in_specs=[ANY] * n, out_specs=[ANY] * n,
        out_shape=[jax.ShapeDtypeStruct(s.shape, s.dtype) for s in fulls],
        input_output_aliases={w: w for w in range(n)},
        scratch_shapes=[pltpu.SemaphoreType.DMA((n,)), pltpu.SemaphoreType.DMA((n,))],
    )(*fulls)


def _all_sum_small(vec, name):
    n = vec.shape[1]

    def body(v_ref, out_ref, buf, send, recv):
        x, y, c, _ = _place()
        me = 4 * x + 2 * y + c
        buf[me] = v_ref[...]
        peers = []
        for mask in range(1, 8):
            px = 1 - x if mask & 4 else x
            py = 1 - y if mask & 2 else y
            pc = 1 - c if mask & 1 else c
            peers.append((px, py, pc))
        cps = []
        for k, peer in enumerate(peers):
            cps.append(pltpu.make_async_remote_copy(src_ref=buf.at[me], dst_ref=buf.at[me], send_sem=send.at[k],
                                                    recv_sem=recv.at[k], device_id=peer, device_id_type=MESH))
            cps[-1].start()
        for k, (px, py, pc) in enumerate(peers):
            slot = buf.at[4 * px + 2 * py + pc]
            pltpu.make_async_remote_copy(src_ref=slot, dst_ref=slot, send_sem=send.at[k], recv_sem=recv.at[k],
                                         device_id=(x, y, c), device_id_type=MESH).wait_recv()
        for cp in cps:
            cp.wait_send()
        total = buf[0]
        for d in range(1, 8):
            total = total + buf[d]
        out_ref[...] = total

    vm = pl.BlockSpec(memory_space=pltpu.VMEM)
    return pl.pallas_call(
        body, name=name, in_specs=[vm], out_specs=vm,
        out_shape=jax.ShapeDtypeStruct(vec.shape, F32),
        scratch_shapes=[pltpu.VMEM((8, 8, n), F32), pltpu.SemaphoreType.DMA((7,)), pltpu.SemaphoreType.DMA((7,))],
    )(vec)


def _pair_sum(g, buf, ids, name):
    p, r, c = g.shape
    h = r // 2
    tr = _pick(h, (256, 128, 64, 32, 16))
    nh = h // tr

    def body(ids_ref, g_ref, b_ref, sums_ref, own_ref):
        s = g_ref[...] + b_ref[...]
        sums_ref[...] = s.astype(BF16)

        @pl.when(pl.program_id(1) == ids_ref[1])
        def _():
            own_ref[...] = s

    return pl.pallas_call(
        body, name=name,
        grid_spec=pltpu.PrefetchScalarGridSpec(
            num_scalar_prefetch=1, grid=(nh, p),
            in_specs=[pl.BlockSpec((None, tr, c), lambda i, q, ids: (q, ids[0] * nh + i, 0)),
                      pl.BlockSpec((None, tr, c), lambda i, q, ids: (q, i, 0))],
            out_specs=[pl.BlockSpec((None, tr, c), lambda i, q, ids: (q, i, 0)),
                       pl.BlockSpec((tr, c), lambda i, q, ids: (i, 0))]),
        out_shape=[jax.ShapeDtypeStruct((p, h, c), BF16), jax.ShapeDtypeStruct((h, c), F32)],
        compiler_params=_params(("parallel", "arbitrary")),
    )(ids, g, buf)


def _final_sum(own, others, ids, name):
    h, c = own.shape
    tr = _pick(h, (256, 128, 64, 32, 16))
    nh = h // tr

    def body(ids_ref, own_ref, oth_ref, out_ref):
        s = own_ref[...]
        for k in range(3):
            s = s + oth_ref[k].astype(F32)
        out_ref[...] = s

    return pl.pallas_call(
        body, name=name,
        grid_spec=pltpu.PrefetchScalarGridSpec(
            num_scalar_prefetch=1, grid=(nh,),
            in_specs=[pl.BlockSpec((tr, c), lambda i, ids: (i, 0)),
                      pl.BlockSpec((3, tr, c), lambda i, ids: (0, i, 0))],
            out_specs=pl.BlockSpec((tr, c), lambda i, ids: (ids[0] * nh + i, 0))),
        out_shape=jax.ShapeDtypeStruct((2 * h, c), F32),
        compiler_params=_params(("parallel",)),
    )(ids, own, others)


def _adamw(w, g, m, v, name):
    r, c = w.shape
    tm = _pick(r, (256, 128, 64, 32, 16, 8)) if r >= 8 else r

    def fn(rows, _):
        w_, g_, m_, v_ = rows
        m2 = ADAM_B1 * m_ + (1.0 - ADAM_B1) * g_
        v2 = ADAM_B2 * v_ + (1.0 - ADAM_B2) * (g_ * g_)
        m_hat = m2 / (1.0 - ADAM_B1 ** ADAM_STEP)
        v_hat = v2 / (1.0 - ADAM_B2 ** ADAM_STEP)
        delta = -ADAM_LR * (m_hat / (jnp.sqrt(v_hat) + ADAM_EPS) + ADAM_WD * w_)
        return [delta, m2, v2], []

    outs, _ = _rows(fn, [w, g, m, v], [], [(c, F32)] * 3, [], tm=tm, name=name)
    return outs


def kernel(x, attn_norm_w, w_in, lb_logits, sb_norm_w, hg_norm_w, w_out, mlp_norm_w, w_up, w_down, final_norm_w, loss_target, m_attn_norm_w, m_w_in, m_lb_logits, m_sb_norm_w, m_hg_norm_w, m_w_out, m_mlp_norm_w, m_w_up, m_w_down, m_final_norm_w, v_attn_norm_w, v_w_in, v_lb_logits, v_sb_norm_w, v_hg_norm_w, v_w_out, v_mlp_norm_w, v_w_up, v_w_down, v_final_norm_w):
    xs, tgt = x[0], loss_target[0]
    t, d = xs.shape
    width = d // 2
    n_heads = width // HEAD
    hps = min(8, n_heads)
    final_w = final_norm_w.reshape(1, d)
    tm_rows = _pick(t, (256, 128))
    tm = _pick(t, (1024, 512, 256))
    blk = min(ATTN_BLOCK, t)
    ones_a = jnp.ones((blk, blk), F32)
    after_tri = jnp.tril(ones_a, -1).astype(BF16)
    before_tri = jnp.triu(ones_a, 1).astype(BF16)
    hg_sums = _hg_sums(min(HG_ROWS, t), HG_CHUNK)
    cx, cy, cc = lax.axis_index("x"), lax.axis_index("y"), lax.axis_index("c")
    ids = jnp.stack([cc, 2 * cx + cy]).astype(jnp.int32)

    shards = [w_in[0], w_out[0], w_up[0], w_down[0]]
    cast = [_cast_to_slot(s, ids, f"cast_w{i}") for i, s in enumerate(shards)]
    d_ff = N_CHIPS * w_up.shape[2]
    cs_in, cs_up = w_in.shape[2], w_up.shape[2]
    tn_in = _pick(cs_in, (1792, 896, 512, 256, 128))
    tn_up = _pick(cs_up, (1024, 512, 256))
    tn_d = _pick(d, (1024, 512, 256))
    tk_d = _pick(d, (2048, 1024, 512))

    (u,), _, (g_in,) = _rows(lambda r, c_: ([r[0] * _rstd(r[0]) * c_[0]], []), [xs], [attn_norm_w], [(d, BF16)], [],
                             tm=tm_rows, name="norm_in", comm=("gather", cast[:1]))
    proj, g_out = _mm_nn(u, g_in, [F32], tm=tm, tn=tn_in, tk=tk_d, name="proj_in",
                         comm=("gather", cast[1:2]))
    o_a, mix_a, sb_tot, g_down = _attn_fwd(proj, after_tri, sb_norm_w, n_heads, "sb_fwd", comm=("gather", cast[3:]))
    w_out_all = g_out.reshape(1, d, d)
    w_down_all = g_down.reshape(1, d_ff, d)
    o_b, mix_b, states, g_up = _hgrn_fwd(proj, lb_logits, hg_norm_w, hg_sums, n_heads, hps, "hg_fwd",
                                         comm=("gather", cast[2:3]))
    mix = jnp.concatenate([mix_a, mix_b], axis=1)
    def out_and_norm(acc, res, w):
        hh = acc + res
        return hh, hh * _rstd(hh) * w

    h1, mn = _mm_nn(mix, w_out_all, [F32, BF16], tm=_pick(t, (512, 256)), tn=d, tk=tk_d, name="proj_out",
                    epi=out_and_norm, extras=(xs, mlp_norm_w))
    up_b, act = _mm_nn(mn, g_up, [BF16, BF16], tm=tm, tn=tn_up, tk=tk_d, name="mlp_up",
                       epi=lambda acc: (acc, jnp.square(jnp.maximum(acc, 0.0))))
    (h2,) = _mm_nn(act, w_down_all, [F32], tm=tm, tn=tn_d, tk=_pick(d_ff, (2048, 1024)), name="mlp_down",
                   epi=lambda acc, res: (acc + res,), extras=(h1,))

    def head(rows, consts):
        hh, tg = rows
        w = consts[0]
        n = hh * _rstd(hh)
        err = n * w - tg
        dhh, dw_rows = _rms_bwd(hh, w, err * (1.0 / d))
        return [dhh, dhh], [_colsum(dw_rows), _colsum(err * err)]

    (dh2, dh2_b), (g_final, loss_cols) = _rows(head, [h2, tgt], [final_w], [(d, F32), (d, BF16)], [d, d],
                                                 tm=tm_rows, name="loss_head")

    (dup,) = _mm_nt(dh2_b, w_down_all, [BF16], tm=tm, tn=_pick(d_ff, (1024, 512)), tk=tk_d, name="mlp_down_dx",
                    epi=lambda acc, upv: (acc * (2.0 * jnp.maximum(upv.astype(F32), 0.0)),), extras=(up_b,))
    gw_down = _mm_tn(act, dh2_b, 1, tm=_pick(d_ff, (1024, 512)), tn=tn_d, tk=_pick(t, (2048, 1024, 512, 256)),
                     name="mlp_down_dw")
    gw_down = gw_down.reshape(N_CHIPS, d_ff // N_CHIPS, d)
    dmn, their_down = _mm_nt(dup, g_up, [F32], tm=tm, tn=tn_d, tk=_pick(cs_up, (2048, 1024, 512)), name="mlp_up_dx",
                             comm=("swap", [gw_down]))
    gw_up = _mm_tn(mn, dup, N_CHIPS, tm=tn_d, tn=tn_up, tk=_pick(t, (2048, 1024, 512, 256)), name="mlp_up_dw")

    def norm_back(rows, consts):
        xx, dy, skip = rows
        dx, dw_rows = _rms_bwd(xx, consts[0], dy)
        tot = dx + skip
        return [tot, tot], [_colsum(dw_rows)]

    (dh1, dh1_b), (g_mlp_norm,) = _rows(norm_back, [h1, dmn, dh2], [mlp_norm_w], [(d, F32), (d, BF16)], [d],
                                         tm=tm_rows, name="norm_mlp_bwd")

    gw_out = _mm_tn(mix, dh1_b, 1, tm=tn_d, tn=tn_d, tk=_pick(t, (2048, 1024, 512, 256)), name="proj_out_dw")
    gw_out = gw_out.reshape(N_CHIPS, d // N_CHIPS, d)
    dmix, their_up, their_out = _mm_nt(dh1_b, w_out_all, [F32], tm=tm, tn=tn_d, tk=tk_d, name="proj_out_dx",
                                       comm=("swap", [gw_up, gw_out]))
    pair_mlp = [_pair_sum(g, b, ids, "grads_pair_sum_" + nm)
                for g, b, nm in ((gw_up, their_up, "up"), (gw_down, their_down, "down"))]

    def sb_norm_back(rows, consts):
        dx, dw_rows = _heads_map(lambda o, dy: _rms_bwd(o, consts[0], dy), width, *rows)
        dw = sum(_colsum(dw_rows[:, h * HEAD:(h + 1) * HEAD]) for h in range(n_heads))
        return [dx], [dw]

    (do_a,), (g_sb_norm,) = _rows(sb_norm_back, [o_a, (dmix, width, 0)], [sb_norm_w], [(width, BF16)], [HEAD],
                                  tm=tm_rows, name="sb_norm_bwd")
    dq_a, dk_a, dv_a, *landed_mlp = _attn_bwd(proj, sb_tot, do_a, after_tri, before_tri, n_heads, "sb_bwd",
                                              comm=("scatter", [p[0] for p in pair_mlp]))

    def hg_out_back(rows, consts):
        def one(o, gate, dy):
            sg = _sigmoid(gate)
            silu = gate * sg
            n = o * _rstd(o) * consts[0]
            do, dw_rows = _rms_bwd(o, consts[0], dy * silu)
            return do, dy * n * (sg * (1.0 + gate * (1.0 - sg))), dw_rows
        do, dgate, dw_rows = _heads_map(one, width, *rows)
        dw = sum(_colsum(dw_rows[:, h * HEAD:(h + 1) * HEAD]) for h in range(n_heads))
        return [do, dgate], [dw]

    (do_b, dgate), (g_hg_norm,) = _rows(hg_out_back, [o_b, (proj, width, 6), (dmix, width, 1)], [hg_norm_w],
                                         [(width, BF16)] * 2, [HEAD], tm=tm_rows, name="hg_out_bwd")
    dhq, dhf, dhi, dlb = _hgrn_bwd(proj, do_b, states, lb_logits, hg_sums, n_heads, hps, "hg_bwd")

    (dproj,), _ = _rows(lambda r, _c: ([jnp.concatenate([p.astype(BF16) for p in r], axis=1)], []),
                        [dq_a, dk_a, dv_a, dhq, dhf, dhi, dgate], [], [(7 * width, BF16)], [],
                        tm=tm_rows, name="pack_dproj")
    gw_in = _mm_tn(u, dproj, N_CHIPS, tm=tn_d, tn=tn_in, tk=_pick(t, (2048, 1024, 512, 256)), name="proj_in_dw")
    (their_in,) = _sibling_swap([gw_in], "grads_to_sibling_in")
    pair_mix = [_pair_sum(g, b, ids, "grads_pair_sum_" + nm)
                for g, b, nm in ((gw_in, their_in, "in"), (gw_out, their_out, "out"))]
    du, *landed_mix = _mm_nt(dproj, g_in, [F32], tm=tm, tn=tn_d, tk=_pick(cs_in, (1792, 896, 512, 256, 128)),
                             name="proj_in_dx", comm=("scatter", [p[0] for p in pair_mix]))
    (dx,), (g_attn_norm,) = _rows(lambda r, c_: (lambda dxx, dwr: ([dxx + r[2]], [_colsum(dwr)]))(
        *_rms_bwd(r[0], c_[0], r[1])), [xs, du, dh1], [attn_norm_w], [(d, F32)], [d], tm=tm_rows, name="norm_in_bwd")

    halves = [_final_sum(p[1], r, ids, f"grads_final_sum{i}")
              for i, (p, r) in enumerate(zip(pair_mix + pair_mlp, list(landed_mix) + list(landed_mlp)))]
    g_w_in, g_w_out, g_w_up, g_w_down = _sibling_join(halves, "grads_join")

    pieces = [g_attn_norm, g_mlp_norm, g_final, g_sb_norm, g_hg_norm, dlb, loss_cols]
    sizes = [p.shape[1] for p in pieces]
    flat = jnp.concatenate(pieces, axis=1)
    n_small = -(-flat.shape[1] // 1024) * 1024
    flat = jnp.pad(flat, ((0, 0), (0, n_small - flat.shape[1]))).reshape(8, n_small // 8)
    flat = _all_sum_small(flat, "small_all_sum").reshape(1, n_small)
    offs = [sum(sizes[:i]) for i in range(len(sizes))]
    g_attn_norm, g_mlp_norm, g_final, g_sb_norm, g_hg_norm, dlb, loss_cols = [
        flat[:, o:o + s] for o, s in zip(offs, sizes)]

    def small_tail(lbl_ref, dlb_ref, loss_ref, glb_ref, out_ref):
        lb = _lower_bound(lbl_ref[...])
        g0 = dlb_ref[...] * lb * (1.0 - lb)
        glb_ref[0:1, :] = g0
        glb_ref[1:2, :] = -g0
        out_ref[...] = jnp.zeros_like(out_ref) + 0.5 * jnp.sum(loss_ref[...]) * (1.0 / d)

    vm = pl.BlockSpec(memory_space=pltpu.VMEM)
    g_lb, loss11 = pl.pallas_call(
        small_tail, name="small_tail", in_specs=[vm, vm, vm], out_specs=[vm, vm],
        out_shape=[jax.ShapeDtypeStruct(lb_logits.shape, F32), jax.ShapeDtypeStruct((1, 128), F32)],
    )(lb_logits, dlb, loss_cols)
    loss = loss11[0, 0]

    names = ["attn_norm_w", "w_in", "lb_logits", "sb_norm_w", "hg_norm_w", "w_out", "mlp_norm_w", "w_up", "w_down",
             "final_norm_w"]
    ws = [attn_norm_w, w_in[0], lb_logits, sb_norm_w, hg_norm_w, w_out[0], mlp_norm_w, w_up[0], w_down[0], final_w]
    gs = [g_attn_norm, g_w_in, g_lb, g_sb_norm, g_hg_norm, g_w_out, g_mlp_norm, g_w_up, g_w_down, g_final]
    ms = [m_attn_norm_w, m_w_in[0], m_lb_logits, m_sb_norm_w, m_hg_norm_w, m_w_out[0], m_mlp_norm_w, m_w_up[0],
          m_w_down[0], m_final_norm_w.reshape(1, d)]
    vs = [v_attn_norm_w, v_w_in[0], v_lb_logits, v_sb_norm_w, v_hg_norm_w, v_w_out[0], v_mlp_norm_w, v_w_up[0],
          v_w_down[0], v_final_norm_w.reshape(1, d)]
    shapes = [attn_norm_w.shape, w_in.shape, lb_logits.shape, sb_norm_w.shape, hg_norm_w.shape, w_out.shape,
              mlp_norm_w.shape, w_up.shape, w_down.shape, final_norm_w.shape]
    deltas, new_ms, new_vs = [], [], []
    for nm, w_, g_, m_, v_ in zip(names, ws, gs, ms, vs):
        dl, m2, v2 = _adamw(w_, g_, m_, v_, "adamw_" + nm)
        deltas.append(dl)
        new_ms.append(m2)
        new_vs.append(v2)

    def shaped(lst):
        return [a.reshape(s) for a, s in zip(lst, shapes)]

    return (loss, dx[None], *shaped(gs), *shaped(deltas), *shaped(new_ms), *shaped(new_vs))
```

```python
import functools

import jax
import jax.numpy as jnp
from jax import lax
from jax.experimental import pallas as pl
from jax.experimental.pallas import tpu as pltpu

F32 = jnp.float32
BF16 = jnp.bfloat16
MESH = pl.DeviceIdType.MESH

HEAD = 128
NORM_EPS = 1e-5
N_CHIPS = 4
ATTN_BLOCK = 256
ATTN_ROWS = 1024
ATTN_DEAD = -110.0
HG_CHUNK = 32
HG_ROWS = 256
HG_UNROLL = 8
VMEM_LIMIT = 56 * 1024 * 1024

ADAM_LR = 0.001
ADAM_B1 = 0.9
ADAM_B2 = 0.999
ADAM_EPS = 1e-08
ADAM_WD = 0.01
ADAM_STEP = 10


def _pick(n, cands):
    for c in cands:
        if n % c == 0:
            return c
    return n


def _params(sem):
    return pltpu.CompilerParams(dimension_semantics=sem, vmem_limit_bytes=VMEM_LIMIT)


def _dot(a, b):
    return jnp.dot(a, b, preferred_element_type=F32)


def _dot_nt(a, b):
    return lax.dot_general(a, b, (((1,), (1,)), ((), ())), preferred_element_type=F32)


def _dot_tn(a, b):
    return lax.dot_general(a, b, (((0,), (0,)), ((), ())), preferred_element_type=F32)


def _hilo(x):
    hi = x.astype(BF16)
    return hi, (x - hi.astype(F32)).astype(BF16)


def _dot_split(tri, x):
    hi, lo = _hilo(x)
    return _dot(tri, hi) + _dot(tri, lo)


def _dot3(dot, a, b):
    return dot(a[0], b[0]) + (dot(a[0], b[1]) + dot(a[1], b[0]))


def _sigmoid(x):
    return 1.0 / (1.0 + jnp.exp(-x))


def _hosted_call(inner, grid, in_specs, out_specs, out_shape, scratch, args, semantics, name, comm=None):
    kind, arrays = comm if comm else (None, ())
    n_i, n_o, n_s, n_c = len(in_specs), len(out_specs), len(scratch), len(arrays)

    def body(*refs):
        c_in = refs[n_i:n_i + n_c]
        c_out = refs[n_i + n_c + n_o:n_i + 2 * n_c + n_o]
        scr = refs[n_i + 2 * n_c + n_o:]
        if n_c:
            ids = [pl.program_id(ax) for ax in range(len(grid))]
            first, last = ids[0] == 0, ids[0] == grid[0] - 1
            for ax in range(1, len(grid)):
                first, last = first & (ids[ax] == 0), last & (ids[ax] == grid[ax] - 1)
            sems = scr[n_s:]
            copies = {"scatter": _scatter_copies, "swap": _swap_copies}
            begin, end = _gather_copies(c_out, *sems) if kind == "gather" else copies[kind](c_in, c_out, *sems)
            pl.when(first)(begin)
        inner(*refs[:n_i], *refs[n_i + n_c:n_i + n_c + n_o], *scr[:n_s])
        if n_c:
            pl.when(last)(end)

    gather = kind == "gather"
    shape = {"gather": lambda a: a.shape, "scatter": lambda a: (3,) + a.shape[1:],
             "swap": lambda a: (a.shape[0], a.shape[1] // 2, a.shape[2])}
    landed = [jax.ShapeDtypeStruct(shape[kind](a), a.dtype) for a in arrays]
    return pl.pallas_call(
        body, name=name, grid=grid,
        in_specs=list(in_specs) + [ANY] * n_c, out_specs=list(out_specs) + [ANY] * n_c,
        out_shape=list(out_shape) + landed,
        input_output_aliases={n_i + w: n_o + w for w in range(n_c)} if gather else {},
        scratch_shapes=list(scratch) + ([pltpu.SemaphoreType.DMA(
            ({"gather": 6, "scatter": 3, "swap": 1}[kind] * n_c,))] * 2 if n_c else []),
        compiler_params=_params(("arbitrary",) * len(grid) if n_c else semantics),
    )(*args, *arrays)


def _mm_body(kind, nk, n_extra, n_out, epi):
    dot = {"nn": _dot, "nt": _dot_nt, "tn": _dot_tn}[kind]

    def finish(acc, extra_refs, out_refs):
        res = epi(acc, *[e[...] for e in extra_refs]) if epi is not None else (acc,)
        for o, r in zip(out_refs, res):
            o[...] = r.astype(o.dtype)

    def body(a_ref, b_ref, *rest):
        extra_refs = rest[:n_extra]
        out_refs = rest[n_extra:n_extra + n_out]
        if nk == 1:
            finish(dot(a_ref[...], b_ref[...]), extra_refs, out_refs)
            return
        acc_ref = rest[n_extra + n_out]
        k = pl.program_id(2)

        @pl.when(k == 0)
        def _():
            acc_ref[...] = jnp.zeros_like(acc_ref)

        acc_ref[...] += dot(a_ref[...], b_ref[...])

        @pl.when(k == nk - 1)
        def _():
            finish(acc_ref[...], extra_refs, out_refs)

    return body


def _mm_nn(a, w, out_dtypes, *, tm, tn, tk, name, epi=None, extras=(), comm=None):
    m, r = a.shape
    p, _, c = w.shape
    npc = c // tn
    nk = r // tk
    tile = pl.BlockSpec((tm, tn), lambda i, j, k: (i, j))
    return _hosted_call(
        _mm_body("nn", nk, len(extras), len(out_dtypes), epi), (m // tm, p * npc, nk),
        [pl.BlockSpec((tm, tk), lambda i, j, k: (i, k)),
         pl.BlockSpec((None, tk, tn), lambda i, j, k: (j // npc, k, j % npc))]
        + [tile if e.shape[0] > 1 else pl.BlockSpec((1, tn), lambda i, j, k: (0, j)) for e in extras],
        [tile] * len(out_dtypes), [jax.ShapeDtypeStruct((m, p * c), d) for d in out_dtypes],
        [pltpu.VMEM((tm, tn), F32)] if nk > 1 else [], (a, w, *extras),
        ("parallel", "parallel", "arbitrary"), name, comm)


def _mm_nt(a, w, out_dtypes, *, tm, tn, tk, name, epi=None, extras=(), comm=None):
    m, _ = a.shape
    p, r, c = w.shape
    kpc = c // tk
    nk = p * kpc
    tile = pl.BlockSpec((tm, tn), lambda i, j, k: (i, j))
    return _hosted_call(
        _mm_body("nt", nk, len(extras), len(out_dtypes), epi), (m // tm, r // tn, nk),
        [pl.BlockSpec((tm, tk), lambda i, j, k: (i, k)),
         pl.BlockSpec((None, tn, tk), lambda i, j, k: (k // kpc, j, k % kpc))] + [tile] * len(extras),
        [tile] * len(out_dtypes), [jax.ShapeDtypeStruct((m, r), d) for d in out_dtypes],
        [pltpu.VMEM((tm, tn), F32)] if nk > 1 else [], (a, w, *extras),
        ("parallel", "parallel", "arbitrary"), name, comm)


def _mm_tn(a, g, p, *, tm, tn, tk, name):
    t, r = a.shape
    c = g.shape[1] // p
    npc = c // tn
    nk = t // tk
    body = _mm_body("tn", nk, 0, 1, None)
    return pl.pallas_call(
        body, name=name,
        grid=(r // tm, p * npc, nk),
        in_specs=[pl.BlockSpec((tk, tm), lambda i, j, k: (k, i)),
                  pl.BlockSpec((tk, tn), lambda i, j, k: (k, j))],
        out_specs=[pl.BlockSpec((None, tm, tn), lambda i, j, k: (j // npc, i, j % npc))],
        out_shape=[jax.ShapeDtypeStruct((p, r, c), F32)],
        scratch_shapes=[pltpu.VMEM((tm, tn), F32)] if nk > 1 else [],
        compiler_params=_params(("parallel", "parallel", "arbitrary")),
    )(a, g)[0]


def _rows(fn, row_ins, const_ins, row_outs, acc_outs, *, tm, name, comm=None):
    specs, arrays = [], []
    t = None
    for item in row_ins:
        if isinstance(item, tuple):
            arr, width, cb = item
            specs.append(pl.BlockSpec((tm, width), functools.partial(lambda i, cb: (i, cb), cb=cb)))
        else:
            arr = item
            specs.append(pl.BlockSpec((tm, arr.shape[1]), lambda i: (i, 0)))
        arrays.append(arr)
        t = arr.shape[0]
    for arr in const_ins:
        specs.append(pl.BlockSpec(arr.shape, lambda i: (0, 0)))
        arrays.append(arr)
    n_in, n_row, n_acc = len(arrays), len(row_outs), len(acc_outs)

    def body(*refs):
        ins = [r[...] for r in refs[:n_in]]
        outs = refs[n_in:]
        row_res, acc_res = fn(ins[:len(row_ins)], ins[len(row_ins):])
        for o, r in zip(outs[:n_row], row_res):
            o[...] = r.astype(o.dtype)
        if n_acc:
            i = pl.program_id(0)

            @pl.when(i == 0)
            def _():
                for o in outs[n_row:]:
                    o[...] = jnp.zeros_like(o)

            for o, r in zip(outs[n_row:], acc_res):
                o[...] += r

    res = _hosted_call(
        body, (t // tm,), specs,
        [pl.BlockSpec((tm, c), lambda i: (i, 0)) for c, _ in row_outs]
        + [pl.BlockSpec((1, c), lambda i: (0, 0)) for c in acc_outs],
        [jax.ShapeDtypeStruct((t, c), d) for c, d in row_outs] + [jax.ShapeDtypeStruct((1, c), F32) for c in acc_outs],
        [], arrays, ("arbitrary",), name, comm)
    if comm:
        return res[:n_row], res[n_row:n_row + n_acc], res[n_row + n_acc:]
    return res[:n_row], res[n_row:]


def _rstd(x):
    return lax.rsqrt(jnp.mean(x * x, axis=-1, keepdims=True) + NORM_EPS)


def _rms_bwd(x, w, dy):
    r = _rstd(x)
    n = x * r
    dn = dy * w
    dx = r * (dn - n * jnp.mean(dn * n, axis=-1, keepdims=True))
    return dx, dy * n


def _colsum(x):
    return jnp.sum(x, axis=0, keepdims=True)


def _heads_map(fn, width, *tiles):
    outs = None
    for h in range(width // HEAD):
        res = fn(*[t[:, h * HEAD:(h + 1) * HEAD] for t in tiles])
        if outs is None:
            outs = [[] for _ in res]
        for lst, r in zip(outs, res):
            lst.append(r)
    return [jnp.concatenate(lst, axis=1) for lst in outs]


def _log_one_minus_beta(z):
    return -(jnp.maximum(z, 0.0) + jnp.log(1.0 + jnp.exp(-jnp.abs(z))))


def _attn_fwd(proj, after_tri, norm_w, n_heads, name, comm=None):
    t = proj.shape[0]
    blk = min(ATTN_BLOCK, t)
    qb = min(ATTN_ROWS, t)
    ns = qb // blk
    scale = HEAD ** -0.5

    def body(q_ref, k_ref, v_ref, tri_ref, w_ref, o_ref, mix_ref, tot_ref):
        i = pl.program_id(1)
        q = (q_ref[...] * scale).astype(BF16)
        tri = tri_ref[...]

        def part(r0, j, acc_l, acc_o, masked):
            sl = pl.ds(pl.multiple_of(j * blk, blk), blk)
            m = qb - r0
            z = _dot_nt(q[r0:, :], k_ref[sl, :].astype(BF16))
            lm = _log_one_minus_beta(z)
            if masked:
                mask = lax.broadcasted_iota(jnp.int32, (m, blk), 1) < lax.broadcasted_iota(jnp.int32, (m, blk), 0)
                lmm = jnp.where(mask, lm, 0.0)
            else:
                lmm = lm
            w = jnp.exp(z + lm + acc_l[r0:, :] + _dot(lmm.astype(BF16), tri))
            if masked:
                w = jnp.where(mask, w, 0.0)
            new_l = acc_l[r0:, :] + jnp.sum(lmm, axis=1, keepdims=True)
            new_o = acc_o[r0:, :] + _dot(w.astype(BF16), v_ref[sl, :].astype(BF16))
            if r0:
                new_l = jnp.concatenate([acc_l[:r0, :], new_l], axis=0)
                new_o = jnp.concatenate([acc_o[:r0, :], new_o], axis=0)
            return new_l, new_o

        acc = (jnp.zeros((qb, 1), F32), jnp.zeros((qb, HEAD), F32))
        for jr in reversed(range(ns)):
            acc = part(jr * blk, ns * i + jr, *acc, True)

        def more(c):
            return (c[0] < ns * i) & (jnp.max(c[1]) > ATTN_DEAD)

        def step(c):
            return (c[0] + 1,) + part(0, ns * i - 1 - c[0], c[1], c[2], False)

        swept, acc_l, acc_o = lax.while_loop(more, step, (jnp.int32(0),) + acc)
        o_ref[...] = acc_o
        mix_ref[...] = (acc_o * _rstd(acc_o) * w_ref[...]).astype(BF16)
        first = (ns * i - swept).astype(F32)
        tot_ref[...] = jnp.where(lax.broadcasted_iota(jnp.int32, (qb, HEAD), 1) == 1, first, acc_l)

    width = n_heads * HEAD
    qblk = pl.BlockSpec((qb, HEAD), lambda h, i: (i, h))
    return _hosted_call(
        body, (n_heads, t // qb),
        [qblk,
         pl.BlockSpec((t, HEAD), lambda h, i: (0, n_heads + h)),
         pl.BlockSpec((t, HEAD), lambda h, i: (0, 2 * n_heads + h)),
         pl.BlockSpec((blk, blk), lambda h, i: (0, 0)),
         pl.BlockSpec((1, HEAD), lambda h, i: (0, 0))],
        [qblk, qblk, qblk],
        [jax.ShapeDtypeStruct((t, width), F32), jax.ShapeDtypeStruct((t, width), BF16),
         jax.ShapeDtypeStruct((t, width), F32)],
        [], (proj, proj, proj, after_tri, norm_w), ("parallel", "arbitrary"), name, comm)


def _attn_bwd(proj, tot, do, after_tri, before_tri, n_heads, name, comm=None):
    t = proj.shape[0]
    blk = min(ATTN_BLOCK, t)
    qb = min(ATTN_ROWS, t)
    ns = qb // blk
    scale = HEAD ** -0.5

    def body(q_ref, k_ref, v_ref, tot_ref, do_ref, after_ref, before_ref, dq_ref, dk_ref, dv_ref):
        i = pl.program_id(1)

        @pl.when(i == 0)
        def _():
            dk_ref[...] = jnp.zeros_like(dk_ref)
            dv_ref[...] = jnp.zeros_like(dv_ref)

        q = (q_ref[...] * scale).astype(BF16)
        dob = do_ref[...].astype(BF16)
        total = tot_ref[:, 0:1]
        after_tri = after_ref[...]
        before = before_ref[...]

        def part(r0, j, seen_l, seen_g, dq, masked):
            sl = pl.ds(pl.multiple_of(j * blk, blk), blk)
            m = qb - r0
            qq, dd = q[r0:, :], dob[r0:, :]
            kb = k_ref[sl, :].astype(BF16)
            z = _dot_nt(qq, kb)
            dw = _dot_nt(dd, v_ref[sl, :].astype(BF16))
            lm = _log_one_minus_beta(z)
            if masked:
                mask = lax.broadcasted_iota(jnp.int32, (m, blk), 1) < lax.broadcasted_iota(jnp.int32, (m, blk), 0)
                lmm = jnp.where(mask, lm, 0.0)
            else:
                lmm = lm
            row_l = jnp.sum(lmm, axis=1, keepdims=True)
            after = (total[r0:, :] - seen_l[r0:, :] - row_l) + _dot(lmm.astype(BF16), after_tri)
            w = jnp.exp(z + lm + after)
            if masked:
                w = jnp.where(mask, w, 0.0)
            sig = jnp.exp(z + lm)
            g = w * dw
            g_before = seen_g[r0:, :] + _dot(g.astype(BF16), before)
            dz = g * (1.0 - sig) - g_before * sig
            if masked:
                dz = jnp.where(mask, dz, 0.0)
            dzb = dz.astype(BF16)
            dk_ref[sl, :] += _dot_tn(dzb, qq)
            dv_ref[sl, :] += _dot_tn(w.astype(BF16), dd)
            new = (seen_l[r0:, :] + row_l,
                   seen_g[r0:, :] + jnp.sum(g, axis=1, keepdims=True), dq[r0:, :] + _dot(dzb, kb))
            if r0:
                new = tuple(jnp.concatenate([old[:r0, :], n], axis=0) for old, n in zip((seen_l, seen_g, dq), new))
            return new

        zero = jnp.zeros((qb, 1), F32)
        first = jnp.max(tot_ref[0:8, 1:2]).astype(jnp.int32)
        carry = lax.fori_loop(first, ns * i, lambda j, c: part(0, j, *c, False),
                              (zero, zero, jnp.zeros((qb, HEAD), F32)))
        for jr in range(ns):
            carry = part(jr * blk, ns * i + jr, *carry, True)
        dq_ref[...] = (carry[2] * scale).astype(dq_ref.dtype)

    width = n_heads * HEAD
    qblk = pl.BlockSpec((qb, HEAD), lambda h, i: (i, h))
    full = pl.BlockSpec((t, HEAD), lambda h, i: (0, h))
    tri = pl.BlockSpec((blk, blk), lambda h, i: (0, 0))
    return _hosted_call(
        body, (n_heads, t // qb),
        [qblk,
         pl.BlockSpec((t, HEAD), lambda h, i: (0, n_heads + h)),
         pl.BlockSpec((t, HEAD), lambda h, i: (0, 2 * n_heads + h)),
         qblk, qblk, tri, tri],
        [qblk, full, full], [jax.ShapeDtypeStruct((t, width), BF16)] + [jax.ShapeDtypeStruct((t, width), F32)] * 2,
        [], (proj, proj, proj, tot, do, after_tri, before_tri), ("parallel", "arbitrary"), name, comm)


def _lower_bound(logits):
    l0, l1 = logits[0:1, :], logits[1:2, :]
    mx = jnp.maximum(l0, l1)
    e0, e1 = jnp.exp(l0 - mx), jnp.exp(l1 - mx)
    return e0 / (e0 + e1)


def _hg_decay(q, k, g, sums):
    cum, mid, last = [_dot_split(m, g) for m in sums]
    return cum, mid, last, q * jnp.exp(cum - mid), k * jnp.exp(mid - cum), q * jnp.exp(cum), k * jnp.exp(last - cum)


def _hg_sums(rows, chunk):
    t = lax.broadcasted_iota(jnp.int32, (rows, rows), 0)
    j = lax.broadcasted_iota(jnp.int32, (rows, rows), 1)
    same = (t // chunk) == (j // chunk)
    mats = [same & (j <= t), same & (j % chunk < chunk // 2), same, same & (j >= t)]
    return [m.astype(BF16) for m in mats]


def _hgrn_fwd(proj, lb_logits, norm_w, sums, n_heads, heads_per_step, name, comm=None):
    t = proj.shape[0]
    bt = min(HG_ROWS, t)
    c = HG_CHUNK
    nc = bt // c
    hw = heads_per_step * HEAD
    width = n_heads * HEAD
    col0 = 3 * width // hw

    def body(hq_ref, hf_ref, hi_ref, hgate_ref, lbl_ref, w_ref, s0_ref, s1_ref, s2_ref, o_ref, mix_ref, st_ref,
             state, qt_scr, kt_scr, qe_scr, kd_scr, el_scr):
        @pl.when(pl.program_id(1) == 0)
        def _():
            state[...] = jnp.zeros_like(state)

        lb = _lower_bound(lbl_ref[...])
        f = hf_ref[...]
        hq = hq_ref[...]
        _, _, last, qt, kt, qe, kd = _hg_decay(hq * _sigmoid(hq), (1.0 - lb) * _sigmoid(-f),
                                               jnp.log(lb + (1.0 - lb) * _sigmoid(f)),
                                               (s0_ref[...], s1_ref[...], s2_ref[...]))
        qt_scr[...] = qt.astype(BF16)
        kt_scr[...] = kt.astype(BF16)
        qe_scr[...] = qe.astype(BF16)
        kd_scr[...] = kd.astype(BF16)
        el_scr[...] = jnp.exp(last)
        causal = lax.broadcasted_iota(jnp.int32, (c, c), 1) <= lax.broadcasted_iota(jnp.int32, (c, c), 0)

        def chunk(ci, carry):
            r = pl.ds(pl.multiple_of(ci * c, c), c)
            vc = hi_ref[r, :].astype(BF16)
            qt, kt, qe, kd = qt_scr[r, :], kt_scr[r, :], qe_scr[r, :], kd_scr[r, :]
            e_last = el_scr[r, :][0:1, :]
            old = [state[h] for h in range(heads_per_step)]
            scores, inter, outs, new = [], [], [], []
            for h in range(heads_per_step):
                cs = slice(h * HEAD, (h + 1) * HEAD)
                scores.append(jnp.where(causal, _dot_nt(qt[:, cs], kt[:, cs]), 0.0).astype(BF16))
                inter.append(_dot_nt(qe[:, cs], old[h].astype(BF16)))
                new.append(old[h] * e_last[:, cs] + _dot_tn(vc[:, cs], kd[:, cs]))
            for h in range(heads_per_step):
                outs.append(_dot(scores[h], vc[:, h * HEAD:(h + 1) * HEAD]) + inter[h])
            for h in range(heads_per_step):
                st_ref[ci, :, h * HEAD:(h + 1) * HEAD] = old[h]
                state[h] = new[h]
            o_ref[r, :] = jnp.concatenate(outs, axis=1)
            return carry

        lax.fori_loop(0, nc, chunk, 0, unroll=HG_UNROLL)

        def finish(o, gate):
            return ((o * _rstd(o) * w_ref[...]) * (gate * _sigmoid(gate)),)

        mix_ref[...] = _heads_map(finish, hw, o_ref[...], hgate_ref[...])[0].astype(BF16)

    def col(group):
        return pl.BlockSpec((bt, hw), functools.partial(lambda hp, tb, g: (tb, col0 + g * (width // hw) + hp), g=group))

    blk = pl.BlockSpec((bt, hw), lambda hp, tb: (tb, hp))
    mat = pl.BlockSpec((bt, bt), lambda hp, tb: (0, 0))
    return _hosted_call(
        body, (n_heads // heads_per_step, t // bt),
        [col(0), col(1), col(2), col(3),
         pl.BlockSpec((2, hw), lambda hp, tb: (0, hp)),
         pl.BlockSpec((1, HEAD), lambda hp, tb: (0, 0)), mat, mat, mat],
        [blk, blk, pl.BlockSpec((nc, HEAD, hw), lambda hp, tb: (tb, 0, hp))],
        [jax.ShapeDtypeStruct((t, width), F32), jax.ShapeDtypeStruct((t, width), BF16),
         jax.ShapeDtypeStruct((t // c, HEAD, width), F32)],
        [pltpu.VMEM((heads_per_step, HEAD, HEAD), F32)] + [pltpu.VMEM((bt, hw), BF16)] * 4
        + [pltpu.VMEM((bt, hw), F32)],
        (proj, proj, proj, proj, lb_logits, norm_w, *sums[:3]), ("parallel", "arbitrary"), name, comm)


def _hgrn_bwd(proj, do, states, lb_logits, sums, n_heads, heads_per_step, name):
    t = proj.shape[0]
    bt = min(HG_ROWS, t)
    c = HG_CHUNK
    nc = bt // c
    nb = t // bt
    hw = heads_per_step * HEAD
    width = n_heads * HEAD
    col0 = 3 * width // hw

    def body(hq_ref, hf_ref, hi_ref, do_ref, st_ref, lbl_ref, s0_ref, s1_ref, s2_ref, s3_ref,
             dq_ref, df_ref, di_ref, dlb_ref,
             dstate, qth, qtl, kth, ktl, qe_scr, kd_scr, el_scr, qa_scr, ka_scr, qb_scr, kb_scr, ss_scr):
        @pl.when(pl.program_id(1) == 0)
        def _():
            dstate[...] = jnp.zeros_like(dstate)
            dlb_ref[...] = jnp.zeros_like(dlb_ref)

        lb = _lower_bound(lbl_ref[...])
        f = hf_ref[...]
        sg = _sigmoid(f)
        sgn = _sigmoid(-f)
        den = lb + (1.0 - lb) * sg
        kk = (1.0 - lb) * sgn
        hq = hq_ref[...]
        sq = _sigmoid(hq)
        qq = hq * sq
        cum, mid, last, qt, kt, qe, kd = _hg_decay(qq, kk, jnp.log(den), (s0_ref[...], s1_ref[...], s2_ref[...]))
        qth[...], qtl[...] = _hilo(qt)
        kth[...], ktl[...] = _hilo(kt)
        qe_scr[...] = qe.astype(BF16)
        kd_scr[...] = kd.astype(BF16)
        e_last = jnp.exp(last)
        el_scr[...] = e_last
        causal = lax.broadcasted_iota(jnp.int32, (c, c), 1) <= lax.broadcasted_iota(jnp.int32, (c, c), 0)

        def chunk(cc, carry):
            ci = nc - 1 - cc
            r = pl.ds(pl.multiple_of(ci * c, c), c)
            qt = (qth[r, :], qtl[r, :])
            kt = (kth[r, :], ktl[r, :])
            qe, kd = qe_scr[r, :], kd_scr[r, :]
            doc, vc = do_ref[r, :].astype(BF16), hi_ref[r, :].astype(BF16)
            e_row = el_scr[r, :][0:1, :]
            sts = [st_ref[ci, :, h * HEAD:(h + 1) * HEAD] for h in range(heads_per_step)]
            dsts = [dstate[h] for h in range(heads_per_step)]
            a, da, di, dq_inter, dk_inter, dq_intra, dk_intra, st_sums, new = [], [], [], [], [], [], [], [], []
            for h in range(heads_per_step):
                cs = slice(h * HEAD, (h + 1) * HEAD)
                st, dst = sts[h].astype(BF16), dsts[h].astype(BF16)
                a.append(jnp.where(causal, _dot_nt(qt[0][:, cs], kt[0][:, cs]), 0.0).astype(BF16))
                da.append(jnp.where(causal, _dot_nt(doc[:, cs], vc[:, cs]), 0.0))
                di.append(_dot_nt(kd[:, cs], dst))
                dq_inter.append(_dot(doc[:, cs], st))
                dk_inter.append(_dot(vc[:, cs], dst))
                st_sums.append(_colsum(dsts[h] * sts[h]))
                new.append(dsts[h] * e_row[:, cs] + _dot_tn(doc[:, cs], qe[:, cs]))
            for h in range(heads_per_step):
                cs = slice(h * HEAD, (h + 1) * HEAD)
                da_h = _hilo(da[h])
                di[h] = di[h] + _dot_tn(a[h], doc[:, cs])
                dq_intra.append(_dot3(_dot, da_h, (kt[0][:, cs], kt[1][:, cs])))
                dk_intra.append(_dot3(_dot_tn, da_h, (qt[0][:, cs], qt[1][:, cs])))

            def wide(parts):
                return jnp.concatenate(parts, axis=1)

            for h in range(heads_per_step):
                dstate[h] = new[h]
            di_ref[r, :] = wide(di).astype(BF16)
            qa_scr[r, :] = wide(dq_intra)
            ka_scr[r, :] = wide(dk_intra)
            qb_scr[r, :] = wide(dq_inter)
            kb_scr[r, :] = wide(dk_inter)
            ss_scr[r, :] = jnp.broadcast_to(wide(st_sums), (c, hw))
            return carry

        lax.fori_loop(0, nc, chunk, 0, unroll=HG_UNROLL)

        dk_inter = kb_scr[...] * jnp.exp(last - cum)
        dq = qa_scr[...] * jnp.exp(cum - mid) + qb_scr[...] * jnp.exp(cum)
        dk = ka_scr[...] * jnp.exp(mid - cum) + dk_inter
        is_last = lax.broadcasted_iota(jnp.int32, (bt, hw), 0) % c == c - 1
        d_last = _dot_split(s2_ref[...], kk * dk_inter) + e_last * ss_scr[...]
        dcum = qq * dq - kk * dk + jnp.where(is_last, d_last, 0.0)
        e = (_dot_split(s3_ref[...], dcum) / den - dk) * sgn
        df_ref[...] = (e * (1.0 - lb) * sg).astype(BF16)
        dlb_ref[...] += _colsum(e)
        dq_ref[...] = (dq * (sq * (1.0 + hq * (1.0 - sq)))).astype(BF16)

    def col(group):
        return pl.BlockSpec((bt, hw), functools.partial(
            lambda hp, tb, g: (nb - 1 - tb, col0 + g * (width // hw) + hp), g=group))

    blk = pl.BlockSpec((bt, hw), lambda hp, tb: (nb - 1 - tb, hp))
    mat = pl.BlockSpec((bt, bt), lambda hp, tb: (0, 0))
    return pl.pallas_call(
        body, name=name,
        grid=(n_heads // heads_per_step, nb),
        in_specs=[col(0), col(1), col(2), blk,
                  pl.BlockSpec((nc, HEAD, hw), lambda hp, tb: (nb - 1 - tb, 0, hp)),
                  pl.BlockSpec((2, hw), lambda hp, tb: (0, hp)), mat, mat, mat, mat],
        out_specs=[blk, blk, blk, pl.BlockSpec((1, hw), lambda hp, tb: (0, hp))],
        out_shape=[jax.ShapeDtypeStruct((t, width), BF16)] * 3 + [jax.ShapeDtypeStruct((1, width), F32)],
        scratch_shapes=[pltpu.VMEM((heads_per_step, HEAD, HEAD), F32)] + [pltpu.VMEM((bt, hw), BF16)] * 6
                       + [pltpu.VMEM((bt, hw), F32)] * 6,
        compiler_params=_params(("parallel", "arbitrary")),
    )(proj, proj, proj, do, states, lb_logits, *sums)


def _place():
    x, y, c = lax.axis_index("x"), lax.axis_index("y"), lax.axis_index("c")
    chips = [(1 - x, y), (x, 1 - y), (1 - x, 1 - y)]
    return x, y, c, chips


ANY = pl.BlockSpec(memory_space=pl.ANY)


def _cast_to_slot(shard, ids, name):
    r, c = shard.shape
    tm = _pick(r, (256, 128, 64, 32, 16))

    def body(ids_ref, s_ref, o_ref):
        o_ref[...] = s_ref[...].astype(BF16)

    return pl.pallas_call(
        body, name=name,
        grid_spec=pltpu.PrefetchScalarGridSpec(
            num_scalar_prefetch=1, grid=(r // tm,),
            in_specs=[pl.BlockSpec((tm, c), lambda i, ids: (i, 0))],
            out_specs=pl.BlockSpec((None, tm, c), lambda i, ids: (ids[1], i, 0))),
        out_shape=jax.ShapeDtypeStruct((N_CHIPS, r, c), BF16),
        compiler_params=_params(("parallel",)),
    )(ids, shard)


def _gather_copies(bufs, send, recv):
    x, y, c, chips = _place()
    mine = 2 * x + y

    def half(ref, who, core):
        h = ref.shape[-2] // 2
        return ref.at[who, pl.ds(core * h, h), :]

    def copy(w, k, rows, to):
        return pltpu.make_async_remote_copy(src_ref=rows, dst_ref=rows, send_sem=send.at[6 * w + k],
                                            recv_sem=recv.at[6 * w + k], device_id=to, device_id_type=MESH)

    def to_chips(w):
        return [copy(w, k, half(bufs[w], mine, c), (qx, qy, c)) for k, (qx, qy) in enumerate(chips)]

    def to_sibling(w):
        return [copy(w, 3 + k, half(bufs[w], 2 * qx + qy, c), (x, y, 1 - c)) for k, (qx, qy) in enumerate(chips)]

    def begin():
        for w in range(len(bufs)):
            for cp in to_chips(w):
                cp.start()

    def end():
        for w in range(len(bufs)):
            for k, (qx, qy) in enumerate(chips):
                copy(w, k, half(bufs[w], 2 * qx + qy, c), (x, y, c)).wait_recv()
                to_sibling(w)[k].start()
        for w in range(len(bufs)):
            for k, (qx, qy) in enumerate(chips):
                copy(w, 3 + k, half(bufs[w], 2 * qx + qy, 1 - c), (x, y, c)).wait_recv()
        for w in range(len(bufs)):
            for cp in to_chips(w) + to_sibling(w):
                cp.wait_send()

    return begin, end


def _scatter_copies(ins, outs, send, recv):
    _, _, c, chips = _place()

    def copies():
        return [pltpu.make_async_remote_copy(
            src_ref=ins[w].at[2 * qx + qy], dst_ref=outs[w].at[k], send_sem=send.at[3 * w + k],
            recv_sem=recv.at[3 * w + k], device_id=(qx, qy, c), device_id_type=MESH)
            for w in range(len(ins)) for k, (qx, qy) in enumerate(chips)]

    def begin():
        for cp in copies():
            cp.start()

    def end():
        for cp in copies():
            cp.wait()

    return begin, end


def _swap_copies(ins, outs, send, recv):
    x, y, c, _ = _place()

    def copies():
        return [pltpu.make_async_remote_copy(
            src_ref=ins[w].at[:, pl.ds((1 - c) * (ins[w].shape[1] // 2), ins[w].shape[1] // 2), :], dst_ref=outs[w],
            send_sem=send.at[w], recv_sem=recv.at[w], device_id=(x, y, 1 - c), device_id_type=MESH)
            for w in range(len(ins))]

    def begin():
        for cp in copies():
            cp.start()

    def end():
        for cp in copies():
            cp.wait()

    return begin, end


def _sibling_swap(grads, name):
    n = len(grads)

    def body(*refs):
        begin, end = _swap_copies(refs[:n], refs[n:2 * n], *refs[2 * n:])
        begin()
        end()

    return pl.pallas_call(
        body, name=name, in_specs=[ANY] * n, out_specs=[ANY] * n,
        out_shape=[jax.ShapeDtypeStruct((g.shape[0], g.shape[1] // 2, g.shape[2]), g.dtype) for g in grads],
        scratch_shapes=[pltpu.SemaphoreType.DMA((n,)), pltpu.SemaphoreType.DMA((n,))],
    )(*grads)


def _sibling_join(fulls, name):
    n = len(fulls)

    def body(*refs):
        bufs = refs[n:2 * n]
        send, recv = refs[2 * n:]
        x, y, c, _ = _place()
        cps = []
        for w in range(n):
            h = bufs[w].shape[0] // 2
            rows = bufs[w].at[pl.ds(c * h, h), :]
            cps.append(pltpu.make_async_remote_copy(
                src_ref=rows, dst_ref=rows, send_sem=send.at[w], recv_sem=recv.at[w],
                device_id=(x, y, 1 - c), device_id_type=MESH))
            cps[-1].start()
        for w in range(n):
            h = bufs[w].shape[0] // 2
            theirs = bufs[w].at[pl.ds((1 - c) * h, h), :]
            pltpu.make_async_remote_copy(src_ref=theirs, dst_ref=theirs, send_sem=send.at[w], recv_sem=recv.at[w],
                                         device_id=(x, y, c), device_id_type=MESH).wait_recv()
        for cp in cps:
            cp.wait_send()

    return pl.pallas_call(
        body, name=name, in_specs=[ANY] * n, out_specs=[ANY] * n,
        out_shape=[jax.ShapeDtypeStruct(s.shape, s.dtype) for s in fulls],
        input_output_aliases={w: w for w in range(n)},
        scratch_shapes=[pltpu.SemaphoreType.DMA((n,)), pltpu.SemaphoreType.DMA((n,))],
    )(*fulls)


def _all_sum_small(vec, name):
    n = vec.shape[1]

    def body(v_ref, out_ref, buf, send, recv):
        x, y, c, _ = _place()
        me = 4 * x + 2 * y + c
        buf[me] = v_ref[...]
        peers = []
        for mask in range(1, 8):
            px = 1 - x if mask & 4 else x
            py = 1 - y if mask & 2 else y
            pc = 1 - c if mask & 1 else c
            peers.append((px, py, pc))
        cps = []
        for k, peer in enumerate(peers):
            cps.append(pltpu.make_async_remote_copy(src_ref=buf.at[me], dst_ref=buf.at[me], send_sem=send.at[k],
                                                    recv_sem=recv.at[k], device_id=peer, device_id_type=MESH))
            cps[-1].start()
        for k, (px, py, pc) in enumerate(peers):
            slot = buf.at[4 * px + 2 * py + pc]
            pltpu.make_async_remote_copy(src_ref=slot, dst_ref=slot, send_sem=send.at[k], recv_sem=recv.at[k],
                                         device_id=(x, y, c), device_id_type=MESH).wait_recv()
        for cp in cps:
            cp.wait_send()
        total = buf[0]
        for d in range(1, 8):
            total = total + buf[d]
        out_ref[...] = total

    vm = pl.BlockSpec(memory_space=pltpu.VMEM)
    return pl.pallas_call(
        body, name=name, in_specs=[vm], out_specs=vm,
        out_shape=jax.ShapeDtypeStruct(vec.shape, F32),
        scratch_shapes=[pltpu.VMEM((8, 8, n), F32), pltpu.SemaphoreType.DMA((7,)), pltpu.SemaphoreType.DMA((7,))],
    )(vec)


def _pair_sum(g, buf, ids, name):
    p, r, c = g.shape
    h = r // 2
    tr = _pick(h, (256, 128, 64, 32, 16))
    nh = h // tr

    def body(ids_ref, g_ref, b_ref, sums_ref, own_ref):
        s = g_ref[...] + b_ref[...]
        sums_ref[...] = s.astype(BF16)

        @pl.when(pl.program_id(1) == ids_ref[1])
        def _():
            own_ref[...] = s

    return pl.pallas_call(
        body, name=name,
        grid_spec=pltpu.PrefetchScalarGridSpec(
            num_scalar_prefetch=1, grid=(nh, p),
            in_specs=[pl.BlockSpec((None, tr, c), lambda i, q, ids: (q, ids[0] * nh + i, 0)),
                      pl.BlockSpec((None, tr, c), lambda i, q, ids: (q, i, 0))],
            out_specs=[pl.BlockSpec((None, tr, c), lambda i, q, ids: (q, i, 0)),
                       pl.BlockSpec((tr, c), lambda i, q, ids: (i, 0))]),
        out_shape=[jax.ShapeDtypeStruct((p, h, c), BF16), jax.ShapeDtypeStruct((h, c), F32)],
        compiler_params=_params(("parallel", "arbitrary")),
    )(ids, g, buf)


def _final_sum(own, others, ids, name):
    h, c = own.shape
    tr = _pick(h, (256, 128, 64, 32, 16))
    nh = h // tr

    def body(ids_ref, own_ref, oth_ref, out_ref):
        s = own_ref[...]
        for k in range(3):
            s = s + oth_ref[k].astype(F32)
        out_ref[...] = s

    return pl.pallas_call(
        body, name=name,
        grid_spec=pltpu.PrefetchScalarGridSpec(
            num_scalar_prefetch=1, grid=(nh,),
            in_specs=[pl.BlockSpec((tr, c), lambda i, ids: (i, 0)),
                      pl.BlockSpec((3, tr, c), lambda i, ids: (0, i, 0))],
            out_specs=pl.BlockSpec((tr, c), lambda i, ids: (ids[0] * nh + i, 0))),
        out_shape=jax.ShapeDtypeStruct((2 * h, c), F32),
        compiler_params=_params(("parallel",)),
    )(ids, own, others)


def _adamw(w, g, m, v, name):
    r, c = w.shape
    tm = _pick(r, (256, 128, 64, 32, 16, 8)) if r >= 8 else r

    def fn(rows, _):
        w_, g_, m_, v_ = rows
        m2 = ADAM_B1 * m_ + (1.0 - ADAM_B1) * g_
        v2 = ADAM_B2 * v_ + (1.0 - ADAM_B2) * (g_ * g_)
        m_hat = m2 / (1.0 - ADAM_B1 ** ADAM_STEP)
        v_hat = v2 / (1.0 - ADAM_B2 ** ADAM_STEP)
        delta = -ADAM_LR * (m_hat / (jnp.sqrt(v_hat) + ADAM_EPS) + ADAM_WD * w_)
        return [delta, m2, v2], []

    outs, _ = _rows(fn, [w, g, m, v], [], [(c, F32)] * 3, [], tm=tm, name=name)
    return outs


def kernel(x, attn_norm_w, w_in, lb_logits, sb_norm_w, hg_norm_w, w_out, mlp_norm_w, w_up, w_down, final_norm_w, loss_target, m_attn_norm_w, m_w_in, m_lb_logits, m_sb_norm_w, m_hg_norm_w, m_w_out, m_mlp_norm_w, m_w_up, m_w_down, m_final_norm_w, v_attn_norm_w, v_w_in, v_lb_logits, v_sb_norm_w, v_hg_norm_w, v_w_out, v_mlp_norm_w, v_w_up, v_w_down, v_final_norm_w):
    xs, tgt = x[0], loss_target[0]
    t, d = xs.shape
    width = d // 2
    n_heads = width // HEAD
    hps = min(8, n_heads)
    final_w = final_norm_w.reshape(1, d)
    tm_rows = _pick(t, (256, 128))
    tm = _pick(t, (1024, 512, 256))
    blk = min(ATTN_BLOCK, t)
    ones_a = jnp.ones((blk, blk), F32)
    after_tri = jnp.tril(ones_a, -1).astype(BF16)
    before_tri = jnp.triu(ones_a, 1).astype(BF16)
    hg_sums = _hg_sums(min(HG_ROWS, t), HG_CHUNK)
    cx, cy, cc = lax.axis_index("x"), lax.axis_index("y"), lax.axis_index("c")
    ids = jnp.stack([cc, 2 * cx + cy]).astype(jnp.int32)

    shards = [w_in[0], w_out[0], w_up[0], w_down[0]]
    cast = [_cast_to_slot(s, ids, f"cast_w{i}") for i, s in enumerate(shards)]
    d_ff = N_CHIPS * w_up.shape[2]
    cs_in, cs_up = w_in.shape[2], w_up.shape[2]
    tn_in = _pick(cs_in, (1792, 896, 512, 256, 128))
    tn_up = _pick(cs_up, (1024, 512, 256))
    tn_d = _pick(d, (1024, 512, 256))
    tk_d = _pick(d, (2048, 1024, 512))

    (u,), _, (g_in,) = _rows(lambda r, c_: ([r[0] * _rstd(r[0]) * c_[0]], []), [xs], [attn_norm_w], [(d, BF16)], [],
                             tm=tm_rows, name="norm_in", comm=("gather", cast[:1]))
    proj, g_out = _mm_nn(u, g_in, [F32], tm=tm, tn=tn_in, tk=tk_d, name="proj_in",
                         comm=("gather", cast[1:2]))
    o_a, mix_a, sb_tot, g_down = _attn_fwd(proj, after_tri, sb_norm_w, n_heads, "sb_fwd", comm=("gather", cast[3:]))
    w_out_all = g_out.reshape(1, d, d)
    w_down_all = g_down.reshape(1, d_ff, d)
    o_b, mix_b, states, g_up = _hgrn_fwd(proj, lb_logits, hg_norm_w, hg_sums, n_heads, hps, "hg_fwd",
                                         comm=("gather", cast[2:3]))
    mix = jnp.concatenate([mix_a, mix_b], axis=1)
    def out_and_norm(acc, res, w):
        hh = acc + res
        return hh, hh * _rstd(hh) * w

    h1, mn = _mm_nn(mix, w_out_all, [F32, BF16], tm=_pick(t, (512, 256)), tn=d, tk=tk_d, name="proj_out",
                    epi=out_and_norm, extras=(xs, mlp_norm_w))
    up_b, act = _mm_nn(mn, g_up, [BF16, BF16], tm=tm, tn=tn_up, tk=tk_d, name="mlp_up",
                       epi=lambda acc: (acc, jnp.square(jnp.maximum(acc, 0.0))))
    (h2,) = _mm_nn(act, w_down_all, [F32], tm=tm, tn=tn_d, tk=_pick(d_ff, (2048, 1024)), name="mlp_down",
                   epi=lambda acc, res: (acc + res,), extras=(h1,))

    def head(rows, consts):
        hh, tg = rows
        w = consts[0]
        n = hh * _rstd(hh)
        err = n * w - tg
        dhh, dw_rows = _rms_bwd(hh, w, err * (1.0 / d))
        return [dhh, dhh], [_colsum(dw_rows), _colsum(err * err)]

    (dh2, dh2_b), (g_final, loss_cols) = _rows(head, [h2, tgt], [final_w], [(d, F32), (d, BF16)], [d, d],
                                                 tm=tm_rows, name="loss_head")

    (dup,) = _mm_nt(dh2_b, w_down_all, [BF16], tm=tm, tn=_pick(d_ff, (1024, 512)), tk=tk_d, name="mlp_down_dx",
                    epi=lambda acc, upv: (acc * (2.0 * jnp.maximum(upv.astype(F32), 0.0)),), extras=(up_b,))
    gw_down = _mm_tn(act, dh2_b, 1, tm=_pick(d_ff, (1024, 512)), tn=tn_d, tk=_pick(t, (2048, 1024, 512, 256)),
                     name="mlp_down_dw")
    gw_down = gw_down.reshape(N_CHIPS, d_ff // N_CHIPS, d)
    dmn, their_down = _mm_nt(dup, g_up, [F32], tm=tm, tn=tn_d, tk=_pick(cs_up, (2048, 1024, 512)), name="mlp_up_dx",
                             comm=("swap", [gw_down]))
    gw_up = _mm_tn(mn, dup, N_CHIPS, tm=tn_d, tn=tn_up, tk=_pick(t, (2048, 1024, 512, 256)), name="mlp_up_dw")

    def norm_back(rows, consts):
        xx, dy, skip = rows
        dx, dw_rows = _rms_bwd(xx, consts[0], dy)
        tot = dx + skip
        return [tot, tot], [_colsum(dw_rows)]

    (dh1, dh1_b), (g_mlp_norm,) = _rows(norm_back, [h1, dmn, dh2], [mlp_norm_w], [(d, F32), (d, BF16)], [d],
                                         tm=tm_rows, name="norm_mlp_bwd")

    gw_out = _mm_tn(mix, dh1_b, 1, tm=tn_d, tn=tn_d, tk=_pick(t, (2048, 1024, 512, 256)), name="proj_out_dw")
    gw_out = gw_out.reshape(N_CHIPS, d // N_CHIPS, d)
    dmix, their_up, their_out = _mm_nt(dh1_b, w_out_all, [F32], tm=tm, tn=tn_d, tk=tk_d, name="proj_out_dx",
                                       comm=("swap", [gw_up, gw_out]))
    pair_mlp = [_pair_sum(g, b, ids, "grads_pair_sum_" + nm)
                for g, b, nm in ((gw_up, their_up, "up"), (gw_down, their_down, "down"))]

    def sb_norm_back(rows, consts):
        dx, dw_rows = _heads_map(lambda o, dy: _rms_bwd(o, consts[0], dy), width, *rows)
        dw = sum(_colsum(dw_rows[:, h * HEAD:(h + 1) * HEAD]) for h in range(n_heads))
        return [dx], [dw]

    (do_a,), (g_sb_norm,) = _rows(sb_norm_back, [o_a, (dmix, width, 0)], [sb_norm_w], [(width, BF16)], [HEAD],
                                  tm=tm_rows, name="sb_norm_bwd")
    dq_a, dk_a, dv_a, *landed_mlp = _attn_bwd(proj, sb_tot, do_a, after_tri, before_tri, n_heads, "sb_bwd",
                                              comm=("scatter", [p[0] for p in pair_mlp]))

    def hg_out_back(rows, consts):
        def one(o, gate, dy):
            sg = _sigmoid(gate)
            silu = gate * sg
            n = o * _rstd(o) * consts[0]
            do, dw_rows = _rms_bwd(o, consts[0], dy * silu)
            return do, dy * n * (sg * (1.0 + gate * (1.0 - sg))), dw_rows
        do, dgate, dw_rows = _heads_map(one, width, *rows)
        dw = sum(_colsum(dw_rows[:, h * HEAD:(h + 1) * HEAD]) for h in range(n_heads))
        return [do, dgate], [dw]

    (do_b, dgate), (g_hg_norm,) = _rows(hg_out_back, [o_b, (proj, width, 6), (dmix, width, 1)], [hg_norm_w],
                                         [(width, BF16)] * 2, [HEAD], tm=tm_rows, name="hg_out_bwd")
    dhq, dhf, dhi, dlb = _hgrn_bwd(proj, do_b, states, lb_logits, hg_sums, n_heads, hps, "hg_bwd")

    (dproj,), _ = _rows(lambda r, _c: ([jnp.concatenate([p.astype(BF16) for p in r], axis=1)], []),
                        [dq_a, dk_a, dv_a, dhq, dhf, dhi, dgate], [], [(7 * width, BF16)], [],
                        tm=tm_rows, name="pack_dproj")
    gw_in = _mm_tn(u, dproj, N_CHIPS, tm=tn_d, tn=tn_in, tk=_pick(t, (2048, 1024, 512, 256)), name="proj_in_dw")
    (their_in,) = _sibling_swap([gw_in], "grads_to_sibling_in")
    pair_mix = [_pair_sum(g, b, ids, "grads_pair_sum_" + nm)
                for g, b, nm in ((gw_in, their_in, "in"), (gw_out, their_out, "out"))]
    du, *landed_mix = _mm_nt(dproj, g_in, [F32], tm=tm, tn=tn_d, tk=_pick(cs_in, (1792, 896, 512, 256, 128)),
                             name="proj_in_dx", comm=("scatter", [p[0] for p in pair_mix]))
    (dx,), (g_attn_norm,) = _rows(lambda r, c_: (lambda dxx, dwr: ([dxx + r[2]], [_colsum(dwr)]))(
        *_rms_bwd(r[0], c_[0], r[1])), [xs, du, dh1], [attn_norm_w], [(d, F32)], [d], tm=tm_rows, name="norm_in_bwd")

    halves = [_final_sum(p[1], r, ids, f"grads_final_sum{i}")
              for i, (p, r) in enumerate(zip(pair_mix + pair_mlp, list(landed_mix) + list(landed_mlp)))]
    g_w_in, g_w_out, g_w_up, g_w_down = _sibling_join(halves, "grads_join")

    pieces = [g_attn_norm, g_mlp_norm, g_final, g_sb_norm, g_hg_norm, dlb, loss_cols]
    sizes = [p.shape[1] for p in pieces]
    flat = jnp.concatenate(pieces, axis=1)
    n_small = -(-flat.shape[1] // 1024) * 1024
    flat = jnp.pad(flat, ((0, 0), (0, n_small - flat.shape[1]))).reshape(8, n_small // 8)
    flat = _all_sum_small(flat, "small_all_sum").reshape(1, n_small)
    offs = [sum(sizes[:i]) for i in range(len(sizes))]
    g_attn_norm, g_mlp_norm, g_final, g_sb_norm, g_hg_norm, dlb, loss_cols = [
        flat[:, o:o + s] for o, s in zip(offs, sizes)]

    def small_tail(lbl_ref, dlb_ref, loss_ref, glb_ref, out_ref):
        lb = _lower_bound(lbl_ref[...])
        g0 = dlb_ref[...] * lb * (1.0 - lb)
        glb_ref[0:1, :] = g0
        glb_ref[1:2, :] = -g0
        out_ref[...] = jnp.zeros_like(out_ref) + 0.5 * jnp.sum(loss_ref[...]) * (1.0 / d)

    vm = pl.BlockSpec(memory_space=pltpu.VMEM)
    g_lb, loss11 = pl.pallas_call(
        small_tail, name="small_tail", in_specs=[vm, vm, vm], out_specs=[vm, vm],
        out_shape=[jax.ShapeDtypeStruct(lb_logits.shape, F32), jax.ShapeDtypeStruct((1, 128), F32)],
    )(lb_logits, dlb, loss_cols)
    loss = loss11[0, 0]

    names = ["attn_norm_w", "w_in", "lb_logits", "sb_norm_w", "hg_norm_w", "w_out", "mlp_norm_w", "w_up", "w_down",
             "final_norm_w"]
    ws = [attn_norm_w, w_in[0], lb_logits, sb_norm_w, hg_norm_w, w_out[0], mlp_norm_w, w_up[0], w_down[0], final_w]
    gs = [g_attn_norm, g_w_in, g_lb, g_sb_norm, g_hg_norm, g_w_out, g_mlp_norm, g_w_up, g_w_down, g_final]
    ms = [m_attn_norm_w, m_w_in[0], m_lb_logits, m_sb_norm_w, m_hg_norm_w, m_w_out[0], m_mlp_norm_w, m_w_up[0],
          m_w_down[0], m_final_norm_w.reshape(1, d)]
    vs = [v_attn_norm_w, v_w_in[0], v_lb_logits, v_sb_norm_w, v_hg_norm_w, v_w_out[0], v_mlp_norm_w, v_w_up[0],
          v_w_down[0], v_final_norm_w.reshape(1, d)]
    shapes = [attn_norm_w.shape, w_in.shape, lb_logits.shape, sb_norm_w.shape, hg_norm_w.shape, w_out.shape,
              mlp_norm_w.shape, w_up.shape, w_down.shape, final_norm_w.shape]
    deltas, new_ms, new_vs = [], [], []
    for nm, w_, g_, m_, v_ in zip(names, ws, gs, ms, vs):
        dl, m2, v2 = _adamw(w_, g_, m_, v_, "adamw_" + nm)
        deltas.append(dl)
        new_ms.append(m2)
        new_vs.append(v2)

    def shaped(lst):
        return [a.reshape(s) for a, s in zip(lst, shapes)]

    return (loss, dx[None], *shaped(gs), *shaped(deltas), *shaped(new_ms), *shaped(new_vs))
```
